```python
import math
import jax, jax.numpy as jnp
from jax import lax
import numpy as np

D_MODEL = 2048
BATCH = 8
SEQ = 2048
DEPTH = 1

MIX_WIDTH = D_MODEL
SSD_WIDTH = MIX_WIDTH // 2
ATTN_WIDTH = MIX_WIDTH - SSD_WIDTH
SSD_HEAD_DIM = 64
SSD_HEADS = SSD_WIDTH // SSD_HEAD_DIM
SSD_GROUPS = 2
SSD_STATE = 128
SSD_CONV = 4
SSD_CHUNK = 128
SSD_CONV_DIM = SSD_WIDTH + 2 * SSD_GROUPS * SSD_STATE
ATTN_V_DIM = 128
ATTN_HEADS = ATTN_WIDTH // ATTN_V_DIM
ATTN_QK_DIM = ATTN_V_DIM // 2
ROPE_THETA = 500000.0
ROPE_DIM = ATTN_QK_DIM // 4
Q_BLOCK = 128
IN_COLS = SSD_WIDTH + SSD_CONV_DIM + SSD_HEADS + 3 * ATTN_WIDTH
N_EXPERT_GROUPS = 4
EXPERTS_PER_GROUP = 8
N_EXPERTS = N_EXPERT_GROUPS * EXPERTS_PER_GROUP
TOP_K = 2
EXPERT_HIDDEN = D_MODEL // 2
EPS = 1e-6

kernel_name = "hybrid_ssd_diffattn_hmoe_layer"


def rms_norm(x, w, eps=EPS):
    xf = x.astype(jnp.float32)
    y = xf * lax.rsqrt(jnp.mean(xf * xf, axis=-1, keepdims=True) + eps)
    return (y * w.astype(jnp.float32)).astype(x.dtype)


def lambda_init_fn(layer_idx):
    return 0.8 - 0.6 * math.exp(-0.3 * layer_idx)


def partial_rotary(t, positions):
    inv_freq = jnp.power(ROPE_THETA, -jnp.arange(0, ROPE_DIM, 2, dtype=jnp.float32) / ROPE_DIM)
    ang = positions.astype(jnp.float32)[..., None] * inv_freq
    cos = jnp.cos(ang)[:, :, None, None, :]
    sin = jnp.sin(ang)[:, :, None, None, :]
    half = ROPE_DIM // 2
    t1 = t[..., :half].astype(jnp.float32)
    t2 = t[..., half:ROPE_DIM].astype(jnp.float32)
    rot = jnp.concatenate([t1 * cos - t2 * sin, t2 * cos + t1 * sin], axis=-1).astype(t.dtype)
    return jnp.concatenate([rot, t[..., ROPE_DIM:]], axis=-1)


def causal_depthwise_conv(u, w, b):
    c = u.shape[-1]
    kern = w.reshape(SSD_CONV, 1, c).astype(u.dtype)
    y = lax.conv_general_dilated(u, kern, window_strides=(1,), padding=[(SSD_CONV - 1, 0)],
                                 dimension_numbers=('NWC', 'WIO', 'NWC'), feature_group_count=c)
    return y + b.astype(u.dtype)


def ssd_chunked(x, dt, A, Bm, Cm):
    b, s, h, p = x.shape
    g, n = Bm.shape[2], Bm.shape[3]
    hg = h // g
    c = s // SSD_CHUNK
    L = SSD_CHUNK
    xdt = (x * dt[..., None]).reshape(b, c, L, g, hg, p)
    a_cs = jnp.cumsum((dt * A).reshape(b, c, L, g, hg), axis=2)
    Bc = Bm.reshape(b, c, L, g, n)
    Cc = Cm.reshape(b, c, L, g, n)
    causal = jnp.tril(jnp.ones((L, L), dtype=bool))[None, None, :, :, None, None]
    seg = a_cs[:, :, :, None] - a_cs[:, :, None, :]
    decay = jnp.exp(jnp.where(causal, seg, -jnp.inf))
    cb = jnp.einsum('bclgn,bcsgn->bclsg', Cc, Bc)
    y_diag = jnp.einsum('bclsg,bclsgh,bcsghp->bclghp', cb, decay, xdt)
    decay_states = jnp.exp(a_cs[:, :, -1:] - a_cs)
    states = jnp.einsum('bclgn,bclgh,bclghp->bcghpn', Bc, decay_states, xdt)
    chunk_decay = jnp.exp(a_cs[:, :, -1])

    def step(carry, inp):
        st, dec = inp
        return carry * dec[..., None, None] + st, carry

    init = jnp.zeros((b, g, hg, p, n), dtype=x.dtype)
    _, prev = lax.scan(step, init, (jnp.moveaxis(states, 1, 0), jnp.moveaxis(chunk_decay, 1, 0)))
    prev = jnp.moveaxis(prev, 0, 1)
    y_off = jnp.einsum('bclgn,bcghpn,bclgh->bclghp', Cc, prev, jnp.exp(a_cs))
    return (y_diag + y_off).reshape(b, s, h, p)


def ssd_branch(z, xbc, dt_raw, conv_w, conv_b, dt_bias, a_log, d_skip, norm_w):
    bsz, s, _ = z.shape
    xbc = jax.nn.silu(causal_depthwise_conv(xbc, conv_w, conv_b))
    xs, Bm, Cm = jnp.split(xbc, [SSD_WIDTH, SSD_WIDTH + SSD_GROUPS * SSD_STATE], axis=-1)
    xs = xs.reshape(bsz, s, SSD_HEADS, SSD_HEAD_DIM).astype(jnp.float32)
    Bm = Bm.reshape(bsz, s, SSD_GROUPS, SSD_STATE).astype(jnp.float32)
    Cm = Cm.reshape(bsz, s, SSD_GROUPS, SSD_STATE).astype(jnp.float32)
    dt = jax.nn.softplus(dt_raw.astype(jnp.float32) + dt_bias.astype(jnp.float32))
    A = -jnp.exp(a_log.astype(jnp.float32))
    y = ssd_chunked(xs, dt, A, Bm, Cm) + xs * d_skip.astype(jnp.float32)[:, None]
    y = y.reshape(bsz, s, SSD_WIDTH) * jax.nn.silu(z.astype(jnp.float32))
    y = y.reshape(bsz, s, SSD_GROUPS, SSD_WIDTH // SSD_GROUPS)
    y = y * lax.rsqrt(jnp.mean(y * y, axis=-1, keepdims=True) + EPS)
    y = y.reshape(bsz, s, SSD_WIDTH) * norm_w.astype(jnp.float32)
    return y.astype(z.dtype)


def diff_attention_branch(q, k, v, positions, q_norm_w, k_norm_w, lq1, lk1, lq2, lk2, subln_w, lam_init):
    bsz, s, _ = q.shape
    q = q.reshape(bsz, s, ATTN_HEADS, 2, ATTN_QK_DIM)
    k = k.reshape(bsz, s, ATTN_HEADS, 2, ATTN_QK_DIM)
    v = v.reshape(bsz, s, ATTN_HEADS, ATTN_V_DIM)
    q = partial_rotary(rms_norm(q, q_norm_w), positions)
    k = partial_rotary(rms_norm(k, k_norm_w), positions)
    f32 = jnp.float32
    lam = (jnp.exp(jnp.sum(lq1.astype(f32) * lk1.astype(f32)))
           - jnp.exp(jnp.sum(lq2.astype(f32) * lk2.astype(f32))) + lam_init)
    scale = 1.0 / math.sqrt(ATTN_QK_DIM)
    outs = []
    for start in range(0, s, Q_BLOCK):
        end = start + Q_BLOCK
        sc = jnp.einsum('bqhcd,bkhcd->bhcqk', q[:, start:end], k[:, :end]).astype(f32) * scale
        mask = (start + jnp.arange(Q_BLOCK))[:, None] >= jnp.arange(end)[None, :]
        pr = jax.nn.softmax(jnp.where(mask, sc, -jnp.inf), axis=-1)
        att = pr[:, :, 0] - lam * pr[:, :, 1]
        outs.append(jnp.einsum('bhqk,bkhd->bqhd', att.astype(v.dtype), v[:, :end]))
    o = jnp.concatenate(outs, axis=1)
    o = rms_norm(o, subln_w) * (1.0 - lam_init)
    return o.reshape(bsz, s, ATTN_WIDTH)


def hierarchical_moe(x, w_rg, b_rg, w_re, b_re, w_gate, w_up, w_down):
    bsz, s, d = x.shape
    xt = x.reshape(-1, d)
    n = xt.shape[0]
    g_logits = (xt @ w_rg).astype(jnp.float32) + b_rg.astype(jnp.float32)
    g_prob = jax.nn.softmax(g_logits, axis=-1)
    g_sel = jnp.argmax(g_logits, axis=-1)
    g_w = jnp.take_along_axis(g_prob, g_sel[:, None], axis=-1)
    e_logits = ((xt @ w_re).astype(jnp.float32) + b_re.astype(jnp.float32)).reshape(n, N_EXPERT_GROUPS, EXPERTS_PER_GROUP)
    e_in = jnp.take_along_axis(e_logits, g_sel[:, None, None], axis=1)[:, 0]
    top_p, top_i = lax.top_k(jax.nn.softmax(e_in, axis=-1), TOP_K)
    weights = top_p / jnp.sum(top_p, axis=-1, keepdims=True) * g_w
    expert = (g_sel[:, None] * EXPERTS_PER_GROUP + top_i).reshape(-1)
    order = jnp.argsort(expert)
    tok = order // TOP_K
    sizes = jnp.bincount(expert, length=N_EXPERTS).astype(jnp.int32)
    xs = xt[tok]
    hid = jax.nn.silu(lax.ragged_dot(xs, w_gate, sizes)) * lax.ragged_dot(xs, w_up, sizes)
    ys = lax.ragged_dot(hid, w_down, sizes) * weights.reshape(-1)[order][:, None].astype(x.dtype)
    out = jnp.zeros_like(xt).at[tok].add(ys)
    return out.reshape(bsz, s, d)


def setup_inputs(seed: int = 0) -> dict:
    key = jax.random.key(seed)
    ks = jax.random.split(key, 32)
    f32 = jnp.float32
    nrm = lambda k, shape, sc: jax.random.normal(k, shape, f32) * sc
    x = jax.random.normal(ks[0], (BATCH, SEQ, D_MODEL), f32)
    offs = jax.random.randint(ks[1], (BATCH, 1), 0, 4096, dtype=jnp.int32)
    positions = (offs + jnp.arange(SEQ, dtype=jnp.int32)[None, :]).astype(jnp.int32)
    dt0 = jnp.exp(jax.random.uniform(ks[6], (DEPTH, SSD_HEADS), f32, math.log(1e-3), math.log(1e-1)))
    return {
        'x': x,
        'positions': positions,
        'ln1_w': 1.0 + nrm(ks[2], (DEPTH, D_MODEL), 0.02),
        'w_in': nrm(ks[3], (DEPTH, D_MODEL, IN_COLS), D_MODEL ** -0.5),
        'conv_w': nrm(ks[4], (DEPTH, SSD_CONV, SSD_CONV_DIM), SSD_CONV ** -0.5),
        'conv_b': nrm(ks[5], (DEPTH, SSD_CONV_DIM), 0.02),
        'dt_bias': dt0 + jnp.log(-jnp.expm1(-dt0)),
        'a_log': jnp.log(jax.random.uniform(ks[7], (DEPTH, SSD_HEADS), f32, 1.0, 16.0)),
        'd_skip': 1.0 + nrm(ks[8], (DEPTH, SSD_HEADS), 0.02),
        'ssd_norm_w': 1.0 + nrm(ks[9], (DEPTH, SSD_WIDTH), 0.02),
        'q_norm_w': 1.0 + nrm(ks[10], (DEPTH, ATTN_QK_DIM), 0.02),
        'k_norm_w': 1.0 + nrm(ks[11], (DEPTH, ATTN_QK_DIM), 0.02),
        'lambda_q1': nrm(ks[12], (DEPTH, ATTN_QK_DIM), 0.1),
        'lambda_k1': nrm(ks[13], (DEPTH, ATTN_QK_DIM), 0.1),
        'lambda_q2': nrm(ks[14], (DEPTH, ATTN_QK_DIM), 0.1),
        'lambda_k2': nrm(ks[15], (DEPTH, ATTN_QK_DIM), 0.1),
        'subln_w': 1.0 + nrm(ks[16], (DEPTH, ATTN_V_DIM), 0.02),
        'w_out': nrm(ks[17], (DEPTH, MIX_WIDTH, D_MODEL), MIX_WIDTH ** -0.5),
        'ln2_w': 1.0 + nrm(ks[18], (DEPTH, D_MODEL), 0.02),
        'w_router_group': nrm(ks[19], (DEPTH, D_MODEL, N_EXPERT_GROUPS), D_MODEL ** -0.5),
        'b_router_group': nrm(ks[20], (DEPTH, N_EXPERT_GROUPS), 0.01),
        'w_router_expert': nrm(ks[21], (DEPTH, D_MODEL, N_EXPERTS), D_MODEL ** -0.5),
        'b_router_expert': nrm(ks[22], (DEPTH, N_EXPERTS), 0.01),
        'w_gate': nrm(ks[23], (DEPTH, N_EXPERTS, D_MODEL, EXPERT_HIDDEN), D_MODEL ** -0.5),
        'w_up': nrm(ks[24], (DEPTH, N_EXPERTS, D_MODEL, EXPERT_HIDDEN), D_MODEL ** -0.5),
        'w_down': nrm(ks[25], (DEPTH, N_EXPERTS, EXPERT_HIDDEN, D_MODEL), EXPERT_HIDDEN ** -0.5),
    }


def reference(x, positions, ln1_w, w_in, conv_w, conv_b, dt_bias, a_log, d_skip, ssd_norm_w,
              q_norm_w, k_norm_w, lambda_q1, lambda_k1, lambda_q2, lambda_k2, subln_w, w_out,
              ln2_w, w_router_group, b_router_group, w_router_expert, b_router_expert,
              w_gate, w_up, w_down):
    splits = np.cumsum([SSD_WIDTH, SSD_CONV_DIM, SSD_HEADS, ATTN_WIDTH, ATTN_WIDTH]).tolist()
    for l in range(DEPTH):
        u = rms_norm(x, ln1_w[l]) @ w_in[l]
        z, xbc, dt_raw, q, k, v = jnp.split(u, splits, axis=-1)
        y_ssd = ssd_branch(z, xbc, dt_raw, conv_w[l], conv_b[l], dt_bias[l], a_log[l], d_skip[l], ssd_norm_w[l])
        y_att = diff_attention_branch(q, k, v, positions, q_norm_w[l], k_norm_w[l], lambda_q1[l], lambda_k1[l],
                                      lambda_q2[l], lambda_k2[l], subln_w[l], lambda_init_fn(l))
        h = x + jnp.concatenate([y_ssd, y_att], axis=-1) @ w_out[l]
        x = h + hierarchical_moe(rms_norm(h, ln2_w[l]), w_router_group[l], b_router_group[l],
                                 w_router_expert[l], b_router_expert[l], w_gate[l], w_up[l], w_down[l])
    return x
```

```python
import functools
import math

import jax
import jax.numpy as jnp
from jax import lax
from jax.experimental import pallas as pl
from jax.experimental.pallas import tpu as pltpu

F32 = jnp.float32
BF16 = jnp.bfloat16
I32 = jnp.int32
U32 = jnp.uint32
HIGHEST = lax.Precision.HIGHEST

D_MODEL = 2048
SSD_WIDTH = 1024
ATTN_WIDTH = 1024
SSD_HEAD_DIM = 64
SSD_HEADS = 16
SSD_GROUPS = 2
SSD_HEADS_PER_GROUP = SSD_HEADS // SSD_GROUPS
SSD_STATE = 128
SSD_CONV = 4
SSD_CHUNK = 128
SSD_CONV_DIM = SSD_WIDTH + 2 * SSD_GROUPS * SSD_STATE
ATTN_V_DIM = 128
ATTN_HEADS = 8
ATTN_QK_DIM = 64
ROPE_THETA = 500000.0
ROPE_DIM = 16
N_EXPERT_GROUPS = 4
EXPERTS_PER_GROUP = 8
N_EXPERTS = 32
TOP_K = 2
EXPERT_HIDDEN = 1024
EPS = 1e-6

LANES = 128
NEG_INF = float("-inf")

TM_IN = 512
TN_IN = 512
TM_QK = 512
TQ = 256
TM_OUT = 256
TM_RANK = 512
TM_EXP = 256
TM_DISP = 256
TM_COMB = 256
U_COLS = SSD_WIDTH + 2 * ATTN_WIDTH + SSD_CONV_DIM
VMEM_LIMIT = 52 * 1024 * 1024


def _cparams(sem):
    return pltpu.CompilerParams(dimension_semantics=sem, vmem_limit_bytes=VMEM_LIMIT)


def _silu(x):
    return x * (1.0 / (1.0 + jnp.exp(-x)))


def _softplus(x):
    return jnp.maximum(x, 0.0) + jnp.log(1.0 + jnp.exp(-jnp.abs(x)))


def _inproj_kernel(x_ref, lnw_ref, w_ref, wvt_ref, wdt_ref, u_ref, vt_ref, dt_ref, xn_ref):
    @pl.when(pl.program_id(1) == 0)
    def _():
        x = x_ref[...]
        ms = jnp.mean(x * x, axis=-1, keepdims=True)
        xn = (x * lax.rsqrt(ms + EPS) * lnw_ref[...]).astype(BF16)
        xn_ref[...] = xn
        dt_ref[...] = jnp.dot(xn, wdt_ref[...], preferred_element_type=F32)
        vt_ref[...] = lax.dot_general(wvt_ref[...], xn, (((1,), (1,)), ((), ())),
                                      preferred_element_type=F32).astype(BF16)

    u_ref[...] = jnp.dot(xn_ref[...], w_ref[...], preferred_element_type=F32).astype(BF16)


def _in_proj(x2, ln_w, w_main, w_vt, w_dt):
    n = x2.shape[0]
    return pl.pallas_call(
        _inproj_kernel,
        grid=(n // TM_IN, U_COLS // TN_IN),
        in_specs=[
            pl.BlockSpec((TM_IN, D_MODEL), lambda i, j: (i, 0)),
            pl.BlockSpec((1, D_MODEL), lambda i, j: (0, 0)),
            pl.BlockSpec((D_MODEL, TN_IN), lambda i, j: (0, j)),
            pl.BlockSpec((ATTN_WIDTH, D_MODEL), lambda i, j: (0, 0)),
            pl.BlockSpec((D_MODEL, LANES), lambda i, j: (0, 0)),
        ],
        out_specs=[
            pl.BlockSpec((TM_IN, TN_IN), lambda i, j: (i, j)),
            pl.BlockSpec((ATTN_WIDTH, TM_IN), lambda i, j: (0, i)),
            pl.BlockSpec((TM_IN, LANES), lambda i, j: (i, 0)),
        ],
        out_shape=[
            jax.ShapeDtypeStruct((n, U_COLS), BF16),
            jax.ShapeDtypeStruct((ATTN_WIDTH, n), BF16),
            jax.ShapeDtypeStruct((n, LANES), F32),
        ],
        scratch_shapes=[pltpu.VMEM((TM_IN, D_MODEL), BF16)],
        compiler_params=_cparams(("parallel", "arbitrary")),
        name="in_proj",
    )(x2, ln_w, w_main, w_vt, w_dt)


def _ssd_kernel(z_ref, xbc_ref, dt_ref, convw_ref, convb_ref, dtb_ref, aneg_ref, dskip_ref, normw_ref,
                y_ref, xp_ref, st_ref, yacc_ref):
    L = SSD_CHUNK
    P = SSD_HEAD_DIM

    @pl.when(pl.program_id(1) == 0)
    def _():
        xp_ref[0:8, :] = jnp.zeros((8, SSD_CONV_DIM), F32)
        st_ref[...] = jnp.zeros_like(st_ref)

    xp_ref[8:8 + L, :] = xbc_ref[...].astype(F32)
    acc = jnp.broadcast_to(convb_ref[...], (L, SSD_CONV_DIM))
    for k in range(SSD_CONV):
        acc = acc + xp_ref[5 + k:5 + k + L, :] * convw_ref[k:k + 1, :]
    xp_ref[0:8, :] = xp_ref[L:L + 8, :]
    xc = _silu(acc)

    dt = _softplus(dt_ref[...] + dtb_ref[...])
    a = dt * aneg_ref[...]
    row = lax.broadcasted_iota(I32, (L, L), 0)
    col = lax.broadcasted_iota(I32, (L, L), 1)
    causal = row >= col
    a_cs = jnp.dot(causal.astype(F32), a, precision=HIGHEST, preferred_element_type=F32)
    a_last = a_cs[L - 1:L, :]
    ea = jnp.exp(a_cs)
    dsdt = jnp.exp(a_last - a_cs) * dt
    cd = jnp.exp(a_last)
    a_cs_t = a_cs.T
    dt_t = dt.T
    dsdt_t = dsdt.T

    for g in range(SSD_GROUPS):
        b_g = xc[:, SSD_WIDTH + g * SSD_STATE:SSD_WIDTH + (g + 1) * SSD_STATE]
        c_off = SSD_WIDTH + SSD_GROUPS * SSD_STATE
        c_g = xc[:, c_off + g * SSD_STATE:c_off + (g + 1) * SSD_STATE]
        cb = lax.dot_general(c_g.astype(BF16), b_g.astype(BF16), (((1,), (1,)), ((), ())),
                             preferred_element_type=F32)
        b_gt = b_g.T
        for hh in range(SSD_HEADS_PER_GROUP):
            h = g * SSD_HEADS_PER_GROUP + hh
            xs_h = xc[:, h * P:(h + 1) * P].astype(BF16)
            seg = a_cs[:, h:h + 1] - a_cs_t[h:h + 1, :]
            dec = jnp.exp(jnp.where(causal, seg, NEG_INF))
            m = (cb * dec * dt_t[h:h + 1, :]).astype(BF16)
            c_s = (c_g * ea[:, h:h + 1]).astype(BF16)
            s_prev = st_ref[h]
            lhs = jnp.concatenate([m, c_s], axis=1)
            rhs = jnp.concatenate([xs_h, s_prev.astype(BF16)], axis=0)
            yacc_ref[:, h * P:(h + 1) * P] = jnp.dot(lhs, rhs, preferred_element_type=F32)
            bw = (b_gt * dsdt_t[h:h + 1, :]).astype(BF16)
            st_ref[h] = s_prev * cd[:, h:h + 1] + jnp.dot(bw, xs_h, preferred_element_type=F32)

    y = yacc_ref[...] + xc[:, :SSD_WIDTH] * dskip_ref[...]
    y = y * _silu(z_ref[...].astype(F32))
    gw = SSD_WIDTH // SSD_GROUPS
    for g in range(SSD_GROUPS):
        yg = y[:, g * gw:(g + 1) * gw]
        ms = jnp.mean(yg * yg, axis=-1, keepdims=True)
        y_ref[:, g * gw:(g + 1) * gw] = (yg * lax.rsqrt(ms + EPS) * normw_ref[:, g * gw:(g + 1) * gw]).astype(BF16)


def _ssd(u, dt_raw, conv_w, conv_b, dt_bias, a_neg, dskip_lanes, norm_w, bsz, seq):
    n = u.shape[0]
    nc = seq // SSD_CHUNK
    xbc_blk = (SSD_WIDTH + 2 * ATTN_WIDTH) // SSD_CONV_DIM
    full = lambda shape: pl.BlockSpec(shape, lambda b, c: (0, 0))
    return pl.pallas_call(
        _ssd_kernel,
        grid=(bsz, nc),
        in_specs=[
            pl.BlockSpec((SSD_CHUNK, SSD_WIDTH), lambda b, c: (b * nc + c, 0)),
            pl.BlockSpec((SSD_CHUNK, SSD_CONV_DIM), lambda b, c: (b * nc + c, xbc_blk)),
            pl.BlockSpec((SSD_CHUNK, LANES), lambda b, c: (b * nc + c, 0)),
            full((SSD_CONV, SSD_CONV_DIM)),
            full((1, SSD_CONV_DIM)),
            full((1, LANES)),
            full((1, LANES)),
            full((1, SSD_WIDTH)),
            full((1, SSD_WIDTH)),
        ],
        out_specs=pl.BlockSpec((SSD_CHUNK, SSD_WIDTH), lambda b, c: (b * nc + c, 0)),
        out_shape=jax.ShapeDtypeStruct((n, SSD_WIDTH), BF16),
        scratch_shapes=[
            pltpu.VMEM((SSD_CHUNK + 8, SSD_CONV_DIM), F32),
            pltpu.VMEM((SSD_HEADS, SSD_STATE, SSD_HEAD_DIM), F32),
            pltpu.VMEM((SSD_CHUNK, SSD_WIDTH), F32),
        ],
        compiler_params=_cparams(("parallel", "arbitrary")),
        name="ssd",
    )(u, u, dt_raw, conv_w, conv_b, dt_bias, a_neg, dskip_lanes, norm_w)


def _qkprep_kernel(q_ref, k_ref, pos_ref, invf_ref, qw_ref, kw_ref, ones_ref, qo_ref, ko_ref):
    tm = q_ref.shape[0]
    ang = pos_ref[...] * invf_ref[...]
    cs = jnp.cos(ang)
    sn = jnp.sin(ang)
    d = lax.broadcasted_iota(I32, (tm, LANES), 1) & (ATTN_QK_DIM - 1)
    half = ROPE_DIM // 2
    s_lo = jnp.where(d < half, -sn, 0.0)
    s_hi = jnp.where((d >= half) & (d < ROPE_DIM), sn, 0.0)
    scale = 1.0 / math.sqrt(ATTN_QK_DIM)
    for src, w_ref, dst, mul in ((q_ref, qw_ref, qo_ref, scale), (k_ref, kw_ref, ko_ref, 1.0)):
        for hb in range(ATTN_HEADS):
            x = src[:, hb * LANES:(hb + 1) * LANES].astype(F32)
            ss = jnp.dot((x * x).astype(BF16), ones_ref[...], preferred_element_type=F32)
            xn = x * lax.rsqrt(ss * (1.0 / ATTN_QK_DIM) + EPS) * w_ref[...]
            out = xn * cs + pltpu.roll(xn, LANES - half, 1) * s_lo + pltpu.roll(xn, half, 1) * s_hi
            dst[:, hb * LANES:(hb + 1) * LANES] = (out * mul).astype(BF16)


def _qk_prep(u, pos_col, invf_lanes, qw_lanes, kw_lanes, seg_ones):
    n = u.shape[0]
    full = lambda shape: pl.BlockSpec(shape, lambda i: (0, 0))
    return pl.pallas_call(
        _qkprep_kernel,
        grid=(n // TM_QK,),
        in_specs=[
            pl.BlockSpec((TM_QK, ATTN_WIDTH), lambda i: (i, 1)),
            pl.BlockSpec((TM_QK, ATTN_WIDTH), lambda i: (i, 2)),
            pl.BlockSpec((TM_QK, 1), lambda i: (i, 0)),
            full((1, LANES)), full((1, LANES)), full((1, LANES)), full((LANES, LANES)),
        ],
        out_specs=[pl.BlockSpec((TM_QK, ATTN_WIDTH), lambda i: (i, 0))] * 2,
        out_shape=[jax.ShapeDtypeStruct((n, ATTN_WIDTH), BF16)] * 2,
        compiler_params=_cparams(("parallel",)),
        name="qk_prep",
    )(u, u, pos_col, invf_lanes, qw_lanes, kw_lanes, seg_ones)


def _attn_kernel(q_ref, k_ref, vt_ref, lamv_ref, subw_ref, o_ref, acc_ref, *, lam_init):
    qi = pl.program_id(2)
    q = q_ref[...]
    lane = lax.broadcasted_iota(I32, (TQ, LANES), 1)
    zero = jnp.zeros_like(q)
    qc = (jnp.where(lane < ATTN_QK_DIM, q, zero), jnp.where(lane >= ATTN_QK_DIM, q, zero))
    acc_ref[...] = jnp.zeros_like(acc_ref)
    kv_idx = lax.broadcasted_iota(I32, (TQ, TQ), 0)
    q_idx = lax.broadcasted_iota(I32, (TQ, TQ), 1)
    nt = (((1,), (1,)), ((), ()))

    def block(j, carry, masked):
        off = pl.multiple_of(j * TQ, TQ)
        kb = k_ref[pl.ds(off, TQ), :]
        vb = vt_ref[:, pl.ds(off, TQ)]
        new = []
        for c in range(2):
            m_old, l_old = carry[2 * c], carry[2 * c + 1]
            s = lax.dot_general(kb, qc[c], nt, preferred_element_type=F32)
            if masked:
                s = jnp.where(kv_idx <= q_idx, s, NEG_INF)
            m_new = jnp.maximum(m_old, jnp.max(s, axis=0, keepdims=True))
            alpha = jnp.exp(m_old - m_new)
            p = jnp.exp(s - m_new)
            l_new = alpha * l_old + jnp.sum(p, axis=0, keepdims=True)
            acc_ref[c] = alpha * acc_ref[c] + jnp.dot(vb, p.astype(BF16), preferred_element_type=F32)
            new += [m_new, l_new]
        return tuple(new)

    init = (jnp.full((1, TQ), NEG_INF, F32), jnp.zeros((1, TQ), F32)) * 2
    carry = lax.fori_loop(0, qi, lambda j, cr: block(j, cr, False), init)
    _, l0, _, l1 = block(qi, carry, True)

    lv = lamv_ref[...]
    lam = (jnp.exp(jnp.sum(lv[0:1] * lv[1:2], axis=1, keepdims=True))
           - jnp.exp(jnp.sum(lv[2:3] * lv[3:4], axis=1, keepdims=True)) + lam_init)
    o_t = acc_ref[0] * (1.0 / l0) - lam * (acc_ref[1] * (1.0 / l1))
    ms = jnp.mean(o_t * o_t, axis=0, keepdims=True)
    o_t = o_t * lax.rsqrt(ms + EPS) * subw_ref[...] * (1.0 - lam_init)
    o_ref[...] = o_t.T.astype(BF16)


def _attention(qp, kp, v_t, lam_vecs, subw_col, bsz, seq, lam_init):
    n = qp.shape[0]
    nq = seq // TQ
    return pl.pallas_call(
        functools.partial(_attn_kernel, lam_init=lam_init),
        grid=(bsz, ATTN_HEADS, nq),
        in_specs=[
            pl.BlockSpec((TQ, ATTN_V_DIM), lambda b, h, i: (b * nq + i, h)),
            pl.BlockSpec((seq, ATTN_V_DIM), lambda b, h, i: (b, h)),
            pl.BlockSpec((ATTN_V_DIM, seq), lambda b, h, i: (h, b)),
            pl.BlockSpec((4, ATTN_QK_DIM), lambda b, h, i: (0, 0)),
            pl.BlockSpec((ATTN_V_DIM, 1), lambda b, h, i: (0, 0)),
        ],
        out_specs=pl.BlockSpec((TQ, ATTN_V_DIM), lambda b, h, i: (b * nq + i, h)),
        out_shape=jax.ShapeDtypeStruct((n, ATTN_WIDTH), BF16),
        scratch_shapes=[pltpu.VMEM((2, ATTN_V_DIM, TQ), F32)],
        compiler_params=_cparams(("parallel", "parallel", "arbitrary")),
        name="attn",
    )(qp, kp, v_t, lam_vecs, subw_col)


def _pack_halves(x):
    c = x.shape[1] // 2
    lo = pltpu.bitcast(x[:, :c].astype(BF16).astype(F32), U32) >> 16
    hi = pltpu.bitcast(x[:, c:].astype(BF16).astype(F32), U32) & jnp.uint32(0xFFFF0000)
    return hi | lo


def _unpack_halves(w):
    lo = pltpu.bitcast(w << 16, F32).astype(BF16)
    hi = pltpu.bitcast(w & jnp.uint32(0xFFFF0000), F32).astype(BF16)
    return lo, hi


def _outproj_kernel(x_ref, ys_ref, ya_ref, wos_ref, woa_ref, ln2_ref, wr_ref, br_ref,
                    h_ref, hnp_ref, meta_ref):
    tm = x_ref.shape[0]
    h = (x_ref[...]
         + jnp.dot(ys_ref[...], wos_ref[...], preferred_element_type=F32)
         + jnp.dot(ya_ref[...], woa_ref[...], preferred_element_type=F32))
    h_ref[...] = h
    ms = jnp.mean(h * h, axis=-1, keepdims=True)
    hn = h * lax.rsqrt(ms + EPS) * ln2_ref[...]
    hnp_ref[...] = _pack_halves(hn)

    lg = jnp.dot(hn, wr_ref[...], precision=HIGHEST, preferred_element_type=F32) + br_ref[...]
    lane = lax.broadcasted_iota(I32, (tm, LANES), 1).astype(F32)
    big = float(LANES)
    gmask = lane < N_EXPERT_GROUPS
    gl = jnp.where(gmask, lg, NEG_INF)
    gmax = jnp.max(gl, axis=1, keepdims=True)
    gsel = jnp.min(jnp.where(gl == gmax, lane, big), axis=1, keepdims=True)
    g_w = 1.0 / jnp.sum(jnp.exp(gl - gmax), axis=1, keepdims=True)
    eid = lane - N_EXPERT_GROUPS
    lo = gsel * EXPERTS_PER_GROUP
    emask = (eid >= lo) & (eid < lo + EXPERTS_PER_GROUP)
    el = jnp.where(emask, lg, NEG_INF)
    m1 = jnp.max(el, axis=1, keepdims=True)
    i1 = jnp.min(jnp.where(el == m1, eid, big), axis=1, keepdims=True)
    el2 = jnp.where(eid == i1, NEG_INF, el)
    m2 = jnp.max(el2, axis=1, keepdims=True)
    i2 = jnp.min(jnp.where(el2 == m2, eid, big), axis=1, keepdims=True)
    e2 = jnp.exp(m2 - m1)
    w1 = g_w / (1.0 + e2)
    w2 = g_w * e2 / (1.0 + e2)
    meta = jnp.where(lane == 0, i1, jnp.where(lane == 1, i2, jnp.where(lane == 2, w1, jnp.where(lane == 3, w2, 0.0))))
    meta_ref[...] = meta


def _out_proj(x2, y_ssd, y_att, wo_s, wo_a, ln2_w, w_router, b_router):
    n = x2.shape[0]
    full = lambda shape: pl.BlockSpec(shape, lambda i: (0, 0))
    return pl.pallas_call(
        _outproj_kernel,
        grid=(n // TM_OUT,),
        in_specs=[
            pl.BlockSpec((TM_OUT, D_MODEL), lambda i: (i, 0)),
            pl.BlockSpec((TM_OUT, SSD_WIDTH), lambda i: (i, 0)),
            pl.BlockSpec((TM_OUT, ATTN_WIDTH), lambda i: (i, 0)),
            full((SSD_WIDTH, D_MODEL)), full((ATTN_WIDTH, D_MODEL)),
            full((1, D_MODEL)), full((D_MODEL, LANES)), full((1, LANES)),
        ],
        out_specs=[
            pl.BlockSpec((TM_OUT, D_MODEL), lambda i: (i, 0)),
            pl.BlockSpec((TM_OUT, D_MODEL // 2), lambda i: (i, 0)),
            pl.BlockSpec((TM_OUT, LANES), lambda i: (i, 0)),
        ],
        out_shape=[
            jax.ShapeDtypeStruct((n, D_MODEL), F32),
            jax.ShapeDtypeStruct((n, D_MODEL // 2), U32),
            jax.ShapeDtypeStruct((n, LANES), F32),
        ],
        compiler_params=_cparams(("parallel",)),
        name="out_proj",
    )(x2, y_ssd, y_att, wo_s, wo_a, ln2_w, w_router, b_router)


def _rank_kernel(meta_ref, dest_ref, cnt_ref, run_ref, offs_ref):
    p = pl.program_id(0)
    i = pl.program_id(1)
    tm = meta_ref.shape[0]
    meta = meta_ref[...]
    lane = lax.broadcasted_iota(I32, (tm, LANES), 1).astype(F32)
    oh0 = (lane == meta[:, 0:1]).astype(F32)
    oh1 = (lane == meta[:, 1:2]).astype(F32)
    oh = oh0 + oh1
    colsum = jnp.sum(oh, axis=0, keepdims=True)

    @pl.when((p == 0) & (i == 0))
    def _():
        run_ref[...] = jnp.zeros_like(run_ref)

    @pl.when(p == 0)
    def _():
        run_ref[...] = run_ref[...] + colsum
        dest_ref[...] = jnp.zeros_like(dest_ref)
        cnt_ref[...] = jnp.broadcast_to(run_ref[...], cnt_ref.shape)

    @pl.when((p == 1) & (i == 0))
    def _():
        cnt = run_ref[...]
        padded = jnp.ceil(cnt * (1.0 / TM_EXP)) * TM_EXP
        r = lax.broadcasted_iota(I32, (LANES, LANES), 0)
        c = lax.broadcasted_iota(I32, (LANES, LANES), 1)
        excl = (r < c).astype(F32)
        offs = jnp.dot(jnp.broadcast_to(padded, (8, LANES)), excl, precision=HIGHEST, preferred_element_type=F32)
        offs_ref[...] = offs[0:1, :]
        cnt_ref[...] = jnp.broadcast_to(cnt, cnt_ref.shape)
        run_ref[...] = jnp.zeros_like(run_ref)

    @pl.when(p == 1)
    def _():
        r = lax.broadcasted_iota(I32, (tm, tm), 0)
        c = lax.broadcasted_iota(I32, (tm, tm), 1)
        before = jnp.dot((c < r).astype(BF16), oh.astype(BF16), preferred_element_type=F32)
        base = before + run_ref[...] + offs_ref[...]
        d0 = jnp.sum(oh0 * base, axis=1, keepdims=True)
        d1 = jnp.sum(oh1 * base, axis=1, keepdims=True)
        dest = jnp.where(lane == 0, d0, jnp.where(lane == 1, d1, 0.0))
        dest_ref[...] = dest.astype(I32)
        run_ref[...] = run_ref[...] + colsum


def _rank(meta):
    n = meta.shape[0]
    return pl.pallas_call(
        _rank_kernel,
        grid=(2, n // TM_RANK),
        in_specs=[pl.BlockSpec((TM_RANK, LANES), lambda p, i: (i, 0))],
        out_specs=[
            pl.BlockSpec((TM_RANK, LANES), lambda p, i: (i * p, 0)),
            pl.BlockSpec((8, LANES), lambda p, i: (0, 0)),
        ],
        out_shape=[
            jax.ShapeDtypeStruct((n, LANES), I32),
            jax.ShapeDtypeStruct((8, LANES), F32),
        ],
        scratch_shapes=[pltpu.VMEM((1, LANES), F32), pltpu.VMEM((1, LANES), F32)],
        compiler_params=_cparams(("arbitrary", "arbitrary")),
        name="rank",
    )(meta)


def _row_copy(src_ref, src_row, dst_ref, dst_row, sem):
    return pltpu.make_async_copy(src_ref.at[pl.ds(src_row, 1)], dst_ref.at[pl.ds(dst_row, 1)], sem)


def _dispatch_kernel(d0_ref, d1_ref, hnp_ref, xs_in_ref, xs_ref, sem):
    del xs_in_ref
    tm = hnp_ref.shape[0]
    base = pl.program_id(0) * tm

    def start(r, _):
        _row_copy(hnp_ref, r, xs_ref, d0_ref[base + r], sem).start()
        _row_copy(hnp_ref, r, xs_ref, d1_ref[base + r], sem).start()
        return 0

    lax.fori_loop(0, tm, start, 0)

    def wait(r, _):
        _row_copy(hnp_ref, r, xs_ref, d0_ref[base + r], sem).wait()
        _row_copy(hnp_ref, r, xs_ref, d1_ref[base + r], sem).wait()
        return 0

    lax.fori_loop(0, tm, wait, 0)


def _dispatch(dest0, dest1, hnp, xs_zero):
    n = hnp.shape[0]
    grid_spec = pltpu.PrefetchScalarGridSpec(
        num_scalar_prefetch=2,
        grid=(n // TM_DISP,),
        in_specs=[
            pl.BlockSpec((TM_DISP, D_MODEL // 2), lambda i, d0, d1: (i, 0)),
            pl.BlockSpec(memory_space=pl.ANY),
        ],
        out_specs=pl.BlockSpec(memory_space=pl.ANY),
        scratch_shapes=[pltpu.SemaphoreType.DMA(())],
    )
    return pl.pallas_call(
        _dispatch_kernel,
        grid_spec=grid_spec,
        out_shape=jax.ShapeDtypeStruct(xs_zero.shape, xs_zero.dtype),
        input_output_aliases={3: 0},
        compiler_params=_cparams(("arbitrary",)),
        name="dispatch",
    )(dest0, dest1, hnp, xs_zero)


def _experts_kernel(te_ref, nu_ref, xs_ref, wg_ref, wu_ref, wd_ref, ys_ref):
    t = pl.program_id(0)

    @pl.when(t < nu_ref[0])
    def _():
        c = D_MODEL // 2
        x_lo, x_hi = _unpack_halves(xs_ref[...])
        gate = (jnp.dot(x_lo, wg_ref[:c, :], preferred_element_type=F32)
                + jnp.dot(x_hi, wg_ref[c:, :], preferred_element_type=F32))
        up = (jnp.dot(x_lo, wu_ref[:c, :], preferred_element_type=F32)
              + jnp.dot(x_hi, wu_ref[c:, :], preferred_element_type=F32))
        hid = (_silu(gate) * up).astype(BF16)
        ys_ref[...] = jnp.dot(hid, wd_ref[...], preferred_element_type=F32)

    @pl.when(t >= nu_ref[0])
    def _():
        ys_ref[...] = jnp.zeros_like(ys_ref)


def _experts(tile_expert, n_used, xs, wg, wu, wd):
    n_slots = xs.shape[0]
    grid_spec = pltpu.PrefetchScalarGridSpec(
        num_scalar_prefetch=2,
        grid=(n_slots // TM_EXP,),
        in_specs=[
            pl.BlockSpec((TM_EXP, D_MODEL // 2), lambda t, te, nu: (t, 0)),
            pl.BlockSpec((None, D_MODEL, EXPERT_HIDDEN), lambda t, te, nu: (te[t], 0, 0)),
            pl.BlockSpec((None, D_MODEL, EXPERT_HIDDEN), lambda t, te, nu: (te[t], 0, 0)),
            pl.BlockSpec((None, EXPERT_HIDDEN, D_MODEL), lambda t, te, nu: (te[t], 0, 0)),
        ],
        out_specs=pl.BlockSpec((TM_EXP, D_MODEL), lambda t, te, nu: (t, 0)),
    )
    return pl.pallas_call(
        _experts_kernel,
        grid_spec=grid_spec,
        out_shape=jax.ShapeDtypeStruct((n_slots, D_MODEL), F32),
        compiler_params=_cparams(("arbitrary",)),
        name="experts",
    )(tile_expert, n_used, xs, wg, wu, wd)


def _combine_kernel(d0_ref, d1_ref, h_ref, meta_ref, ys_ref, o_ref, buf_ref, sem):
    tm = h_ref.shape[0]
    base = pl.program_id(0) * tm

    def start(r, _):
        _row_copy(ys_ref, d0_ref[base + r], buf_ref.at[0], r, sem).start()
        _row_copy(ys_ref, d1_ref[base + r], buf_ref.at[1], r, sem).start()
        return 0

    lax.fori_loop(0, tm, start, 0)

    def wait(r, _):
        _row_copy(ys_ref, d0_ref[base + r], buf_ref.at[0], r, sem).wait()
        _row_copy(ys_ref, d1_ref[base + r], buf_ref.at[1], r, sem).wait()
        return 0

    lax.fori_loop(0, tm, wait, 0)
    meta = meta_ref[...]
    o_ref[...] = h_ref[...] + (buf_ref[0] * meta[:, 2:3] + buf_ref[1] * meta[:, 3:4])


def _combine(dest0, dest1, h, meta, ys):
    n = h.shape[0]
    grid_spec = pltpu.PrefetchScalarGridSpec(
        num_scalar_prefetch=2,
        grid=(n // TM_COMB,),
        in_specs=[
            pl.BlockSpec((TM_COMB, D_MODEL), lambda i, d0, d1: (i, 0)),
            pl.BlockSpec((TM_COMB, LANES), lambda i, d0, d1: (i, 0)),
            pl.BlockSpec(memory_space=pl.ANY),
        ],
        out_specs=pl.BlockSpec((TM_COMB, D_MODEL), lambda i, d0, d1: (i, 0)),
        scratch_shapes=[pltpu.VMEM((2, TM_COMB, D_MODEL), F32), pltpu.SemaphoreType.DMA(())],
    )
    return pl.pallas_call(
        _combine_kernel,
        grid_spec=grid_spec,
        out_shape=jax.ShapeDtypeStruct((n, D_MODEL), F32),
        compiler_params=_cparams(("arbitrary",)),
        name="combine",
    )(dest0, dest1, h, meta, ys)


def _lambda_init(layer_idx):
    return 0.8 - 0.6 * math.exp(-0.3 * layer_idx)


def _pad_lanes(v, width=LANES):
    return jnp.pad(v, ((0, 0), (0, width - v.shape[1])))


def _layer(l, x2, pos_col, bsz, seq, ln1_w, w_in, conv_w, conv_b, dt_bias, a_log, d_skip, ssd_norm_w,
           q_norm_w, k_norm_w, lambda_q1, lambda_k1, lambda_q2, lambda_k2, subln_w, w_out, ln2_w,
           w_router_group, b_router_group, w_router_expert, b_router_expert, w_gate, w_up, w_down):
    n = x2.shape[0]
    c_z, c_xbc, c_dt = SSD_WIDTH, SSD_WIDTH + SSD_CONV_DIM, SSD_WIDTH + SSD_CONV_DIM + SSD_HEADS
    c_q, c_k = c_dt + ATTN_WIDTH, c_dt + 2 * ATTN_WIDTH
    w_main = jnp.concatenate([w_in[:, :c_z], w_in[:, c_dt:c_q], w_in[:, c_q:c_k], w_in[:, c_z:c_xbc]],
                             axis=1).astype(BF16)
    w_vt = w_in[:, c_k:].T.astype(BF16)
    w_dt = _pad_lanes(w_in[:, c_xbc:c_dt]).astype(BF16)

    u, v_t, dt_raw = _in_proj(x2, ln1_w[None, :], w_main, w_vt, w_dt)

    a_neg = _pad_lanes(-jnp.exp(a_log.astype(F32))[None, :])
    y_ssd = _ssd(u, dt_raw, conv_w, conv_b[None, :], _pad_lanes(dt_bias[None, :]), a_neg,
                 jnp.repeat(d_skip, SSD_HEAD_DIM)[None, :], ssd_norm_w[None, :], bsz, seq)

    inv_freq = jnp.power(ROPE_THETA, -jnp.arange(0, ROPE_DIM, 2, dtype=F32) / ROPE_DIM)
    d = jnp.arange(LANES) % ATTN_QK_DIM
    invf_lanes = jnp.where(d < ROPE_DIM, inv_freq[d % (ROPE_DIM // 2)], 0.0)[None, :]
    seg_ones = (jnp.arange(LANES)[:, None] // ATTN_QK_DIM == jnp.arange(LANES)[None, :] // ATTN_QK_DIM).astype(BF16)
    qp, kp = _qk_prep(u, pos_col, invf_lanes, jnp.tile(q_norm_w, 2)[None, :], jnp.tile(k_norm_w, 2)[None, :], seg_ones)

    lam_vecs = jnp.stack([lambda_q1, lambda_k1, lambda_q2, lambda_k2]).astype(F32)
    y_att = _attention(qp, kp, v_t, lam_vecs, subln_w[:, None], bsz, seq, _lambda_init(l))

    w_out_b = w_out.astype(BF16)
    w_router = _pad_lanes(jnp.concatenate([w_router_group, w_router_expert], axis=1))
    b_router = _pad_lanes(jnp.concatenate([b_router_group, b_router_expert])[None, :])
    h, hnp, meta = _out_proj(x2, y_ssd, y_att, w_out_b[:SSD_WIDTH], w_out_b[SSD_WIDTH:], ln2_w[None, :],
                             w_router, b_router)

    dest, cnt = _rank(meta)
    dest0, dest1 = dest[:, 0], dest[:, 1]
    counts = cnt[0, :N_EXPERTS].astype(I32)
    tiles_per_expert = (counts + TM_EXP - 1) // TM_EXP
    tile_end = jnp.cumsum(tiles_per_expert)
    n_slots = (n * TOP_K + N_EXPERTS * (TM_EXP - 1)) // TM_EXP * TM_EXP
    tile_ids = jnp.arange(n_slots // TM_EXP, dtype=I32)
    tile_expert = jnp.minimum(jnp.sum(tile_ids[:, None] >= tile_end[None, :], axis=1), N_EXPERTS - 1).astype(I32)
    n_used = tile_end[-1:].astype(I32)

    xs = _dispatch(dest0, dest1, hnp, jnp.zeros((n_slots, D_MODEL // 2), U32))
    ys = _experts(tile_expert, n_used, xs, w_gate.astype(BF16), w_up.astype(BF16), w_down.astype(BF16))
    return _combine(dest0, dest1, h, meta, ys)


def kernel(x, positions, ln1_w, w_in, conv_w, conv_b, dt_bias, a_log, d_skip, ssd_norm_w, q_norm_w, k_norm_w,
           lambda_q1, lambda_k1, lambda_q2, lambda_k2, subln_w, w_out, ln2_w, w_router_group, b_router_group,
           w_router_expert, b_router_expert, w_gate, w_up, w_down):
    bsz, seq, d = x.shape
    assert d == D_MODEL and seq % TQ == 0 and (bsz * seq) % TM_IN == 0
    x2 = x.reshape(bsz * seq, d)
    pos_col = positions.astype(F32).reshape(bsz * seq, 1)
    params = (ln1_w, w_in, conv_w, conv_b, dt_bias, a_log, d_skip, ssd_norm_w, q_norm_w, k_norm_w,
              lambda_q1, lambda_k1, lambda_q2, lambda_k2, subln_w, w_out, ln2_w, w_router_group, b_router_group,
              w_router_expert, b_router_expert, w_gate, w_up, w_down)
    for l in range(ln1_w.shape[0]):
        x2 = _layer(l, x2, pos_col, bsz, seq, *[p[l] for p in params])
    return x2.reshape(bsz, seq, d)
```

```python
import functools
import math

import jax
import jax.numpy as jnp
from jax import lax
from jax.experimental import pallas as pl
from jax.experimental.pallas import tpu as pltpu

F32 = jnp.float32
BF16 = jnp.bfloat16
I32 = jnp.int32
U32 = jnp.uint32
HIGHEST = lax.Precision.HIGHEST

D_MODEL = 2048
SSD_WIDTH = 1024
ATTN_WIDTH = 1024
SSD_HEAD_DIM = 64
SSD_HEADS = 16
SSD_GROUPS = 2
SSD_HEADS_PER_GROUP = SSD_HEADS // SSD_GROUPS
SSD_STATE = 128
SSD_CONV = 4
SSD_CHUNK = 128
SSD_CONV_DIM = SSD_WIDTH + 2 * SSD_GROUPS * SSD_STATE
ATTN_V_DIM = 128
ATTN_HEADS = 8
ATTN_QK_DIM = 64
ROPE_THETA = 500000.0
ROPE_DIM = 16
N_EXPERT_GROUPS = 4
EXPERTS_PER_GROUP = 8
N_EXPERTS = 32
TOP_K = 2
EXPERT_HIDDEN = 1024
EPS = 1e-6

LANES = 128
NEG_INF = float("-inf")

TM_IN = 512
TN_IN = 1536
TM_QK = 512
TQ = 256
ATTN_HB = 8
TM_OUT = 256
TM_RANK = 512
TM_EXP = 256
TM_DISP = 256
TM_COMB = 256
U_COLS = SSD_WIDTH + 2 * ATTN_WIDTH + SSD_CONV_DIM
VMEM_LIMIT = 52 * 1024 * 1024


def _cparams(sem):
    return pltpu.CompilerParams(dimension_semantics=sem, vmem_limit_bytes=VMEM_LIMIT)


def _silu(x):
    return x * (1.0 / (1.0 + jnp.exp(-x)))


def _softplus(x):
    return jnp.maximum(x, 0.0) + jnp.log(1.0 + jnp.exp(-jnp.abs(x)))


def _inproj_kernel(x_ref, lnw_ref, w_ref, wvt_ref, wdt_ref, u_ref, vt_ref, dt_ref, xn_ref):
    @pl.when(pl.program_id(1) == 0)
    def _():
        x = x_ref[...]
        ms = jnp.mean(x * x, axis=-1, keepdims=True)
        xn = (x * lax.rsqrt(ms + EPS) * lnw_ref[...]).astype(BF16)
        xn_ref[...] = xn
        dt_ref[...] = jnp.dot(xn, wdt_ref[...], preferred_element_type=F32)
        vt_ref[...] = lax.dot_general(wvt_ref[...], xn, (((1,), (1,)), ((), ())),
                                      preferred_element_type=F32).astype(BF16)

    u_ref[...] = jnp.dot(xn_ref[...], w_ref[...], preferred_element_type=F32).astype(BF16)


def _in_proj(x2, ln_w, w_main, w_vt, w_dt):
    n = x2.shape[0]
    return pl.pallas_call(
        _inproj_kernel,
        grid=(n // TM_IN, U_COLS // TN_IN),
        in_specs=[
            pl.BlockSpec((TM_IN, D_MODEL), lambda i, j: (i, 0)),
            pl.BlockSpec((1, D_MODEL), lambda i, j: (0, 0)),
            pl.BlockSpec((D_MODEL, TN_IN), lambda i, j: (0, j)),
            pl.BlockSpec((ATTN_WIDTH, D_MODEL), lambda i, j: (0, 0)),
            pl.BlockSpec((D_MODEL, LANES), lambda i, j: (0, 0)),
        ],
        out_specs=[
            pl.BlockSpec((TM_IN, TN_IN), lambda i, j: (i, j)),
            pl.BlockSpec((ATTN_WIDTH, TM_IN), lambda i, j: (0, i)),
            pl.BlockSpec((TM_IN, LANES), lambda i, j: (i, 0)),
        ],
        out_shape=[
            jax.ShapeDtypeStruct((n, U_COLS), BF16),
            jax.ShapeDtypeStruct((ATTN_WIDTH, n), BF16),
            jax.ShapeDtypeStruct((n, LANES), F32),
        ],
        scratch_shapes=[pltpu.VMEM((TM_IN, D_MODEL), BF16)],
        compiler_params=_cparams(("parallel", "arbitrary")),
        name="in_proj",
    )(x2, ln_w, w_main, w_vt, w_dt)


def _ssd_kernel(z_ref, xbc_ref, dt_ref, convw_ref, convb_ref, dtb_ref, aneg_ref, dskip_ref, normw_ref,
                y_ref, xp_ref, st_ref, yacc_ref):
    L = SSD_CHUNK
    P = SSD_HEAD_DIM

    @pl.when(pl.program_id(1) == 0)
    def _():
        xp_ref[0:8, :] = jnp.zeros((8, SSD_CONV_DIM), F32)
        st_ref[...] = jnp.zeros_like(st_ref)

    xp_ref[8:8 + L, :] = xbc_ref[...].astype(F32)
    acc = jnp.broadcast_to(convb_ref[...], (L, SSD_CONV_DIM))
    for k in range(SSD_CONV):
        acc = acc + xp_ref[5 + k:5 + k + L, :] * convw_ref[k:k + 1, :]
    xp_ref[0:8, :] = xp_ref[L:L + 8, :]
    xc = _silu(acc)

    dt = _softplus(dt_ref[...] + dtb_ref[...])
    a = dt * aneg_ref[...]
    row = lax.broadcasted_iota(I32, (L, L), 0)
    col = lax.broadcasted_iota(I32, (L, L), 1)
    causal = row >= col
    a_cs = jnp.dot(causal.astype(F32), a, precision=HIGHEST, preferred_element_type=F32)
    a_last = a_cs[L - 1:L, :]
    ea = jnp.exp(a_cs)
    dsdt = jnp.exp(a_last - a_cs) * dt
    cd = jnp.exp(a_last)
    a_cs_t = a_cs.T
    dt_t = dt.T
    dsdt_t = dsdt.T

    for g in range(SSD_GROUPS):
        b_g = xc[:, SSD_WIDTH + g * SSD_STATE:SSD_WIDTH + (g + 1) * SSD_STATE]
        c_off = SSD_WIDTH + SSD_GROUPS * SSD_STATE
        c_g = xc[:, c_off + g * SSD_STATE:c_off + (g + 1) * SSD_STATE]
        cb = lax.dot_general(c_g.astype(BF16), b_g.astype(BF16), (((1,), (1,)), ((), ())),
                             preferred_element_type=F32)
        b_gt = b_g.T
        for hh in range(SSD_HEADS_PER_GROUP):
            h = g * SSD_HEADS_PER_GROUP + hh
            xs_h = xc[:, h * P:(h + 1) * P].astype(BF16)
            seg = a_cs[:, h:h + 1] - a_cs_t[h:h + 1, :]
            dec = jnp.exp(jnp.where(causal, seg, NEG_INF))
            m = (cb * dec * dt_t[h:h + 1, :]).astype(BF16)
            c_s = (c_g * ea[:, h:h + 1]).astype(BF16)
            s_prev = st_ref[h]
            lhs = jnp.concatenate([m, c_s], axis=1)
            rhs = jnp.concatenate([xs_h, s_prev.astype(BF16)], axis=0)
            yacc_ref[:, h * P:(h + 1) * P] = jnp.dot(lhs, rhs, preferred_element_type=F32)
            bw = (b_gt * dsdt_t[h:h + 1, :]).astype(BF16)
            st_ref[h] = s_prev * cd[:, h:h + 1] + jnp.dot(bw, xs_h, preferred_element_type=F32)

    y = yacc_ref[...] + xc[:, :SSD_WIDTH] * dskip_ref[...]
    y = y * _silu(z_ref[...].astype(F32))
    gw = SSD_WIDTH // SSD_GROUPS
    for g in range(SSD_GROUPS):
        yg = y[:, g * gw:(g + 1) * gw]
        ms = jnp.mean(yg * yg, axis=-1, keepdims=True)
        y_ref[:, g * gw:(g + 1) * gw] = (yg * lax.rsqrt(ms + EPS) * normw_ref[:, g * gw:(g + 1) * gw]).astype(BF16)


def _ssd(u, dt_raw, conv_w, conv_b, dt_bias, a_neg, dskip_lanes, norm_w, bsz, seq):
    n = u.shape[0]
    nc = seq // SSD_CHUNK
    xbc_blk = (SSD_WIDTH + 2 * ATTN_WIDTH) // SSD_CONV_DIM
    full = lambda shape: pl.BlockSpec(shape, lambda b, c: (0, 0))
    return pl.pallas_call(
        _ssd_kernel,
        grid=(bsz, nc),
        in_specs=[
            pl.BlockSpec((SSD_CHUNK, SSD_WIDTH), lambda b, c: (b * nc + c, 0)),
            pl.BlockSpec((SSD_CHUNK, SSD_CONV_DIM), lambda b, c: (b * nc + c, xbc_blk)),
            pl.BlockSpec((SSD_CHUNK, LANES), lambda b, c: (b * nc + c, 0)),
            full((SSD_CONV, SSD_CONV_DIM)),
            full((1, SSD_CONV_DIM)),
            full((1, LANES)),
            full((1, LANES)),
            full((1, SSD_WIDTH)),
            full((1, SSD_WIDTH)),
        ],
        out_specs=pl.BlockSpec((SSD_CHUNK, SSD_WIDTH), lambda b, c: (b * nc + c, 0)),
        out_shape=jax.ShapeDtypeStruct((n, SSD_WIDTH), BF16),
        scratch_shapes=[
            pltpu.VMEM((SSD_CHUNK + 8, SSD_CONV_DIM), F32),
            pltpu.VMEM((SSD_HEADS, SSD_STATE, SSD_HEAD_DIM), F32),
            pltpu.VMEM((SSD_CHUNK, SSD_WIDTH), F32),
        ],
        compiler_params=_cparams(("parallel", "arbitrary")),
        name="ssd",
    )(u, u, dt_raw, conv_w, conv_b, dt_bias, a_neg, dskip_lanes, norm_w)


def _qkprep_kernel(q_ref, k_ref, pos_ref, invf_ref, qw_ref, kw_ref, ones_ref, qo_ref, ko_ref):
    tm = q_ref.shape[0]
    ang = pos_ref[...] * invf_ref[...]
    cs = jnp.cos(ang)
    sn = jnp.sin(ang)
    d = lax.broadcasted_iota(I32, (tm, LANES), 1) & (ATTN_QK_DIM - 1)
    half = ROPE_DIM // 2
    s_lo = jnp.where(d < half, -sn, 0.0)
    s_hi = jnp.where((d >= half) & (d < ROPE_DIM), sn, 0.0)
    scale = 1.0 / math.sqrt(ATTN_QK_DIM)
    for src, w_ref, dst, mul in ((q_ref, qw_ref, qo_ref, scale), (k_ref, kw_ref, ko_ref, 1.0)):
        for hb in range(ATTN_HEADS):
            x = src[:, hb * LANES:(hb + 1) * LANES].astype(F32)
            ss = jnp.dot((x * x).astype(BF16), ones_ref[...], preferred_element_type=F32)
            xn = x * lax.rsqrt(ss * (1.0 / ATTN_QK_DIM) + EPS) * w_ref[...]
            out = xn * cs + pltpu.roll(xn, LANES - half, 1) * s_lo + pltpu.roll(xn, half, 1) * s_hi
            dst[:, hb * LANES:(hb + 1) * LANES] = (out * mul).astype(BF16)


def _qk_prep(u, pos_col, invf_lanes, qw_lanes, kw_lanes, seg_ones):
    n = u.shape[0]
    full = lambda shape: pl.BlockSpec(shape, lambda i: (0, 0))
    return pl.pallas_call(
        _qkprep_kernel,
        grid=(n // TM_QK,),
        in_specs=[
            pl.BlockSpec((TM_QK, ATTN_WIDTH), lambda i: (i, 1)),
            pl.BlockSpec((TM_QK, ATTN_WIDTH), lambda i: (i, 2)),
            pl.BlockSpec((TM_QK, 1), lambda i: (i, 0)),
            full((1, LANES)), full((1, LANES)), full((1, LANES)), full((LANES, LANES)),
        ],
        out_specs=[pl.BlockSpec((TM_QK, ATTN_WIDTH), lambda i: (i, 0))] * 2,
        out_shape=[jax.ShapeDtypeStruct((n, ATTN_WIDTH), BF16)] * 2,
        compiler_params=_cparams(("parallel",)),
        name="qk_prep",
    )(u, u, pos_col, invf_lanes, qw_lanes, kw_lanes, seg_ones)


def _attn_kernel(q_ref, k_ref, vt_ref, lamv_ref, subw_ref, o_ref, acc_ref, *, lam_init):
    qi = pl.program_id(2)
    lane = lax.broadcasted_iota(I32, (TQ, LANES), 1)
    qs = []
    for hb in range(ATTN_HB):
        q = q_ref[:, hb * LANES:(hb + 1) * LANES]
        zero = jnp.zeros_like(q)
        qs.append(jnp.concatenate([jnp.where(lane < ATTN_QK_DIM, q, zero),
                                   jnp.where(lane >= ATTN_QK_DIM, q, zero)], axis=0))
    acc_ref[...] = jnp.zeros_like(acc_ref)
    kv_idx = lax.broadcasted_iota(I32, (TQ, 2 * TQ), 0)
    q_idx = lax.broadcasted_iota(I32, (TQ, 2 * TQ), 1) & (TQ - 1)
    nt = (((1,), (1,)), ((), ()))

    def block(j, carry, masked):
        off = pl.multiple_of(j * TQ, TQ)
        ss = []
        for hb in range(ATTN_HB):
            kb = k_ref[pl.ds(off, TQ), hb * LANES:(hb + 1) * LANES]
            ss.append(lax.dot_general(kb, qs[hb], nt, preferred_element_type=F32))
        new, ps, alphas = [], [], []
        for hb in range(ATTN_HB):
            m_old, l_old = carry[2 * hb], carry[2 * hb + 1]
            s = ss[hb]
            if masked:
                s = jnp.where(kv_idx <= q_idx, s, NEG_INF)
            m_new = jnp.maximum(m_old, jnp.max(s, axis=0, keepdims=True))
            alpha = jnp.exp(m_old - m_new)
            p = jnp.exp(s - m_new)
            new += [m_new, alpha * l_old + jnp.sum(p, axis=0, keepdims=True)]
            ps.append(p.astype(BF16))
            alphas.append(alpha)
        pvs = []
        for hb in range(ATTN_HB):
            vb = vt_ref[hb * LANES:(hb + 1) * LANES, pl.ds(off, TQ)]
            pvs.append(jnp.dot(vb, ps[hb], preferred_element_type=F32))
        for hb in range(ATTN_HB):
            acc_ref[hb] = alphas[hb] * acc_ref[hb] + pvs[hb]
        return tuple(new)

    init = (jnp.full((1, 2 * TQ), NEG_INF, F32), jnp.zeros((1, 2 * TQ), F32)) * ATTN_HB
    carry = lax.fori_loop(0, qi, lambda j, cr: block(j, cr, False), init)
    carry = block(qi, carry, True)

    lv = lamv_ref[...]
    lam = (jnp.exp(jnp.sum(lv[0:1] * lv[1:2], axis=1, keepdims=True))
           - jnp.exp(jnp.sum(lv[2:3] * lv[3:4], axis=1, keepdims=True)) + lam_init)
    for hb in range(ATTN_HB):
        o2 = acc_ref[hb] * (1.0 / carry[2 * hb + 1])
        o_t = o2[:, :TQ] - lam * o2[:, TQ:]
        ms = jnp.mean(o_t * o_t, axis=0, keepdims=True)
        o_t = o_t * lax.rsqrt(ms + EPS) * subw_ref[...] * (1.0 - lam_init)
        o_ref[:, hb * LANES:(hb + 1) * LANES] = o_t.T.astype(BF16)


def _attention(qp, kp, v_t, lam_vecs, subw_col, bsz, seq, lam_init):
    n = qp.shape[0]
    nq = seq // TQ
    w = ATTN_HB * ATTN_V_DIM
    return pl.pallas_call(
        functools.partial(_attn_kernel, lam_init=lam_init),
        grid=(bsz, ATTN_HEADS // ATTN_HB, nq),
        in_specs=[
            pl.BlockSpec((TQ, w), lambda b, h, i: (b * nq + i, h)),
            pl.BlockSpec((seq, w), lambda b, h, i: (b, h)),
            pl.BlockSpec((w, seq), lambda b, h, i: (h, b)),
            pl.BlockSpec((4, ATTN_QK_DIM), lambda b, h, i: (0, 0)),
            pl.BlockSpec((ATTN_V_DIM, 1), lambda b, h, i: (0, 0)),
        ],
        out_specs=pl.BlockSpec((TQ, w), lambda b, h, i: (b * nq + i, h)),
        out_shape=jax.ShapeDtypeStruct((n, ATTN_WIDTH), BF16),
        scratch_shapes=[pltpu.VMEM((ATTN_HB, ATTN_V_DIM, 2 * TQ), F32)],
        compiler_params=_cparams(("parallel", "parallel", "arbitrary")),
        name="attn",
    )(qp, kp, v_t, lam_vecs, subw_col)


def _pack_halves(x):
    c = x.shape[1] // 2
    lo = pltpu.bitcast(x[:, :c].astype(BF16).astype(F32), U32) >> 16
    hi = pltpu.bitcast(x[:, c:].astype(BF16).astype(F32), U32) & jnp.uint32(0xFFFF0000)
    return hi | lo


def _unpack_halves(w):
    lo = pltpu.bitcast(w << 16, F32).astype(BF16)
    hi = pltpu.bitcast(w & jnp.uint32(0xFFFF0000), F32).astype(BF16)
    return lo, hi


def _outproj_kernel(x_ref, ys_ref, ya_ref, wos_ref, woa_ref, ln2_ref, wrh_ref, wrl_ref, br_ref,
                    h_ref, hnp_ref, meta_ref):
    tm = x_ref.shape[0]
    h = (x_ref[...]
         + jnp.dot(ys_ref[...], wos_ref[...], preferred_element_type=F32)
         + jnp.dot(ya_ref[...], woa_ref[...], preferred_element_type=F32))
    h_ref[...] = h
    ms = jnp.mean(h * h, axis=-1, keepdims=True)
    hn = h * lax.rsqrt(ms + EPS) * ln2_ref[...]
    hnp_ref[...] = _pack_halves(hn)

    hn_hi = hn.astype(BF16)
    hn_lo = (hn - hn_hi.astype(F32)).astype(BF16)
    nt = (((1,), (1,)), ((), ()))
    lg_t = (lax.dot_general(wrh_ref[...], hn_hi, nt, preferred_element_type=F32)
            + lax.dot_general(wrh_ref[...], hn_lo, nt, preferred_element_type=F32)
            + lax.dot_general(wrl_ref[...], hn_hi, nt, preferred_element_type=F32))
    lg = lg_t.T + br_ref[...]
    lane = lax.broadcasted_iota(I32, (tm, LANES), 1).astype(F32)
    big = float(LANES)
    gmask = lane < N_EXPERT_GROUPS
    gl = jnp.where(gmask, lg, NEG_INF)
    gmax = jnp.max(gl, axis=1, keepdims=True)
    gsel = jnp.min(jnp.where(gl == gmax, lane, big), axis=1, keepdims=True)
    g_w = 1.0 / jnp.sum(jnp.exp(gl - gmax), axis=1, keepdims=True)
    eid = lane - N_EXPERT_GROUPS
    lo = gsel * EXPERTS_PER_GROUP
    emask = (eid >= lo) & (eid < lo + EXPERTS_PER_GROUP)
    el = jnp.where(emask, lg, NEG_INF)
    m1 = jnp.max(el, axis=1, keepdims=True)
    i1 = jnp.min(jnp.where(el == m1, eid, big), axis=1, keepdims=True)
    el2 = jnp.where(eid == i1, NEG_INF, el)
    m2 = jnp.max(el2, axis=1, keepdims=True)
    i2 = jnp.min(jnp.where(el2 == m2, eid, big), axis=1, keepdims=True)
    e2 = jnp.exp(m2 - m1)
    w1 = g_w / (1.0 + e2)
    w2 = g_w * e2 / (1.0 + e2)
    meta = jnp.where(lane == 0, i1, jnp.where(lane == 1, i2, jnp.where(lane == 2, w1, jnp.where(lane == 3, w2, 0.0))))
    meta_ref[...] = meta


def _out_proj(x2, y_ssd, y_att, wo_s, wo_a, ln2_w, wr_hi, wr_lo, b_router):
    n = x2.shape[0]
    full = lambda shape: pl.BlockSpec(shape, lambda i: (0, 0))
    return pl.pallas_call(
        _outproj_kernel,
        grid=(n // TM_OUT,),
        in_specs=[
            pl.BlockSpec((TM_OUT, D_MODEL), lambda i: (i, 0)),
            pl.BlockSpec((TM_OUT, SSD_WIDTH), lambda i: (i, 0)),
            pl.BlockSpec((TM_OUT, ATTN_WIDTH), lambda i: (i, 0)),
            full((SSD_WIDTH, D_MODEL)), full((ATTN_WIDTH, D_MODEL)),
            full((1, D_MODEL)), full((LANES, D_MODEL)), full((LANES, D_MODEL)), full((1, LANES)),
        ],
        out_specs=[
            pl.BlockSpec((TM_OUT, D_MODEL), lambda i: (i, 0)),
            pl.BlockSpec((TM_OUT, D_MODEL // 2), lambda i: (i, 0)),
            pl.BlockSpec((TM_OUT, LANES), lambda i: (i, 0)),
        ],
        out_shape=[
            jax.ShapeDtypeStruct((n, D_MODEL), F32),
            jax.ShapeDtypeStruct((n, D_MODEL // 2), U32),
            jax.ShapeDtypeStruct((n, LANES), F32),
        ],
        compiler_params=_cparams(("parallel",)),
        name="out_proj",
    )(x2, y_ssd, y_att, wo_s, wo_a, ln2_w, wr_hi, wr_lo, b_router)


def _rank_kernel(meta_ref, dest_ref, cnt_ref, run_ref, offs_ref):
    p = pl.program_id(0)
    i = pl.program_id(1)
    tm = meta_ref.shape[0]
    meta = meta_ref[...]
    lane = lax.broadcasted_iota(I32, (tm, LANES), 1).astype(F32)
    oh0 = (lane == meta[:, 0:1]).astype(F32)
    oh1 = (lane == meta[:, 1:2]).astype(F32)
    oh = oh0 + oh1
    colsum = jnp.sum(oh, axis=0, keepdims=True)

    @pl.when((p == 0) & (i == 0))
    def _():
        run_ref[...] = jnp.zeros_like(run_ref)

    @pl.when(p == 0)
    def _():
        run_ref[...] = run_ref[...] + colsum
        dest_ref[...] = jnp.zeros_like(dest_ref)
        cnt_ref[...] = jnp.broadcast_to(run_ref[...], cnt_ref.shape)

    @pl.when((p == 1) & (i == 0))
    def _():
        cnt = run_ref[...]
        padded = jnp.ceil(cnt * (1.0 / TM_EXP)) * TM_EXP
        r = lax.broadcasted_iota(I32, (LANES, LANES), 0)
        c = lax.broadcasted_iota(I32, (LANES, LANES), 1)
        excl = (r < c).astype(F32)
        offs = jnp.dot(jnp.broadcast_to(padded, (8, LANES)), excl, precision=HIGHEST, preferred_element_type=F32)
        offs_ref[...] = offs[0:1, :]
        cnt_ref[...] = jnp.broadcast_to(cnt, cnt_ref.shape)
        run_ref[...] = jnp.zeros_like(run_ref)

    @pl.when(p == 1)
    def _():
        r = lax.broadcasted_iota(I32, (tm, tm), 0)
        c = lax.broadcasted_iota(I32, (tm, tm), 1)
        before = jnp.dot((c < r).astype(BF16), oh.astype(BF16), preferred_element_type=F32)
        base = before + run_ref[...] + offs_ref[...]
        d0 = jnp.sum(oh0 * base, axis=1, keepdims=True)
        d1 = jnp.sum(oh1 * base, axis=1, keepdims=True)
        dest = jnp.where(lane == 0, d0, jnp.where(lane == 1, d1, 0.0))
        dest_ref[...] = dest.astype(I32)
        run_ref[...] = run_ref[...] + colsum


def _rank(meta):
    n = meta.shape[0]
    return pl.pallas_call(
        _rank_kernel,
        grid=(2, n // TM_RANK),
        in_specs=[pl.BlockSpec((TM_RANK, LANES), lambda p, i: (i, 0))],
        out_specs=[
            pl.BlockSpec((TM_RANK, LANES), lambda p, i: (i * p, 0)),
            pl.BlockSpec((8, LANES), lambda p, i: (0, 0)),
        ],
        out_shape=[
            jax.ShapeDtypeStruct((n, LANES), I32),
            jax.ShapeDtypeStruct((8, LANES), F32),
        ],
        scratch_shapes=[pltpu.VMEM((1, LANES), F32), pltpu.VMEM((1, LANES), F32)],
        compiler_params=_cparams(("arbitrary", "arbitrary")),
        name="rank",
    )(meta)


def _row_copy(src_ref, src_row, dst_ref, dst_row, sem):
    return pltpu.make_async_copy(src_ref.at[pl.ds(src_row, 1)], dst_ref.at[pl.ds(dst_row, 1)], sem)


def _dispatch_kernel(d0_ref, d1_ref, hnp_ref, xs_in_ref, xs_ref, sem):
    del xs_in_ref
    tm = hnp_ref.shape[0]
    base = pl.program_id(0) * tm

    def start(r, _):
        _row_copy(hnp_ref, r, xs_ref, d0_ref[base + r], sem).start()
        _row_copy(hnp_ref, r, xs_ref, d1_ref[base + r], sem).start()
        return 0

    lax.fori_loop(0, tm, start, 0)

    def wait(r, _):
        _row_copy(hnp_ref, r, xs_ref, d0_ref[base + r], sem).wait()
        _row_copy(hnp_ref, r, xs_ref, d1_ref[base + r], sem).wait()
        return 0

    lax.fori_loop(0, tm, wait, 0)


def _dispatch(dest0, dest1, hnp, xs_zero):
    n = hnp.shape[0]
    grid_spec = pltpu.PrefetchScalarGridSpec(
        num_scalar_prefetch=2,
        grid=(n // TM_DISP,),
        in_specs=[
            pl.BlockSpec((TM_DISP, D_MODEL // 2), lambda i, d0, d1: (i, 0)),
            pl.BlockSpec(memory_space=pl.ANY),
        ],
        out_specs=pl.BlockSpec(memory_space=pl.ANY),
        scratch_shapes=[pltpu.SemaphoreType.DMA(())],
    )
    return pl.pallas_call(
        _dispatch_kernel,
        grid_spec=grid_spec,
        out_shape=jax.ShapeDtypeStruct(xs_zero.shape, xs_zero.dtype),
        input_output_aliases={3: 0},
        compiler_params=_cparams(("arbitrary",)),
        name="dispatch",
    )(dest0, dest1, hnp, xs_zero)


def _experts_kernel(te_ref, nu_ref, xs_ref, wg_ref, wu_ref, wd_ref, ys_ref):
    t = pl.program_id(0)

    @pl.when(t < nu_ref[0])
    def _():
        c = D_MODEL // 2
        x_lo, x_hi = _unpack_halves(xs_ref[...])
        gate = (jnp.dot(x_lo, wg_ref[:c, :], preferred_element_type=F32)
                + jnp.dot(x_hi, wg_ref[c:, :], preferred_element_type=F32))
        up = (jnp.dot(x_lo, wu_ref[:c, :], preferred_element_type=F32)
              + jnp.dot(x_hi, wu_ref[c:, :], preferred_element_type=F32))
        hid = (_silu(gate) * up).astype(BF16)
        ys_ref[...] = jnp.dot(hid, wd_ref[...], preferred_element_type=F32)

    @pl.when(t >= nu_ref[0])
    def _():
        ys_ref[...] = jnp.zeros_like(ys_ref)


def _experts(tile_expert, n_used, xs, wg, wu, wd):
    n_slots = xs.shape[0]
    grid_spec = pltpu.PrefetchScalarGridSpec(
        num_scalar_prefetch=2,
        grid=(n_slots // TM_EXP,),
        in_specs=[
            pl.BlockSpec((TM_EXP, D_MODEL // 2), lambda t, te, nu: (t, 0)),
            pl.BlockSpec((None, D_MODEL, EXPERT_HIDDEN), lambda t, te, nu: (te[t], 0, 0)),
            pl.BlockSpec((None, D_MODEL, EXPERT_HIDDEN), lambda t, te, nu: (te[t], 0, 0)),
            pl.BlockSpec((None, EXPERT_HIDDEN, D_MODEL), lambda t, te, nu: (te[t], 0, 0)),
        ],
        out_specs=pl.BlockSpec((TM_EXP, D_MODEL), lambda t, te, nu: (t, 0)),
    )
    return pl.pallas_call(
        _experts_kernel,
        grid_spec=grid_spec,
        out_shape=jax.ShapeDtypeStruct((n_slots, D_MODEL), F32),
        compiler_params=_cparams(("arbitrary",)),
        name="experts",
    )(tile_expert, n_used, xs, wg, wu, wd)


def _combine_kernel(d0_ref, d1_ref, h_ref, meta_ref, ys_ref, o_ref, buf_ref, sem):
    tm = h_ref.shape[0]
    base = pl.program_id(0) * tm

    def start(r, _):
        _row_copy(ys_ref, d0_ref[base + r], buf_ref.at[0], r, sem).start()
        _row_copy(ys_ref, d1_ref[base + r], buf_ref.at[1], r, sem).start()
        return 0

    lax.fori_loop(0, tm, start, 0)

    def wait(r, _):
        _row_copy(ys_ref, d0_ref[base + r], buf_ref.at[0], r, sem).wait()
        _row_copy(ys_ref, d1_ref[base + r], buf_ref.at[1], r, sem).wait()
        return 0

    lax.fori_loop(0, tm, wait, 0)
    meta = meta_ref[...]
    o_ref[...] = h_ref[...] + (buf_ref[0] * meta[:, 2:3] + buf_ref[1] * meta[:, 3:4])


def _combine(dest0, dest1, h, meta, ys):
    n = h.shape[0]
    grid_spec = pltpu.PrefetchScalarGridSpec(
        num_scalar_prefetch=2,
        grid=(n // TM_COMB,),
        in_specs=[
            pl.BlockSpec((TM_COMB, D_MODEL), lambda i, d0, d1: (i, 0)),
            pl.BlockSpec((TM_COMB, LANES), lambda i, d0, d1: (i, 0)),
            pl.BlockSpec(memory_space=pl.ANY),
        ],
        out_specs=pl.BlockSpec((TM_COMB, D_MODEL), lambda i, d0, d1: (i, 0)),
        scratch_shapes=[pltpu.VMEM((2, TM_COMB, D_MODEL), F32), pltpu.SemaphoreType.DMA(())],
    )
    return pl.pallas_call(
        _combine_kernel,
        grid_spec=grid_spec,
        out_shape=jax.ShapeDtypeStruct((n, D_MODEL), F32),
        compiler_params=_cparams(("arbitrary",)),
        name="combine",
    )(dest0, dest1, h, meta, ys)


def _lambda_init(layer_idx):
    return 0.8 - 0.6 * math.exp(-0.3 * layer_idx)


def _pad_lanes(v, width=LANES):
    return jnp.pad(v, ((0, 0), (0, width - v.shape[1])))


def _layer(l, x2, pos_col, bsz, seq, ln1_w, w_in, conv_w, conv_b, dt_bias, a_log, d_skip, ssd_norm_w,
           q_norm_w, k_norm_w, lambda_q1, lambda_k1, lambda_q2, lambda_k2, subln_w, w_out, ln2_w,
           w_router_group, b_router_group, w_router_expert, b_router_expert, w_gate, w_up, w_down):
    n = x2.shape[0]
    c_z, c_xbc, c_dt = SSD_WIDTH, SSD_WIDTH + SSD_CONV_DIM, SSD_WIDTH + SSD_CONV_DIM + SSD_HEADS
    c_q, c_k = c_dt + ATTN_WIDTH, c_dt + 2 * ATTN_WIDTH
    w_main = jnp.concatenate([w_in[:, :c_z], w_in[:, c_dt:c_q], w_in[:, c_q:c_k], w_in[:, c_z:c_xbc]],
                             axis=1).astype(BF16)
    w_vt = w_in[:, c_k:].T.astype(BF16)
    w_dt = _pad_lanes(w_in[:, c_xbc:c_dt]).astype(BF16)

    u, v_t, dt_raw = _in_proj(x2, ln1_w[None, :], w_main, w_vt, w_dt)

    a_neg = _pad_lanes(-jnp.exp(a_log.astype(F32))[None, :])
    y_ssd = _ssd(u, dt_raw, conv_w, conv_b[None, :], _pad_lanes(dt_bias[None, :]), a_neg,
                 jnp.repeat(d_skip, SSD_HEAD_DIM)[None, :], ssd_norm_w[None, :], bsz, seq)

    inv_freq = jnp.power(ROPE_THETA, -jnp.arange(0, ROPE_DIM, 2, dtype=F32) / ROPE_DIM)
    d = jnp.arange(LANES) % ATTN_QK_DIM
    invf_lanes = jnp.where(d < ROPE_DIM, inv_freq[d % (ROPE_DIM // 2)], 0.0)[None, :]
    seg_ones = (jnp.arange(LANES)[:, None] // ATTN_QK_DIM == jnp.arange(LANES)[None, :] // ATTN_QK_DIM).astype(BF16)
    qp, kp = _qk_prep(u, pos_col, invf_lanes, jnp.tile(q_norm_w, 2)[None, :], jnp.tile(k_norm_w, 2)[None, :], seg_ones)

    lam_vecs = jnp.stack([lambda_q1, lambda_k1, lambda_q2, lambda_k2]).astype(F32)
    y_att = _attention(qp, kp, v_t, lam_vecs, subln_w[:, None], bsz, seq, _lambda_init(l))

    w_out_b = w_out.astype(BF16)
    w_router_t = _pad_lanes(jnp.concatenate([w_router_group, w_router_expert], axis=1)).T
    wr_hi = w_router_t.astype(BF16)
    wr_lo = (w_router_t - wr_hi.astype(F32)).astype(BF16)
    b_router = _pad_lanes(jnp.concatenate([b_router_group, b_router_expert])[None, :])
    h, hnp, meta = _out_proj(x2, y_ssd, y_att, w_out_b[:SSD_WIDTH], w_out_b[SSD_WIDTH:], ln2_w[None, :],
                             wr_hi, wr_lo, b_router)

    dest, cnt = _rank(meta)
    dest0, dest1 = dest[:, 0], dest[:, 1]
    counts = cnt[0, :N_EXPERTS].astype(I32)
    tiles_per_expert = (counts + TM_EXP - 1) // TM_EXP
    tile_end = jnp.cumsum(tiles_per_expert)
    n_slots = (n * TOP_K + N_EXPERTS * (TM_EXP - 1)) // TM_EXP * TM_EXP
    tile_ids = jnp.arange(n_slots // TM_EXP, dtype=I32)
    tile_expert = jnp.minimum(jnp.sum(tile_ids[:, None] >= tile_end[None, :], axis=1), N_EXPERTS - 1).astype(I32)
    n_used = tile_end[-1:].astype(I32)

    xs = _dispatch(dest0, dest1, hnp, jnp.zeros((n_slots, D_MODEL // 2), U32))
    ys = _experts(tile_expert, n_used, xs, w_gate.astype(BF16), w_up.astype(BF16), w_down.astype(BF16))
    return _combine(dest0, dest1, h, meta, ys)


def kernel(x, positions, ln1_w, w_in, conv_w, conv_b, dt_bias, a_log, d_skip, ssd_norm_w, q_norm_w, k_norm_w,
           lambda_q1, lambda_k1, lambda_q2, lambda_k2, subln_w, w_out, ln2_w, w_router_group, b_router_group,
           w_router_expert, b_router_expert, w_gate, w_up, w_down):
    bsz, seq, d = x.shape
    assert d == D_MODEL and seq % TQ == 0 and (bsz * seq) % TM_IN == 0
    x2 = x.reshape(bsz * seq, d)
    pos_col = positions.astype(F32).reshape(bsz * seq, 1)
    params = (ln1_w, w_in, conv_w, conv_b, dt_bias, a_log, d_skip, ssd_norm_w, q_norm_w, k_norm_w,
              lambda_q1, lambda_k1, lambda_q2, lambda_k2, subln_w, w_out, ln2_w, w_router_group, b_router_group,
              w_router_expert, b_router_expert, w_gate, w_up, w_down)
    for l in range(ln1_w.shape[0]):
        x2 = _layer(l, x2, pos_col, bsz, seq, *[p[l] for p in params])
    return x2.reshape(bsz, seq, d)
```

```python
import functools
import math

import jax
import jax.numpy as jnp
from jax import lax
from jax.experimental import pallas as pl
from jax.experimental.pallas import tpu as pltpu

F32 = jnp.float32
BF16 = jnp.bfloat16
I32 = jnp.int32
U32 = jnp.uint32
HIGHEST = lax.Precision.HIGHEST

D_MODEL = 2048
SSD_WIDTH = 1024
ATTN_WIDTH = 1024
SSD_HEAD_DIM = 64
SSD_HEADS = 16
SSD_GROUPS = 2
SSD_HEADS_PER_GROUP = SSD_HEADS // SSD_GROUPS
SSD_STATE = 128
SSD_CONV = 4
SSD_CHUNK = 128
SSD_CONV_DIM = SSD_WIDTH + 2 * SSD_GROUPS * SSD_STATE
ATTN_V_DIM = 128
ATTN_HEADS = 8
ATTN_QK_DIM = 64
ROPE_THETA = 500000.0
ROPE_DIM = 16
N_EXPERT_GROUPS = 4
EXPERTS_PER_GROUP = 8
N_EXPERTS = 32
TOP_K = 2
EXPERT_HIDDEN = 1024
EPS = 1e-6

LANES = 128
NEG_INF = float("-inf")

TM_IN = 512
TN_IN = 1536
TM_QK = 512
TQ = 256
ATTN_HB = 8
TM_OUT = 256
TM_RANK = 512
TM_EXP = 256
TM_DISP = 256
TM_COMB = 256
U_COLS = SSD_WIDTH + 2 * ATTN_WIDTH + SSD_CONV_DIM
VMEM_LIMIT = 52 * 1024 * 1024
EXPERT_VMEM_LIMIT = 58 * 1024 * 1024


def _cparams(sem):
    return pltpu.CompilerParams(dimension_semantics=sem, vmem_limit_bytes=VMEM_LIMIT)


def _silu(x):
    return x * (1.0 / (1.0 + jnp.exp(-x)))


def _softplus(x):
    return jnp.maximum(x, 0.0) + jnp.log(1.0 + jnp.exp(-jnp.abs(x)))


def _inproj_kernel(x_ref, lnw_ref, w_ref, wvt_ref, wdt_ref, u_ref, vt_ref, dt_ref, xn_ref):
    @pl.when(pl.program_id(1) == 0)
    def _():
        x = x_ref[...]
        ms = jnp.mean(x * x, axis=-1, keepdims=True)
        xn = (x * lax.rsqrt(ms + EPS) * lnw_ref[...]).astype(BF16)
        xn_ref[...] = xn
        dt_ref[...] = jnp.dot(xn, wdt_ref[...], preferred_element_type=F32)
        vt_ref[...] = lax.dot_general(wvt_ref[...], xn, (((1,), (1,)), ((), ())),
                                      preferred_element_type=F32).astype(BF16)

    u_ref[...] = jnp.dot(xn_ref[...], w_ref[...], preferred_element_type=F32).astype(BF16)


def _in_proj(x2, ln_w, w_main, w_vt, w_dt):
    n = x2.shape[0]
    return pl.pallas_call(
        _inproj_kernel,
        grid=(n // TM_IN, U_COLS // TN_IN),
        in_specs=[
            pl.BlockSpec((TM_IN, D_MODEL), lambda i, j: (i, 0)),
            pl.BlockSpec((1, D_MODEL), lambda i, j: (0, 0)),
            pl.BlockSpec((D_MODEL, TN_IN), lambda i, j: (0, j)),
            pl.BlockSpec((ATTN_WIDTH, D_MODEL), lambda i, j: (0, 0)),
            pl.BlockSpec((D_MODEL, LANES), lambda i, j: (0, 0)),
        ],
        out_specs=[
            pl.BlockSpec((TM_IN, TN_IN), lambda i, j: (i, j)),
            pl.BlockSpec((ATTN_WIDTH, TM_IN), lambda i, j: (0, i)),
            pl.BlockSpec((TM_IN, LANES), lambda i, j: (i, 0)),
        ],
        out_shape=[
            jax.ShapeDtypeStruct((n, U_COLS), BF16),
            jax.ShapeDtypeStruct((ATTN_WIDTH, n), BF16),
            jax.ShapeDtypeStruct((n, LANES), F32),
        ],
        scratch_shapes=[pltpu.VMEM((TM_IN, D_MODEL), BF16)],
        compiler_params=_cparams(("parallel", "arbitrary")),
        name="in_proj",
    )(x2, ln_w, w_main, w_vt, w_dt)


def _ssd_kernel(z_ref, xbc_ref, dt_ref, convw_ref, convb_ref, dtb_ref, aneg_ref, dskip_ref, normw_ref,
                y_ref, xp_ref, st_ref, yacc_ref):
    L = SSD_CHUNK
    P = SSD_HEAD_DIM

    @pl.when(pl.program_id(1) == 0)
    def _():
        xp_ref[0:8, :] = jnp.zeros((8, SSD_CONV_DIM), F32)
        st_ref[...] = jnp.zeros_like(st_ref)

    xp_ref[8:8 + L, :] = xbc_ref[...].astype(F32)
    acc = jnp.broadcast_to(convb_ref[...], (L, SSD_CONV_DIM))
    for k in range(SSD_CONV):
        acc = acc + xp_ref[5 + k:5 + k + L, :] * convw_ref[k:k + 1, :]
    xp_ref[0:8, :] = xp_ref[L:L + 8, :]
    xc = _silu(acc)

    dt = _softplus(dt_ref[...] + dtb_ref[...])
    a = dt * aneg_ref[...]
    row = lax.broadcasted_iota(I32, (L, L), 0)
    col = lax.broadcasted_iota(I32, (L, L), 1)
    causal = row >= col
    a_cs = jnp.dot(causal.astype(F32), a, precision=HIGHEST, preferred_element_type=F32)
    a_last = a_cs[L - 1:L, :]
    ea = jnp.exp(a_cs)
    dsdt = jnp.exp(a_last - a_cs) * dt
    cd = jnp.exp(a_last)
    a_cs_t = a_cs.T
    dt_t = dt.T
    dsdt_t = dsdt.T

    for g in range(SSD_GROUPS):
        b_g = xc[:, SSD_WIDTH + g * SSD_STATE:SSD_WIDTH + (g + 1) * SSD_STATE]
        c_off = SSD_WIDTH + SSD_GROUPS * SSD_STATE
        c_g = xc[:, c_off + g * SSD_STATE:c_off + (g + 1) * SSD_STATE]
        cb = lax.dot_general(c_g.astype(BF16), b_g.astype(BF16), (((1,), (1,)), ((), ())),
                             preferred_element_type=F32)
        b_gt = b_g.T
        for hh in range(SSD_HEADS_PER_GROUP):
            h = g * SSD_HEADS_PER_GROUP + hh
            xs_h = xc[:, h * P:(h + 1) * P].astype(BF16)
            seg = a_cs[:, h:h + 1] - a_cs_t[h:h + 1, :]
            dec = jnp.exp(jnp.where(causal, seg, NEG_INF))
            m = (cb * dec * dt_t[h:h + 1, :]).astype(BF16)
            c_s = (c_g * ea[:, h:h + 1]).astype(BF16)
            s_prev = st_ref[h]
            lhs = jnp.concatenate([m, c_s], axis=1)
            rhs = jnp.concatenate([xs_h, s_prev.astype(BF16)], axis=0)
            yacc_ref[:, h * P:(h + 1) * P] = jnp.dot(lhs, rhs, preferred_element_type=F32)
            bw = (b_gt * dsdt_t[h:h + 1, :]).astype(BF16)
            st_ref[h] = s_prev * cd[:, h:h + 1] + jnp.dot(bw, xs_h, preferred_element_type=F32)

    y = yacc_ref[...] + xc[:, :SSD_WIDTH] * dskip_ref[...]
    y = y * _silu(z_ref[...].astype(F32))
    gw = SSD_WIDTH // SSD_GROUPS
    for g in range(SSD_GROUPS):
        yg = y[:, g * gw:(g + 1) * gw]
        ms = jnp.mean(yg * yg, axis=-1, keepdims=True)
        y_ref[:, g * gw:(g + 1) * gw] = (yg * lax.rsqrt(ms + EPS) * normw_ref[:, g * gw:(g + 1) * gw]).astype(BF16)


def _ssd(u, dt_raw, conv_w, conv_b, dt_bias, a_neg, dskip_lanes, norm_w, bsz, seq):
    n = u.shape[0]
    nc = seq // SSD_CHUNK
    xbc_blk = (SSD_WIDTH + 2 * ATTN_WIDTH) // SSD_CONV_DIM
    full = lambda shape: pl.BlockSpec(shape, lambda b, c: (0, 0))
    return pl.pallas_call(
        _ssd_kernel,
        grid=(bsz, nc),
        in_specs=[
            pl.BlockSpec((SSD_CHUNK, SSD_WIDTH), lambda b, c: (b * nc + c, 0)),
            pl.BlockSpec((SSD_CHUNK, SSD_CONV_DIM), lambda b, c: (b * nc + c, xbc_blk)),
            pl.BlockSpec((SSD_CHUNK, LANES), lambda b, c: (b * nc + c, 0)),
            full((SSD_CONV, SSD_CONV_DIM)),
            full((1, SSD_CONV_DIM)),
            full((1, LANES)),
            full((1, LANES)),
            full((1, SSD_WIDTH)),
            full((1, SSD_WIDTH)),
        ],
        out_specs=pl.BlockSpec((SSD_CHUNK, SSD_WIDTH), lambda b, c: (b * nc + c, 0)),
        out_shape=jax.ShapeDtypeStruct((n, SSD_WIDTH), BF16),
        scratch_shapes=[
            pltpu.VMEM((SSD_CHUNK + 8, SSD_CONV_DIM), F32),
            pltpu.VMEM((SSD_HEADS, SSD_STATE, SSD_HEAD_DIM), F32),
            pltpu.VMEM((SSD_CHUNK, SSD_WIDTH), F32),
        ],
        compiler_params=_cparams(("parallel", "arbitrary")),
        name="ssd",
    )(u, u, dt_raw, conv_w, conv_b, dt_bias, a_neg, dskip_lanes, norm_w)


def _qkprep_kernel(q_ref, k_ref, pos_ref, invf_ref, qw_ref, kw_ref, ones_ref, qo_ref, ko_ref):
    tm = q_ref.shape[0]
    ang = pos_ref[...] * invf_ref[...]
    cs = jnp.cos(ang)
    sn = jnp.sin(ang)
    d = lax.broadcasted_iota(I32, (tm, LANES), 1) & (ATTN_QK_DIM - 1)
    half = ROPE_DIM // 2
    s_lo = jnp.where(d < half, -sn, 0.0)
    s_hi = jnp.where((d >= half) & (d < ROPE_DIM), sn, 0.0)
    scale = 1.0 / math.sqrt(ATTN_QK_DIM)
    for src, w_ref, dst, mul in ((q_ref, qw_ref, qo_ref, scale), (k_ref, kw_ref, ko_ref, 1.0)):
        for hb in range(ATTN_HEADS):
            x = src[:, hb * LANES:(hb + 1) * LANES].astype(F32)
            ss = jnp.dot((x * x).astype(BF16), ones_ref[...], preferred_element_type=F32)
            xn = x * lax.rsqrt(ss * (1.0 / ATTN_QK_DIM) + EPS) * w_ref[...]
            out = xn * cs + pltpu.roll(xn, LANES - half, 1) * s_lo + pltpu.roll(xn, half, 1) * s_hi
            dst[:, hb * LANES:(hb + 1) * LANES] = (out * mul).astype(BF16)


def _qk_prep(u, pos_col, invf_lanes, qw_lanes, kw_lanes, seg_ones):
    n = u.shape[0]
    full = lambda shape: pl.BlockSpec(shape, lambda i: (0, 0))
    return pl.pallas_call(
        _qkprep_kernel,
        grid=(n // TM_QK,),
        in_specs=[
            pl.BlockSpec((TM_QK, ATTN_WIDTH), lambda i: (i, 1)),
            pl.BlockSpec((TM_QK, ATTN_WIDTH), lambda i: (i, 2)),
            pl.BlockSpec((TM_QK, 1), lambda i: (i, 0)),
            full((1, LANES)), full((1, LANES)), full((1, LANES)), full((LANES, LANES)),
        ],
        out_specs=[pl.BlockSpec((TM_QK, ATTN_WIDTH), lambda i: (i, 0))] * 2,
        out_shape=[jax.ShapeDtypeStruct((n, ATTN_WIDTH), BF16)] * 2,
        compiler_params=_cparams(("parallel",)),
        name="qk_prep",
    )(u, u, pos_col, invf_lanes, qw_lanes, kw_lanes, seg_ones)


def _attn_kernel(q_ref, k_ref, vt_ref, lamv_ref, subw_ref, o_ref, acc_ref, *, lam_init):
    qi = pl.program_id(2)
    lane = lax.broadcasted_iota(I32, (TQ, LANES), 1)
    qs = []
    for hb in range(ATTN_HB):
        q = q_ref[:, hb * LANES:(hb + 1) * LANES]
        zero = jnp.zeros_like(q)
        qs.append(jnp.concatenate([jnp.where(lane < ATTN_QK_DIM, q, zero),
                                   jnp.where(lane >= ATTN_QK_DIM, q, zero)], axis=0))
    acc_ref[...] = jnp.zeros_like(acc_ref)
    kv_idx = lax.broadcasted_iota(I32, (TQ, 2 * TQ), 0)
    q_idx = lax.broadcasted_iota(I32, (TQ, 2 * TQ), 1) & (TQ - 1)
    nt = (((1,), (1,)), ((), ()))

    def block(j, carry, masked):
        off = pl.multiple_of(j * TQ, TQ)
        ss = []
        for hb in range(ATTN_HB):
            kb = k_ref[pl.ds(off, TQ), hb * LANES:(hb + 1) * LANES]
            ss.append(lax.dot_general(kb, qs[hb], nt, preferred_element_type=F32))
        new, ps, alphas = [], [], []
        for hb in range(ATTN_HB):
            m_old, l_old = carry[2 * hb], carry[2 * hb + 1]
            s = ss[hb]
            if masked:
                s = jnp.where(kv_idx <= q_idx, s, NEG_INF)
            m_new = jnp.maximum(m_old, jnp.max(s, axis=0, keepdims=True))
            alpha = jnp.exp(m_old - m_new)
            p = jnp.exp(s - m_new)
            new += [m_new, alpha * l_old + jnp.sum(p, axis=0, keepdims=True)]
            ps.append(p.astype(BF16))
            alphas.append(alpha)
        pvs = []
        for hb in range(ATTN_HB):
            vb = vt_ref[hb * LANES:(hb + 1) * LANES, pl.ds(off, TQ)]
            pvs.append(jnp.dot(vb, ps[hb], preferred_element_type=F32))
        for hb in range(ATTN_HB):
            acc_ref[hb] = alphas[hb] * acc_ref[hb] + pvs[hb]
        return tuple(new)

    init = (jnp.full((1, 2 * TQ), NEG_INF, F32), jnp.zeros((1, 2 * TQ), F32)) * ATTN_HB
    carry = lax.fori_loop(0, qi, lambda j, cr: block(j, cr, False), init)
    carry = block(qi, carry, True)

    lv = lamv_ref[...]
    lam = (jnp.exp(jnp.sum(lv[0:1] * lv[1:2], axis=1, keepdims=True))
           - jnp.exp(jnp.sum(lv[2:3] * lv[3:4], axis=1, keepdims=True)) + lam_init)
    for hb in range(ATTN_HB):
        o2 = acc_ref[hb] * (1.0 / carry[2 * hb + 1])
        o_t = o2[:, :TQ] - lam * o2[:, TQ:]
        ms = jnp.mean(o_t * o_t, axis=0, keepdims=True)
        o_t = o_t * lax.rsqrt(ms + EPS) * subw_ref[...] * (1.0 - lam_init)
        o_ref[:, hb * LANES:(hb + 1) * LANES] = o_t.T.astype(BF16)


def _attention(qp, kp, v_t, lam_vecs, subw_col, bsz, seq, lam_init):
    n = qp.shape[0]
    nq = seq // TQ
    w = ATTN_HB * ATTN_V_DIM
    return pl.pallas_call(
        functools.partial(_attn_kernel, lam_init=lam_init),
        grid=(bsz, ATTN_HEADS // ATTN_HB, nq),
        in_specs=[
            pl.BlockSpec((TQ, w), lambda b, h, i: (b * nq + i, h)),
            pl.BlockSpec((seq, w), lambda b, h, i: (b, h)),
            pl.BlockSpec((w, seq), lambda b, h, i: (h, b)),
            pl.BlockSpec((4, ATTN_QK_DIM), lambda b, h, i: (0, 0)),
            pl.BlockSpec((ATTN_V_DIM, 1), lambda b, h, i: (0, 0)),
        ],
        out_specs=pl.BlockSpec((TQ, w), lambda b, h, i: (b * nq + i, h)),
        out_shape=jax.ShapeDtypeStruct((n, ATTN_WIDTH), BF16),
        scratch_shapes=[pltpu.VMEM((ATTN_HB, ATTN_V_DIM, 2 * TQ), F32)],
        compiler_params=_cparams(("parallel", "parallel", "arbitrary")),
        name="attn",
    )(qp, kp, v_t, lam_vecs, subw_col)


def _pack_halves(x):
    c = x.shape[1] // 2
    lo = pltpu.bitcast(x[:, :c].astype(BF16).astype(F32), U32) >> 16
    hi = pltpu.bitcast(x[:, c:].astype(BF16).astype(F32), U32) & jnp.uint32(0xFFFF0000)
    return hi | lo


def _unpack_halves(w):
    lo = pltpu.bitcast(w << 16, F32).astype(BF16)
    hi = pltpu.bitcast(w & jnp.uint32(0xFFFF0000), F32).astype(BF16)
    return lo, hi


def _outproj_kernel(x_ref, ys_ref, ya_ref, wos_ref, woa_ref, ln2_ref, wrh_ref, wrl_ref, br_ref,
                    h_ref, hnp_ref, meta_ref):
    tm = x_ref.shape[0]
    h = (x_ref[...]
         + jnp.dot(ys_ref[...], wos_ref[...], preferred_element_type=F32)
         + jnp.dot(ya_ref[...], woa_ref[...], preferred_element_type=F32))
    h_ref[...] = h
    ms = jnp.mean(h * h, axis=-1, keepdims=True)
    hn = h * lax.rsqrt(ms + EPS) * ln2_ref[...]
    hnp_ref[...] = _pack_halves(hn)

    hn_hi = hn.astype(BF16)
    hn_lo = (hn - hn_hi.astype(F32)).astype(BF16)
    nt = (((1,), (1,)), ((), ()))
    lg_t = (lax.dot_general(wrh_ref[...], hn_hi, nt, preferred_element_type=F32)
            + lax.dot_general(wrh_ref[...], hn_lo, nt, preferred_element_type=F32)
            + lax.dot_general(wrl_ref[...], hn_hi, nt, preferred_element_type=F32))
    lg = lg_t.T + br_ref[...]
    lane = lax.broadcasted_iota(I32, (tm, LANES), 1).astype(F32)
    big = float(LANES)
    gmask = lane < N_EXPERT_GROUPS
    gl = jnp.where(gmask, lg, NEG_INF)
    gmax = jnp.max(gl, axis=1, keepdims=True)
    gsel = jnp.min(jnp.where(gl == gmax, lane, big), axis=1, keepdims=True)
    g_w = 1.0 / jnp.sum(jnp.exp(gl - gmax), axis=1, keepdims=True)
    eid = lane - N_EXPERT_GROUPS
    lo = gsel * EXPERTS_PER_GROUP
    emask = (eid >= lo) & (eid < lo + EXPERTS_PER_GROUP)
    el = jnp.where(emask, lg, NEG_INF)
    m1 = jnp.max(el, axis=1, keepdims=True)
    i1 = jnp.min(jnp.where(el == m1, eid, big), axis=1, keepdims=True)
    el2 = jnp.where(eid == i1, NEG_INF, el)
    m2 = jnp.max(el2, axis=1, keepdims=True)
    i2 = jnp.min(jnp.where(el2 == m2, eid, big), axis=1, keepdims=True)
    e2 = jnp.exp(m2 - m1)
    w1 = g_w / (1.0 + e2)
    w2 = g_w * e2 / (1.0 + e2)
    meta = jnp.where(lane == 0, i1, jnp.where(lane == 1, i2, jnp.where(lane == 2, w1, jnp.where(lane == 3, w2, 0.0))))
    meta_ref[...] = meta


def _out_proj(x2, y_ssd, y_att, wo_s, wo_a, ln2_w, wr_hi, wr_lo, b_router):
    n = x2.shape[0]
    full = lambda shape: pl.BlockSpec(shape, lambda i: (0, 0))
    return pl.pallas_call(
        _outproj_kernel,
        grid=(n // TM_OUT,),
        in_specs=[
            pl.BlockSpec((TM_OUT, D_MODEL), lambda i: (i, 0)),
            pl.BlockSpec((TM_OUT, SSD_WIDTH), lambda i: (i, 0)),
            pl.BlockSpec((TM_OUT, ATTN_WIDTH), lambda i: (i, 0)),
            full((SSD_WIDTH, D_MODEL)), full((ATTN_WIDTH, D_MODEL)),
            full((1, D_MODEL)), full((LANES, D_MODEL)), full((LANES, D_MODEL)), full((1, LANES)),
        ],
        out_specs=[
            pl.BlockSpec((TM_OUT, D_MODEL), lambda i: (i, 0)),
            pl.BlockSpec((TM_OUT, D_MODEL // 2), lambda i: (i, 0)),
            pl.BlockSpec((TM_OUT, LANES), lambda i: (i, 0)),
        ],
        out_shape=[
            jax.ShapeDtypeStruct((n, D_MODEL), F32),
            jax.ShapeDtypeStruct((n, D_MODEL // 2), U32),
            jax.ShapeDtypeStruct((n, LANES), F32),
        ],
        compiler_params=_cparams(("parallel",)),
        name="out_proj",
    )(x2, y_ssd, y_att, wo_s, wo_a, ln2_w, wr_hi, wr_lo, b_router)


def _rank_kernel(meta_ref, dest_ref, cnt_ref, run_ref, offs_ref):
    p = pl.program_id(0)
    i = pl.program_id(1)
    tm = meta_ref.shape[0]
    meta = meta_ref[...]
    lane = lax.broadcasted_iota(I32, (tm, LANES), 1).astype(F32)
    oh0 = (lane == meta[:, 0:1]).astype(F32)
    oh1 = (lane == meta[:, 1:2]).astype(F32)
    oh = oh0 + oh1
    colsum = jnp.sum(oh, axis=0, keepdims=True)

    @pl.when((p == 0) & (i == 0))
    def _():
        run_ref[...] = jnp.zeros_like(run_ref)

    @pl.when(p == 0)
    def _():
        run_ref[...] = run_ref[...] + colsum
        dest_ref[...] = jnp.zeros_like(dest_ref)
        cnt_ref[...] = jnp.broadcast_to(run_ref[...], cnt_ref.shape)

    @pl.when((p == 1) & (i == 0))
    def _():
        cnt = run_ref[...]
        padded = jnp.ceil(cnt * (1.0 / TM_EXP)) * TM_EXP
        r = lax.broadcasted_iota(I32, (LANES, LANES), 0)
        c = lax.broadcasted_iota(I32, (LANES, LANES), 1)
        excl = (r < c).astype(F32)
        offs = jnp.dot(jnp.broadcast_to(padded, (8, LANES)), excl, precision=HIGHEST, preferred_element_type=F32)
        offs_ref[...] = offs[0:1, :]
        cnt_ref[...] = jnp.broadcast_to(cnt, cnt_ref.shape)
        run_ref[...] = jnp.zeros_like(run_ref)

    @pl.when(p == 1)
    def _():
        r = lax.broadcasted_iota(I32, (tm, tm), 0)
        c = lax.broadcasted_iota(I32, (tm, tm), 1)
        before = jnp.dot((c < r).astype(BF16), oh.astype(BF16), preferred_element_type=F32)
        base = before + run_ref[...] + offs_ref[...]
        d0 = jnp.sum(oh0 * base, axis=1, keepdims=True)
        d1 = jnp.sum(oh1 * base, axis=1, keepdims=True)
        dest = jnp.where(lane == 0, d0, jnp.where(lane == 1, d1, 0.0))
        dest_ref[...] = dest.astype(I32)
        run_ref[...] = run_ref[...] + colsum


def _rank(meta):
    n = meta.shape[0]
    return pl.pallas_call(
        _rank_kernel,
        grid=(2, n // TM_RANK),
        in_specs=[pl.BlockSpec((TM_RANK, LANES), lambda p, i: (i, 0))],
        out_specs=[
            pl.BlockSpec((TM_RANK, LANES), lambda p, i: (i * p, 0)),
            pl.BlockSpec((8, LANES), lambda p, i: (0, 0)),
        ],
        out_shape=[
            jax.ShapeDtypeStruct((n, LANES), I32),
            jax.ShapeDtypeStruct((8, LANES), F32),
        ],
        scratch_shapes=[pltpu.VMEM((1, LANES), F32), pltpu.VMEM((1, LANES), F32)],
        compiler_params=_cparams(("arbitrary", "arbitrary")),
        name="rank",
    )(meta)


def _row_copy(src_ref, src_row, dst_ref, dst_row, sem):
    return pltpu.make_async_copy(src_ref.at[pl.ds(src_row, 1)], dst_ref.at[pl.ds(dst_row, 1)], sem)


def _dispatch_kernel(d0_ref, d1_ref, hnp_ref, xs_in_ref, xs_ref, sem):
    del xs_in_ref
    tm = hnp_ref.shape[0]
    base = pl.program_id(0) * tm

    def start(r, _):
        _row_copy(hnp_ref, r, xs_ref, d0_ref[base + r], sem).start()
        _row_copy(hnp_ref, r, xs_ref, d1_ref[base + r], sem).start()
        return 0

    lax.fori_loop(0, tm, start, 0)

    def wait(r, _):
        _row_copy(hnp_ref, r, xs_ref, d0_ref[base + r], sem).wait()
        _row_copy(hnp_ref, r, xs_ref, d1_ref[base + r], sem).wait()
        return 0

    lax.fori_loop(0, tm, wait, 0)


def _dispatch(dest0, dest1, hnp, xs_zero):
    n = hnp.shape[0]
    grid_spec = pltpu.PrefetchScalarGridSpec(
        num_scalar_prefetch=2,
        grid=(n // TM_DISP,),
        in_specs=[
            pl.BlockSpec((TM_DISP, D_MODEL // 2), lambda i, d0, d1: (i, 0)),
            pl.BlockSpec(memory_space=pl.ANY),
        ],
        out_specs=pl.BlockSpec(memory_space=pl.ANY),
        scratch_shapes=[pltpu.SemaphoreType.DMA(())],
    )
    return pl.pallas_call(
        _dispatch_kernel,
        grid_spec=grid_spec,
        out_shape=jax.ShapeDtypeStruct(xs_zero.shape, xs_zero.dtype),
        input_output_aliases={3: 0},
        compiler_params=_cparams(("arbitrary",)),
        name="dispatch",
    )(dest0, dest1, hnp, xs_zero)


CAST_ROWS = 256


def _cast_weight(src_ref, dst_ref):
    def body(i, _):
        rows = pl.ds(pl.multiple_of(i * CAST_ROWS, CAST_ROWS), CAST_ROWS)
        dst_ref[rows, :] = src_ref[rows, :].astype(BF16)
        return 0

    lax.fori_loop(0, src_ref.shape[0] // CAST_ROWS, body, 0)


def _experts_up_kernel(te_ref, first_ref, nu_ref, xs_ref, wg_ref, wu_ref, hid_ref, wgb_ref, wub_ref):
    del te_ref
    t = pl.program_id(0)

    @pl.when(first_ref[t] == 1)
    def _():
        _cast_weight(wg_ref, wgb_ref)
        _cast_weight(wu_ref, wub_ref)

    @pl.when(t < nu_ref[0])
    def _():
        c = D_MODEL // 2
        x_lo, x_hi = _unpack_halves(xs_ref[...])
        gate = (jnp.dot(x_lo, wgb_ref[:c, :], preferred_element_type=F32)
                + jnp.dot(x_hi, wgb_ref[c:, :], preferred_element_type=F32))
        up = (jnp.dot(x_lo, wub_ref[:c, :], preferred_element_type=F32)
              + jnp.dot(x_hi, wub_ref[c:, :], preferred_element_type=F32))
        hid_ref[...] = (_silu(gate) * up).astype(BF16)

    @pl.when(t >= nu_ref[0])
    def _():
        hid_ref[...] = jnp.zeros_like(hid_ref)


def _experts_down_kernel(te_ref, first_ref, nu_ref, hid_ref, wd_ref, ys_ref, wdb_ref):
    del te_ref
    t = pl.program_id(0)

    @pl.when(first_ref[t] == 1)
    def _():
        _cast_weight(wd_ref, wdb_ref)

    @pl.when(t < nu_ref[0])
    def _():
        ys_ref[...] = jnp.dot(hid_ref[...], wdb_ref[...], preferred_element_type=F32)

    @pl.when(t >= nu_ref[0])
    def _():
        ys_ref[...] = jnp.zeros_like(ys_ref)


def _experts(tile_expert, tile_first, n_used, xs, wg, wu, wd):
    n_slots = xs.shape[0]
    w_spec = lambda shape: pl.BlockSpec((None,) + shape, lambda t, te, fi, nu: (te[t], 0, 0))
    row_spec = lambda width: pl.BlockSpec((TM_EXP, width), lambda t, te, fi, nu: (t, 0))
    big_vmem = pltpu.CompilerParams(dimension_semantics=("arbitrary",), vmem_limit_bytes=EXPERT_VMEM_LIMIT)
    hid = pl.pallas_call(
        _experts_up_kernel,
        grid_spec=pltpu.PrefetchScalarGridSpec(
            num_scalar_prefetch=3,
            grid=(n_slots // TM_EXP,),
            in_specs=[row_spec(D_MODEL // 2), w_spec((D_MODEL, EXPERT_HIDDEN)), w_spec((D_MODEL, EXPERT_HIDDEN))],
            out_specs=row_spec(EXPERT_HIDDEN),
            scratch_shapes=[pltpu.VMEM((D_MODEL, EXPERT_HIDDEN), BF16)] * 2,
        ),
        out_shape=jax.ShapeDtypeStruct((n_slots, EXPERT_HIDDEN), BF16),
        compiler_params=big_vmem,
        name="experts_up",
    )(tile_expert, tile_first, n_used, xs, wg, wu)
    return pl.pallas_call(
        _experts_down_kernel,
        grid_spec=pltpu.PrefetchScalarGridSpec(
            num_scalar_prefetch=3,
            grid=(n_slots // TM_EXP,),
            in_specs=[row_spec(EXPERT_HIDDEN), w_spec((EXPERT_HIDDEN, D_MODEL))],
            out_specs=row_spec(D_MODEL),
            scratch_shapes=[pltpu.VMEM((EXPERT_HIDDEN, D_MODEL), BF16)],
        ),
        out_shape=jax.ShapeDtypeStruct((n_slots, D_MODEL), F32),
        compiler_params=big_vmem,
        name="experts_down",
    )(tile_expert, tile_first, n_used, hid, wd)


def _combine_kernel(d0_ref, d1_ref, h_ref, meta_ref, ys_ref, o_ref, buf_ref, sem):
    tm = h_ref.shape[0]
    base = pl.program_id(0) * tm

    def start(r, _):
        _row_copy(ys_ref, d0_ref[base + r], buf_ref.at[0], r, sem).start()
        _row_copy(ys_ref, d1_ref[base + r], buf_ref.at[1], r, sem).start()
        return 0

    lax.fori_loop(0, tm, start, 0)

    def wait(r, _):
        _row_copy(ys_ref, d0_ref[base + r], buf_ref.at[0], r, sem).wait()
        _row_copy(ys_ref, d1_ref[base + r], buf_ref.at[1], r, sem).wait()
        return 0

    lax.fori_loop(0, tm, wait, 0)
    meta = meta_ref[...]
    o_ref[...] = h_ref[...] + (buf_ref[0] * meta[:, 2:3] + buf_ref[1] * meta[:, 3:4])


def _combine(dest0, dest1, h, meta, ys):
    n = h.shape[0]
    grid_spec = pltpu.PrefetchScalarGridSpec(
        num_scalar_prefetch=2,
        grid=(n // TM_COMB,),
        in_specs=[
            pl.BlockSpec((TM_COMB, D_MODEL), lambda i, d0, d1: (i, 0)),
            pl.BlockSpec((TM_COMB, LANES), lambda i, d0, d1: (i, 0)),
            pl.BlockSpec(memory_space=pl.ANY),
        ],
        out_specs=pl.BlockSpec((TM_COMB, D_MODEL), lambda i, d0, d1: (i, 0)),
        scratch_shapes=[pltpu.VMEM((2, TM_COMB, D_MODEL), F32), pltpu.SemaphoreType.DMA(())],
    )
    return pl.pallas_call(
        _combine_kernel,
        grid_spec=grid_spec,
        out_shape=jax.ShapeDtypeStruct((n, D_MODEL), F32),
        compiler_params=_cparams(("arbitrary",)),
        name="combine",
    )(dest0, dest1, h, meta, ys)


def _lambda_init(layer_idx):
    return 0.8 - 0.6 * math.exp(-0.3 * layer_idx)


def _pad_lanes(v, width=LANES):
    return jnp.pad(v, ((0, 0), (0, width - v.shape[1])))


def _layer(l, x2, pos_col, bsz, seq, ln1_w, w_in, conv_w, conv_b, dt_bias, a_log, d_skip, ssd_norm_w,
           q_norm_w, k_norm_w, lambda_q1, lambda_k1, lambda_q2, lambda_k2, subln_w, w_out, ln2_w,
           w_router_group, b_router_group, w_router_expert, b_router_expert, w_gate, w_up, w_down):
    n = x2.shape[0]
    c_z, c_xbc, c_dt = SSD_WIDTH, SSD_WIDTH + SSD_CONV_DIM, SSD_WIDTH + SSD_CONV_DIM + SSD_HEADS
    c_q, c_k = c_dt + ATTN_WIDTH, c_dt + 2 * ATTN_WIDTH
    w_main = jnp.concatenate([w_in[:, :c_z], w_in[:, c_dt:c_q], w_in[:, c_q:c_k], w_in[:, c_z:c_xbc]],
                             axis=1).astype(BF16)
    w_vt = w_in[:, c_k:].T.astype(BF16)
    w_dt = _pad_lanes(w_in[:, c_xbc:c_dt]).astype(BF16)

    u, v_t, dt_raw = _in_proj(x2, ln1_w[None, :], w_main, w_vt, w_dt)

    a_neg = _pad_lanes(-jnp.exp(a_log.astype(F32))[None, :])
    y_ssd = _ssd(u, dt_raw, conv_w, conv_b[None, :], _pad_lanes(dt_bias[None, :]), a_neg,
                 jnp.repeat(d_skip, SSD_HEAD_DIM)[None, :], ssd_norm_w[None, :], bsz, seq)

    inv_freq = jnp.power(ROPE_THETA, -jnp.arange(0, ROPE_DIM, 2, dtype=F32) / ROPE_DIM)
    d = jnp.arange(LANES) % ATTN_QK_DIM
    invf_lanes = jnp.where(d < ROPE_DIM, inv_freq[d % (ROPE_DIM // 2)], 0.0)[None, :]
    seg_ones = (jnp.arange(LANES)[:, None] // ATTN_QK_DIM == jnp.arange(LANES)[None, :] // ATTN_QK_DIM).astype(BF16)
    qp, kp = _qk_prep(u, pos_col, invf_lanes, jnp.tile(q_norm_w, 2)[None, :], jnp.tile(k_norm_w, 2)[None, :], seg_ones)

    lam_vecs = jnp.stack([lambda_q1, lambda_k1, lambda_q2, lambda_k2]).astype(F32)
    y_att = _attention(qp, kp, v_t, lam_vecs, subln_w[:, None], bsz, seq, _lambda_init(l))

    w_out_b = w_out.astype(BF16)
    w_router_t = _pad_lanes(jnp.concatenate([w_router_group, w_router_expert], axis=1)).T
    wr_hi = w_router_t.astype(BF16)
    wr_lo = (w_router_t - wr_hi.astype(F32)).astype(BF16)
    b_router = _pad_lanes(jnp.concatenate([b_router_group, b_router_expert])[None, :])
    h, hnp, meta = _out_proj(x2, y_ssd, y_att, w_out_b[:SSD_WIDTH], w_out_b[SSD_WIDTH:], ln2_w[None, :],
                             wr_hi, wr_lo, b_router)

    dest, cnt = _rank(meta)
    dest0, dest1 = dest[:, 0], dest[:, 1]
    counts = cnt[0, :N_EXPERTS].astype(I32)
    tiles_per_expert = (counts + TM_EXP - 1) // TM_EXP
    tile_end = jnp.cumsum(tiles_per_expert)
    n_slots = (n * TOP_K + N_EXPERTS * (TM_EXP - 1)) // TM_EXP * TM_EXP
    tile_ids = jnp.arange(n_slots // TM_EXP, dtype=I32)
    tile_expert = jnp.minimum(jnp.sum(tile_ids[:, None] >= tile_end[None, :], axis=1), N_EXPERTS - 1).astype(I32)
    n_used = tile_end[-1:].astype(I32)
    tile_first = jnp.concatenate([jnp.ones((1,), I32), (tile_expert[1:] != tile_expert[:-1]).astype(I32)])

    xs = _dispatch(dest0, dest1, hnp, jnp.zeros((n_slots, D_MODEL // 2), U32))
    ys = _experts(tile_expert, tile_first, n_used, xs, w_gate, w_up, w_down)
    return _combine(dest0, dest1, h, meta, ys)


def kernel(x, positions, ln1_w, w_in, conv_w, conv_b, dt_bias, a_log, d_skip, ssd_norm_w, q_norm_w, k_norm_w,
           lambda_q1, lambda_k1, lambda_q2, lambda_k2, subln_w, w_out, ln2_w, w_router_group, b_router_group,
           w_router_expert, b_router_expert, w_gate, w_up, w_down):
    bsz, seq, d = x.shape
    assert d == D_MODEL and seq % TQ == 0 and (bsz * seq) % TM_IN == 0
    x2 = x.reshape(bsz * seq, d)
    pos_col = positions.astype(F32).reshape(bsz * seq, 1)
    params = (ln1_w, w_in, conv_w, conv_b, dt_bias, a_log, d_skip, ssd_norm_w, q_norm_w, k_norm_w,
              lambda_q1, lambda_k1, lambda_q2, lambda_k2, subln_w, w_out, ln2_w, w_router_group, b_router_group,
              w_router_expert, b_router_expert, w_gate, w_up, w_down)
    for l in range(ln1_w.shape[0]):
        x2 = _layer(l, x2, pos_col, bsz, seq, *[p[l] for p in params])
    return x2.reshape(bsz, seq, d)
```

```python
import functools
import math

import jax
import jax.numpy as jnp
from jax import lax
from jax.experimental import pallas as pl
from jax.experimental.pallas import tpu as pltpu

F32 = jnp.float32
BF16 = jnp.bfloat16
I32 = jnp.int32
U32 = jnp.uint32
HIGHEST = lax.Precision.HIGHEST

D_MODEL = 2048
SSD_WIDTH = 1024
ATTN_WIDTH = 1024
SSD_HEAD_DIM = 64
SSD_HEADS = 16
SSD_GROUPS = 2
SSD_HEADS_PER_GROUP = SSD_HEADS // SSD_GROUPS
SSD_STATE = 128
SSD_CONV = 4
SSD_CHUNK = 128
SSD_CONV_DIM = SSD_WIDTH + 2 * SSD_GROUPS * SSD_STATE
ATTN_V_DIM = 128
ATTN_HEADS = 8
ATTN_QK_DIM = 64
ROPE_THETA = 500000.0
ROPE_DIM = 16
N_EXPERT_GROUPS = 4
EXPERTS_PER_GROUP = 8
N_EXPERTS = 32
TOP_K = 2
EXPERT_HIDDEN = 1024
EPS = 1e-6

LANES = 128
NEG_INF = float("-inf")

TM_IN = 512
TN_IN = 1536
TM_QK = 512
TQ = 256
ATTN_HB = 8
TM_OUT = 256
TM_RANK = 512
TM_EXP = 256
TM_DISP = 256
TM_COMB = 256
U_COLS = SSD_WIDTH + 2 * ATTN_WIDTH + SSD_CONV_DIM
VMEM_LIMIT = 52 * 1024 * 1024
EXPERT_VMEM_LIMIT = 58 * 1024 * 1024


def _cparams(sem):
    return pltpu.CompilerParams(dimension_semantics=sem, vmem_limit_bytes=VMEM_LIMIT)


def _silu(x):
    return x * (1.0 / (1.0 + jnp.exp(-x)))


def _softplus(x):
    return jnp.maximum(x, 0.0) + jnp.log(1.0 + jnp.exp(-jnp.abs(x)))


def _inproj_kernel(x_ref, lnw_ref, w_ref, wvt_ref, wdt_ref, u_ref, vt_ref, dt_ref, xn_ref):
    @pl.when(pl.program_id(1) == 0)
    def _():
        x = x_ref[...]
        ms = jnp.mean(x * x, axis=-1, keepdims=True)
        xn = (x * lax.rsqrt(ms + EPS) * lnw_ref[...]).astype(BF16)
        xn_ref[...] = xn
        dt_ref[...] = jnp.dot(xn, wdt_ref[...], preferred_element_type=F32)
        vt_ref[...] = lax.dot_general(wvt_ref[...], xn, (((1,), (1,)), ((), ())),
                                      preferred_element_type=F32).astype(BF16)

    u_ref[...] = jnp.dot(xn_ref[...], w_ref[...], preferred_element_type=F32).astype(BF16)


def _in_proj(x2, ln_w, w_main, w_vt, w_dt):
    n = x2.shape[0]
    return pl.pallas_call(
        _inproj_kernel,
        grid=(n // TM_IN, U_COLS // TN_IN),
        in_specs=[
            pl.BlockSpec((TM_IN, D_MODEL), lambda i, j: (i, 0)),
            pl.BlockSpec((1, D_MODEL), lambda i, j: (0, 0)),
            pl.BlockSpec((D_MODEL, TN_IN), lambda i, j: (0, j)),
            pl.BlockSpec((ATTN_WIDTH, D_MODEL), lambda i, j: (0, 0)),
            pl.BlockSpec((D_MODEL, LANES), lambda i, j: (0, 0)),
        ],
        out_specs=[
            pl.BlockSpec((TM_IN, TN_IN), lambda i, j: (i, j)),
            pl.BlockSpec((ATTN_WIDTH, TM_IN), lambda i, j: (0, i)),
            pl.BlockSpec((TM_IN, LANES), lambda i, j: (i, 0)),
        ],
        out_shape=[
            jax.ShapeDtypeStruct((n, U_COLS), BF16),
            jax.ShapeDtypeStruct((ATTN_WIDTH, n), BF16),
            jax.ShapeDtypeStruct((n, LANES), F32),
        ],
        scratch_shapes=[pltpu.VMEM((TM_IN, D_MODEL), BF16)],
        compiler_params=_cparams(("parallel", "arbitrary")),
        name="in_proj",
    )(x2, ln_w, w_main, w_vt, w_dt)


def _ssd_kernel(z_ref, xbc_ref, dt_ref, convw_ref, convb_ref, dtb_ref, aneg_ref, dskip_ref, normw_ref,
                y_ref, xp_ref, st_ref, yacc_ref):
    L = SSD_CHUNK
    P = SSD_HEAD_DIM

    @pl.when(pl.program_id(1) == 0)
    def _():
        xp_ref[0:8, :] = jnp.zeros((8, SSD_CONV_DIM), F32)
        st_ref[...] = jnp.zeros_like(st_ref)

    xp_ref[8:8 + L, :] = xbc_ref[...].astype(F32)
    acc = jnp.broadcast_to(convb_ref[...], (L, SSD_CONV_DIM))
    for k in range(SSD_CONV):
        acc = acc + xp_ref[5 + k:5 + k + L, :] * convw_ref[k:k + 1, :]
    xp_ref[0:8, :] = xp_ref[L:L + 8, :]
    xc = _silu(acc)

    dt = _softplus(dt_ref[...] + dtb_ref[...])
    a = dt * aneg_ref[...]
    row = lax.broadcasted_iota(I32, (L, L), 0)
    col = lax.broadcasted_iota(I32, (L, L), 1)
    causal = row >= col
    a_cs = jnp.dot(causal.astype(F32), a, precision=HIGHEST, preferred_element_type=F32)
    a_last = a_cs[L - 1:L, :]
    ea = jnp.exp(a_cs)
    dsdt = jnp.exp(a_last - a_cs) * dt
    cd = jnp.exp(a_last)
    a_cs_t = a_cs.T
    dt_t = dt.T
    dsdt_t = dsdt.T

    for g in range(SSD_GROUPS):
        b_g = xc[:, SSD_WIDTH + g * SSD_STATE:SSD_WIDTH + (g + 1) * SSD_STATE]
        c_off = SSD_WIDTH + SSD_GROUPS * SSD_STATE
        c_g = xc[:, c_off + g * SSD_STATE:c_off + (g + 1) * SSD_STATE]
        cb = lax.dot_general(c_g.astype(BF16), b_g.astype(BF16), (((1,), (1,)), ((), ())),
                             preferred_element_type=F32)
        b_gt = b_g.T
        for hh in range(SSD_HEADS_PER_GROUP):
            h = g * SSD_HEADS_PER_GROUP + hh
            xs_h = xc[:, h * P:(h + 1) * P].astype(BF16)
            seg = a_cs[:, h:h + 1] - a_cs_t[h:h + 1, :]
            dec = jnp.exp(jnp.where(causal, seg, NEG_INF))
            m = (cb * dec * dt_t[h:h + 1, :]).astype(BF16)
            c_s = (c_g * ea[:, h:h + 1]).astype(BF16)
            s_prev = st_ref[h]
            lhs = jnp.concatenate([m, c_s], axis=1)
            rhs = jnp.concatenate([xs_h, s_prev.astype(BF16)], axis=0)
            yacc_ref[:, h * P:(h + 1) * P] = jnp.dot(lhs, rhs, preferred_element_type=F32)
            bw = (b_gt * dsdt_t[h:h + 1, :]).astype(BF16)
            st_ref[h] = s_prev * cd[:, h:h + 1] + jnp.dot(bw, xs_h, preferred_element_type=F32)

    y = yacc_ref[...] + xc[:, :SSD_WIDTH] * dskip_ref[...]
    y = y * _silu(z_ref[...].astype(F32))
    gw = SSD_WIDTH // SSD_GROUPS
    for g in range(SSD_GROUPS):
        yg = y[:, g * gw:(g + 1) * gw]
        ms = jnp.mean(yg * yg, axis=-1, keepdims=True)
        y_ref[:, g * gw:(g + 1) * gw] = (yg * lax.rsqrt(ms + EPS) * normw_ref[:, g * gw:(g + 1) * gw]).astype(BF16)


def _ssd(u, dt_raw, conv_w, conv_b, dt_bias, a_neg, dskip_lanes, norm_w, bsz, seq):
    n = u.shape[0]
    nc = seq // SSD_CHUNK
    xbc_blk = (SSD_WIDTH + 2 * ATTN_WIDTH) // SSD_CONV_DIM
    full = lambda shape: pl.BlockSpec(shape, lambda b, c: (0, 0))
    return pl.pallas_call(
        _ssd_kernel,
        grid=(bsz, nc),
        in_specs=[
            pl.BlockSpec((SSD_CHUNK, SSD_WIDTH), lambda b, c: (b * nc + c, 0)),
            pl.BlockSpec((SSD_CHUNK, SSD_CONV_DIM), lambda b, c: (b * nc + c, xbc_blk)),
            pl.BlockSpec((SSD_CHUNK, LANES), lambda b, c: (b * nc + c, 0)),
            full((SSD_CONV, SSD_CONV_DIM)),
            full((1, SSD_CONV_DIM)),
            full((1, LANES)),
            full((1, LANES)),
            full((1, SSD_WIDTH)),
            full((1, SSD_WIDTH)),
        ],
        out_specs=pl.BlockSpec((SSD_CHUNK, SSD_WIDTH), lambda b, c: (b * nc + c, 0)),
        out_shape=jax.ShapeDtypeStruct((n, SSD_WIDTH), BF16),
        scratch_shapes=[
            pltpu.VMEM((SSD_CHUNK + 8, SSD_CONV_DIM), F32),
            pltpu.VMEM((SSD_HEADS, SSD_STATE, SSD_HEAD_DIM), F32),
            pltpu.VMEM((SSD_CHUNK, SSD_WIDTH), F32),
        ],
        compiler_params=_cparams(("parallel", "arbitrary")),
        name="ssd",
    )(u, u, dt_raw, conv_w, conv_b, dt_bias, a_neg, dskip_lanes, norm_w)


def _qkprep_kernel(q_ref, k_ref, pos_ref, invf_ref, qw_ref, kw_ref, ones_ref, qo_ref, ko_ref):
    tm = q_ref.shape[0]
    ang = pos_ref[...] * invf_ref[...]
    cs = jnp.cos(ang)
    sn = jnp.sin(ang)
    d = lax.broadcasted_iota(I32, (tm, LANES), 1) & (ATTN_QK_DIM - 1)
    half = ROPE_DIM // 2
    s_lo = jnp.where(d < half, -sn, 0.0)
    s_hi = jnp.where((d >= half) & (d < ROPE_DIM), sn, 0.0)
    scale = 1.0 / math.sqrt(ATTN_QK_DIM)
    for src, w_ref, dst, mul in ((q_ref, qw_ref, qo_ref, scale), (k_ref, kw_ref, ko_ref, 1.0)):
        for hb in range(ATTN_HEADS):
            x = src[:, hb * LANES:(hb + 1) * LANES].astype(F32)
            ss = jnp.dot((x * x).astype(BF16), ones_ref[...], preferred_element_type=F32)
            xn = x * lax.rsqrt(ss * (1.0 / ATTN_QK_DIM) + EPS) * w_ref[...]
            out = xn * cs + pltpu.roll(xn, LANES - half, 1) * s_lo + pltpu.roll(xn, half, 1) * s_hi
            dst[:, hb * LANES:(hb + 1) * LANES] = (out * mul).astype(BF16)


def _qk_prep(u, pos_col, invf_lanes, qw_lanes, kw_lanes, seg_ones):
    n = u.shape[0]
    full = lambda shape: pl.BlockSpec(shape, lambda i: (0, 0))
    return pl.pallas_call(
        _qkprep_kernel,
        grid=(n // TM_QK,),
        in_specs=[
            pl.BlockSpec((TM_QK, ATTN_WIDTH), lambda i: (i, 1)),
            pl.BlockSpec((TM_QK, ATTN_WIDTH), lambda i: (i, 2)),
            pl.BlockSpec((TM_QK, 1), lambda i: (i, 0)),
            full((1, LANES)), full((1, LANES)), full((1, LANES)), full((LANES, LANES)),
        ],
        out_specs=[pl.BlockSpec((TM_QK, ATTN_WIDTH), lambda i: (i, 0))] * 2,
        out_shape=[jax.ShapeDtypeStruct((n, ATTN_WIDTH), BF16)] * 2,
        compiler_params=_cparams(("parallel",)),
        name="qk_prep",
    )(u, u, pos_col, invf_lanes, qw_lanes, kw_lanes, seg_ones)


def _attn_kernel(q_ref, k_ref, vt_ref, lamv_ref, subw_ref, o_ref, acc_ref, *, lam_init):
    qi = pl.program_id(2)
    lane = lax.broadcasted_iota(I32, (TQ, LANES), 1)
    qs = []
    for hb in range(ATTN_HB):
        q = q_ref[:, hb * LANES:(hb + 1) * LANES]
        zero = jnp.zeros_like(q)
        qs.append(jnp.concatenate([jnp.where(lane < ATTN_QK_DIM, q, zero),
                                   jnp.where(lane >= ATTN_QK_DIM, q, zero)], axis=0))
    acc_ref[...] = jnp.zeros_like(acc_ref)
    kv_idx = lax.broadcasted_iota(I32, (TQ, 2 * TQ), 0)
    q_idx = lax.broadcasted_iota(I32, (TQ, 2 * TQ), 1) & (TQ - 1)
    nt = (((1,), (1,)), ((), ()))

    def block(j, carry, masked):
        off = pl.multiple_of(j * TQ, TQ)
        ss = []
        for hb in range(ATTN_HB):
            kb = k_ref[pl.ds(off, TQ), hb * LANES:(hb + 1) * LANES]
            ss.append(lax.dot_general(kb, qs[hb], nt, preferred_element_type=F32))
        new, ps, alphas = [], [], []
        for hb in range(ATTN_HB):
            m_old, l_old = carry[2 * hb], carry[2 * hb + 1]
            s = ss[hb]
            if masked:
                s = jnp.where(kv_idx <= q_idx, s, NEG_INF)
            m_new = jnp.maximum(m_old, jnp.max(s, axis=0, keepdims=True))
            alpha = jnp.exp(m_old - m_new)
            p = jnp.exp(s - m_new)
            new += [m_new, alpha * l_old + jnp.sum(p, axis=0, keepdims=True)]
            ps.append(p.astype(BF16))
            alphas.append(alpha)
        pvs = []
        for hb in range(ATTN_HB):
            vb = vt_ref[hb * LANES:(hb + 1) * LANES, pl.ds(off, TQ)]
            pvs.append(jnp.dot(vb, ps[hb], preferred_element_type=F32))
        for hb in range(ATTN_HB):
            acc_ref[hb] = alphas[hb] * acc_ref[hb] + pvs[hb]
        return tuple(new)

    init = (jnp.full((1, 2 * TQ), NEG_INF, F32), jnp.zeros((1, 2 * TQ), F32)) * ATTN_HB
    carry = lax.fori_loop(0, qi, lambda j, cr: block(j, cr, False), init)
    carry = block(qi, carry, True)

    lv = lamv_ref[...]
    lam = (jnp.exp(jnp.sum(lv[0:1] * lv[1:2], axis=1, keepdims=True))
           - jnp.exp(jnp.sum(lv[2:3] * lv[3:4], axis=1, keepdims=True)) + lam_init)
    for hb in range(ATTN_HB):
        o2 = acc_ref[hb] * (1.0 / carry[2 * hb + 1])
        o_t = o2[:, :TQ] - lam * o2[:, TQ:]
        ms = jnp.mean(o_t * o_t, axis=0, keepdims=True)
        o_t = o_t * lax.rsqrt(ms + EPS) * subw_ref[...] * (1.0 - lam_init)
        o_ref[:, hb * LANES:(hb + 1) * LANES] = o_t.T.astype(BF16)


def _attention(qp, kp, v_t, lam_vecs, subw_col, bsz, seq, lam_init):
    n = qp.shape[0]
    nq = seq // TQ
    w = ATTN_HB * ATTN_V_DIM
    return pl.pallas_call(
        functools.partial(_attn_kernel, lam_init=lam_init),
        grid=(bsz, ATTN_HEADS // ATTN_HB, nq),
        in_specs=[
            pl.BlockSpec((TQ, w), lambda b, h, i: (b * nq + i, h)),
            pl.BlockSpec((seq, w), lambda b, h, i: (b, h)),
            pl.BlockSpec((w, seq), lambda b, h, i: (h, b)),
            pl.BlockSpec((4, ATTN_QK_DIM), lambda b, h, i: (0, 0)),
            pl.BlockSpec((ATTN_V_DIM, 1), lambda b, h, i: (0, 0)),
        ],
        out_specs=pl.BlockSpec((TQ, w), lambda b, h, i: (b * nq + i, h)),
        out_shape=jax.ShapeDtypeStruct((n, ATTN_WIDTH), BF16),
        scratch_shapes=[pltpu.VMEM((ATTN_HB, ATTN_V_DIM, 2 * TQ), F32)],
        compiler_params=_cparams(("parallel", "parallel", "arbitrary")),
        name="attn",
    )(qp, kp, v_t, lam_vecs, subw_col)


def _pack_halves(x):
    c = x.shape[1] // 2
    lo = pltpu.bitcast(x[:, :c].astype(BF16).astype(F32), U32) >> 16
    hi = pltpu.bitcast(x[:, c:].astype(BF16).astype(F32), U32) & jnp.uint32(0xFFFF0000)
    return hi | lo


def _unpack_halves(w):
    lo = pltpu.bitcast(w << 16, F32).astype(BF16)
    hi = pltpu.bitcast(w & jnp.uint32(0xFFFF0000), F32).astype(BF16)
    return lo, hi


def _outproj_kernel(x_ref, ys_ref, ya_ref, wos_ref, woa_ref, ln2_ref, wrh_ref, wrl_ref, br_ref,
                    h_ref, hnp_ref, meta_ref):
    tm = x_ref.shape[0]
    h = (x_ref[...]
         + jnp.dot(ys_ref[...], wos_ref[...], preferred_element_type=F32)
         + jnp.dot(ya_ref[...], woa_ref[...], preferred_element_type=F32))
    h_ref[...] = h
    ms = jnp.mean(h * h, axis=-1, keepdims=True)
    hn = h * lax.rsqrt(ms + EPS) * ln2_ref[...]
    hnp_ref[...] = _pack_halves(hn)

    hn_hi = hn.astype(BF16)
    hn_lo = (hn - hn_hi.astype(F32)).astype(BF16)
    nt = (((1,), (1,)), ((), ()))
    lg_t = (lax.dot_general(wrh_ref[...], hn_hi, nt, preferred_element_type=F32)
            + lax.dot_general(wrh_ref[...], hn_lo, nt, preferred_element_type=F32)
            + lax.dot_general(wrl_ref[...], hn_hi, nt, preferred_element_type=F32))
    lg = lg_t.T + br_ref[...]
    lane = lax.broadcasted_iota(I32, (tm, LANES), 1).astype(F32)
    big = float(LANES)
    gmask = lane < N_EXPERT_GROUPS
    gl = jnp.where(gmask, lg, NEG_INF)
    gmax = jnp.max(gl, axis=1, keepdims=True)
    gsel = jnp.min(jnp.where(gl == gmax, lane, big), axis=1, keepdims=True)
    g_w = 1.0 / jnp.sum(jnp.exp(gl - gmax), axis=1, keepdims=True)
    eid = lane - N_EXPERT_GROUPS
    lo = gsel * EXPERTS_PER_GROUP
    emask = (eid >= lo) & (eid < lo + EXPERTS_PER_GROUP)
    el = jnp.where(emask, lg, NEG_INF)
    m1 = jnp.max(el, axis=1, keepdims=True)
    i1 = jnp.min(jnp.where(el == m1, eid, big), axis=1, keepdims=True)
    el2 = jnp.where(eid == i1, NEG_INF, el)
    m2 = jnp.max(el2, axis=1, keepdims=True)
    i2 = jnp.min(jnp.where(el2 == m2, eid, big), axis=1, keepdims=True)
    e2 = jnp.exp(m2 - m1)
    w1 = g_w / (1.0 + e2)
    w2 = g_w * e2 / (1.0 + e2)
    meta = jnp.where(lane == 0, i1, jnp.where(lane == 1, i2, jnp.where(lane == 2, w1, jnp.where(lane == 3, w2, 0.0))))
    meta_ref[...] = meta


def _out_proj(x2, y_ssd, y_att, wo_s, wo_a, ln2_w, wr_hi, wr_lo, b_router):
    n = x2.shape[0]
    full = lambda shape: pl.BlockSpec(shape, lambda i: (0, 0))
    return pl.pallas_call(
        _outproj_kernel,
        grid=(n // TM_OUT,),
        in_specs=[
            pl.BlockSpec((TM_OUT, D_MODEL), lambda i: (i, 0)),
            pl.BlockSpec((TM_OUT, SSD_WIDTH), lambda i: (i, 0)),
            pl.BlockSpec((TM_OUT, ATTN_WIDTH), lambda i: (i, 0)),
            full((SSD_WIDTH, D_MODEL)), full((ATTN_WIDTH, D_MODEL)),
            full((1, D_MODEL)), full((LANES, D_MODEL)), full((LANES, D_MODEL)), full((1, LANES)),
        ],
        out_specs=[
            pl.BlockSpec((TM_OUT, D_MODEL), lambda i: (i, 0)),
            pl.BlockSpec((TM_OUT, D_MODEL // 2), lambda i: (i, 0)),
            pl.BlockSpec((TM_OUT, LANES), lambda i: (i, 0)),
        ],
        out_shape=[
            jax.ShapeDtypeStruct((n, D_MODEL), F32),
            jax.ShapeDtypeStruct((n, D_MODEL // 2), U32),
            jax.ShapeDtypeStruct((n, LANES), F32),
        ],
        compiler_params=_cparams(("parallel",)),
        name="out_proj",
    )(x2, y_ssd, y_att, wo_s, wo_a, ln2_w, wr_hi, wr_lo, b_router)


def _rank_kernel(meta_ref, dest_ref, cnt_ref, run_ref, offs_ref):
    p = pl.program_id(0)
    i = pl.program_id(1)
    tm = meta_ref.shape[0]
    meta = meta_ref[...]
    lane = lax.broadcasted_iota(I32, (tm, LANES), 1).astype(F32)
    oh0 = (lane == meta[:, 0:1]).astype(F32)
    oh1 = (lane == meta[:, 1:2]).astype(F32)
    oh = oh0 + oh1
    colsum = jnp.sum(oh, axis=0, keepdims=True)

    @pl.when((p == 0) & (i == 0))
    def _():
        run_ref[...] = jnp.zeros_like(run_ref)

    @pl.when(p == 0)
    def _():
        run_ref[...] = run_ref[...] + colsum
        dest_ref[...] = jnp.zeros_like(dest_ref)
        cnt_ref[...] = jnp.broadcast_to(run_ref[...], cnt_ref.shape)

    @pl.when((p == 1) & (i == 0))
    def _():
        cnt = run_ref[...]
        padded = jnp.ceil(cnt * (1.0 / TM_EXP)) * TM_EXP
        r = lax.broadcasted_iota(I32, (LANES, LANES), 0)
        c = lax.broadcasted_iota(I32, (LANES, LANES), 1)
        excl = (r < c).astype(F32)
        offs = jnp.dot(jnp.broadcast_to(padded, (8, LANES)), excl, precision=HIGHEST, preferred_element_type=F32)
        offs_ref[...] = offs[0:1, :]
        cnt_ref[...] = jnp.broadcast_to(cnt, cnt_ref.shape)
        run_ref[...] = jnp.zeros_like(run_ref)

    @pl.when(p == 1)
    def _():
        r = lax.broadcasted_iota(I32, (tm, tm), 0)
        c = lax.broadcasted_iota(I32, (tm, tm), 1)
        before = jnp.dot((c < r).astype(BF16), oh.astype(BF16), preferred_element_type=F32)
        base = before + run_ref[...] + offs_ref[...]
        d0 = jnp.sum(oh0 * base, axis=1, keepdims=True)
        d1 = jnp.sum(oh1 * base, axis=1, keepdims=True)
        dest = jnp.where(lane == 0, d0, jnp.where(lane == 1, d1, 0.0))
        dest_ref[...] = dest.astype(I32)
        run_ref[...] = run_ref[...] + colsum


def _rank(meta):
    n = meta.shape[0]
    return pl.pallas_call(
        _rank_kernel,
        grid=(2, n // TM_RANK),
        in_specs=[pl.BlockSpec((TM_RANK, LANES), lambda p, i: (i, 0))],
        out_specs=[
            pl.BlockSpec((TM_RANK, LANES), lambda p, i: (i * p, 0)),
            pl.BlockSpec((8, LANES), lambda p, i: (0, 0)),
        ],
        out_shape=[
            jax.ShapeDtypeStruct((n, LANES), I32),
            jax.ShapeDtypeStruct((8, LANES), F32),
        ],
        scratch_shapes=[pltpu.VMEM((1, LANES), F32), pltpu.VMEM((1, LANES), F32)],
        compiler_params=_cparams(("arbitrary", "arbitrary")),
        name="rank",
    )(meta)


def _row_copy(src_ref, src_row, dst_ref, dst_row, sem):
    return pltpu.make_async_copy(src_ref.at[pl.ds(src_row, 1)], dst_ref.at[pl.ds(dst_row, 1)], sem)


def _dispatch_kernel(d0_ref, d1_ref, hnp_ref, xs_in_ref, xs_ref, sem):
    del xs_in_ref
    tm = hnp_ref.shape[0]
    base = pl.program_id(0) * tm

    def start(r, _):
        _row_copy(hnp_ref, r, xs_ref, d0_ref[base + r], sem).start()
        _row_copy(hnp_ref, r, xs_ref, d1_ref[base + r], sem).start()
        return 0

    lax.fori_loop(0, tm, start, 0)

    def wait(r, _):
        _row_copy(hnp_ref, r, xs_ref, d0_ref[base + r], sem).wait()
        _row_copy(hnp_ref, r, xs_ref, d1_ref[base + r], sem).wait()
        return 0

    lax.fori_loop(0, tm, wait, 0)


def _dispatch(dest0, dest1, hnp, xs_zero):
    n = hnp.shape[0]
    grid_spec = pltpu.PrefetchScalarGridSpec(
        num_scalar_prefetch=2,
        grid=(n // TM_DISP,),
        in_specs=[
            pl.BlockSpec((TM_DISP, D_MODEL // 2), lambda i, d0, d1: (i, 0)),
            pl.BlockSpec(memory_space=pl.ANY),
        ],
        out_specs=pl.BlockSpec(memory_space=pl.ANY),
        scratch_shapes=[pltpu.SemaphoreType.DMA(())],
    )
    return pl.pallas_call(
        _dispatch_kernel,
        grid_spec=grid_spec,
        out_shape=jax.ShapeDtypeStruct(xs_zero.shape, xs_zero.dtype),
        input_output_aliases={3: 0},
        compiler_params=_cparams(("arbitrary",)),
        name="dispatch",
    )(dest0, dest1, hnp, xs_zero)


CAST_ROWS = 256


def _cast_weight(src_ref, dst_ref):
    def body(i, _):
        rows = pl.ds(pl.multiple_of(i * CAST_ROWS, CAST_ROWS), CAST_ROWS)
        dst_ref[rows, :] = src_ref[rows, :].astype(BF16)
        return 0

    lax.fori_loop(0, src_ref.shape[0] // CAST_ROWS, body, 0)


S_EXPERT, S_FIRST, S_SLOT, S_NEXT, S_HAS_NEXT, S_USED = range(6)


def _load_expert_weights(t, sched_ref, triples, sems):
    def copies(expert, slot):
        return [pltpu.make_async_copy(w.at[expert], stage.at[slot], sems.at[i, slot])
                for i, (w, stage, _) in enumerate(triples)]

    @pl.when(sched_ref[S_FIRST, t] == 1)
    def _():
        slot = sched_ref[S_SLOT, t]

        @pl.when(t == 0)
        def _():
            for cp in copies(sched_ref[S_EXPERT, t], slot):
                cp.start()

        for cp in copies(sched_ref[S_EXPERT, t], slot):
            cp.wait()

        @pl.when(sched_ref[S_HAS_NEXT, t] == 1)
        def _():
            for cp in copies(sched_ref[S_NEXT, t], 1 - slot):
                cp.start()

        for _, stage, dst in triples:
            _cast_weight(stage.at[slot], dst)


def _experts_up_kernel(sched_ref, xs_ref, wg_hbm, wu_hbm, hid_ref, wg_stage, wu_stage, wgb_ref, wub_ref, sems):
    t = pl.program_id(0)
    _load_expert_weights(t, sched_ref, [(wg_hbm, wg_stage, wgb_ref), (wu_hbm, wu_stage, wub_ref)], sems)

    @pl.when(t < sched_ref[S_USED, 0])
    def _():
        c = D_MODEL // 2
        x_lo, x_hi = _unpack_halves(xs_ref[...])
        gate = (jnp.dot(x_lo, wgb_ref[:c, :], preferred_element_type=F32)
                + jnp.dot(x_hi, wgb_ref[c:, :], preferred_element_type=F32))
        up = (jnp.dot(x_lo, wub_ref[:c, :], preferred_element_type=F32)
              + jnp.dot(x_hi, wub_ref[c:, :], preferred_element_type=F32))
        hid_ref[...] = (_silu(gate) * up).astype(BF16)

    @pl.when(t >= sched_ref[S_USED, 0])
    def _():
        hid_ref[...] = jnp.zeros_like(hid_ref)


def _experts_down_kernel(sched_ref, hid_ref, wd_hbm, ys_ref, wd_stage, wdb_ref, sems):
    t = pl.program_id(0)
    _load_expert_weights(t, sched_ref, [(wd_hbm, wd_stage, wdb_ref)], sems)

    @pl.when(t < sched_ref[S_USED, 0])
    def _():
        ys_ref[...] = jnp.dot(hid_ref[...], wdb_ref[...], preferred_element_type=F32)

    @pl.when(t >= sched_ref[S_USED, 0])
    def _():
        ys_ref[...] = jnp.zeros_like(ys_ref)


def _tile_schedule(counts, n_tiles):
    tiles_per_expert = (counts + TM_EXP - 1) // TM_EXP
    tile_end = jnp.cumsum(tiles_per_expert)
    ids = jnp.arange(n_tiles, dtype=I32)
    expert = jnp.minimum(jnp.sum(ids[:, None] >= tile_end[None, :], axis=1), N_EXPERTS - 1).astype(I32)
    first = jnp.concatenate([jnp.ones((1,), I32), (expert[1:] != expert[:-1]).astype(I32)])
    slot = (jnp.cumsum(first) - 1) % 2
    start_idx = jnp.where(first == 1, ids, n_tiles)
    next_start = jnp.concatenate([lax.cummin(start_idx, reverse=True)[1:], jnp.full((1,), n_tiles, I32)])
    has_next = (next_start < n_tiles).astype(I32)
    next_expert = jnp.sum(jnp.where(ids[None, :] == next_start[:, None], expert[None, :], 0), axis=1)
    used = jnp.full((n_tiles,), tile_end[-1], I32)
    return jnp.stack([expert, first, slot, next_expert, has_next, used]).astype(I32)


def _experts(sched, xs, wg, wu, wd):
    n_slots = xs.shape[0]
    row_spec = lambda width: pl.BlockSpec((TM_EXP, width), lambda t, sc: (t, 0))
    hbm = pl.BlockSpec(memory_space=pl.ANY)
    big_vmem = pltpu.CompilerParams(dimension_semantics=("arbitrary",), vmem_limit_bytes=EXPERT_VMEM_LIMIT)
    hid = pl.pallas_call(
        _experts_up_kernel,
        grid_spec=pltpu.PrefetchScalarGridSpec(
            num_scalar_prefetch=1,
            grid=(n_slots // TM_EXP,),
            in_specs=[row_spec(D_MODEL // 2), hbm, hbm],
            out_specs=row_spec(EXPERT_HIDDEN),
            scratch_shapes=[pltpu.VMEM((2, D_MODEL, EXPERT_HIDDEN), F32)] * 2
                           + [pltpu.VMEM((D_MODEL, EXPERT_HIDDEN), BF16)] * 2
                           + [pltpu.SemaphoreType.DMA((2, 2))],
        ),
        out_shape=jax.ShapeDtypeStruct((n_slots, EXPERT_HIDDEN), BF16),
        compiler_params=big_vmem,
        name="experts_up",
    )(sched, xs, wg, wu)
    return pl.pallas_call(
        _experts_down_kernel,
        grid_spec=pltpu.PrefetchScalarGridSpec(
            num_scalar_prefetch=1,
            grid=(n_slots // TM_EXP,),
            in_specs=[row_spec(EXPERT_HIDDEN), hbm],
            out_specs=row_spec(D_MODEL),
            scratch_shapes=[pltpu.VMEM((2, EXPERT_HIDDEN, D_MODEL), F32), pltpu.VMEM((EXPERT_HIDDEN, D_MODEL), BF16),
                            pltpu.SemaphoreType.DMA((1, 2))],
        ),
        out_shape=jax.ShapeDtypeStruct((n_slots, D_MODEL), F32),
        compiler_params=big_vmem,
        name="experts_down",
    )(sched, hid, wd)


def _combine_kernel(d0_ref, d1_ref, h_ref, meta_ref, ys_ref, o_ref, buf_ref, sem):
    tm = h_ref.shape[0]
    base = pl.program_id(0) * tm

    def start(r, _):
        _row_copy(ys_ref, d0_ref[base + r], buf_ref.at[0], r, sem).start()
        _row_copy(ys_ref, d1_ref[base + r], buf_ref.at[1], r, sem).start()
        return 0

    lax.fori_loop(0, tm, start, 0)

    def wait(r, _):
        _row_copy(ys_ref, d0_ref[base + r], buf_ref.at[0], r, sem).wait()
        _row_copy(ys_ref, d1_ref[base + r], buf_ref.at[1], r, sem).wait()
        return 0

    lax.fori_loop(0, tm, wait, 0)
    meta = meta_ref[...]
    o_ref[...] = h_ref[...] + (buf_ref[0] * meta[:, 2:3] + buf_ref[1] * meta[:, 3:4])


def _combine(dest0, dest1, h, meta, ys):
    n = h.shape[0]
    grid_spec = pltpu.PrefetchScalarGridSpec(
        num_scalar_prefetch=2,
        grid=(n // TM_COMB,),
        in_specs=[
            pl.BlockSpec((TM_COMB, D_MODEL), lambda i, d0, d1: (i, 0)),
            pl.BlockSpec((TM_COMB, LANES), lambda i, d0, d1: (i, 0)),
            pl.BlockSpec(memory_space=pl.ANY),
        ],
        out_specs=pl.BlockSpec((TM_COMB, D_MODEL), lambda i, d0, d1: (i, 0)),
        scratch_shapes=[pltpu.VMEM((2, TM_COMB, D_MODEL), F32), pltpu.SemaphoreType.DMA(())],
    )
    return pl.pallas_call(
        _combine_kernel,
        grid_spec=grid_spec,
        out_shape=jax.ShapeDtypeStruct((n, D_MODEL), F32),
        compiler_params=_cparams(("arbitrary",)),
        name="combine",
    )(dest0, dest1, h, meta, ys)


def _lambda_init(layer_idx):
    return 0.8 - 0.6 * math.exp(-0.3 * layer_idx)


def _pad_lanes(v, width=LANES):
    return jnp.pad(v, ((0, 0), (0, width - v.shape[1])))


def _layer(l, x2, pos_col, bsz, seq, ln1_w, w_in, conv_w, conv_b, dt_bias, a_log, d_skip, ssd_norm_w,
           q_norm_w, k_norm_w, lambda_q1, lambda_k1, lambda_q2, lambda_k2, subln_w, w_out, ln2_w,
           w_router_group, b_router_group, w_router_expert, b_router_expert, w_gate, w_up, w_down):
    n = x2.shape[0]
    c_z, c_xbc, c_dt = SSD_WIDTH, SSD_WIDTH + SSD_CONV_DIM, SSD_WIDTH + SSD_CONV_DIM + SSD_HEADS
    c_q, c_k = c_dt + ATTN_WIDTH, c_dt + 2 * ATTN_WIDTH
    w_main = jnp.concatenate([w_in[:, :c_z], w_in[:, c_dt:c_q], w_in[:, c_q:c_k], w_in[:, c_z:c_xbc]],
                             axis=1).astype(BF16)
    w_vt = w_in[:, c_k:].T.astype(BF16)
    w_dt = _pad_lanes(w_in[:, c_xbc:c_dt]).astype(BF16)

    u, v_t, dt_raw = _in_proj(x2, ln1_w[None, :], w_main, w_vt, w_dt)

    a_neg = _pad_lanes(-jnp.exp(a_log.astype(F32))[None, :])
    y_ssd = _ssd(u, dt_raw, conv_w, conv_b[None, :], _pad_lanes(dt_bias[None, :]), a_neg,
                 jnp.repeat(d_skip, SSD_HEAD_DIM)[None, :], ssd_norm_w[None, :], bsz, seq)

    inv_freq = jnp.power(ROPE_THETA, -jnp.arange(0, ROPE_DIM, 2, dtype=F32) / ROPE_DIM)
    d = jnp.arange(LANES) % ATTN_QK_DIM
    invf_lanes = jnp.where(d < ROPE_DIM, inv_freq[d % (ROPE_DIM // 2)], 0.0)[None, :]
    seg_ones = (jnp.arange(LANES)[:, None] // ATTN_QK_DIM == jnp.arange(LANES)[None, :] // ATTN_QK_DIM).astype(BF16)
    qp, kp = _qk_prep(u, pos_col, invf_lanes, jnp.tile(q_norm_w, 2)[None, :], jnp.tile(k_norm_w, 2)[None, :], seg_ones)

    lam_vecs = jnp.stack([lambda_q1, lambda_k1, lambda_q2, lambda_k2]).astype(F32)
    y_att = _attention(qp, kp, v_t, lam_vecs, subln_w[:, None], bsz, seq, _lambda_init(l))

    w_out_b = w_out.astype(BF16)
    w_router_t = _pad_lanes(jnp.concatenate([w_router_group, w_router_expert], axis=1)).T
    wr_hi = w_router_t.astype(BF16)
    wr_lo = (w_router_t - wr_hi.astype(F32)).astype(BF16)
    b_router = _pad_lanes(jnp.concatenate([b_router_group, b_router_expert])[None, :])
    h, hnp, meta = _out_proj(x2, y_ssd, y_att, w_out_b[:SSD_WIDTH], w_out_b[SSD_WIDTH:], ln2_w[None, :],
                             wr_hi, wr_lo, b_router)

    dest, cnt = _rank(meta)
    dest0, dest1 = dest[:, 0], dest[:, 1]
    counts = cnt[0, :N_EXPERTS].astype(I32)
    n_slots = (n * TOP_K + N_EXPERTS * (TM_EXP - 1)) // TM_EXP * TM_EXP
    sched = _tile_schedule(counts, n_slots // TM_EXP)

    xs = _dispatch(dest0, dest1, hnp, jnp.zeros((n_slots, D_MODEL // 2), U32))
    ys = _experts(sched, xs, w_gate, w_up, w_down)
    return _combine(dest0, dest1, h, meta, ys)


def kernel(x, positions, ln1_w, w_in, conv_w, conv_b, dt_bias, a_log, d_skip, ssd_norm_w, q_norm_w, k_norm_w,
           lambda_q1, lambda_k1, lambda_q2, lambda_k2, subln_w, w_out, ln2_w, w_router_group, b_router_group,
           w_router_expert, b_router_expert, w_gate, w_up, w_down):
    bsz, seq, d = x.shape
    assert d == D_MODEL and seq % TQ == 0 and (bsz * seq) % TM_IN == 0
    x2 = x.reshape(bsz * seq, d)
    pos_col = positions.astype(F32).reshape(bsz * seq, 1)
    params = (ln1_w, w_in, conv_w, conv_b, dt_bias, a_log, d_skip, ssd_norm_w, q_norm_w, k_norm_w,
              lambda_q1, lambda_k1, lambda_q2, lambda_k2, subln_w, w_out, ln2_w, w_router_group, b_router_group,
              w_router_expert, b_router_expert, w_gate, w_up, w_down)
    for l in range(ln1_w.shape[0]):
        x2 = _layer(l, x2, pos_col, bsz, seq, *[p[l] for p in params])
    return x2.reshape(bsz, seq, d)
```

```python
import functools
import math

import jax
import jax.numpy as jnp
from jax import lax
from jax.experimental import pallas as pl
from jax.experimental.pallas import tpu as pltpu

F32 = jnp.float32
BF16 = jnp.bfloat16
I32 = jnp.int32
U32 = jnp.uint32
HIGHEST = lax.Precision.HIGHEST

D_MODEL = 2048
SSD_WIDTH = 1024
ATTN_WIDTH = 1024
SSD_HEAD_DIM = 64
SSD_HEADS = 16
SSD_GROUPS = 2
SSD_HEADS_PER_GROUP = SSD_HEADS // SSD_GROUPS
SSD_STATE = 128
SSD_CONV = 4
SSD_CHUNK = 128
SSD_CONV_DIM = SSD_WIDTH + 2 * SSD_GROUPS * SSD_STATE
ATTN_V_DIM = 128
ATTN_HEADS = 8
ATTN_QK_DIM = 64
ROPE_THETA = 500000.0
ROPE_DIM = 16
N_EXPERT_GROUPS = 4
EXPERTS_PER_GROUP = 8
N_EXPERTS = 32
TOP_K = 2
EXPERT_HIDDEN = 1024
EPS = 1e-6

LANES = 128
NEG_INF = float("-inf")

TM_IN = 512
TN_IN = 1536
TM_QK = 512
TQ = 256
ATTN_HB = 8
TM_OUT = 256
TM_RANK = 512
TM_EXP = 256
TM_DISP = 256
TM_COMB = 256
U_COLS = SSD_WIDTH + 2 * ATTN_WIDTH + SSD_CONV_DIM
VMEM_LIMIT = 52 * 1024 * 1024
EXPERT_VMEM_LIMIT = 58 * 1024 * 1024


def _cparams(sem):
    return pltpu.CompilerParams(dimension_semantics=sem, vmem_limit_bytes=VMEM_LIMIT)


def _silu(x):
    return x * (1.0 / (1.0 + jnp.exp(-x)))


def _softplus(x):
    return jnp.maximum(x, 0.0) + jnp.log(1.0 + jnp.exp(-jnp.abs(x)))


def _inproj_kernel(x_ref, lnw_ref, w_ref, wvt_ref, wdt_ref, u_ref, vt_ref, dt_ref, xn_ref):
    @pl.when(pl.program_id(1) == 0)
    def _():
        x = x_ref[...]
        ms = jnp.mean(x * x, axis=-1, keepdims=True)
        xn = (x * lax.rsqrt(ms + EPS) * lnw_ref[...]).astype(BF16)
        xn_ref[...] = xn
        dt_ref[...] = jnp.dot(xn, wdt_ref[...], preferred_element_type=F32)
        vt_ref[...] = lax.dot_general(wvt_ref[...], xn, (((1,), (1,)), ((), ())),
                                      preferred_element_type=F32).astype(BF16)

    u_ref[...] = jnp.dot(xn_ref[...], w_ref[...], preferred_element_type=F32).astype(BF16)


def _in_proj(x2, ln_w, w_main, w_vt, w_dt):
    n = x2.shape[0]
    return pl.pallas_call(
        _inproj_kernel,
        grid=(n // TM_IN, U_COLS // TN_IN),
        in_specs=[
            pl.BlockSpec((TM_IN, D_MODEL), lambda i, j: (i, 0)),
            pl.BlockSpec((1, D_MODEL), lambda i, j: (0, 0)),
            pl.BlockSpec((D_MODEL, TN_IN), lambda i, j: (0, j)),
            pl.BlockSpec((ATTN_WIDTH, D_MODEL), lambda i, j: (0, 0)),
            pl.BlockSpec((D_MODEL, LANES), lambda i, j: (0, 0)),
        ],
        out_specs=[
            pl.BlockSpec((TM_IN, TN_IN), lambda i, j: (i, j)),
            pl.BlockSpec((ATTN_WIDTH, TM_IN), lambda i, j: (0, i)),
            pl.BlockSpec((TM_IN, LANES), lambda i, j: (i, 0)),
        ],
        out_shape=[
            jax.ShapeDtypeStruct((n, U_COLS), BF16),
            jax.ShapeDtypeStruct((ATTN_WIDTH, n), BF16),
            jax.ShapeDtypeStruct((n, LANES), F32),
        ],
        scratch_shapes=[pltpu.VMEM((TM_IN, D_MODEL), BF16)],
        compiler_params=_cparams(("parallel", "arbitrary")),
        name="in_proj",
    )(x2, ln_w, w_main, w_vt, w_dt)


def _ssd_kernel(z_ref, xbc_ref, dt_ref, convw_ref, convb_ref, dtb_ref, aneg_ref, dskip_ref, normw_ref,
                y_ref, xp_ref, st_ref, yacc_ref):
    L = SSD_CHUNK
    P = SSD_HEAD_DIM

    @pl.when(pl.program_id(1) == 0)
    def _():
        xp_ref[0:8, :] = jnp.zeros((8, SSD_CONV_DIM), F32)
        st_ref[...] = jnp.zeros_like(st_ref)

    xp_ref[8:8 + L, :] = xbc_ref[...].astype(F32)
    acc = jnp.broadcast_to(convb_ref[...], (L, SSD_CONV_DIM))
    for k in range(SSD_CONV):
        acc = acc + xp_ref[5 + k:5 + k + L, :] * convw_ref[k:k + 1, :]
    xp_ref[0:8, :] = xp_ref[L:L + 8, :]
    xc = _silu(acc)

    dt = _softplus(dt_ref[...] + dtb_ref[...])
    a = dt * aneg_ref[...]
    row = lax.broadcasted_iota(I32, (L, L), 0)
    col = lax.broadcasted_iota(I32, (L, L), 1)
    causal = row >= col
    a_cs = jnp.dot(causal.astype(F32), a, precision=HIGHEST, preferred_element_type=F32)
    a_last = a_cs[L - 1:L, :]
    ea = jnp.exp(a_cs)
    dsdt = jnp.exp(a_last - a_cs) * dt
    cd = jnp.exp(a_last)
    a_cs_t = a_cs.T
    dt_t = dt.T
    dsdt_t = dsdt.T

    for g in range(SSD_GROUPS):
        b_g = xc[:, SSD_WIDTH + g * SSD_STATE:SSD_WIDTH + (g + 1) * SSD_STATE]
        c_off = SSD_WIDTH + SSD_GROUPS * SSD_STATE
        c_g = xc[:, c_off + g * SSD_STATE:c_off + (g + 1) * SSD_STATE]
        cb = lax.dot_general(c_g.astype(BF16), b_g.astype(BF16), (((1,), (1,)), ((), ())),
                             preferred_element_type=F32)
        b_gt = b_g.T
        for hh in range(SSD_HEADS_PER_GROUP):
            h = g * SSD_HEADS_PER_GROUP + hh
            xs_h = xc[:, h * P:(h + 1) * P].astype(BF16)
            seg = a_cs[:, h:h + 1] - a_cs_t[h:h + 1, :]
            dec = jnp.exp(jnp.where(causal, seg, NEG_INF))
            m = (cb * dec * dt_t[h:h + 1, :]).astype(BF16)
            c_s = (c_g * ea[:, h:h + 1]).astype(BF16)
            s_prev = st_ref[h]
            lhs = jnp.concatenate([m, c_s], axis=1)
            rhs = jnp.concatenate([xs_h, s_prev.astype(BF16)], axis=0)
            yacc_ref[:, h * P:(h + 1) * P] = jnp.dot(lhs, rhs, preferred_element_type=F32)
            bw = (b_gt * dsdt_t[h:h + 1, :]).astype(BF16)
            st_ref[h] = s_prev * cd[:, h:h + 1] + jnp.dot(bw, xs_h, preferred_element_type=F32)

    y = yacc_ref[...] + xc[:, :SSD_WIDTH] * dskip_ref[...]
    y = y * _silu(z_ref[...].astype(F32))
    gw = SSD_WIDTH // SSD_GROUPS
    for g in range(SSD_GROUPS):
        yg = y[:, g * gw:(g + 1) * gw]
        ms = jnp.mean(yg * yg, axis=-1, keepdims=True)
        y_ref[:, g * gw:(g + 1) * gw] = (yg * lax.rsqrt(ms + EPS) * normw_ref[:, g * gw:(g + 1) * gw]).astype(BF16)


def _ssd(u, dt_raw, conv_w, conv_b, dt_bias, a_neg, dskip_lanes, norm_w, bsz, seq):
    n = u.shape[0]
    nc = seq // SSD_CHUNK
    xbc_blk = (SSD_WIDTH + 2 * ATTN_WIDTH) // SSD_CONV_DIM
    full = lambda shape: pl.BlockSpec(shape, lambda b, c: (0, 0))
    return pl.pallas_call(
        _ssd_kernel,
        grid=(bsz, nc),
        in_specs=[
            pl.BlockSpec((SSD_CHUNK, SSD_WIDTH), lambda b, c: (b * nc + c, 0)),
            pl.BlockSpec((SSD_CHUNK, SSD_CONV_DIM), lambda b, c: (b * nc + c, xbc_blk)),
            pl.BlockSpec((SSD_CHUNK, LANES), lambda b, c: (b * nc + c, 0)),
            full((SSD_CONV, SSD_CONV_DIM)),
            full((1, SSD_CONV_DIM)),
            full((1, LANES)),
            full((1, LANES)),
            full((1, SSD_WIDTH)),
            full((1, SSD_WIDTH)),
        ],
        out_specs=pl.BlockSpec((SSD_CHUNK, SSD_WIDTH), lambda b, c: (b * nc + c, 0)),
        out_shape=jax.ShapeDtypeStruct((n, SSD_WIDTH), BF16),
        scratch_shapes=[
            pltpu.VMEM((SSD_CHUNK + 8, SSD_CONV_DIM), F32),
            pltpu.VMEM((SSD_HEADS, SSD_STATE, SSD_HEAD_DIM), F32),
            pltpu.VMEM((SSD_CHUNK, SSD_WIDTH), F32),
        ],
        compiler_params=_cparams(("parallel", "arbitrary")),
        name="ssd",
    )(u, u, dt_raw, conv_w, conv_b, dt_bias, a_neg, dskip_lanes, norm_w)


def _qkprep_kernel(q_ref, k_ref, pos_ref, invf_ref, qw_ref, kw_ref, ones_ref, qo_ref, ko_ref):
    tm = q_ref.shape[0]
    ang = pos_ref[...] * invf_ref[...]
    cs = jnp.cos(ang)
    sn = jnp.sin(ang)
    d = lax.broadcasted_iota(I32, (tm, LANES), 1) & (ATTN_QK_DIM - 1)
    half = ROPE_DIM // 2
    s_lo = jnp.where(d < half, -sn, 0.0)
    s_hi = jnp.where((d >= half) & (d < ROPE_DIM), sn, 0.0)
    scale = 1.0 / math.sqrt(ATTN_QK_DIM)
    for src, w_ref, dst, mul in ((q_ref, qw_ref, qo_ref, scale), (k_ref, kw_ref, ko_ref, 1.0)):
        for hb in range(ATTN_HEADS):
            x = src[:, hb * LANES:(hb + 1) * LANES].astype(F32)
            ss = jnp.dot((x * x).astype(BF16), ones_ref[...], preferred_element_type=F32)
            xn = x * lax.rsqrt(ss * (1.0 / ATTN_QK_DIM) + EPS) * w_ref[...]
            out = xn * cs + pltpu.roll(xn, LANES - half, 1) * s_lo + pltpu.roll(xn, half, 1) * s_hi
            dst[:, hb * LANES:(hb + 1) * LANES] = (out * mul).astype(BF16)


def _qk_prep(u, pos_col, invf_lanes, qw_lanes, kw_lanes, seg_ones):
    n = u.shape[0]
    full = lambda shape: pl.BlockSpec(shape, lambda i: (0, 0))
    return pl.pallas_call(
        _qkprep_kernel,
        grid=(n // TM_QK,),
        in_specs=[
            pl.BlockSpec((TM_QK, ATTN_WIDTH), lambda i: (i, 1)),
            pl.BlockSpec((TM_QK, ATTN_WIDTH), lambda i: (i, 2)),
            pl.BlockSpec((TM_QK, 1), lambda i: (i, 0)),
            full((1, LANES)), full((1, LANES)), full((1, LANES)), full((LANES, LANES)),
        ],
        out_specs=[pl.BlockSpec((TM_QK, ATTN_WIDTH), lambda i: (i, 0))] * 2,
        out_shape=[jax.ShapeDtypeStruct((n, ATTN_WIDTH), BF16)] * 2,
        compiler_params=_cparams(("parallel",)),
        name="qk_prep",
    )(u, u, pos_col, invf_lanes, qw_lanes, kw_lanes, seg_ones)


def _attn_kernel(q_ref, k_ref, vt_ref, lamv_ref, subw_ref, o_ref, acc_ref, *, lam_init):
    qi = pl.program_id(2)
    lane = lax.broadcasted_iota(I32, (TQ, LANES), 1)
    qs = []
    for hb in range(ATTN_HB):
        q = q_ref[:, hb * LANES:(hb + 1) * LANES]
        zero = jnp.zeros_like(q)
        qs.append(jnp.concatenate([jnp.where(lane < ATTN_QK_DIM, q, zero),
                                   jnp.where(lane >= ATTN_QK_DIM, q, zero)], axis=0))
    acc_ref[...] = jnp.zeros_like(acc_ref)
    kv_idx = lax.broadcasted_iota(I32, (TQ, 2 * TQ), 0)
    q_idx = lax.broadcasted_iota(I32, (TQ, 2 * TQ), 1) & (TQ - 1)
    nt = (((1,), (1,)), ((), ()))

    def block(j, carry, masked):
        off = pl.multiple_of(j * TQ, TQ)
        ss = []
        for hb in range(ATTN_HB):
            kb = k_ref[pl.ds(off, TQ), hb * LANES:(hb + 1) * LANES]
            ss.append(lax.dot_general(kb, qs[hb], nt, preferred_element_type=F32))
        new, ps, alphas = [], [], []
        for hb in range(ATTN_HB):
            m_old, l_old = carry[2 * hb], carry[2 * hb + 1]
            s = ss[hb]
            if masked:
                s = jnp.where(kv_idx <= q_idx, s, NEG_INF)
            m_new = jnp.maximum(m_old, jnp.max(s, axis=0, keepdims=True))
            alpha = jnp.exp(m_old - m_new)
            p = jnp.exp(s - m_new)
            new += [m_new, alpha * l_old + jnp.sum(p, axis=0, keepdims=True)]
            ps.append(p.astype(BF16))
            alphas.append(alpha)
        pvs = []
        for hb in range(ATTN_HB):
            vb = vt_ref[hb * LANES:(hb + 1) * LANES, pl.ds(off, TQ)]
            pvs.append(jnp.dot(vb, ps[hb], preferred_element_type=F32))
        for hb in range(ATTN_HB):
            acc_ref[hb] = alphas[hb] * acc_ref[hb] + pvs[hb]
        return tuple(new)

    init = (jnp.full((1, 2 * TQ), NEG_INF, F32), jnp.zeros((1, 2 * TQ), F32)) * ATTN_HB
    carry = lax.fori_loop(0, qi, lambda j, cr: block(j, cr, False), init)
    carry = block(qi, carry, True)

    lv = lamv_ref[...]
    lam = (jnp.exp(jnp.sum(lv[0:1] * lv[1:2], axis=1, keepdims=True))
           - jnp.exp(jnp.sum(lv[2:3] * lv[3:4], axis=1, keepdims=True)) + lam_init)
    for hb in range(ATTN_HB):
        o2 = acc_ref[hb] * (1.0 / carry[2 * hb + 1])
        o_t = o2[:, :TQ] - lam * o2[:, TQ:]
        ms = jnp.mean(o_t * o_t, axis=0, keepdims=True)
        o_t = o_t * lax.rsqrt(ms + EPS) * subw_ref[...] * (1.0 - lam_init)
        o_ref[:, hb * LANES:(hb + 1) * LANES] = o_t.T.astype(BF16)


def _attention(qp, kp, v_t, lam_vecs, subw_col, bsz, seq, lam_init):
    n = qp.shape[0]
    nq = seq // TQ
    w = ATTN_HB * ATTN_V_DIM
    return pl.pallas_call(
        functools.partial(_attn_kernel, lam_init=lam_init),
        grid=(bsz, ATTN_HEADS // ATTN_HB, nq),
        in_specs=[
            pl.BlockSpec((TQ, w), lambda b, h, i: (b * nq + i, h)),
            pl.BlockSpec((seq, w), lambda b, h, i: (b, h)),
            pl.BlockSpec((w, seq), lambda b, h, i: (h, b)),
            pl.BlockSpec((4, ATTN_QK_DIM), lambda b, h, i: (0, 0)),
            pl.BlockSpec((ATTN_V_DIM, 1), lambda b, h, i: (0, 0)),
        ],
        out_specs=pl.BlockSpec((TQ, w), lambda b, h, i: (b * nq + i, h)),
        out_shape=jax.ShapeDtypeStruct((n, ATTN_WIDTH), BF16),
        scratch_shapes=[pltpu.VMEM((ATTN_HB, ATTN_V_DIM, 2 * TQ), F32)],
        compiler_params=_cparams(("parallel", "parallel", "arbitrary")),
        name="attn",
    )(qp, kp, v_t, lam_vecs, subw_col)


def _pack_halves(x):
    c = x.shape[1] // 2
    lo = pltpu.bitcast(x[:, :c].astype(BF16).astype(F32), U32) >> 16
    hi = pltpu.bitcast(x[:, c:].astype(BF16).astype(F32), U32) & jnp.uint32(0xFFFF0000)
    return hi | lo


def _unpack_halves(w, dtype=BF16):
    lo = pltpu.bitcast(w << 16, F32).astype(dtype)
    hi = pltpu.bitcast(w & jnp.uint32(0xFFFF0000), F32).astype(dtype)
    return lo, hi


def _outproj_kernel(x_ref, ys_ref, ya_ref, wos_ref, woa_ref, ln2_ref, wrh_ref, wrl_ref, br_ref,
                    h_ref, hnp_ref, meta_ref):
    tm = x_ref.shape[0]
    h = (x_ref[...]
         + jnp.dot(ys_ref[...], wos_ref[...], preferred_element_type=F32)
         + jnp.dot(ya_ref[...], woa_ref[...], preferred_element_type=F32))
    h_ref[...] = h
    ms = jnp.mean(h * h, axis=-1, keepdims=True)
    hn = h * lax.rsqrt(ms + EPS) * ln2_ref[...]
    hnp_ref[...] = _pack_halves(hn)

    hn_hi = hn.astype(BF16)
    hn_lo = (hn - hn_hi.astype(F32)).astype(BF16)
    nt = (((1,), (1,)), ((), ()))
    lg_t = (lax.dot_general(wrh_ref[...], hn_hi, nt, preferred_element_type=F32)
            + lax.dot_general(wrh_ref[...], hn_lo, nt, preferred_element_type=F32)
            + lax.dot_general(wrl_ref[...], hn_hi, nt, preferred_element_type=F32))
    lg = lg_t.T + br_ref[...]
    lane = lax.broadcasted_iota(I32, (tm, LANES), 1).astype(F32)
    big = float(LANES)
    gmask = lane < N_EXPERT_GROUPS
    gl = jnp.where(gmask, lg, NEG_INF)
    gmax = jnp.max(gl, axis=1, keepdims=True)
    gsel = jnp.min(jnp.where(gl == gmax, lane, big), axis=1, keepdims=True)
    g_w = 1.0 / jnp.sum(jnp.exp(gl - gmax), axis=1, keepdims=True)
    eid = lane - N_EXPERT_GROUPS
    lo = gsel * EXPERTS_PER_GROUP
    emask = (eid >= lo) & (eid < lo + EXPERTS_PER_GROUP)
    el = jnp.where(emask, lg, NEG_INF)
    m1 = jnp.max(el, axis=1, keepdims=True)
    i1 = jnp.min(jnp.where(el == m1, eid, big), axis=1, keepdims=True)
    el2 = jnp.where(eid == i1, NEG_INF, el)
    m2 = jnp.max(el2, axis=1, keepdims=True)
    i2 = jnp.min(jnp.where(el2 == m2, eid, big), axis=1, keepdims=True)
    e2 = jnp.exp(m2 - m1)
    w1 = g_w / (1.0 + e2)
    w2 = g_w * e2 / (1.0 + e2)
    meta = jnp.where(lane == 0, i1, jnp.where(lane == 1, i2, jnp.where(lane == 2, w1, jnp.where(lane == 3, w2, 0.0))))
    meta_ref[...] = meta


def _out_proj(x2, y_ssd, y_att, wo_s, wo_a, ln2_w, wr_hi, wr_lo, b_router):
    n = x2.shape[0]
    full = lambda shape: pl.BlockSpec(shape, lambda i: (0, 0))
    return pl.pallas_call(
        _outproj_kernel,
        grid=(n // TM_OUT,),
        in_specs=[
            pl.BlockSpec((TM_OUT, D_MODEL), lambda i: (i, 0)),
            pl.BlockSpec((TM_OUT, SSD_WIDTH), lambda i: (i, 0)),
            pl.BlockSpec((TM_OUT, ATTN_WIDTH), lambda i: (i, 0)),
            full((SSD_WIDTH, D_MODEL)), full((ATTN_WIDTH, D_MODEL)),
            full((1, D_MODEL)), full((LANES, D_MODEL)), full((LANES, D_MODEL)), full((1, LANES)),
        ],
        out_specs=[
            pl.BlockSpec((TM_OUT, D_MODEL), lambda i: (i, 0)),
            pl.BlockSpec((TM_OUT, D_MODEL // 2), lambda i: (i, 0)),
            pl.BlockSpec((TM_OUT, LANES), lambda i: (i, 0)),
        ],
        out_shape=[
            jax.ShapeDtypeStruct((n, D_MODEL), F32),
            jax.ShapeDtypeStruct((n, D_MODEL // 2), U32),
            jax.ShapeDtypeStruct((n, LANES), F32),
        ],
        compiler_params=_cparams(("parallel",)),
        name="out_proj",
    )(x2, y_ssd, y_att, wo_s, wo_a, ln2_w, wr_hi, wr_lo, b_router)


def _rank_kernel(meta_ref, dest_ref, cnt_ref, run_ref, offs_ref):
    p = pl.program_id(0)
    i = pl.program_id(1)
    tm = meta_ref.shape[0]
    meta = meta_ref[...]
    lane = lax.broadcasted_iota(I32, (tm, LANES), 1).astype(F32)
    oh0 = (lane == meta[:, 0:1]).astype(F32)
    oh1 = (lane == meta[:, 1:2]).astype(F32)
    oh = oh0 + oh1
    colsum = jnp.sum(oh, axis=0, keepdims=True)

    @pl.when((p == 0) & (i == 0))
    def _():
        run_ref[...] = jnp.zeros_like(run_ref)

    @pl.when(p == 0)
    def _():
        run_ref[...] = run_ref[...] + colsum
        dest_ref[...] = jnp.zeros_like(dest_ref)
        cnt_ref[...] = jnp.broadcast_to(run_ref[...], cnt_ref.shape)

    @pl.when((p == 1) & (i == 0))
    def _():
        cnt = run_ref[...]
        padded = jnp.ceil(cnt * (1.0 / TM_EXP)) * TM_EXP
        r = lax.broadcasted_iota(I32, (LANES, LANES), 0)
        c = lax.broadcasted_iota(I32, (LANES, LANES), 1)
        excl = (r < c).astype(F32)
        offs = jnp.dot(jnp.broadcast_to(padded, (8, LANES)), excl, precision=HIGHEST, preferred_element_type=F32)
        offs_ref[...] = offs[0:1, :]
        cnt_ref[...] = jnp.broadcast_to(cnt, cnt_ref.shape)
        run_ref[...] = jnp.zeros_like(run_ref)

    @pl.when(p == 1)
    def _():
        r = lax.broadcasted_iota(I32, (tm, tm), 0)
        c = lax.broadcasted_iota(I32, (tm, tm), 1)
        before = jnp.dot((c < r).astype(BF16), oh.astype(BF16), preferred_element_type=F32)
        base = before + run_ref[...] + offs_ref[...]
        d0 = jnp.sum(oh0 * base, axis=1, keepdims=True)
        d1 = jnp.sum(oh1 * base, axis=1, keepdims=True)
        dest = jnp.where(lane == 0, d0, jnp.where(lane == 1, d1, 0.0))
        dest_ref[...] = dest.astype(I32)
        run_ref[...] = run_ref[...] + colsum


def _rank(meta):
    n = meta.shape[0]
    return pl.pallas_call(
        _rank_kernel,
        grid=(2, n // TM_RANK),
        in_specs=[pl.BlockSpec((TM_RANK, LANES), lambda p, i: (i, 0))],
        out_specs=[
            pl.BlockSpec((TM_RANK, LANES), lambda p, i: (i * p, 0)),
            pl.BlockSpec((8, LANES), lambda p, i: (0, 0)),
        ],
        out_shape=[
            jax.ShapeDtypeStruct((n, LANES), I32),
            jax.ShapeDtypeStruct((8, LANES), F32),
        ],
        scratch_shapes=[pltpu.VMEM((1, LANES), F32), pltpu.VMEM((1, LANES), F32)],
        compiler_params=_cparams(("arbitrary", "arbitrary")),
        name="rank",
    )(meta)


def _row_copy(src_ref, src_row, dst_ref, dst_row, sem):
    return pltpu.make_async_copy(src_ref.at[pl.ds(src_row, 1)], dst_ref.at[pl.ds(dst_row, 1)], sem)


def _invert_kernel(d0_ref, d1_ref, inv_ref):
    n = d0_ref.shape[0]

    def clear(s, _):
        inv_ref[s] = 0
        return 0

    lax.fori_loop(0, inv_ref.shape[0], clear, 0, unroll=8)

    def put(t, _):
        inv_ref[d0_ref[t]] = t
        inv_ref[d1_ref[t]] = n + t
        return 0

    lax.fori_loop(0, n, put, 0, unroll=4)


def _invert(dest0, dest1, n_slots):
    smem = pl.BlockSpec(memory_space=pltpu.SMEM)
    return pl.pallas_call(
        _invert_kernel,
        in_specs=[smem, smem],
        out_specs=smem,
        out_shape=jax.ShapeDtypeStruct((n_slots,), I32),
        name="invert",
    )(dest0, dest1)


CAST_ROWS = 256


def _cast_weight(src_ref, dst_ref):
    def body(i, _):
        rows = pl.ds(pl.multiple_of(i * CAST_ROWS, CAST_ROWS), CAST_ROWS)
        dst_ref[rows, :] = src_ref[rows, :].astype(BF16)
        return 0

    lax.fori_loop(0, src_ref.shape[0] // CAST_ROWS, body, 0)


S_EXPERT, S_FIRST, S_SLOT, S_NEXT, S_HAS_NEXT, S_USED, S_VALID = range(7)


def _load_expert_weights(t, sched_ref, triples, sems):
    def copies(expert, slot):
        return [pltpu.make_async_copy(w.at[expert], stage.at[slot], sems.at[i, slot])
                for i, (w, stage, _) in enumerate(triples)]

    @pl.when(sched_ref[S_FIRST, t] == 1)
    def _():
        slot = sched_ref[S_SLOT, t]

        @pl.when(t == 0)
        def _():
            for cp in copies(sched_ref[S_EXPERT, t], slot):
                cp.start()

        for cp in copies(sched_ref[S_EXPERT, t], slot):
            cp.wait()

        @pl.when(sched_ref[S_HAS_NEXT, t] == 1)
        def _():
            for cp in copies(sched_ref[S_NEXT, t], 1 - slot):
                cp.start()

        for _, stage, dst in triples:
            _cast_weight(stage.at[slot], dst)


ROW_DMA_GROUPS = 4


def _tile_rows(sched_ref, tile, group):
    valid = sched_ref[S_VALID, tile]
    per = TM_EXP // ROW_DMA_GROUPS
    for r in range(group * per, (group + 1) * per):
        ok = r < valid
        yield r, jnp.where(ok, tile * TM_EXP + r, 0), ok


def _experts_up_kernel(sched_ref, inv_ref, hnp_hbm, wg_hbm, wu_hbm, hid_ref,
                       xbuf_ref, wg_stage, wu_stage, wgb_ref, wub_ref, wsems, gsem, *, n_tokens):
    t = pl.program_id(0)
    used = sched_ref[S_USED, 0]
    cur = t % 2
    _load_expert_weights(t, sched_ref, [(wg_hbm, wg_stage, wgb_ref), (wu_hbm, wu_stage, wub_ref)], wsems)

    def start_rows(tile, buf, group):
        for r, slot, _ in _tile_rows(sched_ref, tile, group):
            token = inv_ref[slot] & (n_tokens - 1)
            _row_copy(hnp_hbm, token, xbuf_ref.at[buf], r, gsem.at[buf]).start()

    def wait_rows(buf):
        pltpu.make_async_copy(hnp_hbm.at[pl.ds(0, TM_EXP)], xbuf_ref.at[buf], gsem.at[buf]).wait()

    @pl.when(t == 0)
    def _():
        for g in range(ROW_DMA_GROUPS):
            start_rows(0, 0, g)

    def multiply(fetch_next):
        wait_rows(cur)
        c = D_MODEL // 2
        x_lo, x_hi = _unpack_halves(xbuf_ref[cur])
        pieces = ((x_lo, wgb_ref, 0), (x_hi, wgb_ref, c), (x_lo, wub_ref, 0), (x_hi, wub_ref, c))
        acc = []
        for g, (x, w_ref, row0) in enumerate(pieces):
            if fetch_next:
                start_rows(t + 1, 1 - cur, g)
            acc.append(jnp.dot(x, w_ref[row0:row0 + c, :], preferred_element_type=F32))
        hid_ref[...] = (_silu(acc[0] + acc[1]) * (acc[2] + acc[3])).astype(BF16)

    @pl.when(t < used - 1)
    def _():
        multiply(True)

    @pl.when(t == used - 1)
    def _():
        multiply(False)

    @pl.when(t >= used)
    def _():
        hid_ref[...] = jnp.zeros_like(hid_ref)


def _experts_down_kernel(sched_ref, inv_ref, hid_ref, wd_hbm, y_hbm,
                         ybuf_ref, wd_stage, wdb_ref, wsems, ssem, *, n_tokens):
    t = pl.program_id(0)
    used = sched_ref[S_USED, 0]
    cur = t % 2
    _load_expert_weights(t, sched_ref, [(wd_hbm, wd_stage, wdb_ref)], wsems)

    def start_rows(tile, buf, group):
        for r, slot, ok in _tile_rows(sched_ref, tile, group):
            dst = jnp.where(ok, inv_ref[slot], TOP_K * n_tokens + r)
            _row_copy(ybuf_ref.at[buf], r, y_hbm, dst, ssem.at[buf]).start()

    def wait_rows(buf):
        pltpu.make_async_copy(ybuf_ref.at[buf], y_hbm.at[pl.ds(0, TM_EXP)], ssem.at[buf]).wait()

    def multiply(send_previous):
        hid = hid_ref[...]
        q = D_MODEL // 4
        for half in range(2):
            if send_previous:
                start_rows(t - 1, 1 - cur, 2 * half)
            lo = jnp.dot(hid, wdb_ref[:, half * q:(half + 1) * q], preferred_element_type=F32)
            if send_previous:
                start_rows(t - 1, 1 - cur, 2 * half + 1)
            hi = jnp.dot(hid, wdb_ref[:, (half + 2) * q:(half + 3) * q], preferred_element_type=F32)
            ybuf_ref[cur, :, half * q:(half + 1) * q] = _pack_halves(jnp.concatenate([lo, hi], axis=1))

    @pl.when((t >= 2) & (t < used))
    def _():
        wait_rows(cur)

    @pl.when(t == 0)
    def _():
        ybuf_ref[1] = jnp.zeros(ybuf_ref.shape[1:], ybuf_ref.dtype)
        spare = pltpu.make_async_copy(ybuf_ref.at[1], y_hbm.at[pl.ds(TOP_K * n_tokens, TM_EXP)], ssem.at[1])
        spare.start()
        spare.wait()
        multiply(False)

    @pl.when((t > 0) & (t < used))
    def _():
        multiply(True)

    @pl.when(t == used - 1)
    def _():
        for g in range(ROW_DMA_GROUPS):
            start_rows(t, cur, g)

        @pl.when(t > 0)
        def _():
            wait_rows(1 - cur)

        wait_rows(cur)


def _tile_schedule(counts, n_tiles):
    tiles_per_expert = (counts + TM_EXP - 1) // TM_EXP
    tile_end = jnp.cumsum(tiles_per_expert)
    ids = jnp.arange(n_tiles, dtype=I32)
    raw = jnp.sum(ids[:, None] >= tile_end[None, :], axis=1)
    expert = jnp.minimum(raw, N_EXPERTS - 1).astype(I32)
    first = jnp.concatenate([jnp.ones((1,), I32), (expert[1:] != expert[:-1]).astype(I32)])
    slot = (jnp.cumsum(first) - 1) % 2
    start_idx = jnp.where(first == 1, ids, n_tiles)
    next_start = jnp.concatenate([lax.cummin(start_idx, reverse=True)[1:], jnp.full((1,), n_tiles, I32)])
    has_next = (next_start < n_tiles).astype(I32)
    next_expert = jnp.sum(jnp.where(ids[None, :] == next_start[:, None], expert[None, :], 0), axis=1)
    used = jnp.full((n_tiles,), tile_end[-1], I32)
    onehot = expert[:, None] == jnp.arange(N_EXPERTS)[None, :]
    tile_in_expert = ids - jnp.sum(jnp.where(onehot, (tile_end - tiles_per_expert)[None, :], 0), axis=1)
    rows_left = jnp.sum(jnp.where(onehot, counts[None, :], 0), axis=1) - tile_in_expert * TM_EXP
    valid = jnp.where(raw < N_EXPERTS, jnp.clip(rows_left, 0, TM_EXP), 0)
    return jnp.stack([expert, first, slot, next_expert, has_next, used, valid]).astype(I32)


def _experts(sched, inv, hnp, wg, wu, wd):
    n = hnp.shape[0]
    n_slots = inv.shape[0]
    row_spec = lambda width: pl.BlockSpec((TM_EXP, width), lambda t, sc, iv: (t, 0))
    hbm = pl.BlockSpec(memory_space=pl.ANY)
    big_vmem = pltpu.CompilerParams(dimension_semantics=("arbitrary",), vmem_limit_bytes=EXPERT_VMEM_LIMIT)
    hid = pl.pallas_call(
        functools.partial(_experts_up_kernel, n_tokens=n),
        grid_spec=pltpu.PrefetchScalarGridSpec(
            num_scalar_prefetch=2,
            grid=(n_slots // TM_EXP,),
            in_specs=[hbm, hbm, hbm],
            out_specs=row_spec(EXPERT_HIDDEN),
            scratch_shapes=[pltpu.VMEM((2, TM_EXP, D_MODEL // 2), U32)]
                           + [pltpu.VMEM((2, D_MODEL, EXPERT_HIDDEN), F32)] * 2
                           + [pltpu.VMEM((D_MODEL, EXPERT_HIDDEN), BF16)] * 2
                           + [pltpu.SemaphoreType.DMA((2, 2)), pltpu.SemaphoreType.DMA((2,))],
        ),
        out_shape=jax.ShapeDtypeStruct((n_slots, EXPERT_HIDDEN), BF16),
        compiler_params=big_vmem,
        name="experts_up",
    )(sched, inv, hnp, wg, wu)
    return pl.pallas_call(
        functools.partial(_experts_down_kernel, n_tokens=n),
        grid_spec=pltpu.PrefetchScalarGridSpec(
            num_scalar_prefetch=2,
            grid=(n_slots // TM_EXP,),
            in_specs=[row_spec(EXPERT_HIDDEN), hbm],
            out_specs=hbm,
            scratch_shapes=[pltpu.VMEM((2, TM_EXP, D_MODEL // 2), U32),
                            pltpu.VMEM((2, EXPERT_HIDDEN, D_MODEL), F32), pltpu.VMEM((EXPERT_HIDDEN, D_MODEL), BF16),
                            pltpu.SemaphoreType.DMA((1, 2)), pltpu.SemaphoreType.DMA((2,))],
        ),
        out_shape=jax.ShapeDtypeStruct((TOP_K * n + TM_EXP, D_MODEL // 2), U32),
        compiler_params=big_vmem,
        name="experts_down",
    )(sched, inv, hid, wd)


def _combine_kernel(h_ref, meta_ref, y0_ref, y1_ref, o_ref):
    c = D_MODEL // 2
    meta = meta_ref[...]
    w0, w1 = meta[:, 2:3], meta[:, 3:4]
    lo0, hi0 = _unpack_halves(y0_ref[...], F32)
    lo1, hi1 = _unpack_halves(y1_ref[...], F32)
    o_ref[:, :c] = h_ref[:, :c] + (lo0 * w0 + lo1 * w1)
    o_ref[:, c:] = h_ref[:, c:] + (hi0 * w0 + hi1 * w1)


def _combine(h, meta, y):
    n = h.shape[0]
    nb = n // TM_COMB
    return pl.pallas_call(
        _combine_kernel,
        grid=(nb,),
        in_specs=[
            pl.BlockSpec((TM_COMB, D_MODEL), lambda i: (i, 0)),
            pl.BlockSpec((TM_COMB, LANES), lambda i: (i, 0)),
            pl.BlockSpec((TM_COMB, D_MODEL // 2), lambda i: (i, 0)),
            pl.BlockSpec((TM_COMB, D_MODEL // 2), lambda i: (nb + i, 0)),
        ],
        out_specs=pl.BlockSpec((TM_COMB, D_MODEL), lambda i: (i, 0)),
        out_shape=jax.ShapeDtypeStruct((n, D_MODEL), F32),
        compiler_params=_cparams(("parallel",)),
        name="combine",
    )(h, meta, y, y)


def _lambda_init(layer_idx):
    return 0.8 - 0.6 * math.exp(-0.3 * layer_idx)


def _pad_lanes(v, width=LANES):
    return jnp.pad(v, ((0, 0), (0, width - v.shape[1])))


def _layer(l, x2, pos_col, bsz, seq, ln1_w, w_in, conv_w, conv_b, dt_bias, a_log, d_skip, ssd_norm_w,
           q_norm_w, k_norm_w, lambda_q1, lambda_k1, lambda_q2, lambda_k2, subln_w, w_out, ln2_w,
           w_router_group, b_router_group, w_router_expert, b_router_expert, w_gate, w_up, w_down):
    n = x2.shape[0]
    c_z, c_xbc, c_dt = SSD_WIDTH, SSD_WIDTH + SSD_CONV_DIM, SSD_WIDTH + SSD_CONV_DIM + SSD_HEADS
    c_q, c_k = c_dt + ATTN_WIDTH, c_dt + 2 * ATTN_WIDTH
    w_main = jnp.concatenate([w_in[:, :c_z], w_in[:, c_dt:c_q], w_in[:, c_q:c_k], w_in[:, c_z:c_xbc]],
                             axis=1).astype(BF16)
    w_vt = w_in[:, c_k:].T.astype(BF16)
    w_dt = _pad_lanes(w_in[:, c_xbc:c_dt]).astype(BF16)

    u, v_t, dt_raw = _in_proj(x2, ln1_w[None, :], w_main, w_vt, w_dt)

    a_neg = _pad_lanes(-jnp.exp(a_log.astype(F32))[None, :])
    y_ssd = _ssd(u, dt_raw, conv_w, conv_b[None, :], _pad_lanes(dt_bias[None, :]), a_neg,
                 jnp.repeat(d_skip, SSD_HEAD_DIM)[None, :], ssd_norm_w[None, :], bsz, seq)

    inv_freq = jnp.power(ROPE_THETA, -jnp.arange(0, ROPE_DIM, 2, dtype=F32) / ROPE_DIM)
    d = jnp.arange(LANES) % ATTN_QK_DIM
    invf_lanes = jnp.where(d < ROPE_DIM, inv_freq[d % (ROPE_DIM // 2)], 0.0)[None, :]
    seg_ones = (jnp.arange(LANES)[:, None] // ATTN_QK_DIM == jnp.arange(LANES)[None, :] // ATTN_QK_DIM).astype(BF16)
    qp, kp = _qk_prep(u, pos_col, invf_lanes, jnp.tile(q_norm_w, 2)[None, :], jnp.tile(k_norm_w, 2)[None, :], seg_ones)

    lam_vecs = jnp.stack([lambda_q1, lambda_k1, lambda_q2, lambda_k2]).astype(F32)
    y_att = _attention(qp, kp, v_t, lam_vecs, subln_w[:, None], bsz, seq, _lambda_init(l))

    w_out_b = w_out.astype(BF16)
    w_router_t = _pad_lanes(jnp.concatenate([w_router_group, w_router_expert], axis=1)).T
    wr_hi = w_router_t.astype(BF16)
    wr_lo = (w_router_t - wr_hi.astype(F32)).astype(BF16)
    b_router = _pad_lanes(jnp.concatenate([b_router_group, b_router_expert])[None, :])
    h, hnp, meta = _out_proj(x2, y_ssd, y_att, w_out_b[:SSD_WIDTH], w_out_b[SSD_WIDTH:], ln2_w[None, :],
                             wr_hi, wr_lo, b_router)

    dest, cnt = _rank(meta)
    dest0, dest1 = dest[:, 0], dest[:, 1]
    counts = cnt[0, :N_EXPERTS].astype(I32)
    n_slots = (n * TOP_K + N_EXPERTS * (TM_EXP - 1)) // TM_EXP * TM_EXP
    sched = _tile_schedule(counts, n_slots // TM_EXP)

    inv = _invert(dest0, dest1, n_slots)
    y = _experts(sched, inv, hnp, w_gate, w_up, w_down)
    return _combine(h, meta, y)


def kernel(x, positions, ln1_w, w_in, conv_w, conv_b, dt_bias, a_log, d_skip, ssd_norm_w, q_norm_w, k_norm_w,
           lambda_q1, lambda_k1, lambda_q2, lambda_k2, subln_w, w_out, ln2_w, w_router_group, b_router_group,
           w_router_expert, b_router_expert, w_gate, w_up, w_down):
    bsz, seq, d = x.shape
    assert d == D_MODEL and seq % TQ == 0 and (bsz * seq) % TM_IN == 0
    x2 = x.reshape(bsz * seq, d)
    pos_col = positions.astype(F32).reshape(bsz * seq, 1)
    params = (ln1_w, w_in, conv_w, conv_b, dt_bias, a_log, d_skip, ssd_norm_w, q_norm_w, k_norm_w,
              lambda_q1, lambda_k1, lambda_q2, lambda_k2, subln_w, w_out, ln2_w, w_router_group, b_router_group,
              w_router_expert, b_router_expert, w_gate, w_up, w_down)
    for l in range(ln1_w.shape[0]):
        x2 = _layer(l, x2, pos_col, bsz, seq, *[p[l] for p in params])
    return x2.reshape(bsz, seq, d)
```

```python
import functools
import math

import jax
import jax.numpy as jnp
from jax import lax
from jax.experimental import pallas as pl
from jax.experimental.pallas import tpu as pltpu

F32 = jnp.float32
BF16 = jnp.bfloat16
I32 = jnp.int32
U32 = jnp.uint32
HIGHEST = lax.Precision.HIGHEST

D_MODEL = 2048
SSD_WIDTH = 1024
ATTN_WIDTH = 1024
SSD_HEAD_DIM = 64
SSD_HEADS = 16
SSD_GROUPS = 2
SSD_HEADS_PER_GROUP = SSD_HEADS // SSD_GROUPS
SSD_STATE = 128
SSD_CONV = 4
SSD_CHUNK = 128
SSD_CONV_DIM = SSD_WIDTH + 2 * SSD_GROUPS * SSD_STATE
ATTN_V_DIM = 128
ATTN_HEADS = 8
ATTN_QK_DIM = 64
ROPE_THETA = 500000.0
ROPE_DIM = 16
N_EXPERT_GROUPS = 4
EXPERTS_PER_GROUP = 8
N_EXPERTS = 32
TOP_K = 2
EXPERT_HIDDEN = 1024
EPS = 1e-6

LANES = 128
NEG_INF = float("-inf")

TM_IN = 512
TN_IN = 1536
TM_QK = 512
TQ = 256
ATTN_HB = 8
TM_OUT = 256
TM_RANK = 512
TM_EXP = 256
TM_DISP = 256
TM_COMB = 256
U_COLS = SSD_WIDTH + 2 * ATTN_WIDTH + SSD_CONV_DIM
VMEM_LIMIT = 52 * 1024 * 1024
EXPERT_VMEM_LIMIT = 58 * 1024 * 1024


def _cparams(sem):
    return pltpu.CompilerParams(dimension_semantics=sem, vmem_limit_bytes=VMEM_LIMIT)


def _silu(x):
    return x * (1.0 / (1.0 + jnp.exp(-x)))


def _softplus(x):
    return jnp.maximum(x, 0.0) + jnp.log(1.0 + jnp.exp(-jnp.abs(x)))


def _inproj_kernel(x_ref, lnw_ref, w_ref, wvt_ref, wdt_ref, u_ref, vt_ref, dt_ref, xn_ref):
    @pl.when(pl.program_id(1) == 0)
    def _():
        x = x_ref[...]
        ms = jnp.mean(x * x, axis=-1, keepdims=True)
        xn = (x * lax.rsqrt(ms + EPS) * lnw_ref[...]).astype(BF16)
        xn_ref[...] = xn
        dt_ref[...] = jnp.dot(xn, wdt_ref[...], preferred_element_type=F32)
        vt_ref[...] = lax.dot_general(wvt_ref[...], xn, (((1,), (1,)), ((), ())),
                                      preferred_element_type=F32).astype(BF16)

    u_ref[...] = jnp.dot(xn_ref[...], w_ref[...], preferred_element_type=F32).astype(BF16)


def _in_proj(x2, ln_w, w_main, w_vt, w_dt):
    n = x2.shape[0]
    return pl.pallas_call(
        _inproj_kernel,
        grid=(n // TM_IN, U_COLS // TN_IN),
        in_specs=[
            pl.BlockSpec((TM_IN, D_MODEL), lambda i, j: (i, 0)),
            pl.BlockSpec((1, D_MODEL), lambda i, j: (0, 0)),
            pl.BlockSpec((D_MODEL, TN_IN), lambda i, j: (0, j)),
            pl.BlockSpec((ATTN_WIDTH, D_MODEL), lambda i, j: (0, 0)),
            pl.BlockSpec((D_MODEL, LANES), lambda i, j: (0, 0)),
        ],
        out_specs=[
            pl.BlockSpec((TM_IN, TN_IN), lambda i, j: (i, j)),
            pl.BlockSpec((ATTN_WIDTH, TM_IN), lambda i, j: (0, i)),
            pl.BlockSpec((TM_IN, LANES), lambda i, j: (i, 0)),
        ],
        out_shape=[
            jax.ShapeDtypeStruct((n, U_COLS), BF16),
            jax.ShapeDtypeStruct((ATTN_WIDTH, n), BF16),
            jax.ShapeDtypeStruct((n, LANES), F32),
        ],
        scratch_shapes=[pltpu.VMEM((TM_IN, D_MODEL), BF16)],
        compiler_params=_cparams(("parallel", "arbitrary")),
        name="in_proj",
    )(x2, ln_w, w_main, w_vt, w_dt)


def _ssd_kernel(z_ref, xbc_ref, dt_ref, convw_ref, convb_ref, dtb_ref, aneg_ref, dskip_ref, normw_ref,
                y_ref, xp_ref, st_ref, yacc_ref):
    L = SSD_CHUNK
    P = SSD_HEAD_DIM

    @pl.when(pl.program_id(1) == 0)
    def _():
        xp_ref[0:8, :] = jnp.zeros((8, SSD_CONV_DIM), F32)
        st_ref[...] = jnp.zeros_like(st_ref)

    xp_ref[8:8 + L, :] = xbc_ref[...].astype(F32)
    acc = jnp.broadcast_to(convb_ref[...], (L, SSD_CONV_DIM))
    for k in range(SSD_CONV):
        acc = acc + xp_ref[5 + k:5 + k + L, :] * convw_ref[k:k + 1, :]
    xp_ref[0:8, :] = xp_ref[L:L + 8, :]
    xc = _silu(acc)

    dt = _softplus(dt_ref[...] + dtb_ref[...])
    a = dt * aneg_ref[...]
    row = lax.broadcasted_iota(I32, (L, L), 0)
    col = lax.broadcasted_iota(I32, (L, L), 1)
    causal = row >= col
    a_cs = jnp.dot(causal.astype(F32), a, precision=HIGHEST, preferred_element_type=F32)
    a_last = a_cs[L - 1:L, :]
    ea = jnp.exp(a_cs)
    dsdt = jnp.exp(a_last - a_cs) * dt
    cd = jnp.exp(a_last)
    a_cs_t = a_cs.T
    dt_t = dt.T
    dsdt_t = dsdt.T

    for g in range(SSD_GROUPS):
        b_g = xc[:, SSD_WIDTH + g * SSD_STATE:SSD_WIDTH + (g + 1) * SSD_STATE]
        c_off = SSD_WIDTH + SSD_GROUPS * SSD_STATE
        c_g = xc[:, c_off + g * SSD_STATE:c_off + (g + 1) * SSD_STATE]
        cb = lax.dot_general(c_g.astype(BF16), b_g.astype(BF16), (((1,), (1,)), ((), ())),
                             preferred_element_type=F32)
        b_gt = b_g.T
        for hh in range(SSD_HEADS_PER_GROUP):
            h = g * SSD_HEADS_PER_GROUP + hh
            xs_h = xc[:, h * P:(h + 1) * P].astype(BF16)
            seg = a_cs[:, h:h + 1] - a_cs_t[h:h + 1, :]
            dec = jnp.exp(jnp.where(causal, seg, NEG_INF))
            m = (cb * dec * dt_t[h:h + 1, :]).astype(BF16)
            c_s = (c_g * ea[:, h:h + 1]).astype(BF16)
            s_prev = st_ref[h]
            lhs = jnp.concatenate([m, c_s], axis=1)
            rhs = jnp.concatenate([xs_h, s_prev.astype(BF16)], axis=0)
            yacc_ref[:, h * P:(h + 1) * P] = jnp.dot(lhs, rhs, preferred_element_type=F32)
            bw = (b_gt * dsdt_t[h:h + 1, :]).astype(BF16)
            st_ref[h] = s_prev * cd[:, h:h + 1] + jnp.dot(bw, xs_h, preferred_element_type=F32)

    y = yacc_ref[...] + xc[:, :SSD_WIDTH] * dskip_ref[...]
    y = y * _silu(z_ref[...].astype(F32))
    gw = SSD_WIDTH // SSD_GROUPS
    for g in range(SSD_GROUPS):
        yg = y[:, g * gw:(g + 1) * gw]
        ms = jnp.mean(yg * yg, axis=-1, keepdims=True)
        y_ref[:, g * gw:(g + 1) * gw] = (yg * lax.rsqrt(ms + EPS) * normw_ref[:, g * gw:(g + 1) * gw]).astype(BF16)


def _ssd(u, dt_raw, conv_w, conv_b, dt_bias, a_neg, dskip_lanes, norm_w, bsz, seq):
    n = u.shape[0]
    nc = seq // SSD_CHUNK
    xbc_blk = (SSD_WIDTH + 2 * ATTN_WIDTH) // SSD_CONV_DIM
    full = lambda shape: pl.BlockSpec(shape, lambda b, c: (0, 0))
    return pl.pallas_call(
        _ssd_kernel,
        grid=(bsz, nc),
        in_specs=[
            pl.BlockSpec((SSD_CHUNK, SSD_WIDTH), lambda b, c: (b * nc + c, 0)),
            pl.BlockSpec((SSD_CHUNK, SSD_CONV_DIM), lambda b, c: (b * nc + c, xbc_blk)),
            pl.BlockSpec((SSD_CHUNK, LANES), lambda b, c: (b * nc + c, 0)),
            full((SSD_CONV, SSD_CONV_DIM)),
            full((1, SSD_CONV_DIM)),
            full((1, LANES)),
            full((1, LANES)),
            full((1, SSD_WIDTH)),
            full((1, SSD_WIDTH)),
        ],
        out_specs=pl.BlockSpec((SSD_CHUNK, SSD_WIDTH), lambda b, c: (b * nc + c, 0)),
        out_shape=jax.ShapeDtypeStruct((n, SSD_WIDTH), BF16),
        scratch_shapes=[
            pltpu.VMEM((SSD_CHUNK + 8, SSD_CONV_DIM), F32),
            pltpu.VMEM((SSD_HEADS, SSD_STATE, SSD_HEAD_DIM), F32),
            pltpu.VMEM((SSD_CHUNK, SSD_WIDTH), F32),
        ],
        compiler_params=_cparams(("parallel", "arbitrary")),
        name="ssd",
    )(u, u, dt_raw, conv_w, conv_b, dt_bias, a_neg, dskip_lanes, norm_w)


def _qkprep_kernel(q_ref, k_ref, pos_ref, invf_ref, qw_ref, kw_ref, ones_ref, qo_ref, ko_ref):
    tm = q_ref.shape[0]
    ang = pos_ref[...] * invf_ref[...]
    cs = jnp.cos(ang)
    sn = jnp.sin(ang)
    d = lax.broadcasted_iota(I32, (tm, LANES), 1) & (ATTN_QK_DIM - 1)
    half = ROPE_DIM // 2
    s_lo = jnp.where(d < half, -sn, 0.0)
    s_hi = jnp.where((d >= half) & (d < ROPE_DIM), sn, 0.0)
    scale = 1.0 / math.sqrt(ATTN_QK_DIM)
    for src, w_ref, dst, mul in ((q_ref, qw_ref, qo_ref, scale), (k_ref, kw_ref, ko_ref, 1.0)):
        for hb in range(ATTN_HEADS):
            x = src[:, hb * LANES:(hb + 1) * LANES].astype(F32)
            ss = jnp.dot((x * x).astype(BF16), ones_ref[...], preferred_element_type=F32)
            xn = x * lax.rsqrt(ss * (1.0 / ATTN_QK_DIM) + EPS) * w_ref[...]
            out = xn * cs + pltpu.roll(xn, LANES - half, 1) * s_lo + pltpu.roll(xn, half, 1) * s_hi
            dst[:, hb * LANES:(hb + 1) * LANES] = (out * mul).astype(BF16)


def _qk_prep(u, pos_col, invf_lanes, qw_lanes, kw_lanes, seg_ones):
    n = u.shape[0]
    full = lambda shape: pl.BlockSpec(shape, lambda i: (0, 0))
    return pl.pallas_call(
        _qkprep_kernel,
        grid=(n // TM_QK,),
        in_specs=[
            pl.BlockSpec((TM_QK, ATTN_WIDTH), lambda i: (i, 1)),
            pl.BlockSpec((TM_QK, ATTN_WIDTH), lambda i: (i, 2)),
            pl.BlockSpec((TM_QK, 1), lambda i: (i, 0)),
            full((1, LANES)), full((1, LANES)), full((1, LANES)), full((LANES, LANES)),
        ],
        out_specs=[pl.BlockSpec((TM_QK, ATTN_WIDTH), lambda i: (i, 0))] * 2,
        out_shape=[jax.ShapeDtypeStruct((n, ATTN_WIDTH), BF16)] * 2,
        compiler_params=_cparams(("parallel",)),
        name="qk_prep",
    )(u, u, pos_col, invf_lanes, qw_lanes, kw_lanes, seg_ones)


def _attn_kernel(q_ref, k_ref, vt_ref, lamv_ref, subw_ref, o_ref, acc_ref, *, lam_init):
    qi = pl.program_id(2)
    lane = lax.broadcasted_iota(I32, (TQ, LANES), 1)
    qs = []
    for hb in range(ATTN_HB):
        q = q_ref[:, hb * LANES:(hb + 1) * LANES]
        zero = jnp.zeros_like(q)
        qs.append(jnp.concatenate([jnp.where(lane < ATTN_QK_DIM, q, zero),
                                   jnp.where(lane >= ATTN_QK_DIM, q, zero)], axis=0))
    acc_ref[...] = jnp.zeros_like(acc_ref)
    kv_idx = lax.broadcasted_iota(I32, (TQ, 2 * TQ), 0)
    q_idx = lax.broadcasted_iota(I32, (TQ, 2 * TQ), 1) & (TQ - 1)
    nt = (((1,), (1,)), ((), ()))

    def block(j, carry, masked):
        off = pl.multiple_of(j * TQ, TQ)
        ss = []
        for hb in range(ATTN_HB):
            kb = k_ref[pl.ds(off, TQ), hb * LANES:(hb + 1) * LANES]
            ss.append(lax.dot_general(kb, qs[hb], nt, preferred_element_type=F32))
        new, ps, alphas = [], [], []
        for hb in range(ATTN_HB):
            m_old, l_old = carry[2 * hb], carry[2 * hb + 1]
            s = ss[hb]
            if masked:
                s = jnp.where(kv_idx <= q_idx, s, NEG_INF)
            m_new = jnp.maximum(m_old, jnp.max(s, axis=0, keepdims=True))
            alpha = jnp.exp(m_old - m_new)
            p = jnp.exp(s - m_new)
            new += [m_new, alpha * l_old + jnp.sum(p, axis=0, keepdims=True)]
            ps.append(p.astype(BF16))
            alphas.append(alpha)
        pvs = []
        for hb in range(ATTN_HB):
            vb = vt_ref[hb * LANES:(hb + 1) * LANES, pl.ds(off, TQ)]
            pvs.append(jnp.dot(vb, ps[hb], preferred_element_type=F32))
        for hb in range(ATTN_HB):
            acc_ref[hb] = alphas[hb] * acc_ref[hb] + pvs[hb]
        return tuple(new)

    init = (jnp.full((1, 2 * TQ), NEG_INF, F32), jnp.zeros((1, 2 * TQ), F32)) * ATTN_HB
    carry = lax.fori_loop(0, qi, lambda j, cr: block(j, cr, False), init)
    carry = block(qi, carry, True)

    lv = lamv_ref[...]
    lam = (jnp.exp(jnp.sum(lv[0:1] * lv[1:2], axis=1, keepdims=True))
           - jnp.exp(jnp.sum(lv[2:3] * lv[3:4], axis=1, keepdims=True)) + lam_init)
    for hb in range(ATTN_HB):
        o2 = acc_ref[hb] * (1.0 / carry[2 * hb + 1])
        o_t = o2[:, :TQ] - lam * o2[:, TQ:]
        ms = jnp.mean(o_t * o_t, axis=0, keepdims=True)
        o_t = o_t * lax.rsqrt(ms + EPS) * subw_ref[...] * (1.0 - lam_init)
        o_ref[:, hb * LANES:(hb + 1) * LANES] = o_t.T.astype(BF16)


def _attention(qp, kp, v_t, lam_vecs, subw_col, bsz, seq, lam_init):
    n = qp.shape[0]
    nq = seq // TQ
    w = ATTN_HB * ATTN_V_DIM
    return pl.pallas_call(
        functools.partial(_attn_kernel, lam_init=lam_init),
        grid=(bsz, ATTN_HEADS // ATTN_HB, nq),
        in_specs=[
            pl.BlockSpec((TQ, w), lambda b, h, i: (b * nq + i, h)),
            pl.BlockSpec((seq, w), lambda b, h, i: (b, h)),
            pl.BlockSpec((w, seq), lambda b, h, i: (h, b)),
            pl.BlockSpec((4, ATTN_QK_DIM), lambda b, h, i: (0, 0)),
            pl.BlockSpec((ATTN_V_DIM, 1), lambda b, h, i: (0, 0)),
        ],
        out_specs=pl.BlockSpec((TQ, w), lambda b, h, i: (b * nq + i, h)),
        out_shape=jax.ShapeDtypeStruct((n, ATTN_WIDTH), BF16),
        scratch_shapes=[pltpu.VMEM((ATTN_HB, ATTN_V_DIM, 2 * TQ), F32)],
        compiler_params=_cparams(("parallel", "parallel", "arbitrary")),
        name="attn",
    )(qp, kp, v_t, lam_vecs, subw_col)


def _pack_halves(x):
    c = x.shape[1] // 2
    lo = pltpu.bitcast(x[:, :c].astype(BF16).astype(F32), U32) >> 16
    hi = pltpu.bitcast(x[:, c:].astype(BF16).astype(F32), U32) & jnp.uint32(0xFFFF0000)
    return hi | lo


TOKEN_ROWS = D_MODEL // 2 // LANES


def _store_token_rows(ref, first_piece, packed):
    tm = packed.shape[0]
    for j in range(packed.shape[1] // LANES):
        ref[pl.ds(first_piece + j, tm, stride=TOKEN_ROWS), :] = packed[:, j * LANES:(j + 1) * LANES]


def _load_token_rows(ref, tm):
    return jnp.concatenate([ref[pl.ds(s, tm, stride=TOKEN_ROWS), :] for s in range(TOKEN_ROWS)], axis=1)


def _unpack_halves(w, dtype=BF16):
    lo = pltpu.bitcast(w << 16, F32).astype(dtype)
    hi = pltpu.bitcast(w & jnp.uint32(0xFFFF0000), F32).astype(dtype)
    return lo, hi


def _outproj_kernel(x_ref, ys_ref, ya_ref, wos_ref, woa_ref, ln2_ref, wrh_ref, wrl_ref, br_ref,
                    h_ref, hnp_ref, meta_ref):
    tm = x_ref.shape[0]
    h = (x_ref[...]
         + jnp.dot(ys_ref[...], wos_ref[...], preferred_element_type=F32)
         + jnp.dot(ya_ref[...], woa_ref[...], preferred_element_type=F32))
    h_ref[...] = h
    ms = jnp.mean(h * h, axis=-1, keepdims=True)
    hn = h * lax.rsqrt(ms + EPS) * ln2_ref[...]
    _store_token_rows(hnp_ref, 0, _pack_halves(hn))

    hn_hi = hn.astype(BF16)
    hn_lo = (hn - hn_hi.astype(F32)).astype(BF16)
    nt = (((1,), (1,)), ((), ()))
    lg_t = (lax.dot_general(wrh_ref[...], hn_hi, nt, preferred_element_type=F32)
            + lax.dot_general(wrh_ref[...], hn_lo, nt, preferred_element_type=F32)
            + lax.dot_general(wrl_ref[...], hn_hi, nt, preferred_element_type=F32))
    lg = lg_t.T + br_ref[...]
    lane = lax.broadcasted_iota(I32, (tm, LANES), 1).astype(F32)
    big = float(LANES)
    gmask = lane < N_EXPERT_GROUPS
    gl = jnp.where(gmask, lg, NEG_INF)
    gmax = jnp.max(gl, axis=1, keepdims=True)
    gsel = jnp.min(jnp.where(gl == gmax, lane, big), axis=1, keepdims=True)
    g_w = 1.0 / jnp.sum(jnp.exp(gl - gmax), axis=1, keepdims=True)
    eid = lane - N_EXPERT_GROUPS
    lo = gsel * EXPERTS_PER_GROUP
    emask = (eid >= lo) & (eid < lo + EXPERTS_PER_GROUP)
    el = jnp.where(emask, lg, NEG_INF)
    m1 = jnp.max(el, axis=1, keepdims=True)
    i1 = jnp.min(jnp.where(el == m1, eid, big), axis=1, keepdims=True)
    el2 = jnp.where(eid == i1, NEG_INF, el)
    m2 = jnp.max(el2, axis=1, keepdims=True)
    i2 = jnp.min(jnp.where(el2 == m2, eid, big), axis=1, keepdims=True)
    e2 = jnp.exp(m2 - m1)
    w1 = g_w / (1.0 + e2)
    w2 = g_w * e2 / (1.0 + e2)
    meta = jnp.where(lane == 0, i1, jnp.where(lane == 1, i2, jnp.where(lane == 2, w1, jnp.where(lane == 3, w2, 0.0))))
    meta_ref[...] = meta


def _out_proj(x2, y_ssd, y_att, wo_s, wo_a, ln2_w, wr_hi, wr_lo, b_router):
    n = x2.shape[0]
    full = lambda shape: pl.BlockSpec(shape, lambda i: (0, 0))
    return pl.pallas_call(
        _outproj_kernel,
        grid=(n // TM_OUT,),
        in_specs=[
            pl.BlockSpec((TM_OUT, D_MODEL), lambda i: (i, 0)),
            pl.BlockSpec((TM_OUT, SSD_WIDTH), lambda i: (i, 0)),
            pl.BlockSpec((TM_OUT, ATTN_WIDTH), lambda i: (i, 0)),
            full((SSD_WIDTH, D_MODEL)), full((ATTN_WIDTH, D_MODEL)),
            full((1, D_MODEL)), full((LANES, D_MODEL)), full((LANES, D_MODEL)), full((1, LANES)),
        ],
        out_specs=[
            pl.BlockSpec((TM_OUT, D_MODEL), lambda i: (i, 0)),
            pl.BlockSpec((TM_OUT * TOKEN_ROWS, LANES), lambda i: (i, 0)),
            pl.BlockSpec((TM_OUT, LANES), lambda i: (i, 0)),
        ],
        out_shape=[
            jax.ShapeDtypeStruct((n, D_MODEL), F32),
            jax.ShapeDtypeStruct((n * TOKEN_ROWS, LANES), U32),
            jax.ShapeDtypeStruct((n, LANES), F32),
        ],
        compiler_params=_cparams(("parallel",)),
        name="out_proj",
    )(x2, y_ssd, y_att, wo_s, wo_a, ln2_w, wr_hi, wr_lo, b_router)


def _rank_kernel(meta_ref, dest_ref, cnt_ref, run_ref, offs_ref):
    p = pl.program_id(0)
    i = pl.program_id(1)
    tm = meta_ref.shape[0]
    meta = meta_ref[...]
    lane = lax.broadcasted_iota(I32, (tm, LANES), 1).astype(F32)
    oh0 = (lane == meta[:, 0:1]).astype(F32)
    oh1 = (lane == meta[:, 1:2]).astype(F32)
    oh = oh0 + oh1
    colsum = jnp.sum(oh, axis=0, keepdims=True)

    @pl.when((p == 0) & (i == 0))
    def _():
        run_ref[...] = jnp.zeros_like(run_ref)

    @pl.when(p == 0)
    def _():
        run_ref[...] = run_ref[...] + colsum
        dest_ref[...] = jnp.zeros_like(dest_ref)
        cnt_ref[...] = jnp.broadcast_to(run_ref[...], cnt_ref.shape)

    @pl.when((p == 1) & (i == 0))
    def _():
        cnt = run_ref[...]
        padded = jnp.ceil(cnt * (1.0 / TM_EXP)) * TM_EXP
        r = lax.broadcasted_iota(I32, (LANES, LANES), 0)
        c = lax.broadcasted_iota(I32, (LANES, LANES), 1)
        excl = (r < c).astype(F32)
        offs = jnp.dot(jnp.broadcast_to(padded, (8, LANES)), excl, precision=HIGHEST, preferred_element_type=F32)
        offs_ref[...] = offs[0:1, :]
        cnt_ref[...] = jnp.broadcast_to(cnt, cnt_ref.shape)
        run_ref[...] = jnp.zeros_like(run_ref)

    @pl.when(p == 1)
    def _():
        r = lax.broadcasted_iota(I32, (tm, tm), 0)
        c = lax.broadcasted_iota(I32, (tm, tm), 1)
        before = jnp.dot((c < r).astype(BF16), oh.astype(BF16), preferred_element_type=F32)
        base = before + run_ref[...] + offs_ref[...]
        d0 = jnp.sum(oh0 * base, axis=1, keepdims=True)
        d1 = jnp.sum(oh1 * base, axis=1, keepdims=True)
        dest = jnp.where(lane == 0, d0, jnp.where(lane == 1, d1, 0.0))
        dest_ref[...] = dest.astype(I32)
        run_ref[...] = run_ref[...] + colsum


def _rank(meta):
    n = meta.shape[0]
    return pl.pallas_call(
        _rank_kernel,
        grid=(2, n // TM_RANK),
        in_specs=[pl.BlockSpec((TM_RANK, LANES), lambda p, i: (i, 0))],
        out_specs=[
            pl.BlockSpec((TM_RANK, LANES), lambda p, i: (i * p, 0)),
            pl.BlockSpec((8, LANES), lambda p, i: (0, 0)),
        ],
        out_shape=[
            jax.ShapeDtypeStruct((n, LANES), I32),
            jax.ShapeDtypeStruct((8, LANES), F32),
        ],
        scratch_shapes=[pltpu.VMEM((1, LANES), F32), pltpu.VMEM((1, LANES), F32)],
        compiler_params=_cparams(("arbitrary", "arbitrary")),
        name="rank",
    )(meta)


def _token_copy(src_ref, src_token, dst_ref, dst_token, sem):
    src = src_ref.at[pl.ds(pl.multiple_of(src_token * TOKEN_ROWS, TOKEN_ROWS), TOKEN_ROWS)]
    dst = dst_ref.at[pl.ds(pl.multiple_of(dst_token * TOKEN_ROWS, TOKEN_ROWS), TOKEN_ROWS)]
    return pltpu.make_async_copy(src, dst, sem)


def _invert_kernel(d0_ref, d1_ref, inv_ref):
    n = d0_ref.shape[0]

    def clear(s, _):
        inv_ref[s] = 0
        return 0

    lax.fori_loop(0, inv_ref.shape[0], clear, 0, unroll=8)

    def put(t, _):
        inv_ref[d0_ref[t]] = t
        inv_ref[d1_ref[t]] = n + t
        return 0

    lax.fori_loop(0, n, put, 0, unroll=4)


def _invert(dest0, dest1, n_slots):
    smem = pl.BlockSpec(memory_space=pltpu.SMEM)
    return pl.pallas_call(
        _invert_kernel,
        in_specs=[smem, smem],
        out_specs=smem,
        out_shape=jax.ShapeDtypeStruct((n_slots,), I32),
        name="invert",
    )(dest0, dest1)


CAST_ROWS = 256


def _cast_weight(src_ref, dst_ref):
    def body(i, _):
        rows = pl.ds(pl.multiple_of(i * CAST_ROWS, CAST_ROWS), CAST_ROWS)
        dst_ref[rows, :] = src_ref[rows, :].astype(BF16)
        return 0

    lax.fori_loop(0, src_ref.shape[0] // CAST_ROWS, body, 0)


S_EXPERT, S_FIRST, S_SLOT, S_NEXT, S_HAS_NEXT, S_USED, S_VALID = range(7)


def _load_expert_weights(t, sched_ref, triples, sems):
    def copies(expert, slot):
        return [pltpu.make_async_copy(w.at[expert], stage.at[slot], sems.at[i, slot])
                for i, (w, stage, _) in enumerate(triples)]

    @pl.when(sched_ref[S_FIRST, t] == 1)
    def _():
        slot = sched_ref[S_SLOT, t]

        @pl.when(t == 0)
        def _():
            for cp in copies(sched_ref[S_EXPERT, t], slot):
                cp.start()

        for cp in copies(sched_ref[S_EXPERT, t], slot):
            cp.wait()

        @pl.when(sched_ref[S_HAS_NEXT, t] == 1)
        def _():
            for cp in copies(sched_ref[S_NEXT, t], 1 - slot):
                cp.start()

        for _, stage, dst in triples:
            _cast_weight(stage.at[slot], dst)


ROW_DMA_GROUPS = 4


def _tile_rows(sched_ref, tile, group):
    valid = sched_ref[S_VALID, tile]
    per = TM_EXP // ROW_DMA_GROUPS
    for r in range(group * per, (group + 1) * per):
        ok = r < valid
        yield r, jnp.where(ok, tile * TM_EXP + r, 0), ok


def _experts_up_kernel(sched_ref, inv_ref, hnp_hbm, wg_hbm, wu_hbm, hid_ref,
                       xbuf_ref, wg_stage, wu_stage, wgb_ref, wub_ref, wsems, gsem, *, n_tokens):
    t = pl.program_id(0)
    used = sched_ref[S_USED, 0]
    cur = t % 2
    _load_expert_weights(t, sched_ref, [(wg_hbm, wg_stage, wgb_ref), (wu_hbm, wu_stage, wub_ref)], wsems)

    def start_rows(tile, buf, group):
        for r, slot, _ in _tile_rows(sched_ref, tile, group):
            token = inv_ref[slot] & (n_tokens - 1)
            _token_copy(hnp_hbm, token, xbuf_ref.at[buf], r, gsem.at[buf]).start()

    def wait_rows(buf):
        pltpu.make_async_copy(hnp_hbm.at[pl.ds(0, TM_EXP * TOKEN_ROWS)], xbuf_ref.at[buf], gsem.at[buf]).wait()

    @pl.when(t == 0)
    def _():
        for g in range(ROW_DMA_GROUPS):
            start_rows(0, 0, g)

    def multiply(fetch_next):
        wait_rows(cur)
        c = D_MODEL // 2
        x_lo, x_hi = _unpack_halves(_load_token_rows(xbuf_ref.at[cur], TM_EXP))
        pieces = ((x_lo, wgb_ref, 0), (x_hi, wgb_ref, c), (x_lo, wub_ref, 0), (x_hi, wub_ref, c))
        acc = []
        for g, (x, w_ref, row0) in enumerate(pieces):
            if fetch_next:
                start_rows(t + 1, 1 - cur, g)
            acc.append(jnp.dot(x, w_ref[row0:row0 + c, :], preferred_element_type=F32))
        hid_ref[...] = (_silu(acc[0] + acc[1]) * (acc[2] + acc[3])).astype(BF16)

    @pl.when(t < used - 1)
    def _():
        multiply(True)

    @pl.when(t == used - 1)
    def _():
        multiply(False)

    @pl.when(t >= used)
    def _():
        hid_ref[...] = jnp.zeros_like(hid_ref)


def _experts_down_kernel(sched_ref, inv_ref, hid_ref, wd_hbm, y_hbm,
                         ybuf_ref, wd_stage, wdb_ref, wsems, ssem, *, n_tokens):
    t = pl.program_id(0)
    used = sched_ref[S_USED, 0]
    cur = t % 2
    _load_expert_weights(t, sched_ref, [(wd_hbm, wd_stage, wdb_ref)], wsems)

    def start_rows(tile, buf, group):
        for r, slot, ok in _tile_rows(sched_ref, tile, group):
            dst = jnp.where(ok, inv_ref[slot], TOP_K * n_tokens + r)
            _token_copy(ybuf_ref.at[buf], r, y_hbm, dst, ssem.at[buf]).start()

    def wait_rows(buf):
        pltpu.make_async_copy(ybuf_ref.at[buf], y_hbm.at[pl.ds(0, TM_EXP * TOKEN_ROWS)], ssem.at[buf]).wait()

    def multiply(send_previous):
        hid = hid_ref[...]
        q = D_MODEL // 4
        for half in range(2):
            if send_previous:
                start_rows(t - 1, 1 - cur, 2 * half)
            lo = jnp.dot(hid, wdb_ref[:, half * q:(half + 1) * q], preferred_element_type=F32)
            if send_previous:
                start_rows(t - 1, 1 - cur, 2 * half + 1)
            hi = jnp.dot(hid, wdb_ref[:, (half + 2) * q:(half + 3) * q], preferred_element_type=F32)
            _store_token_rows(ybuf_ref.at[cur], half * (q // LANES), _pack_halves(jnp.concatenate([lo, hi], axis=1)))

    @pl.when((t >= 2) & (t < used))
    def _():
        wait_rows(cur)

    @pl.when(t == 0)
    def _():
        ybuf_ref[1] = jnp.zeros(ybuf_ref.shape[1:], ybuf_ref.dtype)
        spare = pltpu.make_async_copy(ybuf_ref.at[1], y_hbm.at[pl.ds(TOP_K * n_tokens * TOKEN_ROWS, TM_EXP * TOKEN_ROWS)],
                                      ssem.at[1])
        spare.start()
        spare.wait()
        multiply(False)

    @pl.when((t > 0) & (t < used))
    def _():
        multiply(True)

    @pl.when(t == used - 1)
    def _():
        for g in range(ROW_DMA_GROUPS):
            start_rows(t, cur, g)

        @pl.when(t > 0)
        def _():
            wait_rows(1 - cur)

        wait_rows(cur)


def _tile_schedule(counts, n_tiles):
    tiles_per_expert = (counts + TM_EXP - 1) // TM_EXP
    tile_end = jnp.cumsum(tiles_per_expert)
    ids = jnp.arange(n_tiles, dtype=I32)
    raw = jnp.sum(ids[:, None] >= tile_end[None, :], axis=1)
    expert = jnp.minimum(raw, N_EXPERTS - 1).astype(I32)
    first = jnp.concatenate([jnp.ones((1,), I32), (expert[1:] != expert[:-1]).astype(I32)])
    slot = (jnp.cumsum(first) - 1) % 2
    start_idx = jnp.where(first == 1, ids, n_tiles)
    next_start = jnp.concatenate([lax.cummin(start_idx, reverse=True)[1:], jnp.full((1,), n_tiles, I32)])
    has_next = (next_start < n_tiles).astype(I32)
    next_expert = jnp.sum(jnp.where(ids[None, :] == next_start[:, None], expert[None, :], 0), axis=1)
    used = jnp.full((n_tiles,), tile_end[-1], I32)
    onehot = expert[:, None] == jnp.arange(N_EXPERTS)[None, :]
    tile_in_expert = ids - jnp.sum(jnp.where(onehot, (tile_end - tiles_per_expert)[None, :], 0), axis=1)
    rows_left = jnp.sum(jnp.where(onehot, counts[None, :], 0), axis=1) - tile_in_expert * TM_EXP
    valid = jnp.where(raw < N_EXPERTS, jnp.clip(rows_left, 0, TM_EXP), 0)
    return jnp.stack([expert, first, slot, next_expert, has_next, used, valid]).astype(I32)


def _experts(sched, inv, hnp, wg, wu, wd):
    n = hnp.shape[0] // TOKEN_ROWS
    n_slots = inv.shape[0]
    token_buf = pltpu.VMEM((2, TM_EXP * TOKEN_ROWS, LANES), U32)
    row_spec = lambda width: pl.BlockSpec((TM_EXP, width), lambda t, sc, iv: (t, 0))
    hbm = pl.BlockSpec(memory_space=pl.ANY)
    big_vmem = pltpu.CompilerParams(dimension_semantics=("arbitrary",), vmem_limit_bytes=EXPERT_VMEM_LIMIT)
    hid = pl.pallas_call(
        functools.partial(_experts_up_kernel, n_tokens=n),
        grid_spec=pltpu.PrefetchScalarGridSpec(
            num_scalar_prefetch=2,
            grid=(n_slots // TM_EXP,),
            in_specs=[hbm, hbm, hbm],
            out_specs=row_spec(EXPERT_HIDDEN),
            scratch_shapes=[token_buf]
                           + [pltpu.VMEM((2, D_MODEL, EXPERT_HIDDEN), F32)] * 2
                           + [pltpu.VMEM((D_MODEL, EXPERT_HIDDEN), BF16)] * 2
                           + [pltpu.SemaphoreType.DMA((2, 2)), pltpu.SemaphoreType.DMA((2,))],
        ),
        out_shape=jax.ShapeDtypeStruct((n_slots, EXPERT_HIDDEN), BF16),
        compiler_params=big_vmem,
        name="experts_up",
    )(sched, inv, hnp, wg, wu)
    return pl.pallas_call(
        functools.partial(_experts_down_kernel, n_tokens=n),
        grid_spec=pltpu.PrefetchScalarGridSpec(
            num_scalar_prefetch=2,
            grid=(n_slots // TM_EXP,),
            in_specs=[row_spec(EXPERT_HIDDEN), hbm],
            out_specs=hbm,
            scratch_shapes=[token_buf,
                            pltpu.VMEM((2, EXPERT_HIDDEN, D_MODEL), F32), pltpu.VMEM((EXPERT_HIDDEN, D_MODEL), BF16),
                            pltpu.SemaphoreType.DMA((1, 2)), pltpu.SemaphoreType.DMA((2,))],
        ),
        out_shape=jax.ShapeDtypeStruct(((TOP_K * n + TM_EXP) * TOKEN_ROWS, LANES), U32),
        compiler_params=big_vmem,
        name="experts_down",
    )(sched, inv, hid, wd)


def _combine_kernel(h_ref, meta_ref, y0_ref, y1_ref, o_ref):
    c = D_MODEL // 2
    meta = meta_ref[...]
    w0, w1 = meta[:, 2:3], meta[:, 3:4]
    tm = h_ref.shape[0]
    lo0, hi0 = _unpack_halves(_load_token_rows(y0_ref, tm), F32)
    lo1, hi1 = _unpack_halves(_load_token_rows(y1_ref, tm), F32)
    o_ref[:, :c] = h_ref[:, :c] + (lo0 * w0 + lo1 * w1)
    o_ref[:, c:] = h_ref[:, c:] + (hi0 * w0 + hi1 * w1)


def _combine(h, meta, y):
    n = h.shape[0]
    nb = n // TM_COMB
    return pl.pallas_call(
        _combine_kernel,
        grid=(nb,),
        in_specs=[
            pl.BlockSpec((TM_COMB, D_MODEL), lambda i: (i, 0)),
            pl.BlockSpec((TM_COMB, LANES), lambda i: (i, 0)),
            pl.BlockSpec((TM_COMB * TOKEN_ROWS, LANES), lambda i: (i, 0)),
            pl.BlockSpec((TM_COMB * TOKEN_ROWS, LANES), lambda i: (nb + i, 0)),
        ],
        out_specs=pl.BlockSpec((TM_COMB, D_MODEL), lambda i: (i, 0)),
        out_shape=jax.ShapeDtypeStruct((n, D_MODEL), F32),
        compiler_params=_cparams(("parallel",)),
        name="combine",
    )(h, meta, y, y)


def _lambda_init(layer_idx):
    return 0.8 - 0.6 * math.exp(-0.3 * layer_idx)


def _pad_lanes(v, width=LANES):
    return jnp.pad(v, ((0, 0), (0, width - v.shape[1])))


def _layer(l, x2, pos_col, bsz, seq, ln1_w, w_in, conv_w, conv_b, dt_bias, a_log, d_skip, ssd_norm_w,
           q_norm_w, k_norm_w, lambda_q1, lambda_k1, lambda_q2, lambda_k2, subln_w, w_out, ln2_w,
           w_router_group, b_router_group, w_router_expert, b_router_expert, w_gate, w_up, w_down):
    n = x2.shape[0]
    c_z, c_xbc, c_dt = SSD_WIDTH, SSD_WIDTH + SSD_CONV_DIM, SSD_WIDTH + SSD_CONV_DIM + SSD_HEADS
    c_q, c_k = c_dt + ATTN_WIDTH, c_dt + 2 * ATTN_WIDTH
    w_main = jnp.concatenate([w_in[:, :c_z], w_in[:, c_dt:c_q], w_in[:, c_q:c_k], w_in[:, c_z:c_xbc]],
                             axis=1).astype(BF16)
    w_vt = w_in[:, c_k:].T.astype(BF16)
    w_dt = _pad_lanes(w_in[:, c_xbc:c_dt]).astype(BF16)

    u, v_t, dt_raw = _in_proj(x2, ln1_w[None, :], w_main, w_vt, w_dt)

    a_neg = _pad_lanes(-jnp.exp(a_log.astype(F32))[None, :])
    y_ssd = _ssd(u, dt_raw, conv_w, conv_b[None, :], _pad_lanes(dt_bias[None, :]), a_neg,
                 jnp.repeat(d_skip, SSD_HEAD_DIM)[None, :], ssd_norm_w[None, :], bsz, seq)

    inv_freq = jnp.power(ROPE_THETA, -jnp.arange(0, ROPE_DIM, 2, dtype=F32) / ROPE_DIM)
    d = jnp.arange(LANES) % ATTN_QK_DIM
    invf_lanes = jnp.where(d < ROPE_DIM, inv_freq[d % (ROPE_DIM // 2)], 0.0)[None, :]
    seg_ones = (jnp.arange(LANES)[:, None] // ATTN_QK_DIM == jnp.arange(LANES)[None, :] // ATTN_QK_DIM).astype(BF16)
    qp, kp = _qk_prep(u, pos_col, invf_lanes, jnp.tile(q_norm_w, 2)[None, :], jnp.tile(k_norm_w, 2)[None, :], seg_ones)

    lam_vecs = jnp.stack([lambda_q1, lambda_k1, lambda_q2, lambda_k2]).astype(F32)
    y_att = _attention(qp, kp, v_t, lam_vecs, subln_w[:, None], bsz, seq, _lambda_init(l))

    w_out_b = w_out.astype(BF16)
    w_router_t = _pad_lanes(jnp.concatenate([w_router_group, w_router_expert], axis=1)).T
    wr_hi = w_router_t.astype(BF16)
    wr_lo = (w_router_t - wr_hi.astype(F32)).astype(BF16)
    b_router = _pad_lanes(jnp.concatenate([b_router_group, b_router_expert])[None, :])
    h, hnp, meta = _out_proj(x2, y_ssd, y_att, w_out_b[:SSD_WIDTH], w_out_b[SSD_WIDTH:], ln2_w[None, :],
                             wr_hi, wr_lo, b_router)

    dest, cnt = _rank(meta)
    dest0, dest1 = dest[:, 0], dest[:, 1]
    counts = cnt[0, :N_EXPERTS].astype(I32)
    n_slots = (n * TOP_K + N_EXPERTS * (TM_EXP - 1)) // TM_EXP * TM_EXP
    sched = _tile_schedule(counts, n_slots // TM_EXP)

    inv = _invert(dest0, dest1, n_slots)
    y = _experts(sched, inv, hnp, w_gate, w_up, w_down)
    return _combine(h, meta, y)


def kernel(x, positions, ln1_w, w_in, conv_w, conv_b, dt_bias, a_log, d_skip, ssd_norm_w, q_norm_w, k_norm_w,
           lambda_q1, lambda_k1, lambda_q2, lambda_k2, subln_w, w_out, ln2_w, w_router_group, b_router_group,
           w_router_expert, b_router_expert, w_gate, w_up, w_down):
    bsz, seq, d = x.shape
    assert d == D_MODEL and seq % TQ == 0 and (bsz * seq) % TM_IN == 0
    x2 = x.reshape(bsz * seq, d)
    pos_col = positions.astype(F32).reshape(bsz * seq, 1)
    params = (ln1_w, w_in, conv_w, conv_b, dt_bias, a_log, d_skip, ssd_norm_w, q_norm_w, k_norm_w,
              lambda_q1, lambda_k1, lambda_q2, lambda_k2, subln_w, w_out, ln2_w, w_router_group, b_router_group,
              w_router_expert, b_router_expert, w_gate, w_up, w_down)
    for l in range(ln1_w.shape[0]):
        x2 = _layer(l, x2, pos_col, bsz, seq, *[p[l] for p in params])
    return x2.reshape(bsz, seq, d)
```

```python
import functools
import math

import jax
import jax.numpy as jnp
from jax import lax
from jax.experimental import pallas as pl
from jax.experimental.pallas import tpu as pltpu

F32 = jnp.float32
BF16 = jnp.bfloat16
I32 = jnp.int32
U32 = jnp.uint32
HIGHEST = lax.Precision.HIGHEST

D_MODEL = 2048
SSD_WIDTH = 1024
ATTN_WIDTH = 1024
SSD_HEAD_DIM = 64
SSD_HEADS = 16
SSD_GROUPS = 2
SSD_HEADS_PER_GROUP = SSD_HEADS // SSD_GROUPS
SSD_STATE = 128
SSD_CONV = 4
SSD_CHUNK = 128
SSD_CONV_DIM = SSD_WIDTH + 2 * SSD_GROUPS * SSD_STATE
ATTN_V_DIM = 128
ATTN_HEADS = 8
ATTN_QK_DIM = 64
ROPE_THETA = 500000.0
ROPE_DIM = 16
N_EXPERT_GROUPS = 4
EXPERTS_PER_GROUP = 8
N_EXPERTS = 32
TOP_K = 2
EXPERT_HIDDEN = 1024
EPS = 1e-6

LANES = 128
NEG_INF = float("-inf")

TM_IN = 512
TN_IN = 1536
TM_QK = 512
TQ = 256
ATTN_HB = 8
TM_OUT = 256
TM_RANK = 512
TM_EXP = 256
TM_DISP = 256
TM_COMB = 256
U_COLS = SSD_WIDTH + 2 * ATTN_WIDTH + SSD_CONV_DIM
VMEM_LIMIT = 52 * 1024 * 1024
EXPERT_VMEM_LIMIT = 58 * 1024 * 1024


def _cparams(sem):
    return pltpu.CompilerParams(dimension_semantics=sem, vmem_limit_bytes=VMEM_LIMIT)


def _silu(x):
    return x * (1.0 / (1.0 + jnp.exp(-x)))


def _softplus(x):
    return jnp.maximum(x, 0.0) + jnp.log(1.0 + jnp.exp(-jnp.abs(x)))


def _inproj_kernel(x_ref, lnw_ref, w_ref, wvt_ref, wdt_ref, u_ref, vt_ref, dt_ref, xn_ref):
    @pl.when(pl.program_id(1) == 0)
    def _():
        x = x_ref[...]
        ms = jnp.mean(x * x, axis=-1, keepdims=True)
        xn = (x * lax.rsqrt(ms + EPS) * lnw_ref[...]).astype(BF16)
        xn_ref[...] = xn
        dt_ref[...] = jnp.dot(xn, wdt_ref[...], preferred_element_type=F32)
        vt_ref[...] = lax.dot_general(wvt_ref[...], xn, (((1,), (1,)), ((), ())),
                                      preferred_element_type=F32).astype(BF16)

    u_ref[...] = jnp.dot(xn_ref[...], w_ref[...], preferred_element_type=F32).astype(BF16)


def _in_proj(x2, ln_w, w_main, w_vt, w_dt):
    n = x2.shape[0]
    return pl.pallas_call(
        _inproj_kernel,
        grid=(n // TM_IN, U_COLS // TN_IN),
        in_specs=[
            pl.BlockSpec((TM_IN, D_MODEL), lambda i, j: (i, 0)),
            pl.BlockSpec((1, D_MODEL), lambda i, j: (0, 0)),
            pl.BlockSpec((D_MODEL, TN_IN), lambda i, j: (0, j)),
            pl.BlockSpec((ATTN_WIDTH, D_MODEL), lambda i, j: (0, 0)),
            pl.BlockSpec((D_MODEL, LANES), lambda i, j: (0, 0)),
        ],
        out_specs=[
            pl.BlockSpec((TM_IN, TN_IN), lambda i, j: (i, j)),
            pl.BlockSpec((ATTN_WIDTH, TM_IN), lambda i, j: (0, i)),
            pl.BlockSpec((TM_IN, LANES), lambda i, j: (i, 0)),
        ],
        out_shape=[
            jax.ShapeDtypeStruct((n, U_COLS), BF16),
            jax.ShapeDtypeStruct((ATTN_WIDTH, n), BF16),
            jax.ShapeDtypeStruct((n, LANES), F32),
        ],
        scratch_shapes=[pltpu.VMEM((TM_IN, D_MODEL), BF16)],
        compiler_params=_cparams(("parallel", "arbitrary")),
        name="in_proj",
    )(x2, ln_w, w_main, w_vt, w_dt)


def _ssd_kernel(z_ref, xbc_ref, dt_ref, convw_ref, convb_ref, dtb_ref, aneg_ref, dskip_ref, normw_ref,
                y_ref, xp_ref, st_ref, yacc_ref):
    L = SSD_CHUNK
    P = SSD_HEAD_DIM

    @pl.when(pl.program_id(1) == 0)
    def _():
        xp_ref[0:8, :] = jnp.zeros((8, SSD_CONV_DIM), F32)
        st_ref[...] = jnp.zeros_like(st_ref)

    xp_ref[8:8 + L, :] = xbc_ref[...].astype(F32)
    acc = jnp.broadcast_to(convb_ref[...], (L, SSD_CONV_DIM))
    for k in range(SSD_CONV):
        acc = acc + xp_ref[5 + k:5 + k + L, :] * convw_ref[k:k + 1, :]
    xp_ref[0:8, :] = xp_ref[L:L + 8, :]
    xc = _silu(acc)

    dt = _softplus(dt_ref[...] + dtb_ref[...])
    a = dt * aneg_ref[...]
    row = lax.broadcasted_iota(I32, (L, L), 0)
    col = lax.broadcasted_iota(I32, (L, L), 1)
    causal = row >= col
    a_cs = jnp.dot(causal.astype(F32), a, precision=HIGHEST, preferred_element_type=F32)
    a_last = a_cs[L - 1:L, :]
    ea = jnp.exp(a_cs)
    dsdt = jnp.exp(a_last - a_cs) * dt
    cd = jnp.exp(a_last)
    a_cs_t = a_cs.T
    dt_t = dt.T
    dsdt_t = dsdt.T

    for g in range(SSD_GROUPS):
        b_g = xc[:, SSD_WIDTH + g * SSD_STATE:SSD_WIDTH + (g + 1) * SSD_STATE]
        c_off = SSD_WIDTH + SSD_GROUPS * SSD_STATE
        c_g = xc[:, c_off + g * SSD_STATE:c_off + (g + 1) * SSD_STATE]
        cb = lax.dot_general(c_g.astype(BF16), b_g.astype(BF16), (((1,), (1,)), ((), ())),
                             preferred_element_type=F32)
        b_gt = b_g.T
        for hh in range(SSD_HEADS_PER_GROUP):
            h = g * SSD_HEADS_PER_GROUP + hh
            xs_h = xc[:, h * P:(h + 1) * P].astype(BF16)
            seg = a_cs[:, h:h + 1] - a_cs_t[h:h + 1, :]
            dec = jnp.exp(jnp.where(causal, seg, NEG_INF))
            m = (cb * dec * dt_t[h:h + 1, :]).astype(BF16)
            c_s = (c_g * ea[:, h:h + 1]).astype(BF16)
            s_prev = st_ref[h]
            lhs = jnp.concatenate([m, c_s], axis=1)
            rhs = jnp.concatenate([xs_h, s_prev.astype(BF16)], axis=0)
            yacc_ref[:, h * P:(h + 1) * P] = jnp.dot(lhs, rhs, preferred_element_type=F32)
            bw = (b_gt * dsdt_t[h:h + 1, :]).astype(BF16)
            st_ref[h] = s_prev * cd[:, h:h + 1] + jnp.dot(bw, xs_h, preferred_element_type=F32)

    y = yacc_ref[...] + xc[:, :SSD_WIDTH] * dskip_ref[...]
    y = y * _silu(z_ref[...].astype(F32))
    gw = SSD_WIDTH // SSD_GROUPS
    for g in range(SSD_GROUPS):
        yg = y[:, g * gw:(g + 1) * gw]
        ms = jnp.mean(yg * yg, axis=-1, keepdims=True)
        y_ref[:, g * gw:(g + 1) * gw] = (yg * lax.rsqrt(ms + EPS) * normw_ref[:, g * gw:(g + 1) * gw]).astype(BF16)


def _ssd(u, dt_raw, conv_w, conv_b, dt_bias, a_neg, dskip_lanes, norm_w, bsz, seq):
    n = u.shape[0]
    nc = seq // SSD_CHUNK
    xbc_blk = (SSD_WIDTH + 2 * ATTN_WIDTH) // SSD_CONV_DIM
    full = lambda shape: pl.BlockSpec(shape, lambda b, c: (0, 0))
    return pl.pallas_call(
        _ssd_kernel,
        grid=(bsz, nc),
        in_specs=[
            pl.BlockSpec((SSD_CHUNK, SSD_WIDTH), lambda b, c: (b * nc + c, 0)),
            pl.BlockSpec((SSD_CHUNK, SSD_CONV_DIM), lambda b, c: (b * nc + c, xbc_blk)),
            pl.BlockSpec((SSD_CHUNK, LANES), lambda b, c: (b * nc + c, 0)),
            full((SSD_CONV, SSD_CONV_DIM)),
            full((1, SSD_CONV_DIM)),
            full((1, LANES)),
            full((1, LANES)),
            full((1, SSD_WIDTH)),
            full((1, SSD_WIDTH)),
        ],
        out_specs=pl.BlockSpec((SSD_CHUNK, SSD_WIDTH), lambda b, c: (b * nc + c, 0)),
        out_shape=jax.ShapeDtypeStruct((n, SSD_WIDTH), BF16),
        scratch_shapes=[
            pltpu.VMEM((SSD_CHUNK + 8, SSD_CONV_DIM), F32),
            pltpu.VMEM((SSD_HEADS, SSD_STATE, SSD_HEAD_DIM), F32),
            pltpu.VMEM((SSD_CHUNK, SSD_WIDTH), F32),
        ],
        compiler_params=_cparams(("parallel", "arbitrary")),
        name="ssd",
    )(u, u, dt_raw, conv_w, conv_b, dt_bias, a_neg, dskip_lanes, norm_w)


def _qkprep_kernel(q_ref, k_ref, pos_ref, invf_ref, qw_ref, kw_ref, ones_ref, qo_ref, ko_ref):
    tm = q_ref.shape[0]
    ang = pos_ref[...] * invf_ref[...]
    cs = jnp.cos(ang)
    sn = jnp.sin(ang)
    d = lax.broadcasted_iota(I32, (tm, LANES), 1) & (ATTN_QK_DIM - 1)
    half = ROPE_DIM // 2
    s_lo = jnp.where(d < half, -sn, 0.0)
    s_hi = jnp.where((d >= half) & (d < ROPE_DIM), sn, 0.0)
    scale = 1.0 / math.sqrt(ATTN_QK_DIM)
    for src, w_ref, dst, mul in ((q_ref, qw_ref, qo_ref, scale), (k_ref, kw_ref, ko_ref, 1.0)):
        for hb in range(ATTN_HEADS):
            x = src[:, hb * LANES:(hb + 1) * LANES].astype(F32)
            ss = jnp.dot((x * x).astype(BF16), ones_ref[...], preferred_element_type=F32)
            xn = x * lax.rsqrt(ss * (1.0 / ATTN_QK_DIM) + EPS) * w_ref[...]
            out = xn * cs + pltpu.roll(xn, LANES - half, 1) * s_lo + pltpu.roll(xn, half, 1) * s_hi
            dst[:, hb * LANES:(hb + 1) * LANES] = (out * mul).astype(BF16)


def _qk_prep(u, pos_col, invf_lanes, qw_lanes, kw_lanes, seg_ones):
    n = u.shape[0]
    full = lambda shape: pl.BlockSpec(shape, lambda i: (0, 0))
    return pl.pallas_call(
        _qkprep_kernel,
        grid=(n // TM_QK,),
        in_specs=[
            pl.BlockSpec((TM_QK, ATTN_WIDTH), lambda i: (i, 1)),
            pl.BlockSpec((TM_QK, ATTN_WIDTH), lambda i: (i, 2)),
            pl.BlockSpec((TM_QK, 1), lambda i: (i, 0)),
            full((1, LANES)), full((1, LANES)), full((1, LANES)), full((LANES, LANES)),
        ],
        out_specs=[pl.BlockSpec((TM_QK, ATTN_WIDTH), lambda i: (i, 0))] * 2,
        out_shape=[jax.ShapeDtypeStruct((n, ATTN_WIDTH), BF16)] * 2,
        compiler_params=_cparams(("parallel",)),
        name="qk_prep",
    )(u, u, pos_col, invf_lanes, qw_lanes, kw_lanes, seg_ones)


def _attn_kernel(q_ref, k_ref, vt_ref, lamv_ref, subw_ref, o_ref, acc_ref, *, lam_init):
    qi = pl.program_id(2)
    lane = lax.broadcasted_iota(I32, (TQ, LANES), 1)
    qs = []
    for hb in range(ATTN_HB):
        q = q_ref[:, hb * LANES:(hb + 1) * LANES]
        zero = jnp.zeros_like(q)
        qs.append(jnp.concatenate([jnp.where(lane < ATTN_QK_DIM, q, zero),
                                   jnp.where(lane >= ATTN_QK_DIM, q, zero)], axis=0))
    acc_ref[...] = jnp.zeros_like(acc_ref)
    kv_idx = lax.broadcasted_iota(I32, (TQ, 2 * TQ), 0)
    q_idx = lax.broadcasted_iota(I32, (TQ, 2 * TQ), 1) & (TQ - 1)
    nt = (((1,), (1,)), ((), ()))

    def block(j, carry, masked):
        off = pl.multiple_of(j * TQ, TQ)
        ss = []
        for hb in range(ATTN_HB):
            kb = k_ref[pl.ds(off, TQ), hb * LANES:(hb + 1) * LANES]
            ss.append(lax.dot_general(kb, qs[hb], nt, preferred_element_type=F32))
        new, ps, alphas = [], [], []
        for hb in range(ATTN_HB):
            m_old, l_old = carry[2 * hb], carry[2 * hb + 1]
            s = ss[hb]
            if masked:
                s = jnp.where(kv_idx <= q_idx, s, NEG_INF)
            m_new = jnp.maximum(m_old, jnp.max(s, axis=0, keepdims=True))
            alpha = jnp.exp(m_old - m_new)
            p = jnp.exp(s - m_new)
            new += [m_new, alpha * l_old + jnp.sum(p, axis=0, keepdims=True)]
            ps.append(p.astype(BF16))
            alphas.append(alpha)
        pvs = []
        for hb in range(ATTN_HB):
            vb = vt_ref[hb * LANES:(hb + 1) * LANES, pl.ds(off, TQ)]
            pvs.append(jnp.dot(vb, ps[hb], preferred_element_type=F32))
        for hb in range(ATTN_HB):
            acc_ref[hb] = alphas[hb] * acc_ref[hb] + pvs[hb]
        return tuple(new)

    init = (jnp.full((1, 2 * TQ), NEG_INF, F32), jnp.zeros((1, 2 * TQ), F32)) * ATTN_HB
    carry = lax.fori_loop(0, qi, lambda j, cr: block(j, cr, False), init)
    carry = block(qi, carry, True)

    lv = lamv_ref[...]
    lam = (jnp.exp(jnp.sum(lv[0:1] * lv[1:2], axis=1, keepdims=True))
           - jnp.exp(jnp.sum(lv[2:3] * lv[3:4], axis=1, keepdims=True)) + lam_init)
    for hb in range(ATTN_HB):
        o2 = acc_ref[hb] * (1.0 / carry[2 * hb + 1])
        o_t = o2[:, :TQ] - lam * o2[:, TQ:]
        ms = jnp.mean(o_t * o_t, axis=0, keepdims=True)
        o_t = o_t * lax.rsqrt(ms + EPS) * subw_ref[...] * (1.0 - lam_init)
        o_ref[:, hb * LANES:(hb + 1) * LANES] = o_t.T.astype(BF16)


def _attention(qp, kp, v_t, lam_vecs, subw_col, bsz, seq, lam_init):
    n = qp.shape[0]
    nq = seq // TQ
    w = ATTN_HB * ATTN_V_DIM
    return pl.pallas_call(
        functools.partial(_attn_kernel, lam_init=lam_init),
        grid=(bsz, ATTN_HEADS // ATTN_HB, nq),
        in_specs=[
            pl.BlockSpec((TQ, w), lambda b, h, i: (b * nq + i, h)),
            pl.BlockSpec((seq, w), lambda b, h, i: (b, h)),
            pl.BlockSpec((w, seq), lambda b, h, i: (h, b)),
            pl.BlockSpec((4, ATTN_QK_DIM), lambda b, h, i: (0, 0)),
            pl.BlockSpec((ATTN_V_DIM, 1), lambda b, h, i: (0, 0)),
        ],
        out_specs=pl.BlockSpec((TQ, w), lambda b, h, i: (b * nq + i, h)),
        out_shape=jax.ShapeDtypeStruct((n, ATTN_WIDTH), BF16),
        scratch_shapes=[pltpu.VMEM((ATTN_HB, ATTN_V_DIM, 2 * TQ), F32)],
        compiler_params=_cparams(("parallel", "parallel", "arbitrary")),
        name="attn",
    )(qp, kp, v_t, lam_vecs, subw_col)


def _pack_halves(x):
    c = x.shape[1] // 2
    lo = pltpu.bitcast(x[:, :c].astype(BF16).astype(F32), U32) >> 16
    hi = pltpu.bitcast(x[:, c:].astype(BF16).astype(F32), U32) & jnp.uint32(0xFFFF0000)
    return hi | lo


TOKEN_ROWS = D_MODEL // 2 // LANES


def _store_token_rows(ref, first_piece, packed):
    tm = packed.shape[0]
    for j in range(packed.shape[1] // LANES):
        ref[pl.ds(first_piece + j, tm, stride=TOKEN_ROWS), :] = packed[:, j * LANES:(j + 1) * LANES]


def _load_token_rows(ref, tm):
    return jnp.concatenate([ref[pl.ds(s, tm, stride=TOKEN_ROWS), :] for s in range(TOKEN_ROWS)], axis=1)


def _unpack_halves(w, dtype=BF16):
    lo = pltpu.bitcast(w << 16, F32).astype(dtype)
    hi = pltpu.bitcast(w & jnp.uint32(0xFFFF0000), F32).astype(dtype)
    return lo, hi


def _outproj_kernel(x_ref, ys_ref, ya_ref, wos_ref, woa_ref, ln2_ref, wrh_ref, wrl_ref, br_ref,
                    h_ref, hnp_ref, meta_ref):
    tm = x_ref.shape[0]
    h = (x_ref[...]
         + jnp.dot(ys_ref[...], wos_ref[...], preferred_element_type=F32)
         + jnp.dot(ya_ref[...], woa_ref[...], preferred_element_type=F32))
    h_ref[...] = h
    ms = jnp.mean(h * h, axis=-1, keepdims=True)
    hn = h * lax.rsqrt(ms + EPS) * ln2_ref[...]
    _store_token_rows(hnp_ref, 0, _pack_halves(hn))

    hn_hi = hn.astype(BF16)
    hn_lo = (hn - hn_hi.astype(F32)).astype(BF16)
    nt = (((1,), (1,)), ((), ()))
    lg_t = (lax.dot_general(wrh_ref[...], hn_hi, nt, preferred_element_type=F32)
            + lax.dot_general(wrh_ref[...], hn_lo, nt, preferred_element_type=F32)
            + lax.dot_general(wrl_ref[...], hn_hi, nt, preferred_element_type=F32))
    lg = lg_t.T + br_ref[...]
    lane = lax.broadcasted_iota(I32, (tm, LANES), 1).astype(F32)
    big = float(LANES)
    gmask = lane < N_EXPERT_GROUPS
    gl = jnp.where(gmask, lg, NEG_INF)
    gmax = jnp.max(gl, axis=1, keepdims=True)
    gsel = jnp.min(jnp.where(gl == gmax, lane, big), axis=1, keepdims=True)
    g_w = 1.0 / jnp.sum(jnp.exp(gl - gmax), axis=1, keepdims=True)
    eid = lane - N_EXPERT_GROUPS
    lo = gsel * EXPERTS_PER_GROUP
    emask = (eid >= lo) & (eid < lo + EXPERTS_PER_GROUP)
    el = jnp.where(emask, lg, NEG_INF)
    m1 = jnp.max(el, axis=1, keepdims=True)
    i1 = jnp.min(jnp.where(el == m1, eid, big), axis=1, keepdims=True)
    el2 = jnp.where(eid == i1, NEG_INF, el)
    m2 = jnp.max(el2, axis=1, keepdims=True)
    i2 = jnp.min(jnp.where(el2 == m2, eid, big), axis=1, keepdims=True)
    e2 = jnp.exp(m2 - m1)
    w1 = g_w / (1.0 + e2)
    w2 = g_w * e2 / (1.0 + e2)
    meta = jnp.where(lane == 0, i1, jnp.where(lane == 1, i2, jnp.where(lane == 2, w1, jnp.where(lane == 3, w2, 0.0))))
    meta_ref[...] = meta


def _out_proj(x2, y_ssd, y_att, wo_s, wo_a, ln2_w, wr_hi, wr_lo, b_router):
    n = x2.shape[0]
    full = lambda shape: pl.BlockSpec(shape, lambda i: (0, 0))
    return pl.pallas_call(
        _outproj_kernel,
        grid=(n // TM_OUT,),
        in_specs=[
            pl.BlockSpec((TM_OUT, D_MODEL), lambda i: (i, 0)),
            pl.BlockSpec((TM_OUT, SSD_WIDTH), lambda i: (i, 0)),
            pl.BlockSpec((TM_OUT, ATTN_WIDTH), lambda i: (i, 0)),
            full((SSD_WIDTH, D_MODEL)), full((ATTN_WIDTH, D_MODEL)),
            full((1, D_MODEL)), full((LANES, D_MODEL)), full((LANES, D_MODEL)), full((1, LANES)),
        ],
        out_specs=[
            pl.BlockSpec((TM_OUT, D_MODEL), lambda i: (i, 0)),
            pl.BlockSpec((TM_OUT * TOKEN_ROWS, LANES), lambda i: (i, 0)),
            pl.BlockSpec((TM_OUT, LANES), lambda i: (i, 0)),
        ],
        out_shape=[
            jax.ShapeDtypeStruct((n, D_MODEL), F32),
            jax.ShapeDtypeStruct((n * TOKEN_ROWS, LANES), U32),
            jax.ShapeDtypeStruct((n, LANES), F32),
        ],
        compiler_params=_cparams(("parallel",)),
        name="out_proj",
    )(x2, y_ssd, y_att, wo_s, wo_a, ln2_w, wr_hi, wr_lo, b_router)


def _rank_kernel(meta_ref, dest_ref, cnt_ref, run_ref, offs_ref):
    p = pl.program_id(0)
    i = pl.program_id(1)
    tm = meta_ref.shape[0]
    meta = meta_ref[...]
    lane = lax.broadcasted_iota(I32, (tm, LANES), 1).astype(F32)
    oh0 = (lane == meta[:, 0:1]).astype(F32)
    oh1 = (lane == meta[:, 1:2]).astype(F32)
    oh = oh0 + oh1
    colsum = jnp.sum(oh, axis=0, keepdims=True)

    @pl.when((p == 0) & (i == 0))
    def _():
        run_ref[...] = jnp.zeros_like(run_ref)

    @pl.when(p == 0)
    def _():
        run_ref[...] = run_ref[...] + colsum
        dest_ref[...] = jnp.zeros_like(dest_ref)
        cnt_ref[...] = jnp.broadcast_to(run_ref[...], cnt_ref.shape)

    @pl.when((p == 1) & (i == 0))
    def _():
        cnt = run_ref[...]
        padded = jnp.ceil(cnt * (1.0 / TM_EXP)) * TM_EXP
        r = lax.broadcasted_iota(I32, (LANES, LANES), 0)
        c = lax.broadcasted_iota(I32, (LANES, LANES), 1)
        excl = (r < c).astype(F32)
        offs = jnp.dot(jnp.broadcast_to(padded, (8, LANES)), excl, precision=HIGHEST, preferred_element_type=F32)
        offs_ref[...] = offs[0:1, :]
        cnt_ref[...] = jnp.broadcast_to(cnt, cnt_ref.shape)
        run_ref[...] = jnp.zeros_like(run_ref)

    @pl.when(p == 1)
    def _():
        r = lax.broadcasted_iota(I32, (tm, tm), 0)
        c = lax.broadcasted_iota(I32, (tm, tm), 1)
        before = jnp.dot((c < r).astype(BF16), oh.astype(BF16), preferred_element_type=F32)
        base = before + run_ref[...] + offs_ref[...]
        d0 = jnp.sum(oh0 * base, axis=1, keepdims=True)
        d1 = jnp.sum(oh1 * base, axis=1, keepdims=True)
        dest = jnp.where(lane == 0, d0, jnp.where(lane == 1, d1, 0.0))
        dest_ref[...] = dest.astype(I32)
        run_ref[...] = run_ref[...] + colsum


def _rank(meta):
    n = meta.shape[0]
    return pl.pallas_call(
        _rank_kernel,
        grid=(2, n // TM_RANK),
        in_specs=[pl.BlockSpec((TM_RANK, LANES), lambda p, i: (i, 0))],
        out_specs=[
            pl.BlockSpec((TM_RANK, LANES), lambda p, i: (i * p, 0)),
            pl.BlockSpec((8, LANES), lambda p, i: (0, 0)),
        ],
        out_shape=[
            jax.ShapeDtypeStruct((n, LANES), I32),
            jax.ShapeDtypeStruct((8, LANES), F32),
        ],
        scratch_shapes=[pltpu.VMEM((1, LANES), F32), pltpu.VMEM((1, LANES), F32)],
        compiler_params=_cparams(("arbitrary", "arbitrary")),
        name="rank",
    )(meta)


def _token_copy(src_ref, src_token, dst_ref, dst_token, sem):
    src = src_ref.at[pl.ds(pl.multiple_of(src_token * TOKEN_ROWS, TOKEN_ROWS), TOKEN_ROWS)]
    dst = dst_ref.at[pl.ds(pl.multiple_of(dst_token * TOKEN_ROWS, TOKEN_ROWS), TOKEN_ROWS)]
    return pltpu.make_async_copy(src, dst, sem)


def _invert_kernel(d0_ref, d1_ref, inv_ref):
    n = d0_ref.shape[0]

    def clear(s, _):
        inv_ref[s] = 0
        return 0

    lax.fori_loop(0, inv_ref.shape[0], clear, 0, unroll=8)

    def put(t, _):
        inv_ref[d0_ref[t]] = t
        inv_ref[d1_ref[t]] = n + t
        return 0

    lax.fori_loop(0, n, put, 0, unroll=4)


def _invert(dest0, dest1, n_slots):
    smem = pl.BlockSpec(memory_space=pltpu.SMEM)
    return pl.pallas_call(
        _invert_kernel,
        in_specs=[smem, smem],
        out_specs=smem,
        out_shape=jax.ShapeDtypeStruct((n_slots,), I32),
        name="invert",
    )(dest0, dest1)


CAST_ROWS = 256


def _cast_weight(src_ref, dst_ref):
    def body(i, _):
        rows = pl.ds(pl.multiple_of(i * CAST_ROWS, CAST_ROWS), CAST_ROWS)
        dst_ref[rows, :] = src_ref[rows, :].astype(BF16)
        return 0

    lax.fori_loop(0, src_ref.shape[0] // CAST_ROWS, body, 0)


S_EXPERT, S_FIRST, S_SLOT, S_NEXT, S_HAS_NEXT, S_USED, S_VALID = range(7)
WEIGHT_DMA_PRIORITY = 1


def _load_expert_weights(t, sched_ref, triples, sems):
    def copies(expert, slot):
        return [pltpu.make_async_copy(w.at[expert], stage.at[slot], sems.at[i, slot])
                for i, (w, stage, _) in enumerate(triples)]

    @pl.when(sched_ref[S_FIRST, t] == 1)
    def _():
        slot = sched_ref[S_SLOT, t]

        @pl.when(t == 0)
        def _():
            for cp in copies(sched_ref[S_EXPERT, t], slot):
                cp.start(priority=WEIGHT_DMA_PRIORITY)

        for cp in copies(sched_ref[S_EXPERT, t], slot):
            cp.wait()

        @pl.when(sched_ref[S_HAS_NEXT, t] == 1)
        def _():
            for cp in copies(sched_ref[S_NEXT, t], 1 - slot):
                cp.start(priority=WEIGHT_DMA_PRIORITY)

        for _, stage, dst in triples:
            _cast_weight(stage.at[slot], dst)


ROW_DMA_GROUPS = 4


def _tile_rows(sched_ref, tile, group):
    valid = sched_ref[S_VALID, tile]
    per = TM_EXP // ROW_DMA_GROUPS
    for r in range(group * per, (group + 1) * per):
        ok = r < valid
        yield r, jnp.where(ok, tile * TM_EXP + r, 0), ok


def _experts_up_kernel(sched_ref, inv_ref, hnp_hbm, wg_hbm, wu_hbm, hid_ref,
                       xbuf_ref, wg_stage, wu_stage, wgb_ref, wub_ref, wsems, gsem, *, n_tokens):
    t = pl.program_id(0)
    used = sched_ref[S_USED, 0]
    cur = t % 2
    _load_expert_weights(t, sched_ref, [(wg_hbm, wg_stage, wgb_ref), (wu_hbm, wu_stage, wub_ref)], wsems)

    def start_rows(tile, buf, group):
        for r, slot, _ in _tile_rows(sched_ref, tile, group):
            token = inv_ref[slot] & (n_tokens - 1)
            _token_copy(hnp_hbm, token, xbuf_ref.at[buf], r, gsem.at[buf]).start()

    def wait_rows(buf):
        pltpu.make_async_copy(hnp_hbm.at[pl.ds(0, TM_EXP * TOKEN_ROWS)], xbuf_ref.at[buf], gsem.at[buf]).wait()

    @pl.when(t == 0)
    def _():
        for g in range(ROW_DMA_GROUPS):
            start_rows(0, 0, g)

    def multiply(fetch_next):
        wait_rows(cur)
        c = D_MODEL // 2
        x_lo, x_hi = _unpack_halves(_load_token_rows(xbuf_ref.at[cur], TM_EXP))
        pieces = ((x_lo, wgb_ref, 0), (x_hi, wgb_ref, c), (x_lo, wub_ref, 0), (x_hi, wub_ref, c))
        acc = []
        for g, (x, w_ref, row0) in enumerate(pieces):
            if fetch_next:
                start_rows(t + 1, 1 - cur, g)
            acc.append(jnp.dot(x, w_ref[row0:row0 + c, :], preferred_element_type=F32))
        hid_ref[...] = (_silu(acc[0] + acc[1]) * (acc[2] + acc[3])).astype(BF16)

    @pl.when(t < used - 1)
    def _():
        multiply(True)

    @pl.when(t == used - 1)
    def _():
        multiply(False)

    @pl.when(t >= used)
    def _():
        hid_ref[...] = jnp.zeros_like(hid_ref)


def _experts_down_kernel(sched_ref, inv_ref, hid_ref, wd_hbm, y_hbm,
                         ybuf_ref, wd_stage, wdb_ref, wsems, ssem, *, n_tokens):
    t = pl.program_id(0)
    used = sched_ref[S_USED, 0]
    cur = t % 2
    _load_expert_weights(t, sched_ref, [(wd_hbm, wd_stage, wdb_ref)], wsems)

    def start_rows(tile, buf, group):
        for r, slot, ok in _tile_rows(sched_ref, tile, group):
            dst = jnp.where(ok, inv_ref[slot], TOP_K * n_tokens + r)
            _token_copy(ybuf_ref.at[buf], r, y_hbm, dst, ssem.at[buf]).start(priority=r % 2)

    def wait_rows(buf):
        pltpu.make_async_copy(ybuf_ref.at[buf], y_hbm.at[pl.ds(0, TM_EXP * TOKEN_ROWS)], ssem.at[buf]).wait()

    def multiply(send_previous):
        hid = hid_ref[...]
        q = D_MODEL // 4
        for half in range(2):
            if send_previous:
                start_rows(t - 1, 1 - cur, 2 * half)
            lo = jnp.dot(hid, wdb_ref[:, half * q:(half + 1) * q], preferred_element_type=F32)
            if send_previous:
                start_rows(t - 1, 1 - cur, 2 * half + 1)
            hi = jnp.dot(hid, wdb_ref[:, (half + 2) * q:(half + 3) * q], preferred_element_type=F32)
            _store_token_rows(ybuf_ref.at[cur], half * (q // LANES), _pack_halves(jnp.concatenate([lo, hi], axis=1)))

    @pl.when((t >= 2) & (t < used))
    def _():
        wait_rows(cur)

    @pl.when(t == 0)
    def _():
        ybuf_ref[1] = jnp.zeros(ybuf_ref.shape[1:], ybuf_ref.dtype)
        spare = pltpu.make_async_copy(ybuf_ref.at[1], y_hbm.at[pl.ds(TOP_K * n_tokens * TOKEN_ROWS, TM_EXP * TOKEN_ROWS)],
                                      ssem.at[1])
        spare.start()
        spare.wait()
        multiply(False)

    @pl.when((t > 0) & (t < used))
    def _():
        multiply(True)

    @pl.when(t == used - 1)
    def _():
        for g in range(ROW_DMA_GROUPS):
            start_rows(t, cur, g)

        @pl.when(t > 0)
        def _():
            wait_rows(1 - cur)

        wait_rows(cur)


def _tile_schedule(counts, n_tiles):
    tiles_per_expert = (counts + TM_EXP - 1) // TM_EXP
    tile_end = jnp.cumsum(tiles_per_expert)
    ids = jnp.arange(n_tiles, dtype=I32)
    raw = jnp.sum(ids[:, None] >= tile_end[None, :], axis=1)
    expert = jnp.minimum(raw, N_EXPERTS - 1).astype(I32)
    first = jnp.concatenate([jnp.ones((1,), I32), (expert[1:] != expert[:-1]).astype(I32)])
    slot = (jnp.cumsum(first) - 1) % 2
    start_idx = jnp.where(first == 1, ids, n_tiles)
    next_start = jnp.concatenate([lax.cummin(start_idx, reverse=True)[1:], jnp.full((1,), n_tiles, I32)])
    has_next = (next_start < n_tiles).astype(I32)
    next_expert = jnp.sum(jnp.where(ids[None, :] == next_start[:, None], expert[None, :], 0), axis=1)
    used = jnp.full((n_tiles,), tile_end[-1], I32)
    onehot = expert[:, None] == jnp.arange(N_EXPERTS)[None, :]
    tile_in_expert = ids - jnp.sum(jnp.where(onehot, (tile_end - tiles_per_expert)[None, :], 0), axis=1)
    rows_left = jnp.sum(jnp.where(onehot, counts[None, :], 0), axis=1) - tile_in_expert * TM_EXP
    valid = jnp.where(raw < N_EXPERTS, jnp.clip(rows_left, 0, TM_EXP), 0)
    return jnp.stack([expert, first, slot, next_expert, has_next, used, valid]).astype(I32)


def _experts(sched, inv, hnp, wg, wu, wd):
    n = hnp.shape[0] // TOKEN_ROWS
    n_slots = inv.shape[0]
    token_buf = pltpu.VMEM((2, TM_EXP * TOKEN_ROWS, LANES), U32)
    row_spec = lambda width: pl.BlockSpec((TM_EXP, width), lambda t, sc, iv: (t, 0))
    hbm = pl.BlockSpec(memory_space=pl.ANY)
    big_vmem = pltpu.CompilerParams(dimension_semantics=("arbitrary",), vmem_limit_bytes=EXPERT_VMEM_LIMIT)
    hid = pl.pallas_call(
        functools.partial(_experts_up_kernel, n_tokens=n),
        grid_spec=pltpu.PrefetchScalarGridSpec(
            num_scalar_prefetch=2,
            grid=(n_slots // TM_EXP,),
            in_specs=[hbm, hbm, hbm],
            out_specs=row_spec(EXPERT_HIDDEN),
            scratch_shapes=[token_buf]
                           + [pltpu.VMEM((2, D_MODEL, EXPERT_HIDDEN), F32)] * 2
                           + [pltpu.VMEM((D_MODEL, EXPERT_HIDDEN), BF16)] * 2
                           + [pltpu.SemaphoreType.DMA((2, 2)), pltpu.SemaphoreType.DMA((2,))],
        ),
        out_shape=jax.ShapeDtypeStruct((n_slots, EXPERT_HIDDEN), BF16),
        compiler_params=big_vmem,
        name="experts_up",
    )(sched, inv, hnp, wg, wu)
    return pl.pallas_call(
        functools.partial(_experts_down_kernel, n_tokens=n),
        grid_spec=pltpu.PrefetchScalarGridSpec(
            num_scalar_prefetch=2,
            grid=(n_slots // TM_EXP,),
            in_specs=[row_spec(EXPERT_HIDDEN), hbm],
            out_specs=hbm,
            scratch_shapes=[token_buf,
                            pltpu.VMEM((2, EXPERT_HIDDEN, D_MODEL), F32), pltpu.VMEM((EXPERT_HIDDEN, D_MODEL), BF16),
                            pltpu.SemaphoreType.DMA((1, 2)), pltpu.SemaphoreType.DMA((2,))],
        ),
        out_shape=jax.ShapeDtypeStruct(((TOP_K * n + TM_EXP) * TOKEN_ROWS, LANES), U32),
        compiler_params=big_vmem,
        name="experts_down",
    )(sched, inv, hid, wd)


def _combine_kernel(h_ref, meta_ref, y0_ref, y1_ref, o_ref):
    c = D_MODEL // 2
    meta = meta_ref[...]
    w0, w1 = meta[:, 2:3], meta[:, 3:4]
    tm = h_ref.shape[0]
    lo0, hi0 = _unpack_halves(_load_token_rows(y0_ref, tm), F32)
    lo1, hi1 = _unpack_halves(_load_token_rows(y1_ref, tm), F32)
    o_ref[:, :c] = h_ref[:, :c] + (lo0 * w0 + lo1 * w1)
    o_ref[:, c:] = h_ref[:, c:] + (hi0 * w0 + hi1 * w1)


def _combine(h, meta, y):
    n = h.shape[0]
    nb = n // TM_COMB
    return pl.pallas_call(
        _combine_kernel,
        grid=(nb,),
        in_specs=[
            pl.BlockSpec((TM_COMB, D_MODEL), lambda i: (i, 0)),
            pl.BlockSpec((TM_COMB, LANES), lambda i: (i, 0)),
            pl.BlockSpec((TM_COMB * TOKEN_ROWS, LANES), lambda i: (i, 0)),
            pl.BlockSpec((TM_COMB * TOKEN_ROWS, LANES), lambda i: (nb + i, 0)),
        ],
        out_specs=pl.BlockSpec((TM_COMB, D_MODEL), lambda i: (i, 0)),
        out_shape=jax.ShapeDtypeStruct((n, D_MODEL), F32),
        compiler_params=_cparams(("parallel",)),
        name="combine",
    )(h, meta, y, y)


def _lambda_init(layer_idx):
    return 0.8 - 0.6 * math.exp(-0.3 * layer_idx)


def _pad_lanes(v, width=LANES):
    return jnp.pad(v, ((0, 0), (0, width - v.shape[1])))


def _layer(l, x2, pos_col, bsz, seq, ln1_w, w_in, conv_w, conv_b, dt_bias, a_log, d_skip, ssd_norm_w,
           q_norm_w, k_norm_w, lambda_q1, lambda_k1, lambda_q2, lambda_k2, subln_w, w_out, ln2_w,
           w_router_group, b_router_group, w_router_expert, b_router_expert, w_gate, w_up, w_down):
    n = x2.shape[0]
    c_z, c_xbc, c_dt = SSD_WIDTH, SSD_WIDTH + SSD_CONV_DIM, SSD_WIDTH + SSD_CONV_DIM + SSD_HEADS
    c_q, c_k = c_dt + ATTN_WIDTH, c_dt + 2 * ATTN_WIDTH
    w_main = jnp.concatenate([w_in[:, :c_z], w_in[:, c_dt:c_q], w_in[:, c_q:c_k], w_in[:, c_z:c_xbc]],
                             axis=1).astype(BF16)
    w_vt = w_in[:, c_k:].T.astype(BF16)
    w_dt = _pad_lanes(w_in[:, c_xbc:c_dt]).astype(BF16)

    u, v_t, dt_raw = _in_proj(x2, ln1_w[None, :], w_main, w_vt, w_dt)

    a_neg = _pad_lanes(-jnp.exp(a_log.astype(F32))[None, :])
    y_ssd = _ssd(u, dt_raw, conv_w, conv_b[None, :], _pad_lanes(dt_bias[None, :]), a_neg,
                 jnp.repeat(d_skip, SSD_HEAD_DIM)[None, :], ssd_norm_w[None, :], bsz, seq)

    inv_freq = jnp.power(ROPE_THETA, -jnp.arange(0, ROPE_DIM, 2, dtype=F32) / ROPE_DIM)
    d = jnp.arange(LANES) % ATTN_QK_DIM
    invf_lanes = jnp.where(d < ROPE_DIM, inv_freq[d % (ROPE_DIM // 2)], 0.0)[None, :]
    seg_ones = (jnp.arange(LANES)[:, None] // ATTN_QK_DIM == jnp.arange(LANES)[None, :] // ATTN_QK_DIM).astype(BF16)
    qp, kp = _qk_prep(u, pos_col, invf_lanes, jnp.tile(q_norm_w, 2)[None, :], jnp.tile(k_norm_w, 2)[None, :], seg_ones)

    lam_vecs = jnp.stack([lambda_q1, lambda_k1, lambda_q2, lambda_k2]).astype(F32)
    y_att = _attention(qp, kp, v_t, lam_vecs, subln_w[:, None], bsz, seq, _lambda_init(l))

    w_out_b = w_out.astype(BF16)
    w_router_t = _pad_lanes(jnp.concatenate([w_router_group, w_router_expert], axis=1)).T
    wr_hi = w_router_t.astype(BF16)
    wr_lo = (w_router_t - wr_hi.astype(F32)).astype(BF16)
    b_router = _pad_lanes(jnp.concatenate([b_router_group, b_router_expert])[None, :])
    h, hnp, meta = _out_proj(x2, y_ssd, y_att, w_out_b[:SSD_WIDTH], w_out_b[SSD_WIDTH:], ln2_w[None, :],
                             wr_hi, wr_lo, b_router)

    dest, cnt = _rank(meta)
    dest0, dest1 = dest[:, 0], dest[:, 1]
    counts = cnt[0, :N_EXPERTS].astype(I32)
    n_slots = (n * TOP_K + N_EXPERTS * (TM_EXP - 1)) // TM_EXP * TM_EXP
    sched = _tile_schedule(counts, n_slots // TM_EXP)

    inv = _invert(dest0, dest1, n_slots)
    y = _experts(sched, inv, hnp, w_gate, w_up, w_down)
    return _combine(h, meta, y)


def kernel(x, positions, ln1_w, w_in, conv_w, conv_b, dt_bias, a_log, d_skip, ssd_norm_w, q_norm_w, k_norm_w,
           lambda_q1, lambda_k1, lambda_q2, lambda_k2, subln_w, w_out, ln2_w, w_router_group, b_router_group,
           w_router_expert, b_router_expert, w_gate, w_up, w_down):
    bsz, seq, d = x.shape
    assert d == D_MODEL and seq % TQ == 0 and (bsz * seq) % TM_IN == 0
    x2 = x.reshape(bsz * seq, d)
    pos_col = positions.astype(F32).reshape(bsz * seq, 1)
    params = (ln1_w, w_in, conv_w, conv_b, dt_bias, a_log, d_skip, ssd_norm_w, q_norm_w, k_norm_w,
              lambda_q1, lambda_k1, lambda_q2, lambda_k2, subln_w, w_out, ln2_w, w_router_group, b_router_group,
              w_router_expert, b_router_expert, w_gate, w_up, w_down)
    for l in range(ln1_w.shape[0]):
        x2 = _layer(l, x2, pos_col, bsz, seq, *[p[l] for p in params])
    return x2.reshape(bsz, seq, d)
```

```python
import functools
import math

import jax
import jax.numpy as jnp
from jax import lax
from jax.experimental import pallas as pl
from jax.experimental.pallas import tpu as pltpu

F32 = jnp.float32
BF16 = jnp.bfloat16
I32 = jnp.int32
U32 = jnp.uint32
HIGHEST = lax.Precision.HIGHEST

D_MODEL = 2048
SSD_WIDTH = 1024
ATTN_WIDTH = 1024
SSD_HEAD_DIM = 64
SSD_HEADS = 16
SSD_GROUPS = 2
SSD_HEADS_PER_GROUP = SSD_HEADS // SSD_GROUPS
SSD_STATE = 128
SSD_CONV = 4
SSD_CHUNK = 128
SSD_CONV_DIM = SSD_WIDTH + 2 * SSD_GROUPS * SSD_STATE
ATTN_V_DIM = 128
ATTN_HEADS = 8
ATTN_QK_DIM = 64
ROPE_THETA = 500000.0
ROPE_DIM = 16
N_EXPERT_GROUPS = 4
EXPERTS_PER_GROUP = 8
N_EXPERTS = 32
TOP_K = 2
EXPERT_HIDDEN = 1024
EPS = 1e-6

LANES = 128
NEG_INF = float("-inf")

TM_IN = 512
TN_IN = 1536
TM_QK = 512
TQ = 256
ATTN_HB = 8
TM_OUT = 256
TM_RANK = 512
TM_EXP = 256
TM_COMB = 256
ROUTER_ROWS = 40
U_COLS = SSD_WIDTH + 2 * ATTN_WIDTH + SSD_CONV_DIM
VMEM_LIMIT = 52 * 1024 * 1024
EXPERT_VMEM_LIMIT = 58 * 1024 * 1024


def _cparams(sem):
    return pltpu.CompilerParams(dimension_semantics=sem, vmem_limit_bytes=VMEM_LIMIT)


def _silu(x):
    return x * (1.0 / (1.0 + jnp.exp(-x)))


def _softplus(x):
    return jnp.maximum(x, 0.0) + jnp.log(1.0 + jnp.exp(-jnp.abs(x)))


def _inproj_kernel(x_ref, lnw_ref, w_ref, wvt_ref, wdt_ref, u_ref, vt_ref, dt_ref, xn_ref):
    @pl.when(pl.program_id(1) == 0)
    def _():
        x = x_ref[...]
        ms = jnp.mean(x * x, axis=-1, keepdims=True)
        xn = (x * lax.rsqrt(ms + EPS) * lnw_ref[...]).astype(BF16)
        xn_ref[...] = xn
        dt_ref[...] = jnp.dot(xn, wdt_ref[...], preferred_element_type=F32)
        vt_ref[...] = lax.dot_general(wvt_ref[...], xn, (((1,), (1,)), ((), ())),
                                      preferred_element_type=F32).astype(BF16)

    u_ref[...] = jnp.dot(xn_ref[...], w_ref[...], preferred_element_type=F32).astype(BF16)


def _in_proj(x2, ln_w, w_main, w_vt, w_dt):
    n = x2.shape[0]
    return pl.pallas_call(
        _inproj_kernel,
        grid=(n // TM_IN, U_COLS // TN_IN),
        in_specs=[
            pl.BlockSpec((TM_IN, D_MODEL), lambda i, j: (i, 0)),
            pl.BlockSpec((1, D_MODEL), lambda i, j: (0, 0)),
            pl.BlockSpec((D_MODEL, TN_IN), lambda i, j: (0, j)),
            pl.BlockSpec((ATTN_WIDTH, D_MODEL), lambda i, j: (0, 0)),
            pl.BlockSpec((D_MODEL, LANES), lambda i, j: (0, 0)),
        ],
        out_specs=[
            pl.BlockSpec((TM_IN, TN_IN), lambda i, j: (i, j)),
            pl.BlockSpec((ATTN_WIDTH, TM_IN), lambda i, j: (0, i)),
            pl.BlockSpec((TM_IN, LANES), lambda i, j: (i, 0)),
        ],
        out_shape=[
            jax.ShapeDtypeStruct((n, U_COLS), BF16),
            jax.ShapeDtypeStruct((ATTN_WIDTH, n), BF16),
            jax.ShapeDtypeStruct((n, LANES), F32),
        ],
        scratch_shapes=[pltpu.VMEM((TM_IN, D_MODEL), BF16)],
        compiler_params=_cparams(("parallel", "arbitrary")),
        name="in_proj",
    )(x2, ln_w, w_main, w_vt, w_dt)


def _ssd_kernel(z_ref, xbc_ref, dt_ref, convw_ref, convb_ref, dtb_ref, aneg_ref, dskip_ref, normw_ref,
                y_ref, xp_ref, st_ref, yacc_ref):
    L = SSD_CHUNK
    P = SSD_HEAD_DIM

    @pl.when(pl.program_id(1) == 0)
    def _():
        xp_ref[0:8, :] = jnp.zeros((8, SSD_CONV_DIM), F32)
        st_ref[...] = jnp.zeros_like(st_ref)

    xp_ref[8:8 + L, :] = xbc_ref[...].astype(F32)
    acc = jnp.broadcast_to(convb_ref[...], (L, SSD_CONV_DIM))
    for k in range(SSD_CONV):
        acc = acc + xp_ref[5 + k:5 + k + L, :] * convw_ref[k:k + 1, :]
    xp_ref[0:8, :] = xp_ref[L:L + 8, :]
    xc = _silu(acc)

    dt = _softplus(dt_ref[...] + dtb_ref[...])
    a = dt * aneg_ref[...]
    row = lax.broadcasted_iota(I32, (L, L), 0)
    col = lax.broadcasted_iota(I32, (L, L), 1)
    causal = row >= col
    a_cs = jnp.dot(causal.astype(F32), a, precision=HIGHEST, preferred_element_type=F32)
    a_last = a_cs[L - 1:L, :]
    ea = jnp.exp(a_cs)
    dsdt = jnp.exp(a_last - a_cs) * dt
    cd = jnp.exp(a_last)
    a_cs_t = a_cs.T
    dt_t = dt.T
    dsdt_t = dsdt.T

    for g in range(SSD_GROUPS):
        b_g = xc[:, SSD_WIDTH + g * SSD_STATE:SSD_WIDTH + (g + 1) * SSD_STATE]
        c_off = SSD_WIDTH + SSD_GROUPS * SSD_STATE
        c_g = xc[:, c_off + g * SSD_STATE:c_off + (g + 1) * SSD_STATE]
        cb = lax.dot_general(c_g.astype(BF16), b_g.astype(BF16), (((1,), (1,)), ((), ())),
                             preferred_element_type=F32)
        b_gt = b_g.T
        for hh in range(SSD_HEADS_PER_GROUP):
            h = g * SSD_HEADS_PER_GROUP + hh
            xs_h = xc[:, h * P:(h + 1) * P].astype(BF16)
            seg = a_cs[:, h:h + 1] - a_cs_t[h:h + 1, :]
            dec = jnp.exp(jnp.where(causal, seg, NEG_INF))
            m = (cb * dec * dt_t[h:h + 1, :]).astype(BF16)
            c_s = (c_g * ea[:, h:h + 1]).astype(BF16)
            s_prev = st_ref[h]
            lhs = jnp.concatenate([m, c_s], axis=1)
            rhs = jnp.concatenate([xs_h, s_prev.astype(BF16)], axis=0)
            yacc_ref[:, h * P:(h + 1) * P] = jnp.dot(lhs, rhs, preferred_element_type=F32)
            bw = (b_gt * dsdt_t[h:h + 1, :]).astype(BF16)
            st_ref[h] = s_prev * cd[:, h:h + 1] + jnp.dot(bw, xs_h, preferred_element_type=F32)

    y = yacc_ref[...] + xc[:, :SSD_WIDTH] * dskip_ref[...]
    y = y * _silu(z_ref[...].astype(F32))
    gw = SSD_WIDTH // SSD_GROUPS
    for g in range(SSD_GROUPS):
        yg = y[:, g * gw:(g + 1) * gw]
        ms = jnp.mean(yg * yg, axis=-1, keepdims=True)
        y_ref[:, g * gw:(g + 1) * gw] = (yg * lax.rsqrt(ms + EPS) * normw_ref[:, g * gw:(g + 1) * gw]).astype(BF16)


def _ssd(u, dt_raw, conv_w, conv_b, dt_bias, a_neg, dskip_lanes, norm_w, bsz, seq):
    n = u.shape[0]
    nc = seq // SSD_CHUNK
    xbc_blk = (SSD_WIDTH + 2 * ATTN_WIDTH) // SSD_CONV_DIM
    full = lambda shape: pl.BlockSpec(shape, lambda b, c: (0, 0))
    return pl.pallas_call(
        _ssd_kernel,
        grid=(bsz, nc),
        in_specs=[
            pl.BlockSpec((SSD_CHUNK, SSD_WIDTH), lambda b, c: (b * nc + c, 0)),
            pl.BlockSpec((SSD_CHUNK, SSD_CONV_DIM), lambda b, c: (b * nc + c, xbc_blk)),
            pl.BlockSpec((SSD_CHUNK, LANES), lambda b, c: (b * nc + c, 0)),
            full((SSD_CONV, SSD_CONV_DIM)),
            full((1, SSD_CONV_DIM)),
            full((1, LANES)),
            full((1, LANES)),
            full((1, SSD_WIDTH)),
            full((1, SSD_WIDTH)),
        ],
        out_specs=pl.BlockSpec((SSD_CHUNK, SSD_WIDTH), lambda b, c: (b * nc + c, 0)),
        out_shape=jax.ShapeDtypeStruct((n, SSD_WIDTH), BF16),
        scratch_shapes=[
            pltpu.VMEM((SSD_CHUNK + 8, SSD_CONV_DIM), F32),
            pltpu.VMEM((SSD_HEADS, SSD_STATE, SSD_HEAD_DIM), F32),
            pltpu.VMEM((SSD_CHUNK, SSD_WIDTH), F32),
        ],
        compiler_params=_cparams(("parallel", "arbitrary")),
        name="ssd",
    )(u, u, dt_raw, conv_w, conv_b, dt_bias, a_neg, dskip_lanes, norm_w)


def _qkprep_kernel(q_ref, k_ref, pos_ref, invf_ref, qw_ref, kw_ref, ones_ref, qo_ref, ko_ref):
    tm = q_ref.shape[0]
    ang = pos_ref[...] * invf_ref[...]
    cs = jnp.cos(ang)
    sn = jnp.sin(ang)
    d = lax.broadcasted_iota(I32, (tm, LANES), 1) & (ATTN_QK_DIM - 1)
    half = ROPE_DIM // 2
    s_lo = jnp.where(d < half, -sn, 0.0)
    s_hi = jnp.where((d >= half) & (d < ROPE_DIM), sn, 0.0)
    scale = 1.0 / math.sqrt(ATTN_QK_DIM)
    for src, w_ref, dst, mul in ((q_ref, qw_ref, qo_ref, scale), (k_ref, kw_ref, ko_ref, 1.0)):
        for hb in range(ATTN_HEADS):
            x = src[:, hb * LANES:(hb + 1) * LANES].astype(F32)
            ss = jnp.dot((x * x).astype(BF16), ones_ref[...], preferred_element_type=F32)
            xn = x * lax.rsqrt(ss * (1.0 / ATTN_QK_DIM) + EPS) * w_ref[...]
            out = xn * cs + pltpu.roll(xn, LANES - half, 1) * s_lo + pltpu.roll(xn, half, 1) * s_hi
            dst[:, hb * LANES:(hb + 1) * LANES] = (out * mul).astype(BF16)


def _qk_prep(u, pos_col, invf_lanes, qw_lanes, kw_lanes, seg_ones):
    n = u.shape[0]
    full = lambda shape: pl.BlockSpec(shape, lambda i: (0, 0))
    return pl.pallas_call(
        _qkprep_kernel,
        grid=(n // TM_QK,),
        in_specs=[
            pl.BlockSpec((TM_QK, ATTN_WIDTH), lambda i: (i, 1)),
            pl.BlockSpec((TM_QK, ATTN_WIDTH), lambda i: (i, 2)),
            pl.BlockSpec((TM_QK, 1), lambda i: (i, 0)),
            full((1, LANES)), full((1, LANES)), full((1, LANES)), full((LANES, LANES)),
        ],
        out_specs=[pl.BlockSpec((TM_QK, ATTN_WIDTH), lambda i: (i, 0))] * 2,
        out_shape=[jax.ShapeDtypeStruct((n, ATTN_WIDTH), BF16)] * 2,
        compiler_params=_cparams(("parallel",)),
        name="qk_prep",
    )(u, u, pos_col, invf_lanes, qw_lanes, kw_lanes, seg_ones)


def _attn_kernel(q_ref, k_ref, vt_ref, lamv_ref, subw_ref, o_ref, acc_ref, *, lam_init):
    qi = pl.program_id(2)
    lane = lax.broadcasted_iota(I32, (TQ, LANES), 1)
    qs = []
    for hb in range(ATTN_HB):
        q = q_ref[:, hb * LANES:(hb + 1) * LANES]
        zero = jnp.zeros_like(q)
        qs.append(jnp.concatenate([jnp.where(lane < ATTN_QK_DIM, q, zero),
                                   jnp.where(lane >= ATTN_QK_DIM, q, zero)], axis=0))
    acc_ref[...] = jnp.zeros_like(acc_ref)
    kv_idx = lax.broadcasted_iota(I32, (TQ, 2 * TQ), 0)
    q_idx = lax.broadcasted_iota(I32, (TQ, 2 * TQ), 1) & (TQ - 1)
    nt = (((1,), (1,)), ((), ()))

    def block(j, carry, masked):
        off = pl.multiple_of(j * TQ, TQ)
        ss = []
        for hb in range(ATTN_HB):
            kb = k_ref[pl.ds(off, TQ), hb * LANES:(hb + 1) * LANES]
            ss.append(lax.dot_general(kb, qs[hb], nt, preferred_element_type=F32))
        new, ps, alphas = [], [], []
        for hb in range(ATTN_HB):
            m_old, l_old = carry[2 * hb], carry[2 * hb + 1]
            s = ss[hb]
            if masked:
                s = jnp.where(kv_idx <= q_idx, s, NEG_INF)
            m_new = jnp.maximum(m_old, jnp.max(s, axis=0, keepdims=True))
            alpha = jnp.exp(m_old - m_new)
            p = jnp.exp(s - m_new)
            new += [m_new, alpha * l_old + jnp.sum(p, axis=0, keepdims=True)]
            ps.append(p.astype(BF16))
            alphas.append(alpha)
        pvs = []
        for hb in range(ATTN_HB):
            vb = vt_ref[hb * LANES:(hb + 1) * LANES, pl.ds(off, TQ)]
            pvs.append(jnp.dot(vb, ps[hb], preferred_element_type=F32))
        for hb in range(ATTN_HB):
            acc_ref[hb] = alphas[hb] * acc_ref[hb] + pvs[hb]
        return tuple(new)

    init = (jnp.full((1, 2 * TQ), NEG_INF, F32), jnp.zeros((1, 2 * TQ), F32)) * ATTN_HB
    carry = lax.fori_loop(0, qi, lambda j, cr: block(j, cr, False), init)
    carry = block(qi, carry, True)

    lv = lamv_ref[...]
    lam = (jnp.exp(jnp.sum(lv[0:1] * lv[1:2], axis=1, keepdims=True))
           - jnp.exp(jnp.sum(lv[2:3] * lv[3:4], axis=1, keepdims=True)) + lam_init)
    for hb in range(ATTN_HB):
        o2 = acc_ref[hb] * (1.0 / carry[2 * hb + 1])
        o_t = o2[:, :TQ] - lam * o2[:, TQ:]
        ms = jnp.mean(o_t * o_t, axis=0, keepdims=True)
        o_t = o_t * lax.rsqrt(ms + EPS) * subw_ref[...] * (1.0 - lam_init)
        o_ref[:, hb * LANES:(hb + 1) * LANES] = o_t.T.astype(BF16)


def _attention(qp, kp, v_t, lam_vecs, subw_col, bsz, seq, lam_init):
    n = qp.shape[0]
    nq = seq // TQ
    w = ATTN_HB * ATTN_V_DIM
    return pl.pallas_call(
        functools.partial(_attn_kernel, lam_init=lam_init),
        grid=(bsz, ATTN_HEADS // ATTN_HB, nq),
        in_specs=[
            pl.BlockSpec((TQ, w), lambda b, h, i: (b * nq + i, h)),
            pl.BlockSpec((seq, w), lambda b, h, i: (b, h)),
            pl.BlockSpec((w, seq), lambda b, h, i: (h, b)),
            pl.BlockSpec((4, ATTN_QK_DIM), lambda b, h, i: (0, 0)),
            pl.BlockSpec((ATTN_V_DIM, 1), lambda b, h, i: (0, 0)),
        ],
        out_specs=pl.BlockSpec((TQ, w), lambda b, h, i: (b * nq + i, h)),
        out_shape=jax.ShapeDtypeStruct((n, ATTN_WIDTH), BF16),
        scratch_shapes=[pltpu.VMEM((ATTN_HB, ATTN_V_DIM, 2 * TQ), F32)],
        compiler_params=_cparams(("parallel", "parallel", "arbitrary")),
        name="attn",
    )(qp, kp, v_t, lam_vecs, subw_col)


def _pack_halves(x):
    c = x.shape[1] // 2
    lo = pltpu.bitcast(x[:, :c].astype(BF16).astype(F32), U32) >> 16
    hi = pltpu.bitcast(x[:, c:].astype(BF16).astype(F32), U32) & jnp.uint32(0xFFFF0000)
    return hi | lo


TOKEN_ROWS = D_MODEL // 2 // LANES


def _store_token_rows(ref, first_piece, packed):
    tm = packed.shape[0]
    for j in range(packed.shape[1] // LANES):
        ref[pl.ds(first_piece + j, tm, stride=TOKEN_ROWS), :] = packed[:, j * LANES:(j + 1) * LANES]


def _load_token_rows(ref, tm):
    return jnp.concatenate([ref[pl.ds(s, tm, stride=TOKEN_ROWS), :] for s in range(TOKEN_ROWS)], axis=1)


def _unpack_halves(w, dtype=BF16):
    lo = pltpu.bitcast(w << 16, F32).astype(dtype)
    hi = pltpu.bitcast(w & jnp.uint32(0xFFFF0000), F32).astype(dtype)
    return lo, hi


def _outproj_kernel(x_ref, ys_ref, ya_ref, wos_ref, woa_ref, ln2_ref, wr_ref, br_ref,
                    h_ref, hnp_ref, meta_ref):
    tm = x_ref.shape[0]
    h = (x_ref[...]
         + jnp.dot(ys_ref[...], wos_ref[...], preferred_element_type=F32)
         + jnp.dot(ya_ref[...], woa_ref[...], preferred_element_type=F32))
    h_ref[...] = h
    ms = jnp.mean(h * h, axis=-1, keepdims=True)
    hn = h * lax.rsqrt(ms + EPS) * ln2_ref[...]
    _store_token_rows(hnp_ref, 0, _pack_halves(hn))

    lg_t = lax.dot_general(wr_ref[...], hn.astype(BF16), (((1,), (1,)), ((), ())),
                           preferred_element_type=F32)
    lg = lg_t[0:ROUTER_ROWS, :] + br_ref[0:ROUTER_ROWS, :]
    row = lax.broadcasted_iota(I32, (ROUTER_ROWS, tm), 0).astype(F32)
    big = float(LANES)
    gl = jnp.where(row < N_EXPERT_GROUPS, lg, NEG_INF)
    gmax = jnp.max(gl, axis=0, keepdims=True)
    gsel = jnp.min(jnp.where(gl == gmax, row, big), axis=0, keepdims=True)
    g_w = 1.0 / jnp.sum(jnp.exp(gl - gmax), axis=0, keepdims=True)
    eid = row - N_EXPERT_GROUPS
    lo = gsel * EXPERTS_PER_GROUP
    emask = (eid >= lo) & (eid < lo + EXPERTS_PER_GROUP)
    el = jnp.where(emask, lg, NEG_INF)
    m1 = jnp.max(el, axis=0, keepdims=True)
    i1 = jnp.min(jnp.where(el == m1, eid, big), axis=0, keepdims=True)
    el2 = jnp.where(eid == i1, NEG_INF, el)
    m2 = jnp.max(el2, axis=0, keepdims=True)
    i2 = jnp.min(jnp.where(el2 == m2, eid, big), axis=0, keepdims=True)
    e2 = jnp.exp(m2 - m1)
    w1 = g_w / (1.0 + e2)
    w2 = g_w * e2 / (1.0 + e2)
    mrow = lax.broadcasted_iota(I32, (LANES, tm), 0)
    meta_ref[...] = jnp.where(mrow == 0, i1, jnp.where(mrow == 1, i2, jnp.where(mrow == 2, w1, jnp.where(mrow == 3, w2, 0.0))))


def _out_proj(x2, y_ssd, y_att, wo_s, wo_a, ln2_w, w_router_t, b_router):
    n = x2.shape[0]
    full = lambda shape: pl.BlockSpec(shape, lambda i: (0, 0))
    return pl.pallas_call(
        _outproj_kernel,
        grid=(n // TM_OUT,),
        in_specs=[
            pl.BlockSpec((TM_OUT, D_MODEL), lambda i: (i, 0)),
            pl.BlockSpec((TM_OUT, SSD_WIDTH), lambda i: (i, 0)),
            pl.BlockSpec((TM_OUT, ATTN_WIDTH), lambda i: (i, 0)),
            full((SSD_WIDTH, D_MODEL)), full((ATTN_WIDTH, D_MODEL)),
            full((1, D_MODEL)), full((LANES, D_MODEL)), full((LANES, 1)),
        ],
        out_specs=[
            pl.BlockSpec((TM_OUT, D_MODEL), lambda i: (i, 0)),
            pl.BlockSpec((TM_OUT * TOKEN_ROWS, LANES), lambda i: (i, 0)),
            pl.BlockSpec((LANES, TM_OUT), lambda i: (0, i)),
        ],
        out_shape=[
            jax.ShapeDtypeStruct((n, D_MODEL), F32),
            jax.ShapeDtypeStruct((n * TOKEN_ROWS, LANES), U32),
            jax.ShapeDtypeStruct((LANES, n), F32),
        ],
        compiler_params=_cparams(("parallel",)),
        name="out_proj",
    )(x2, y_ssd, y_att, wo_s, wo_a, ln2_w, w_router_t, b_router)


def _rank_kernel(meta_ref, dest_ref, cnt_ref, run_ref, offs_ref):
    p = pl.program_id(0)
    i = pl.program_id(1)
    tm = meta_ref.shape[1]
    meta = meta_ref[...]
    row = lax.broadcasted_iota(I32, (N_EXPERTS, tm), 0).astype(F32)
    oh0 = (row == meta[0:1, :]).astype(F32)
    oh1 = (row == meta[1:2, :]).astype(F32)
    oh = oh0 + oh1
    rowsum = jnp.sum(oh, axis=1, keepdims=True)

    @pl.when((p == 0) & (i == 0))
    def _():
        run_ref[...] = jnp.zeros_like(run_ref)

    @pl.when(p == 0)
    def _():
        run_ref[...] = run_ref[...] + rowsum
        dest_ref[...] = jnp.zeros_like(dest_ref)
        cnt_ref[...] = run_ref[...]

    @pl.when((p == 1) & (i == 0))
    def _():
        cnt = run_ref[...]
        padded = jnp.ceil(cnt * (1.0 / TM_EXP)) * TM_EXP
        r = lax.broadcasted_iota(I32, (N_EXPERTS, N_EXPERTS), 0)
        c = lax.broadcasted_iota(I32, (N_EXPERTS, N_EXPERTS), 1)
        offs_ref[...] = jnp.dot((c < r).astype(F32), padded, precision=HIGHEST, preferred_element_type=F32)
        cnt_ref[...] = cnt
        run_ref[...] = jnp.zeros_like(run_ref)

    @pl.when(p == 1)
    def _():
        r = lax.broadcasted_iota(I32, (tm, tm), 0)
        c = lax.broadcasted_iota(I32, (tm, tm), 1)
        before = jnp.dot(oh.astype(BF16), (r < c).astype(BF16), preferred_element_type=F32)
        base = before + jnp.tile(run_ref[...] + offs_ref[...], (1, tm // LANES))
        d0 = jnp.sum(oh0 * base, axis=0, keepdims=True)
        d1 = jnp.sum(oh1 * base, axis=0, keepdims=True)
        drow = lax.broadcasted_iota(I32, (8, tm), 0)
        dest_ref[...] = jnp.where(drow == 0, d0, jnp.where(drow == 1, d1, 0.0)).astype(I32)
        run_ref[...] = run_ref[...] + rowsum


def _rank(meta_t):
    n = meta_t.shape[1]
    per_expert = lambda: pl.BlockSpec((N_EXPERTS, LANES), lambda p, i: (0, 0))
    return pl.pallas_call(
        _rank_kernel,
        grid=(2, n // TM_RANK),
        in_specs=[pl.BlockSpec((8, TM_RANK), lambda p, i: (0, i))],
        out_specs=[pl.BlockSpec((8, TM_RANK), lambda p, i: (0, i * p)), per_expert()],
        out_shape=[jax.ShapeDtypeStruct((8, n), I32), jax.ShapeDtypeStruct((N_EXPERTS, LANES), F32)],
        scratch_shapes=[pltpu.VMEM((N_EXPERTS, LANES), F32), pltpu.VMEM((N_EXPERTS, LANES), F32)],
        compiler_params=_cparams(("arbitrary", "arbitrary")),
        name="rank",
    )(meta_t)


def _token_copy(src_ref, src_token, dst_ref, dst_token, sem):
    src = src_ref.at[pl.ds(pl.multiple_of(src_token * TOKEN_ROWS, TOKEN_ROWS), TOKEN_ROWS)]
    dst = dst_ref.at[pl.ds(pl.multiple_of(dst_token * TOKEN_ROWS, TOKEN_ROWS), TOKEN_ROWS)]
    return pltpu.make_async_copy(src, dst, sem)


def _invert_kernel(d0_ref, d1_ref, inv_ref):
    n = d0_ref.shape[0]

    def clear(s, _):
        inv_ref[s] = 0
        return 0

    lax.fori_loop(0, inv_ref.shape[0], clear, 0, unroll=8)

    def put(t, _):
        inv_ref[d0_ref[t]] = t
        inv_ref[d1_ref[t]] = n + t
        return 0

    lax.fori_loop(0, n, put, 0, unroll=4)


def _invert(dest0, dest1, n_slots):
    smem = pl.BlockSpec(memory_space=pltpu.SMEM)
    return pl.pallas_call(
        _invert_kernel,
        in_specs=[smem, smem],
        out_specs=smem,
        out_shape=jax.ShapeDtypeStruct((n_slots,), I32),
        name="invert",
    )(dest0, dest1)


CAST_ROWS = 256


def _cast_weight(src_ref, dst_ref):
    def body(i, _):
        rows = pl.ds(pl.multiple_of(i * CAST_ROWS, CAST_ROWS), CAST_ROWS)
        dst_ref[rows, :] = src_ref[rows, :].astype(BF16)
        return 0

    lax.fori_loop(0, src_ref.shape[0] // CAST_ROWS, body, 0)


S_EXPERT, S_FIRST, S_SLOT, S_NEXT, S_HAS_NEXT, S_USED, S_VALID = range(7)
WEIGHT_DMA_PRIORITY = 1


def _load_expert_weights(t, sched_ref, triples, sems):
    def copies(expert, slot):
        return [pltpu.make_async_copy(w.at[expert], stage.at[slot], sems.at[i, slot])
                for i, (w, stage, _) in enumerate(triples)]

    @pl.when(sched_ref[S_FIRST, t] == 1)
    def _():
        slot = sched_ref[S_SLOT, t]

        @pl.when(t == 0)
        def _():
            for cp in copies(sched_ref[S_EXPERT, t], slot):
                cp.start(priority=WEIGHT_DMA_PRIORITY)

        for cp in copies(sched_ref[S_EXPERT, t], slot):
            cp.wait()

        @pl.when(sched_ref[S_HAS_NEXT, t] == 1)
        def _():
            for cp in copies(sched_ref[S_NEXT, t], 1 - slot):
                cp.start(priority=WEIGHT_DMA_PRIORITY)

        for _, stage, dst in triples:
            _cast_weight(stage.at[slot], dst)


ROW_DMA_GROUPS = 4
GATHER_AHEAD = 2


def _tile_rows(sched_ref, tile, group):
    valid = sched_ref[S_VALID, tile]
    per = TM_EXP // ROW_DMA_GROUPS
    for r in range(group * per, (group + 1) * per):
        ok = r < valid
        yield r, jnp.where(ok, tile * TM_EXP + r, 0), ok


def _experts_up_kernel(sched_ref, inv_ref, hnp_hbm, wg_hbm, wu_hbm, hid_ref,
                       xbuf_ref, wg_stage, wu_stage, wgb_ref, wub_ref, wsems, gsem, *, n_tokens):
    t = pl.program_id(0)
    used = sched_ref[S_USED, 0]
    nbuf = GATHER_AHEAD + 1
    cur = t % nbuf
    ahead = (t + GATHER_AHEAD) % nbuf
    _load_expert_weights(t, sched_ref, [(wg_hbm, wg_stage, wgb_ref), (wu_hbm, wu_stage, wub_ref)], wsems)

    def start_rows(tile, buf, group):
        for r, slot, _ in _tile_rows(sched_ref, tile, group):
            token = inv_ref[slot] & (n_tokens - 1)
            _token_copy(hnp_hbm, token, xbuf_ref.at[buf], r, gsem.at[buf]).start()

    def wait_rows(buf):
        pltpu.make_async_copy(hnp_hbm.at[pl.ds(0, TM_EXP * TOKEN_ROWS)], xbuf_ref.at[buf], gsem.at[buf]).wait()

    for first in range(GATHER_AHEAD):
        @pl.when((t == 0) & (first < used))
        def _():
            for g in range(ROW_DMA_GROUPS):
                start_rows(first, first, g)

    def multiply(fetch_ahead):
        wait_rows(cur)
        c = D_MODEL // 2
        x_lo, x_hi = _unpack_halves(_load_token_rows(xbuf_ref.at[cur], TM_EXP))
        pieces = ((x_lo, wgb_ref, 0), (x_hi, wgb_ref, c), (x_lo, wub_ref, 0), (x_hi, wub_ref, c))
        acc = []
        for g, (x, w_ref, row0) in enumerate(pieces):
            if fetch_ahead:
                start_rows(t + GATHER_AHEAD, ahead, g)
            acc.append(jnp.dot(x, w_ref[row0:row0 + c, :], preferred_element_type=F32))
        hid_ref[...] = (_silu(acc[0] + acc[1]) * (acc[2] + acc[3])).astype(BF16)

    @pl.when(t + GATHER_AHEAD < used)
    def _():
        multiply(True)

    @pl.when((t + GATHER_AHEAD >= used) & (t < used))
    def _():
        multiply(False)

    @pl.when(t >= used)
    def _():
        hid_ref[...] = jnp.zeros_like(hid_ref)


def _experts_down_kernel(sched_ref, inv_ref, hid_ref, wd_hbm, y_hbm,
                         ybuf_ref, wd_stage, wdb_ref, wsems, ssem, *, n_tokens):
    t = pl.program_id(0)
    used = sched_ref[S_USED, 0]
    cur = t % 2
    _load_expert_weights(t, sched_ref, [(wd_hbm, wd_stage, wdb_ref)], wsems)

    def start_rows(tile, buf, group):
        for r, slot, ok in _tile_rows(sched_ref, tile, group):
            dst = jnp.where(ok, inv_ref[slot], TOP_K * n_tokens + r)
            _token_copy(ybuf_ref.at[buf], r, y_hbm, dst, ssem.at[buf]).start(priority=r % 2)

    def wait_rows(buf):
        pltpu.make_async_copy(ybuf_ref.at[buf], y_hbm.at[pl.ds(0, TM_EXP * TOKEN_ROWS)], ssem.at[buf]).wait()

    def multiply(send_previous):
        hid = hid_ref[...]
        q = D_MODEL // 4
        for half in range(2):
            if send_previous:
                start_rows(t - 1, 1 - cur, 2 * half)
            lo = jnp.dot(hid, wdb_ref[:, half * q:(half + 1) * q], preferred_element_type=F32)
            if send_previous:
                start_rows(t - 1, 1 - cur, 2 * half + 1)
            hi = jnp.dot(hid, wdb_ref[:, (half + 2) * q:(half + 3) * q], preferred_element_type=F32)
            _store_token_rows(ybuf_ref.at[cur], half * (q // LANES), _pack_halves(jnp.concatenate([lo, hi], axis=1)))

    @pl.when((t >= 2) & (t < used))
    def _():
        wait_rows(cur)

    @pl.when(t == 0)
    def _():
        ybuf_ref[1] = jnp.zeros(ybuf_ref.shape[1:], ybuf_ref.dtype)
        spare = pltpu.make_async_copy(ybuf_ref.at[1], y_hbm.at[pl.ds(TOP_K * n_tokens * TOKEN_ROWS, TM_EXP * TOKEN_ROWS)],
                                      ssem.at[1])
        spare.start()
        spare.wait()
        multiply(False)

    @pl.when((t > 0) & (t < used))
    def _():
        multiply(True)

    @pl.when(t == used - 1)
    def _():
        for g in range(ROW_DMA_GROUPS):
            start_rows(t, cur, g)

        @pl.when(t > 0)
        def _():
            wait_rows(1 - cur)

        wait_rows(cur)


def _tile_schedule(counts, n_tiles):
    tiles_per_expert = (counts + TM_EXP - 1) // TM_EXP
    tile_end = jnp.cumsum(tiles_per_expert)
    ids = jnp.arange(n_tiles, dtype=I32)
    raw = jnp.sum(ids[:, None] >= tile_end[None, :], axis=1)
    expert = jnp.minimum(raw, N_EXPERTS - 1).astype(I32)
    first = jnp.concatenate([jnp.ones((1,), I32), (expert[1:] != expert[:-1]).astype(I32)])
    slot = (jnp.cumsum(first) - 1) % 2
    start_idx = jnp.where(first == 1, ids, n_tiles)
    next_start = jnp.concatenate([lax.cummin(start_idx, reverse=True)[1:], jnp.full((1,), n_tiles, I32)])
    has_next = (next_start < n_tiles).astype(I32)
    next_expert = jnp.sum(jnp.where(ids[None, :] == next_start[:, None], expert[None, :], 0), axis=1)
    used = jnp.full((n_tiles,), tile_end[-1], I32)
    onehot = expert[:, None] == jnp.arange(N_EXPERTS)[None, :]
    tile_in_expert = ids - jnp.sum(jnp.where(onehot, (tile_end - tiles_per_expert)[None, :], 0), axis=1)
    rows_left = jnp.sum(jnp.where(onehot, counts[None, :], 0), axis=1) - tile_in_expert * TM_EXP
    valid = jnp.where(raw < N_EXPERTS, jnp.clip(rows_left, 0, TM_EXP), 0)
    return jnp.stack([expert, first, slot, next_expert, has_next, used, valid]).astype(I32)


def _experts(sched, inv, hnp, wg, wu, wd):
    n = hnp.shape[0] // TOKEN_ROWS
    n_slots = inv.shape[0]
    token_buf = lambda nbuf: pltpu.VMEM((nbuf, TM_EXP * TOKEN_ROWS, LANES), U32)
    row_spec = lambda width: pl.BlockSpec((TM_EXP, width), lambda t, sc, iv: (t, 0))
    hbm = pl.BlockSpec(memory_space=pl.ANY)
    big_vmem = pltpu.CompilerParams(dimension_semantics=("arbitrary",), vmem_limit_bytes=EXPERT_VMEM_LIMIT)
    hid = pl.pallas_call(
        functools.partial(_experts_up_kernel, n_tokens=n),
        grid_spec=pltpu.PrefetchScalarGridSpec(
            num_scalar_prefetch=2,
            grid=(n_slots // TM_EXP,),
            in_specs=[hbm, hbm, hbm],
            out_specs=row_spec(EXPERT_HIDDEN),
            scratch_shapes=[token_buf(GATHER_AHEAD + 1)]
                           + [pltpu.VMEM((2, D_MODEL, EXPERT_HIDDEN), F32)] * 2
                           + [pltpu.VMEM((D_MODEL, EXPERT_HIDDEN), BF16)] * 2
                           + [pltpu.SemaphoreType.DMA((2, 2)), pltpu.SemaphoreType.DMA((GATHER_AHEAD + 1,))],
        ),
        out_shape=jax.ShapeDtypeStruct((n_slots, EXPERT_HIDDEN), BF16),
        compiler_params=big_vmem,
        name="experts_up",
    )(sched, inv, hnp, wg, wu)
    return pl.pallas_call(
        functools.partial(_experts_down_kernel, n_tokens=n),
        grid_spec=pltpu.PrefetchScalarGridSpec(
            num_scalar_prefetch=2,
            grid=(n_slots // TM_EXP,),
            in_specs=[row_spec(EXPERT_HIDDEN), hbm],
            out_specs=hbm,
            scratch_shapes=[token_buf(2),
                            pltpu.VMEM((2, EXPERT_HIDDEN, D_MODEL), F32), pltpu.VMEM((EXPERT_HIDDEN, D_MODEL), BF16),
                            pltpu.SemaphoreType.DMA((1, 2)), pltpu.SemaphoreType.DMA((2,))],
        ),
        out_shape=jax.ShapeDtypeStruct(((TOP_K * n + TM_EXP) * TOKEN_ROWS, LANES), U32),
        compiler_params=big_vmem,
        name="experts_down",
    )(sched, inv, hid, wd)


def _combine_kernel(h_ref, meta_ref, y0_ref, y1_ref, o_ref):
    c = D_MODEL // 2
    meta = meta_ref[...].T
    w0, w1 = meta[:, 2:3], meta[:, 3:4]
    tm = h_ref.shape[0]
    lo0, hi0 = _unpack_halves(_load_token_rows(y0_ref, tm), F32)
    lo1, hi1 = _unpack_halves(_load_token_rows(y1_ref, tm), F32)
    o_ref[:, :c] = h_ref[:, :c] + (lo0 * w0 + lo1 * w1)
    o_ref[:, c:] = h_ref[:, c:] + (hi0 * w0 + hi1 * w1)


def _combine(h, meta, y):
    n = h.shape[0]
    nb = n // TM_COMB
    return pl.pallas_call(
        _combine_kernel,
        grid=(nb,),
        in_specs=[
            pl.BlockSpec((TM_COMB, D_MODEL), lambda i: (i, 0)),
            pl.BlockSpec((LANES, TM_COMB), lambda i: (0, i)),
            pl.BlockSpec((TM_COMB * TOKEN_ROWS, LANES), lambda i: (i, 0)),
            pl.BlockSpec((TM_COMB * TOKEN_ROWS, LANES), lambda i: (nb + i, 0)),
        ],
        out_specs=pl.BlockSpec((TM_COMB, D_MODEL), lambda i: (i, 0)),
        out_shape=jax.ShapeDtypeStruct((n, D_MODEL), F32),
        compiler_params=_cparams(("parallel",)),
        name="combine",
    )(h, meta, y, y)


def _lambda_init(layer_idx):
    return 0.8 - 0.6 * math.exp(-0.3 * layer_idx)


def _pad_lanes(v, width=LANES):
    return jnp.pad(v, ((0, 0), (0, width - v.shape[1])))


def _layer(l, x2, pos_col, bsz, seq, ln1_w, w_in, conv_w, conv_b, dt_bias, a_log, d_skip, ssd_norm_w,
           q_norm_w, k_norm_w, lambda_q1, lambda_k1, lambda_q2, lambda_k2, subln_w, w_out, ln2_w,
           w_router_group, b_router_group, w_router_expert, b_router_expert, w_gate, w_up, w_down):
    n = x2.shape[0]
    c_z, c_xbc, c_dt = SSD_WIDTH, SSD_WIDTH + SSD_CONV_DIM, SSD_WIDTH + SSD_CONV_DIM + SSD_HEADS
    c_q, c_k = c_dt + ATTN_WIDTH, c_dt + 2 * ATTN_WIDTH
    w_main = jnp.concatenate([w_in[:, :c_z], w_in[:, c_dt:c_q], w_in[:, c_q:c_k], w_in[:, c_z:c_xbc]],
                             axis=1).astype(BF16)
    w_vt = w_in[:, c_k:].T.astype(BF16)
    w_dt = _pad_lanes(w_in[:, c_xbc:c_dt]).astype(BF16)

    u, v_t, dt_raw = _in_proj(x2, ln1_w[None, :], w_main, w_vt, w_dt)

    a_neg = _pad_lanes(-jnp.exp(a_log.astype(F32))[None, :])
    y_ssd = _ssd(u, dt_raw, conv_w, conv_b[None, :], _pad_lanes(dt_bias[None, :]), a_neg,
                 jnp.repeat(d_skip, SSD_HEAD_DIM)[None, :], ssd_norm_w[None, :], bsz, seq)

    inv_freq = jnp.power(ROPE_THETA, -jnp.arange(0, ROPE_DIM, 2, dtype=F32) / ROPE_DIM)
    d = jnp.arange(LANES) % ATTN_QK_DIM
    invf_lanes = jnp.where(d < ROPE_DIM, inv_freq[d % (ROPE_DIM // 2)], 0.0)[None, :]
    seg_ones = (jnp.arange(LANES)[:, None] // ATTN_QK_DIM == jnp.arange(LANES)[None, :] // ATTN_QK_DIM).astype(BF16)
    qp, kp = _qk_prep(u, pos_col, invf_lanes, jnp.tile(q_norm_w, 2)[None, :], jnp.tile(k_norm_w, 2)[None, :], seg_ones)

    lam_vecs = jnp.stack([lambda_q1, lambda_k1, lambda_q2, lambda_k2]).astype(F32)
    y_att = _attention(qp, kp, v_t, lam_vecs, subln_w[:, None], bsz, seq, _lambda_init(l))

    w_out_b = w_out.astype(BF16)
    w_router_t = _pad_lanes(jnp.concatenate([w_router_group, w_router_expert], axis=1)).T.astype(BF16)
    b_router = _pad_lanes(jnp.concatenate([b_router_group, b_router_expert])[None, :]).T
    h, hnp, meta = _out_proj(x2, y_ssd, y_att, w_out_b[:SSD_WIDTH], w_out_b[SSD_WIDTH:], ln2_w[None, :],
                             w_router_t, b_router)

    dest, cnt = _rank(meta)
    dest0, dest1 = dest[0], dest[1]
    counts = cnt[:, 0].astype(I32)
    n_slots = (n * TOP_K + N_EXPERTS * (TM_EXP - 1)) // TM_EXP * TM_EXP
    sched = _tile_schedule(counts, n_slots // TM_EXP)

    inv = _invert(dest0, dest1, n_slots)
    y = _experts(sched, inv, hnp, w_gate, w_up, w_down)
    return _combine(h, meta, y)


def kernel(x, positions, ln1_w, w_in, conv_w, conv_b, dt_bias, a_log, d_skip, ssd_norm_w, q_norm_w, k_norm_w,
           lambda_q1, lambda_k1, lambda_q2, lambda_k2, subln_w, w_out, ln2_w, w_router_group, b_router_group,
           w_router_expert, b_router_expert, w_gate, w_up, w_down):
    bsz, seq, d = x.shape
    assert d == D_MODEL and seq % TQ == 0 and (bsz * seq) % TM_IN == 0
    x2 = x.reshape(bsz * seq, d)
    pos_col = positions.astype(F32).reshape(bsz * seq, 1)
    params = (ln1_w, w_in, conv_w, conv_b, dt_bias, a_log, d_skip, ssd_norm_w, q_norm_w, k_norm_w,
              lambda_q1, lambda_k1, lambda_q2, lambda_k2, subln_w, w_out, ln2_w, w_router_group, b_router_group,
              w_router_expert, b_router_expert, w_gate, w_up, w_down)
    for l in range(ln1_w.shape[0]):
        x2 = _layer(l, x2, pos_col, bsz, seq, *[p[l] for p in params])
    return x2.reshape(bsz, seq, d)
```

```python
import functools
import math

import jax
import jax.numpy as jnp
from jax import lax
from jax.experimental import pallas as pl
from jax.experimental.pallas import tpu as pltpu

F32 = jnp.float32
BF16 = jnp.bfloat16
I32 = jnp.int32
U32 = jnp.uint32
HIGHEST = lax.Precision.HIGHEST

D_MODEL = 2048
SSD_WIDTH = 1024
ATTN_WIDTH = 1024
SSD_HEAD_DIM = 64
SSD_HEADS = 16
SSD_GROUPS = 2
SSD_HEADS_PER_GROUP = SSD_HEADS // SSD_GROUPS
SSD_STATE = 128
SSD_CONV = 4
SSD_CHUNK = 128
SSD_CONV_DIM = SSD_WIDTH + 2 * SSD_GROUPS * SSD_STATE
ATTN_V_DIM = 128
ATTN_HEADS = 8
ATTN_QK_DIM = 64
ROPE_THETA = 500000.0
ROPE_DIM = 16
N_EXPERT_GROUPS = 4
EXPERTS_PER_GROUP = 8
N_EXPERTS = 32
TOP_K = 2
EXPERT_HIDDEN = 1024
EPS = 1e-6

LANES = 128
NEG_INF = float("-inf")

TM_IN = 512
TN_IN = 1536
TM_QK = 512
TQ = 256
ATTN_HB = 8
ONES_ROWS = 16
TM_OUT = 256
TM_RANK = 512
TM_EXP = 256
TM_COMB = 256
ROUTER_ROWS = 40
U_COLS = SSD_WIDTH + 2 * ATTN_WIDTH + SSD_CONV_DIM
VMEM_LIMIT = 52 * 1024 * 1024
EXPERT_VMEM_LIMIT = 58 * 1024 * 1024


def _cparams(sem):
    return pltpu.CompilerParams(dimension_semantics=sem, vmem_limit_bytes=VMEM_LIMIT)


def _silu(x):
    return x * (1.0 / (1.0 + jnp.exp(-x)))


def _softplus(x):
    return jnp.maximum(x, 0.0) + jnp.log(1.0 + jnp.exp(-jnp.abs(x)))


def _inproj_kernel(x_ref, lnw_ref, w_ref, wvt_ref, wdt_ref, u_ref, vt_ref, dt_ref, xn_ref):
    @pl.when(pl.program_id(1) == 0)
    def _():
        x = x_ref[...]
        ms = jnp.mean(x * x, axis=-1, keepdims=True)
        xn = (x * lax.rsqrt(ms + EPS) * lnw_ref[...]).astype(BF16)
        xn_ref[...] = xn
        dt_ref[...] = jnp.dot(xn, wdt_ref[...], preferred_element_type=F32)
        vt_ref[...] = lax.dot_general(wvt_ref[...], xn, (((1,), (1,)), ((), ())),
                                      preferred_element_type=F32).astype(BF16)

    u_ref[...] = jnp.dot(xn_ref[...], w_ref[...], preferred_element_type=F32).astype(BF16)


def _in_proj(x2, ln_w, w_main, w_vt, w_dt):
    n = x2.shape[0]
    return pl.pallas_call(
        _inproj_kernel,
        grid=(n // TM_IN, U_COLS // TN_IN),
        in_specs=[
            pl.BlockSpec((TM_IN, D_MODEL), lambda i, j: (i, 0)),
            pl.BlockSpec((1, D_MODEL), lambda i, j: (0, 0)),
            pl.BlockSpec((D_MODEL, TN_IN), lambda i, j: (0, j)),
            pl.BlockSpec((ATTN_WIDTH, D_MODEL), lambda i, j: (0, 0)),
            pl.BlockSpec((D_MODEL, LANES), lambda i, j: (0, 0)),
        ],
        out_specs=[
            pl.BlockSpec((TM_IN, TN_IN), lambda i, j: (i, j)),
            pl.BlockSpec((ATTN_WIDTH, TM_IN), lambda i, j: (0, i)),
            pl.BlockSpec((TM_IN, LANES), lambda i, j: (i, 0)),
        ],
        out_shape=[
            jax.ShapeDtypeStruct((n, U_COLS), BF16),
            jax.ShapeDtypeStruct((ATTN_WIDTH, n), BF16),
            jax.ShapeDtypeStruct((n, LANES), F32),
        ],
        scratch_shapes=[pltpu.VMEM((TM_IN, D_MODEL), BF16)],
        compiler_params=_cparams(("parallel", "arbitrary")),
        name="in_proj",
    )(x2, ln_w, w_main, w_vt, w_dt)


def _ssd_kernel(z_ref, xbc_ref, dt_ref, convw_ref, convb_ref, dtb_ref, aneg_ref, dskip_ref, normw_ref,
                y_ref, xp_ref, st_ref, yacc_ref):
    L = SSD_CHUNK
    P = SSD_HEAD_DIM

    @pl.when(pl.program_id(1) == 0)
    def _():
        xp_ref[0:8, :] = jnp.zeros((8, SSD_CONV_DIM), F32)
        st_ref[...] = jnp.zeros_like(st_ref)

    xp_ref[8:8 + L, :] = xbc_ref[...].astype(F32)
    acc = jnp.broadcast_to(convb_ref[...], (L, SSD_CONV_DIM))
    for k in range(SSD_CONV):
        acc = acc + xp_ref[5 + k:5 + k + L, :] * convw_ref[k:k + 1, :]
    xp_ref[0:8, :] = xp_ref[L:L + 8, :]
    xc = _silu(acc)

    dt = _softplus(dt_ref[...] + dtb_ref[...])
    a = dt * aneg_ref[...]
    row = lax.broadcasted_iota(I32, (L, L), 0)
    col = lax.broadcasted_iota(I32, (L, L), 1)
    causal = row >= col
    a_cs = jnp.dot(causal.astype(F32), a, precision=HIGHEST, preferred_element_type=F32)
    a_last = a_cs[L - 1:L, :]
    ea = jnp.exp(a_cs)
    dsdt = jnp.exp(a_last - a_cs) * dt
    cd = jnp.exp(a_last)
    a_cs_t = a_cs.T
    dt_t = dt.T
    dsdt_t = dsdt.T

    for g in range(SSD_GROUPS):
        b_g = xc[:, SSD_WIDTH + g * SSD_STATE:SSD_WIDTH + (g + 1) * SSD_STATE]
        c_off = SSD_WIDTH + SSD_GROUPS * SSD_STATE
        c_g = xc[:, c_off + g * SSD_STATE:c_off + (g + 1) * SSD_STATE]
        cb = lax.dot_general(c_g.astype(BF16), b_g.astype(BF16), (((1,), (1,)), ((), ())),
                             preferred_element_type=F32)
        b_gt = b_g.T
        for hh in range(SSD_HEADS_PER_GROUP):
            h = g * SSD_HEADS_PER_GROUP + hh
            xs_h = xc[:, h * P:(h + 1) * P].astype(BF16)
            seg = a_cs[:, h:h + 1] - a_cs_t[h:h + 1, :]
            dec = jnp.exp(jnp.where(causal, seg, NEG_INF))
            m = (cb * dec * dt_t[h:h + 1, :]).astype(BF16)
            c_s = (c_g * ea[:, h:h + 1]).astype(BF16)
            s_prev = st_ref[h]
            lhs = jnp.concatenate([m, c_s], axis=1)
            rhs = jnp.concatenate([xs_h, s_prev.astype(BF16)], axis=0)
            yacc_ref[:, h * P:(h + 1) * P] = jnp.dot(lhs, rhs, preferred_element_type=F32)
            bw = (b_gt * dsdt_t[h:h + 1, :]).astype(BF16)
            st_ref[h] = s_prev * cd[:, h:h + 1] + jnp.dot(bw, xs_h, preferred_element_type=F32)

    y = yacc_ref[...] + xc[:, :SSD_WIDTH] * dskip_ref[...]
    y = y * _silu(z_ref[...].astype(F32))
    gw = SSD_WIDTH // SSD_GROUPS
    for g in range(SSD_GROUPS):
        yg = y[:, g * gw:(g + 1) * gw]
        ms = jnp.mean(yg * yg, axis=-1, keepdims=True)
        y_ref[:, g * gw:(g + 1) * gw] = (yg * lax.rsqrt(ms + EPS) * normw_ref[:, g * gw:(g + 1) * gw]).astype(BF16)


def _ssd(u, dt_raw, conv_w, conv_b, dt_bias, a_neg, dskip_lanes, norm_w, bsz, seq):
    n = u.shape[0]
    nc = seq // SSD_CHUNK
    xbc_blk = (SSD_WIDTH + 2 * ATTN_WIDTH) // SSD_CONV_DIM
    full = lambda shape: pl.BlockSpec(shape, lambda b, c: (0, 0))
    return pl.pallas_call(
        _ssd_kernel,
        grid=(bsz, nc),
        in_specs=[
            pl.BlockSpec((SSD_CHUNK, SSD_WIDTH), lambda b, c: (b * nc + c, 0)),
            pl.BlockSpec((SSD_CHUNK, SSD_CONV_DIM), lambda b, c: (b * nc + c, xbc_blk)),
            pl.BlockSpec((SSD_CHUNK, LANES), lambda b, c: (b * nc + c, 0)),
            full((SSD_CONV, SSD_CONV_DIM)),
            full((1, SSD_CONV_DIM)),
            full((1, LANES)),
            full((1, LANES)),
            full((1, SSD_WIDTH)),
            full((1, SSD_WIDTH)),
        ],
        out_specs=pl.BlockSpec((SSD_CHUNK, SSD_WIDTH), lambda b, c: (b * nc + c, 0)),
        out_shape=jax.ShapeDtypeStruct((n, SSD_WIDTH), BF16),
        scratch_shapes=[
            pltpu.VMEM((SSD_CHUNK + 8, SSD_CONV_DIM), F32),
            pltpu.VMEM((SSD_HEADS, SSD_STATE, SSD_HEAD_DIM), F32),
            pltpu.VMEM((SSD_CHUNK, SSD_WIDTH), F32),
        ],
        compiler_params=_cparams(("parallel", "arbitrary")),
        name="ssd",
    )(u, u, dt_raw, conv_w, conv_b, dt_bias, a_neg, dskip_lanes, norm_w)


def _qkprep_kernel(q_ref, k_ref, pos_ref, invf_ref, qw_ref, kw_ref, ones_ref, qo_ref, ko_ref):
    tm = q_ref.shape[0]
    ang = pos_ref[...] * invf_ref[...]
    cs = jnp.cos(ang)
    sn = jnp.sin(ang)
    d = lax.broadcasted_iota(I32, (tm, LANES), 1) & (ATTN_QK_DIM - 1)
    half = ROPE_DIM // 2
    s_lo = jnp.where(d < half, -sn, 0.0)
    s_hi = jnp.where((d >= half) & (d < ROPE_DIM), sn, 0.0)
    scale = math.log2(math.e) / math.sqrt(ATTN_QK_DIM)
    for src, w_ref, dst, mul in ((q_ref, qw_ref, qo_ref, scale), (k_ref, kw_ref, ko_ref, 1.0)):
        for hb in range(ATTN_HEADS):
            x = src[:, hb * LANES:(hb + 1) * LANES].astype(F32)
            ss = jnp.dot((x * x).astype(BF16), ones_ref[...], preferred_element_type=F32)
            xn = x * lax.rsqrt(ss * (1.0 / ATTN_QK_DIM) + EPS) * w_ref[...]
            out = xn * cs + pltpu.roll(xn, LANES - half, 1) * s_lo + pltpu.roll(xn, half, 1) * s_hi
            dst[:, hb * LANES:(hb + 1) * LANES] = (out * mul).astype(BF16)


def _qk_prep(u, pos_col, invf_lanes, qw_lanes, kw_lanes, seg_ones):
    n = u.shape[0]
    full = lambda shape: pl.BlockSpec(shape, lambda i: (0, 0))
    return pl.pallas_call(
        _qkprep_kernel,
        grid=(n // TM_QK,),
        in_specs=[
            pl.BlockSpec((TM_QK, ATTN_WIDTH), lambda i: (i, 1)),
            pl.BlockSpec((TM_QK, ATTN_WIDTH), lambda i: (i, 2)),
            pl.BlockSpec((TM_QK, 1), lambda i: (i, 0)),
            full((1, LANES)), full((1, LANES)), full((1, LANES)), full((LANES, LANES)),
        ],
        out_specs=[pl.BlockSpec((TM_QK, ATTN_WIDTH), lambda i: (i, 0))] * 2,
        out_shape=[jax.ShapeDtypeStruct((n, ATTN_WIDTH), BF16)] * 2,
        compiler_params=_cparams(("parallel",)),
        name="qk_prep",
    )(u, u, pos_col, invf_lanes, qw_lanes, kw_lanes, seg_ones)


def _attn_kernel(q_ref, k_ref, vt_ref, lamv_ref, subw_ref, o_ref, acc_ref, *, lam_init):
    qi = pl.program_id(2)
    lane = lax.broadcasted_iota(I32, (TQ, LANES), 1)
    qs = []
    for hb in range(ATTN_HB):
        q = q_ref[:, hb * LANES:(hb + 1) * LANES]
        zero = jnp.zeros_like(q)
        qs.append(jnp.concatenate([jnp.where(lane < ATTN_QK_DIM, q, zero),
                                   jnp.where(lane >= ATTN_QK_DIM, q, zero)], axis=0))
    acc_ref[...] = jnp.zeros_like(acc_ref)
    kv_idx = lax.broadcasted_iota(I32, (TQ, 2 * TQ), 0)
    q_idx = lax.broadcasted_iota(I32, (TQ, 2 * TQ), 1) & (TQ - 1)
    nt = (((1,), (1,)), ((), ()))

    def block(j, carry, masked):
        off = pl.multiple_of(j * TQ, TQ)
        ss = []
        for hb in range(ATTN_HB):
            kb = k_ref[pl.ds(off, TQ), hb * LANES:(hb + 1) * LANES]
            ss.append(lax.dot_general(kb, qs[hb], nt, preferred_element_type=F32))
        new, ps, alphas = [], [], []
        for hb in range(ATTN_HB):
            m_old = carry[hb]
            s = ss[hb]
            if masked:
                s = jnp.where(kv_idx <= q_idx, s, NEG_INF)
            m_new = jnp.maximum(m_old, jnp.max(s, axis=0, keepdims=True))
            alphas.append(jnp.exp2(m_old - m_new))
            ps.append(jnp.exp2(s - m_new).astype(BF16))
            new.append(m_new)
        pvs = []
        for hb in range(ATTN_HB):
            vb = jnp.concatenate([vt_ref[hb * LANES:(hb + 1) * LANES, pl.ds(off, TQ)], ones_rows], axis=0)
            pvs.append(jnp.dot(vb, ps[hb], preferred_element_type=F32))
        for hb in range(ATTN_HB):
            acc_ref[hb] = alphas[hb] * acc_ref[hb] + pvs[hb]
        return tuple(new)

    ones_rows = jnp.ones((ONES_ROWS, TQ), BF16)
    init = (jnp.full((1, 2 * TQ), NEG_INF, F32),) * ATTN_HB
    carry = lax.fori_loop(0, qi, lambda j, cr: block(j, cr, False), init)
    block(qi, carry, True)

    lv = lamv_ref[...]
    lam = (jnp.exp(jnp.sum(lv[0:1] * lv[1:2], axis=1, keepdims=True))
           - jnp.exp(jnp.sum(lv[2:3] * lv[3:4], axis=1, keepdims=True)) + lam_init)
    for hb in range(ATTN_HB):
        acc = acc_ref[hb]
        o2 = acc[:ATTN_V_DIM] * (1.0 / acc[ATTN_V_DIM:ATTN_V_DIM + 1])
        o_t = o2[:, :TQ] - lam * o2[:, TQ:]
        ms = jnp.mean(o_t * o_t, axis=0, keepdims=True)
        o_t = o_t * lax.rsqrt(ms + EPS) * subw_ref[...] * (1.0 - lam_init)
        o_ref[:, hb * LANES:(hb + 1) * LANES] = o_t.T.astype(BF16)


def _attention(qp, kp, v_t, lam_vecs, subw_col, bsz, seq, lam_init):
    n = qp.shape[0]
    nq = seq // TQ
    w = ATTN_HB * ATTN_V_DIM
    return pl.pallas_call(
        functools.partial(_attn_kernel, lam_init=lam_init),
        grid=(bsz, ATTN_HEADS // ATTN_HB, nq),
        in_specs=[
            pl.BlockSpec((TQ, w), lambda b, h, i: (b * nq + i, h)),
            pl.BlockSpec((seq, w), lambda b, h, i: (b, h)),
            pl.BlockSpec((w, seq), lambda b, h, i: (h, b)),
            pl.BlockSpec((4, ATTN_QK_DIM), lambda b, h, i: (0, 0)),
            pl.BlockSpec((ATTN_V_DIM, 1), lambda b, h, i: (0, 0)),
        ],
        out_specs=pl.BlockSpec((TQ, w), lambda b, h, i: (b * nq + i, h)),
        out_shape=jax.ShapeDtypeStruct((n, ATTN_WIDTH), BF16),
        scratch_shapes=[pltpu.VMEM((ATTN_HB, ATTN_V_DIM + ONES_ROWS, 2 * TQ), F32)],
        compiler_params=_cparams(("parallel", "parallel", "arbitrary")),
        name="attn",
    )(qp, kp, v_t, lam_vecs, subw_col)


def _pack_halves(x):
    c = x.shape[1] // 2
    lo = pltpu.bitcast(x[:, :c].astype(BF16).astype(F32), U32) >> 16
    hi = pltpu.bitcast(x[:, c:].astype(BF16).astype(F32), U32) & jnp.uint32(0xFFFF0000)
    return hi | lo


TOKEN_ROWS = D_MODEL // 2 // LANES


def _store_token_rows(ref, first_piece, packed):
    tm = packed.shape[0]
    for j in range(packed.shape[1] // LANES):
        ref[pl.ds(first_piece + j, tm, stride=TOKEN_ROWS), :] = packed[:, j * LANES:(j + 1) * LANES]


def _load_token_rows(ref, tm):
    return jnp.concatenate([ref[pl.ds(s, tm, stride=TOKEN_ROWS), :] for s in range(TOKEN_ROWS)], axis=1)


def _unpack_halves(w, dtype=BF16):
    lo = pltpu.bitcast(w << 16, F32).astype(dtype)
    hi = pltpu.bitcast(w & jnp.uint32(0xFFFF0000), F32).astype(dtype)
    return lo, hi


def _outproj_kernel(x_ref, ys_ref, ya_ref, wos_ref, woa_ref, ln2_ref, wr_ref, br_ref,
                    h_ref, hnp_ref, meta_ref):
    tm = x_ref.shape[0]
    h = (x_ref[...]
         + jnp.dot(ys_ref[...], wos_ref[...], preferred_element_type=F32)
         + jnp.dot(ya_ref[...], woa_ref[...], preferred_element_type=F32))
    h_ref[...] = h
    ms = jnp.mean(h * h, axis=-1, keepdims=True)
    hn = h * lax.rsqrt(ms + EPS) * ln2_ref[...]
    _store_token_rows(hnp_ref, 0, _pack_halves(hn))

    lg_t = lax.dot_general(wr_ref[...], hn.astype(BF16), (((1,), (1,)), ((), ())),
                           preferred_element_type=F32)
    lg = lg_t[0:ROUTER_ROWS, :] + br_ref[0:ROUTER_ROWS, :]
    row = lax.broadcasted_iota(I32, (ROUTER_ROWS, tm), 0).astype(F32)
    big = float(LANES)
    gl = jnp.where(row < N_EXPERT_GROUPS, lg, NEG_INF)
    gmax = jnp.max(gl, axis=0, keepdims=True)
    gsel = jnp.min(jnp.where(gl == gmax, row, big), axis=0, keepdims=True)
    g_w = 1.0 / jnp.sum(jnp.exp(gl - gmax), axis=0, keepdims=True)
    eid = row - N_EXPERT_GROUPS
    lo = gsel * EXPERTS_PER_GROUP
    emask = (eid >= lo) & (eid < lo + EXPERTS_PER_GROUP)
    el = jnp.where(emask, lg, NEG_INF)
    m1 = jnp.max(el, axis=0, keepdims=True)
    i1 = jnp.min(jnp.where(el == m1, eid, big), axis=0, keepdims=True)
    el2 = jnp.where(eid == i1, NEG_INF, el)
    m2 = jnp.max(el2, axis=0, keepdims=True)
    i2 = jnp.min(jnp.where(el2 == m2, eid, big), axis=0, keepdims=True)
    e2 = jnp.exp(m2 - m1)
    w1 = g_w / (1.0 + e2)
    w2 = g_w * e2 / (1.0 + e2)
    mrow = lax.broadcasted_iota(I32, (LANES, tm), 0)
    meta_ref[...] = jnp.where(mrow == 0, i1, jnp.where(mrow == 1, i2, jnp.where(mrow == 2, w1, jnp.where(mrow == 3, w2, 0.0))))


def _out_proj(x2, y_ssd, y_att, wo_s, wo_a, ln2_w, w_router_t, b_router):
    n = x2.shape[0]
    full = lambda shape: pl.BlockSpec(shape, lambda i: (0, 0))
    return pl.pallas_call(
        _outproj_kernel,
        grid=(n // TM_OUT,),
        in_specs=[
            pl.BlockSpec((TM_OUT, D_MODEL), lambda i: (i, 0)),
            pl.BlockSpec((TM_OUT, SSD_WIDTH), lambda i: (i, 0)),
            pl.BlockSpec((TM_OUT, ATTN_WIDTH), lambda i: (i, 0)),
            full((SSD_WIDTH, D_MODEL)), full((ATTN_WIDTH, D_MODEL)),
            full((1, D_MODEL)), full((LANES, D_MODEL)), full((LANES, 1)),
        ],
        out_specs=[
            pl.BlockSpec((TM_OUT, D_MODEL), lambda i: (i, 0)),
            pl.BlockSpec((TM_OUT * TOKEN_ROWS, LANES), lambda i: (i, 0)),
            pl.BlockSpec((LANES, TM_OUT), lambda i: (0, i)),
        ],
        out_shape=[
            jax.ShapeDtypeStruct((n, D_MODEL), F32),
            jax.ShapeDtypeStruct((n * TOKEN_ROWS, LANES), U32),
            jax.ShapeDtypeStruct((LANES, n), F32),
        ],
        compiler_params=_cparams(("parallel",)),
        name="out_proj",
    )(x2, y_ssd, y_att, wo_s, wo_a, ln2_w, w_router_t, b_router)


def _rank_kernel(meta_ref, dest_ref, cnt_ref, run_ref, offs_ref):
    p = pl.program_id(0)
    i = pl.program_id(1)
    tm = meta_ref.shape[1]
    meta = meta_ref[...]
    row = lax.broadcasted_iota(I32, (N_EXPERTS, tm), 0).astype(F32)
    oh0 = (row == meta[0:1, :]).astype(F32)
    oh1 = (row == meta[1:2, :]).astype(F32)
    oh = oh0 + oh1
    rowsum = jnp.sum(oh, axis=1, keepdims=True)

    @pl.when((p == 0) & (i == 0))
    def _():
        run_ref[...] = jnp.zeros_like(run_ref)

    @pl.when(p == 0)
    def _():
        run_ref[...] = run_ref[...] + rowsum
        dest_ref[...] = jnp.zeros_like(dest_ref)
        cnt_ref[...] = run_ref[...]

    @pl.when((p == 1) & (i == 0))
    def _():
        cnt = run_ref[...]
        padded = jnp.ceil(cnt * (1.0 / TM_EXP)) * TM_EXP
        r = lax.broadcasted_iota(I32, (N_EXPERTS, N_EXPERTS), 0)
        c = lax.broadcasted_iota(I32, (N_EXPERTS, N_EXPERTS), 1)
        offs_ref[...] = jnp.dot((c < r).astype(F32), padded, precision=HIGHEST, preferred_element_type=F32)
        cnt_ref[...] = cnt
        run_ref[...] = jnp.zeros_like(run_ref)

    @pl.when(p == 1)
    def _():
        r = lax.broadcasted_iota(I32, (tm, tm), 0)
        c = lax.broadcasted_iota(I32, (tm, tm), 1)
        before = jnp.dot(oh.astype(BF16), (r < c).astype(BF16), preferred_element_type=F32)
        base = before + jnp.tile(run_ref[...] + offs_ref[...], (1, tm // LANES))
        d0 = jnp.sum(oh0 * base, axis=0, keepdims=True)
        d1 = jnp.sum(oh1 * base, axis=0, keepdims=True)
        drow = lax.broadcasted_iota(I32, (8, tm), 0)
        dest_ref[...] = jnp.where(drow == 0, d0, jnp.where(drow == 1, d1, 0.0)).astype(I32)
        run_ref[...] = run_ref[...] + rowsum


def _rank(meta_t):
    n = meta_t.shape[1]
    per_expert = lambda: pl.BlockSpec((N_EXPERTS, LANES), lambda p, i: (0, 0))
    return pl.pallas_call(
        _rank_kernel,
        grid=(2, n // TM_RANK),
        in_specs=[pl.BlockSpec((8, TM_RANK), lambda p, i: (0, i))],
        out_specs=[pl.BlockSpec((8, TM_RANK), lambda p, i: (0, i * p)), per_expert()],
        out_shape=[jax.ShapeDtypeStruct((8, n), I32), jax.ShapeDtypeStruct((N_EXPERTS, LANES), F32)],
        scratch_shapes=[pltpu.VMEM((N_EXPERTS, LANES), F32), pltpu.VMEM((N_EXPERTS, LANES), F32)],
        compiler_params=_cparams(("arbitrary", "arbitrary")),
        name="rank",
    )(meta_t)


def _token_copy(src_ref, src_token, dst_ref, dst_token, sem):
    src = src_ref.at[pl.ds(pl.multiple_of(src_token * TOKEN_ROWS, TOKEN_ROWS), TOKEN_ROWS)]
    dst = dst_ref.at[pl.ds(pl.multiple_of(dst_token * TOKEN_ROWS, TOKEN_ROWS), TOKEN_ROWS)]
    return pltpu.make_async_copy(src, dst, sem)


def _invert_kernel(d0_ref, d1_ref, inv_ref):
    n = d0_ref.shape[0]

    def clear(s, _):
        inv_ref[s] = 0
        return 0

    lax.fori_loop(0, inv_ref.shape[0], clear, 0, unroll=8)

    def put(t, _):
        inv_ref[d0_ref[t]] = t
        inv_ref[d1_ref[t]] = n + t
        return 0

    lax.fori_loop(0, n, put, 0, unroll=4)


def _invert(dest0, dest1, n_slots):
    smem = pl.BlockSpec(memory_space=pltpu.SMEM)
    return pl.pallas_call(
        _invert_kernel,
        in_specs=[smem, smem],
        out_specs=smem,
        out_shape=jax.ShapeDtypeStruct((n_slots,), I32),
        name="invert",
    )(dest0, dest1)


CAST_ROWS = 256


def _cast_weight(src_ref, dst_ref):
    def body(i, _):
        rows = pl.ds(pl.multiple_of(i * CAST_ROWS, CAST_ROWS), CAST_ROWS)
        dst_ref[rows, :] = src_ref[rows, :].astype(BF16)
        return 0

    lax.fori_loop(0, src_ref.shape[0] // CAST_ROWS, body, 0)


S_EXPERT, S_FIRST, S_SLOT, S_NEXT, S_HAS_NEXT, S_USED, S_VALID = range(7)
WEIGHT_DMA_PRIORITY = 1


def _load_expert_weights(t, sched_ref, triples, sems):
    def copies(expert, slot):
        return [pltpu.make_async_copy(w.at[expert], stage.at[slot], sems.at[i, slot])
                for i, (w, stage, _) in enumerate(triples)]

    @pl.when(sched_ref[S_FIRST, t] == 1)
    def _():
        slot = sched_ref[S_SLOT, t]

        @pl.when(t == 0)
        def _():
            for cp in copies(sched_ref[S_EXPERT, t], slot):
                cp.start(priority=WEIGHT_DMA_PRIORITY)

        for cp in copies(sched_ref[S_EXPERT, t], slot):
            cp.wait()

        @pl.when(sched_ref[S_HAS_NEXT, t] == 1)
        def _():
            for cp in copies(sched_ref[S_NEXT, t], 1 - slot):
                cp.start(priority=WEIGHT_DMA_PRIORITY)

        for _, stage, dst in triples:
            _cast_weight(stage.at[slot], dst)


ROW_DMA_GROUPS = 4
GATHER_AHEAD = 2


def _tile_rows(sched_ref, tile, group):
    valid = sched_ref[S_VALID, tile]
    per = TM_EXP // ROW_DMA_GROUPS
    for r in range(group * per, (group + 1) * per):
        ok = r < valid
        yield r, jnp.where(ok, tile * TM_EXP + r, 0), ok


def _experts_up_kernel(sched_ref, inv_ref, hnp_hbm, wg_hbm, wu_hbm, hid_ref,
                       xbuf_ref, wg_stage, wu_stage, wgb_ref, wub_ref, wsems, gsem, *, n_tokens):
    t = pl.program_id(0)
    used = sched_ref[S_USED, 0]
    nbuf = GATHER_AHEAD + 1
    cur = t % nbuf
    ahead = (t + GATHER_AHEAD) % nbuf
    _load_expert_weights(t, sched_ref, [(wg_hbm, wg_stage, wgb_ref), (wu_hbm, wu_stage, wub_ref)], wsems)

    def start_rows(tile, buf, group):
        for r, slot, _ in _tile_rows(sched_ref, tile, group):
            token = inv_ref[slot] & (n_tokens - 1)
            _token_copy(hnp_hbm, token, xbuf_ref.at[buf], r, gsem.at[buf]).start()

    def wait_rows(buf):
        pltpu.make_async_copy(hnp_hbm.at[pl.ds(0, TM_EXP * TOKEN_ROWS)], xbuf_ref.at[buf], gsem.at[buf]).wait()

    for first in range(GATHER_AHEAD):
        @pl.when((t == 0) & (first < used))
        def _():
            for g in range(ROW_DMA_GROUPS):
                start_rows(first, first, g)

    def multiply(fetch_ahead):
        wait_rows(cur)
        c = D_MODEL // 2
        x_lo, x_hi = _unpack_halves(_load_token_rows(xbuf_ref.at[cur], TM_EXP))
        pieces = ((x_lo, wgb_ref, 0), (x_hi, wgb_ref, c), (x_lo, wub_ref, 0), (x_hi, wub_ref, c))
        acc = []
        for g, (x, w_ref, row0) in enumerate(pieces):
            if fetch_ahead:
                start_rows(t + GATHER_AHEAD, ahead, g)
            acc.append(jnp.dot(x, w_ref[row0:row0 + c, :], preferred_element_type=F32))
        hid_ref[...] = (_silu(acc[0] + acc[1]) * (acc[2] + acc[3])).astype(BF16)

    @pl.when(t + GATHER_AHEAD < used)
    def _():
        multiply(True)

    @pl.when((t + GATHER_AHEAD >= used) & (t < used))
    def _():
        multiply(False)

    @pl.when(t >= used)
    def _():
        hid_ref[...] = jnp.zeros_like(hid_ref)


def _experts_down_kernel(sched_ref, inv_ref, hid_ref, wd_hbm, y_hbm,
                         ybuf_ref, wd_stage, wdb_ref, wsems, ssem, *, n_tokens):
    t = pl.program_id(0)
    used = sched_ref[S_USED, 0]
    cur = t % 2
    _load_expert_weights(t, sched_ref, [(wd_hbm, wd_stage, wdb_ref)], wsems)

    def start_rows(tile, buf, group):
        for r, slot, ok in _tile_rows(sched_ref, tile, group):
            dst = jnp.where(ok, inv_ref[slot], TOP_K * n_tokens + r)
            _token_copy(ybuf_ref.at[buf], r, y_hbm, dst, ssem.at[buf]).start(priority=r % 2)

    def wait_rows(buf):
        pltpu.make_async_copy(ybuf_ref.at[buf], y_hbm.at[pl.ds(0, TM_EXP * TOKEN_ROWS)], ssem.at[buf]).wait()

    def multiply(send_previous):
        hid = hid_ref[...]
        q = D_MODEL // 4
        for half in range(2):
            if send_previous:
                start_rows(t - 1, 1 - cur, 2 * half)
            lo = jnp.dot(hid, wdb_ref[:, half * q:(half + 1) * q], preferred_element_type=F32)
            if send_previous:
                start_rows(t - 1, 1 - cur, 2 * half + 1)
            hi = jnp.dot(hid, wdb_ref[:, (half + 2) * q:(half + 3) * q], preferred_element_type=F32)
            _store_token_rows(ybuf_ref.at[cur], half * (q // LANES), _pack_halves(jnp.concatenate([lo, hi], axis=1)))

    @pl.when((t >= 2) & (t < used))
    def _():
        wait_rows(cur)

    @pl.when(t == 0)
    def _():
        ybuf_ref[1] = jnp.zeros(ybuf_ref.shape[1:], ybuf_ref.dtype)
        spare = pltpu.make_async_copy(ybuf_ref.at[1], y_hbm.at[pl.ds(TOP_K * n_tokens * TOKEN_ROWS, TM_EXP * TOKEN_ROWS)],
                                      ssem.at[1])
        spare.start()
        spare.wait()
        multiply(False)

    @pl.when((t > 0) & (t < used))
    def _():
        multiply(True)

    @pl.when(t == used - 1)
    def _():
        for g in range(ROW_DMA_GROUPS):
            start_rows(t, cur, g)

        @pl.when(t > 0)
        def _():
            wait_rows(1 - cur)

        wait_rows(cur)


def _tile_schedule(counts, n_tiles):
    tiles_per_expert = (counts + TM_EXP - 1) // TM_EXP
    tile_end = jnp.cumsum(tiles_per_expert)
    ids = jnp.arange(n_tiles, dtype=I32)
    raw = jnp.sum(ids[:, None] >= tile_end[None, :], axis=1)
    expert = jnp.minimum(raw, N_EXPERTS - 1).astype(I32)
    first = jnp.concatenate([jnp.ones((1,), I32), (expert[1:] != expert[:-1]).astype(I32)])
    slot = (jnp.cumsum(first) - 1) % 2
    start_idx = jnp.where(first == 1, ids, n_tiles)
    next_start = jnp.concatenate([lax.cummin(start_idx, reverse=True)[1:], jnp.full((1,), n_tiles, I32)])
    has_next = (next_start < n_tiles).astype(I32)
    next_expert = jnp.sum(jnp.where(ids[None, :] == next_start[:, None], expert[None, :], 0), axis=1)
    used = jnp.full((n_tiles,), tile_end[-1], I32)
    onehot = expert[:, None] == jnp.arange(N_EXPERTS)[None, :]
    tile_in_expert = ids - jnp.sum(jnp.where(onehot, (tile_end - tiles_per_expert)[None, :], 0), axis=1)
    rows_left = jnp.sum(jnp.where(onehot, counts[None, :], 0), axis=1) - tile_in_expert * TM_EXP
    valid = jnp.where(raw < N_EXPERTS, jnp.clip(rows_left, 0, TM_EXP), 0)
    return jnp.stack([expert, first, slot, next_expert, has_next, used, valid]).astype(I32)


def _experts(sched, inv, hnp, wg, wu, wd):
    n = hnp.shape[0] // TOKEN_ROWS
    n_slots = inv.shape[0]
    token_buf = lambda nbuf: pltpu.VMEM((nbuf, TM_EXP * TOKEN_ROWS, LANES), U32)
    row_spec = lambda width: pl.BlockSpec((TM_EXP, width), lambda t, sc, iv: (t, 0))
    hbm = pl.BlockSpec(memory_space=pl.ANY)
    big_vmem = pltpu.CompilerParams(dimension_semantics=("arbitrary",), vmem_limit_bytes=EXPERT_VMEM_LIMIT)
    hid = pl.pallas_call(
        functools.partial(_experts_up_kernel, n_tokens=n),
        grid_spec=pltpu.PrefetchScalarGridSpec(
            num_scalar_prefetch=2,
            grid=(n_slots // TM_EXP,),
            in_specs=[hbm, hbm, hbm],
            out_specs=row_spec(EXPERT_HIDDEN),
            scratch_shapes=[token_buf(GATHER_AHEAD + 1)]
                           + [pltpu.VMEM((2, D_MODEL, EXPERT_HIDDEN), F32)] * 2
                           + [pltpu.VMEM((D_MODEL, EXPERT_HIDDEN), BF16)] * 2
                           + [pltpu.SemaphoreType.DMA((2, 2)), pltpu.SemaphoreType.DMA((GATHER_AHEAD + 1,))],
        ),
        out_shape=jax.ShapeDtypeStruct((n_slots, EXPERT_HIDDEN), BF16),
        compiler_params=big_vmem,
        name="experts_up",
    )(sched, inv, hnp, wg, wu)
    return pl.pallas_call(
        functools.partial(_experts_down_kernel, n_tokens=n),
        grid_spec=pltpu.PrefetchScalarGridSpec(
            num_scalar_prefetch=2,
            grid=(n_slots // TM_EXP,),
            in_specs=[row_spec(EXPERT_HIDDEN), hbm],
            out_specs=hbm,
            scratch_shapes=[token_buf(2),
                            pltpu.VMEM((2, EXPERT_HIDDEN, D_MODEL), F32), pltpu.VMEM((EXPERT_HIDDEN, D_MODEL), BF16),
                            pltpu.SemaphoreType.DMA((1, 2)), pltpu.SemaphoreType.DMA((2,))],
        ),
        out_shape=jax.ShapeDtypeStruct(((TOP_K * n + TM_EXP) * TOKEN_ROWS, LANES), U32),
        compiler_params=big_vmem,
        name="experts_down",
    )(sched, inv, hid, wd)


def _combine_kernel(h_ref, meta_ref, y0_ref, y1_ref, o_ref):
    c = D_MODEL // 2
    meta = meta_ref[...].T
    w0, w1 = meta[:, 2:3], meta[:, 3:4]
    tm = h_ref.shape[0]
    lo0, hi0 = _unpack_halves(_load_token_rows(y0_ref, tm), F32)
    lo1, hi1 = _unpack_halves(_load_token_rows(y1_ref, tm), F32)
    o_ref[:, :c] = h_ref[:, :c] + (lo0 * w0 + lo1 * w1)
    o_ref[:, c:] = h_ref[:, c:] + (hi0 * w0 + hi1 * w1)


def _combine(h, meta, y):
    n = h.shape[0]
    nb = n // TM_COMB
    return pl.pallas_call(
        _combine_kernel,
        grid=(nb,),
        in_specs=[
            pl.BlockSpec((TM_COMB, D_MODEL), lambda i: (i, 0)),
            pl.BlockSpec((LANES, TM_COMB), lambda i: (0, i)),
            pl.BlockSpec((TM_COMB * TOKEN_ROWS, LANES), lambda i: (i, 0)),
            pl.BlockSpec((TM_COMB * TOKEN_ROWS, LANES), lambda i: (nb + i, 0)),
        ],
        out_specs=pl.BlockSpec((TM_COMB, D_MODEL), lambda i: (i, 0)),
        out_shape=jax.ShapeDtypeStruct((n, D_MODEL), F32),
        compiler_params=_cparams(("parallel",)),
        name="combine",
    )(h, meta, y, y)


def _lambda_init(layer_idx):
    return 0.8 - 0.6 * math.exp(-0.3 * layer_idx)


def _pad_lanes(v, width=LANES):
    return jnp.pad(v, ((0, 0), (0, width - v.shape[1])))


def _layer(l, x2, pos_col, bsz, seq, ln1_w, w_in, conv_w, conv_b, dt_bias, a_log, d_skip, ssd_norm_w,
           q_norm_w, k_norm_w, lambda_q1, lambda_k1, lambda_q2, lambda_k2, subln_w, w_out, ln2_w,
           w_router_group, b_router_group, w_router_expert, b_router_expert, w_gate, w_up, w_down):
    n = x2.shape[0]
    c_z, c_xbc, c_dt = SSD_WIDTH, SSD_WIDTH + SSD_CONV_DIM, SSD_WIDTH + SSD_CONV_DIM + SSD_HEADS
    c_q, c_k = c_dt + ATTN_WIDTH, c_dt + 2 * ATTN_WIDTH
    w_main = jnp.concatenate([w_in[:, :c_z], w_in[:, c_dt:c_q], w_in[:, c_q:c_k], w_in[:, c_z:c_xbc]],
                             axis=1).astype(BF16)
    w_vt = w_in[:, c_k:].T.astype(BF16)
    w_dt = _pad_lanes(w_in[:, c_xbc:c_dt]).astype(BF16)

    u, v_t, dt_raw = _in_proj(x2, ln1_w[None, :], w_main, w_vt, w_dt)

    a_neg = _pad_lanes(-jnp.exp(a_log.astype(F32))[None, :])
    y_ssd = _ssd(u, dt_raw, conv_w, conv_b[None, :], _pad_lanes(dt_bias[None, :]), a_neg,
                 jnp.repeat(d_skip, SSD_HEAD_DIM)[None, :], ssd_norm_w[None, :], bsz, seq)

    inv_freq = jnp.power(ROPE_THETA, -jnp.arange(0, ROPE_DIM, 2, dtype=F32) / ROPE_DIM)
    d = jnp.arange(LANES) % ATTN_QK_DIM
    invf_lanes = jnp.where(d < ROPE_DIM, inv_freq[d % (ROPE_DIM // 2)], 0.0)[None, :]
    seg_ones = (jnp.arange(LANES)[:, None] // ATTN_QK_DIM == jnp.arange(LANES)[None, :] // ATTN_QK_DIM).astype(BF16)
    qp, kp = _qk_prep(u, pos_col, invf_lanes, jnp.tile(q_norm_w, 2)[None, :], jnp.tile(k_norm_w, 2)[None, :], seg_ones)

    lam_vecs = jnp.stack([lambda_q1, lambda_k1, lambda_q2, lambda_k2]).astype(F32)
    y_att = _attention(qp, kp, v_t, lam_vecs, subln_w[:, None], bsz, seq, _lambda_init(l))

    w_out_b = w_out.astype(BF16)
    w_router_t = _pad_lanes(jnp.concatenate([w_router_group, w_router_expert], axis=1)).T.astype(BF16)
    b_router = _pad_lanes(jnp.concatenate([b_router_group, b_router_expert])[None, :]).T
    h, hnp, meta = _out_proj(x2, y_ssd, y_att, w_out_b[:SSD_WIDTH], w_out_b[SSD_WIDTH:], ln2_w[None, :],
                             w_router_t, b_router)

    dest, cnt = _rank(meta)
    dest0, dest1 = dest[0], dest[1]
    counts = cnt[:, 0].astype(I32)
    n_slots = (n * TOP_K + N_EXPERTS * (TM_EXP - 1)) // TM_EXP * TM_EXP
    sched = _tile_schedule(counts, n_slots // TM_EXP)

    inv = _invert(dest0, dest1, n_slots)
    y = _experts(sched, inv, hnp, w_gate, w_up, w_down)
    return _combine(h, meta, y)


def kernel(x, positions, ln1_w, w_in, conv_w, conv_b, dt_bias, a_log, d_skip, ssd_norm_w, q_norm_w, k_norm_w,
           lambda_q1, lambda_k1, lambda_q2, lambda_k2, subln_w, w_out, ln2_w, w_router_group, b_router_group,
           w_router_expert, b_router_expert, w_gate, w_up, w_down):
    bsz, seq, d = x.shape
    assert d == D_MODEL and seq % TQ == 0 and (bsz * seq) % TM_IN == 0
    x2 = x.reshape(bsz * seq, d)
    pos_col = positions.astype(F32).reshape(bsz * seq, 1)
    params = (ln1_w, w_in, conv_w, conv_b, dt_bias, a_log, d_skip, ssd_norm_w, q_norm_w, k_norm_w,
              lambda_q1, lambda_k1, lambda_q2, lambda_k2, subln_w, w_out, ln2_w, w_router_group, b_router_group,
              w_router_expert, b_router_expert, w_gate, w_up, w_down)
    for l in range(ln1_w.shape[0]):
        x2 = _layer(l, x2, pos_col, bsz, seq, *[p[l] for p in params])
    return x2.reshape(bsz, seq, d)
```

```python
import functools
import math

import jax
import jax.numpy as jnp
from jax import lax
from jax.experimental import pallas as pl
from jax.experimental.pallas import tpu as pltpu

F32 = jnp.float32
BF16 = jnp.bfloat16
I32 = jnp.int32
U32 = jnp.uint32
HIGHEST = lax.Precision.HIGHEST

D_MODEL = 2048
SSD_WIDTH = 1024
ATTN_WIDTH = 1024
SSD_HEAD_DIM = 64
SSD_HEADS = 16
SSD_GROUPS = 2
SSD_HEADS_PER_GROUP = SSD_HEADS // SSD_GROUPS
SSD_STATE = 128
SSD_CONV = 4
SSD_CHUNK = 128
SSD_CONV_DIM = SSD_WIDTH + 2 * SSD_GROUPS * SSD_STATE
ATTN_V_DIM = 128
ATTN_HEADS = 8
ATTN_QK_DIM = 64
ROPE_THETA = 500000.0
ROPE_DIM = 16
N_EXPERT_GROUPS = 4
EXPERTS_PER_GROUP = 8
N_EXPERTS = 32
TOP_K = 2
EXPERT_HIDDEN = 1024
EPS = 1e-6

LANES = 128
NEG_INF = float("-inf")

TM_IN = 512
TN_IN = 1536
TM_QK = 512
TQ = 256
ATTN_HB = 8
ONES_ROWS = 16
TM_OUT = 256
TM_RANK = 512
TM_EXP = 256
TM_COMB = 256
ROUTER_ROWS = 40
U_COLS = SSD_WIDTH + 2 * ATTN_WIDTH + SSD_CONV_DIM
VMEM_LIMIT = 52 * 1024 * 1024
EXPERT_VMEM_LIMIT = 58 * 1024 * 1024


def _cparams(sem):
    return pltpu.CompilerParams(dimension_semantics=sem, vmem_limit_bytes=VMEM_LIMIT)


def _silu(x):
    return x * (1.0 / (1.0 + jnp.exp(-x)))


def _softplus(x):
    return jnp.maximum(x, 0.0) + jnp.log(1.0 + jnp.exp(-jnp.abs(x)))


def _inproj_kernel(x_ref, lnw_ref, w_ref, wvt_ref, wdt_ref, u_ref, vt_ref, dt_ref, xn_ref):
    @pl.when(pl.program_id(1) == 0)
    def _():
        x = x_ref[...]
        ms = jnp.mean(x * x, axis=-1, keepdims=True)
        xn = (x * lax.rsqrt(ms + EPS) * lnw_ref[...]).astype(BF16)
        xn_ref[...] = xn
        dt_ref[...] = jnp.dot(xn, wdt_ref[...], preferred_element_type=F32)
        vt_ref[...] = lax.dot_general(wvt_ref[...], xn, (((1,), (1,)), ((), ())),
                                      preferred_element_type=F32).astype(BF16)

    u_ref[...] = jnp.dot(xn_ref[...], w_ref[...], preferred_element_type=F32).astype(BF16)


def _in_proj(x2, ln_w, w_main, w_vt, w_dt):
    n = x2.shape[0]
    return pl.pallas_call(
        _inproj_kernel,
        grid=(n // TM_IN, U_COLS // TN_IN),
        in_specs=[
            pl.BlockSpec((TM_IN, D_MODEL), lambda i, j: (i, 0)),
            pl.BlockSpec((1, D_MODEL), lambda i, j: (0, 0)),
            pl.BlockSpec((D_MODEL, TN_IN), lambda i, j: (0, j)),
            pl.BlockSpec((ATTN_WIDTH, D_MODEL), lambda i, j: (0, 0)),
            pl.BlockSpec((D_MODEL, LANES), lambda i, j: (0, 0)),
        ],
        out_specs=[
            pl.BlockSpec((TM_IN, TN_IN), lambda i, j: (i, j)),
            pl.BlockSpec((ATTN_WIDTH, TM_IN), lambda i, j: (0, i)),
            pl.BlockSpec((TM_IN, LANES), lambda i, j: (i, 0)),
        ],
        out_shape=[
            jax.ShapeDtypeStruct((n, U_COLS), BF16),
            jax.ShapeDtypeStruct((ATTN_WIDTH, n), BF16),
            jax.ShapeDtypeStruct((n, LANES), F32),
        ],
        scratch_shapes=[pltpu.VMEM((TM_IN, D_MODEL), BF16)],
        compiler_params=_cparams(("parallel", "arbitrary")),
        name="in_proj",
    )(x2, ln_w, w_main, w_vt, w_dt)


def _ssd_kernel(z_ref, xbc_ref, dt_ref, convw_ref, convb_ref, dtb_ref, aneg_ref, dskip_ref, normw_ref,
                y_ref, xp_ref, st_ref, yacc_ref):
    L = SSD_CHUNK
    P = SSD_HEAD_DIM

    @pl.when(pl.program_id(1) == 0)
    def _():
        xp_ref[0:8, :] = jnp.zeros((8, SSD_CONV_DIM), F32)
        st_ref[...] = jnp.zeros_like(st_ref)

    xp_ref[8:8 + L, :] = xbc_ref[...].astype(F32)
    acc = jnp.broadcast_to(convb_ref[...], (L, SSD_CONV_DIM))
    for k in range(SSD_CONV):
        acc = acc + xp_ref[5 + k:5 + k + L, :] * convw_ref[k:k + 1, :]
    xp_ref[0:8, :] = xp_ref[L:L + 8, :]
    xc = _silu(acc)

    dt = _softplus(dt_ref[...] + dtb_ref[...])
    a = dt * aneg_ref[...]
    row = lax.broadcasted_iota(I32, (L, L), 0)
    col = lax.broadcasted_iota(I32, (L, L), 1)
    causal = row >= col
    a_cs = jnp.dot(causal.astype(F32), a, precision=HIGHEST, preferred_element_type=F32)
    a_last = a_cs[L - 1:L, :]
    ea = jnp.exp(a_cs)
    dsdt = jnp.exp(a_last - a_cs) * dt
    cd = jnp.exp(a_last)
    a_cs_t = a_cs.T
    dt_t = dt.T
    dsdt_t = dsdt.T

    for g in range(SSD_GROUPS):
        b_g = xc[:, SSD_WIDTH + g * SSD_STATE:SSD_WIDTH + (g + 1) * SSD_STATE]
        c_off = SSD_WIDTH + SSD_GROUPS * SSD_STATE
        c_g = xc[:, c_off + g * SSD_STATE:c_off + (g + 1) * SSD_STATE]
        cb = lax.dot_general(c_g.astype(BF16), b_g.astype(BF16), (((1,), (1,)), ((), ())),
                             preferred_element_type=F32)
        b_gt = b_g.T
        for hh in range(SSD_HEADS_PER_GROUP):
            h = g * SSD_HEADS_PER_GROUP + hh
            xs_h = xc[:, h * P:(h + 1) * P].astype(BF16)
            seg = a_cs[:, h:h + 1] - a_cs_t[h:h + 1, :]
            dec = jnp.exp(jnp.where(causal, seg, NEG_INF))
            m = (cb * dec * dt_t[h:h + 1, :]).astype(BF16)
            c_s = (c_g * ea[:, h:h + 1]).astype(BF16)
            s_prev = st_ref[h]
            lhs = jnp.concatenate([m, c_s], axis=1)
            rhs = jnp.concatenate([xs_h, s_prev.astype(BF16)], axis=0)
            yacc_ref[:, h * P:(h + 1) * P] = jnp.dot(lhs, rhs, preferred_element_type=F32)
            bw = (b_gt * dsdt_t[h:h + 1, :]).astype(BF16)
            st_ref[h] = s_prev * cd[:, h:h + 1] + jnp.dot(bw, xs_h, preferred_element_type=F32)

    y = yacc_ref[...] + xc[:, :SSD_WIDTH] * dskip_ref[...]
    y = y * _silu(z_ref[...].astype(F32))
    gw = SSD_WIDTH // SSD_GROUPS
    for g in range(SSD_GROUPS):
        yg = y[:, g * gw:(g + 1) * gw]
        ms = jnp.mean(yg * yg, axis=-1, keepdims=True)
        y_ref[:, g * gw:(g + 1) * gw] = (yg * lax.rsqrt(ms + EPS) * normw_ref[:, g * gw:(g + 1) * gw]).astype(BF16)


def _ssd(u, dt_raw, conv_w, conv_b, dt_bias, a_neg, dskip_lanes, norm_w, bsz, seq):
    n = u.shape[0]
    nc = seq // SSD_CHUNK
    xbc_blk = (SSD_WIDTH + 2 * ATTN_WIDTH) // SSD_CONV_DIM
    full = lambda shape: pl.BlockSpec(shape, lambda b, c: (0, 0))
    return pl.pallas_call(
        _ssd_kernel,
        grid=(bsz, nc),
        in_specs=[
            pl.BlockSpec((SSD_CHUNK, SSD_WIDTH), lambda b, c: (b * nc + c, 0)),
            pl.BlockSpec((SSD_CHUNK, SSD_CONV_DIM), lambda b, c: (b * nc + c, xbc_blk)),
            pl.BlockSpec((SSD_CHUNK, LANES), lambda b, c: (b * nc + c, 0)),
            full((SSD_CONV, SSD_CONV_DIM)),
            full((1, SSD_CONV_DIM)),
            full((1, LANES)),
            full((1, LANES)),
            full((1, SSD_WIDTH)),
            full((1, SSD_WIDTH)),
        ],
        out_specs=pl.BlockSpec((SSD_CHUNK, SSD_WIDTH), lambda b, c: (b * nc + c, 0)),
        out_shape=jax.ShapeDtypeStruct((n, SSD_WIDTH), BF16),
        scratch_shapes=[
            pltpu.VMEM((SSD_CHUNK + 8, SSD_CONV_DIM), F32),
            pltpu.VMEM((SSD_HEADS, SSD_STATE, SSD_HEAD_DIM), F32),
            pltpu.VMEM((SSD_CHUNK, SSD_WIDTH), F32),
        ],
        compiler_params=_cparams(("parallel", "arbitrary")),
        name="ssd",
    )(u, u, dt_raw, conv_w, conv_b, dt_bias, a_neg, dskip_lanes, norm_w)


def _qkprep_kernel(q_ref, k_ref, pos_ref, invf_ref, qw_ref, kw_ref, ones_ref, expand_ref, qo_ref, ko_ref):
    tm = q_ref.shape[0]
    half = ROPE_DIM // 2
    ang_t = invf_ref[...] * pos_ref[0:1, :]
    trig = jnp.concatenate([jnp.cos(ang_t), jnp.sin(ang_t), jnp.zeros((LANES - 4 * half, tm), F32)], axis=0).T
    t1 = trig.astype(BF16)
    r1 = trig - t1.astype(F32)
    t2 = r1.astype(BF16)
    t3 = (r1 - t2.astype(F32)).astype(BF16)
    spread = expand_ref[...]
    tab = (jnp.dot(t1, spread, preferred_element_type=F32) + jnp.dot(t2, spread, preferred_element_type=F32)
           + jnp.dot(t3, spread, preferred_element_type=F32))
    cs = tab[:, :LANES]
    s_lo = tab[:, LANES:2 * LANES]
    s_hi = tab[:, 2 * LANES:]
    for src, w_ref, dst in ((q_ref, qw_ref, qo_ref), (k_ref, kw_ref, ko_ref)):
        for hb in range(ATTN_HEADS):
            x = src[:, hb * LANES:(hb + 1) * LANES].astype(F32)
            ss = jnp.dot((x * x).astype(BF16), ones_ref[...], preferred_element_type=F32)
            xn = x * lax.rsqrt(ss * (1.0 / ATTN_QK_DIM) + EPS) * w_ref[...]
            out = xn * cs + pltpu.roll(xn, LANES - half, 1) * s_lo + pltpu.roll(xn, half, 1) * s_hi
            dst[:, hb * LANES:(hb + 1) * LANES] = out.astype(BF16)


def _rope_spread():
    half = ROPE_DIM // 2
    lane = jnp.arange(LANES)
    d = lane % ATTN_QK_DIM
    src = jnp.arange(LANES)[:, None]
    cos_src = jnp.where(d < ROPE_DIM, d % half, half)
    cos_tab = (src == cos_src[None, :]).astype(F32)
    lo_tab = -((src == (2 * half + d)[None, :]) & (d < half)[None, :]).astype(F32)
    hi_tab = ((src == (2 * half + d - half)[None, :]) & ((d >= half) & (d < ROPE_DIM))[None, :]).astype(F32)
    return jnp.concatenate([cos_tab, lo_tab, hi_tab], axis=1).astype(BF16)


def _qk_prep(u, pos_rows, invf_col, qw_lanes, kw_lanes, seg_ones, spread):
    n = u.shape[0]
    full = lambda shape: pl.BlockSpec(shape, lambda i: (0, 0))
    return pl.pallas_call(
        _qkprep_kernel,
        grid=(n // TM_QK,),
        in_specs=[
            pl.BlockSpec((TM_QK, ATTN_WIDTH), lambda i: (i, 1)),
            pl.BlockSpec((TM_QK, ATTN_WIDTH), lambda i: (i, 2)),
            pl.BlockSpec((8, TM_QK), lambda i: (0, i)),
            full((ROPE_DIM, 1)), full((1, LANES)), full((1, LANES)), full((LANES, LANES)), full((LANES, 3 * LANES)),
        ],
        out_specs=[pl.BlockSpec((TM_QK, ATTN_WIDTH), lambda i: (i, 0))] * 2,
        out_shape=[jax.ShapeDtypeStruct((n, ATTN_WIDTH), BF16)] * 2,
        compiler_params=_cparams(("parallel",)),
        name="qk_prep",
    )(u, u, pos_rows, invf_col, qw_lanes, kw_lanes, seg_ones, spread)


def _attn_kernel(q_ref, k_ref, vt_ref, lamv_ref, subw_ref, o_ref, acc_ref, *, lam_init):
    qi = pl.program_id(2)
    lane = lax.broadcasted_iota(I32, (TQ, LANES), 1)
    qs = []
    for hb in range(ATTN_HB):
        q = q_ref[:, hb * LANES:(hb + 1) * LANES]
        zero = jnp.zeros_like(q)
        qs.append(jnp.concatenate([jnp.where(lane < ATTN_QK_DIM, q, zero),
                                   jnp.where(lane >= ATTN_QK_DIM, q, zero)], axis=0))
    acc_ref[...] = jnp.zeros_like(acc_ref)
    kv_idx = lax.broadcasted_iota(I32, (TQ, 2 * TQ), 0)
    q_idx = lax.broadcasted_iota(I32, (TQ, 2 * TQ), 1) & (TQ - 1)
    nt = (((1,), (1,)), ((), ()))

    def block(j, carry, masked):
        off = pl.multiple_of(j * TQ, TQ)
        ss = []
        for hb in range(ATTN_HB):
            kb = k_ref[pl.ds(off, TQ), hb * LANES:(hb + 1) * LANES]
            ss.append(lax.dot_general(kb, qs[hb], nt, preferred_element_type=F32))
        new, ps, alphas = [], [], []
        for hb in range(ATTN_HB):
            m_old = carry[hb]
            s = ss[hb]
            if masked:
                s = jnp.where(kv_idx <= q_idx, s, NEG_INF)
            m_new = jnp.maximum(m_old, jnp.max(s, axis=0, keepdims=True))
            alphas.append(jnp.exp2(m_old - m_new))
            ps.append(jnp.exp2(s - m_new).astype(BF16))
            new.append(m_new)
        pvs = []
        for hb in range(ATTN_HB):
            vb = jnp.concatenate([vt_ref[hb * LANES:(hb + 1) * LANES, pl.ds(off, TQ)], ones_rows], axis=0)
            pvs.append(jnp.dot(vb, ps[hb], preferred_element_type=F32))
        for hb in range(ATTN_HB):
            acc_ref[hb] = alphas[hb] * acc_ref[hb] + pvs[hb]
        return tuple(new)

    ones_rows = jnp.ones((ONES_ROWS, TQ), BF16)
    init = (jnp.full((1, 2 * TQ), NEG_INF, F32),) * ATTN_HB
    carry = lax.fori_loop(0, qi, lambda j, cr: block(j, cr, False), init)
    block(qi, carry, True)

    lv = lamv_ref[...]
    lam = (jnp.exp(jnp.sum(lv[0:1] * lv[1:2], axis=1, keepdims=True))
           - jnp.exp(jnp.sum(lv[2:3] * lv[3:4], axis=1, keepdims=True)) + lam_init)
    for hb in range(ATTN_HB):
        acc = acc_ref[hb]
        o2 = acc[:ATTN_V_DIM] * (1.0 / acc[ATTN_V_DIM:ATTN_V_DIM + 1])
        o_t = o2[:, :TQ] - lam * o2[:, TQ:]
        ms = jnp.mean(o_t * o_t, axis=0, keepdims=True)
        o_t = o_t * lax.rsqrt(ms + EPS) * subw_ref[...] * (1.0 - lam_init)
        o_ref[:, hb * LANES:(hb + 1) * LANES] = o_t.T.astype(BF16)


def _attention(qp, kp, v_t, lam_vecs, subw_col, bsz, seq, lam_init):
    n = qp.shape[0]
    nq = seq // TQ
    w = ATTN_HB * ATTN_V_DIM
    return pl.pallas_call(
        functools.partial(_attn_kernel, lam_init=lam_init),
        grid=(bsz, ATTN_HEADS // ATTN_HB, nq),
        in_specs=[
            pl.BlockSpec((TQ, w), lambda b, h, i: (b * nq + i, h)),
            pl.BlockSpec((seq, w), lambda b, h, i: (b, h)),
            pl.BlockSpec((w, seq), lambda b, h, i: (h, b)),
            pl.BlockSpec((4, ATTN_QK_DIM), lambda b, h, i: (0, 0)),
            pl.BlockSpec((ATTN_V_DIM, 1), lambda b, h, i: (0, 0)),
        ],
        out_specs=pl.BlockSpec((TQ, w), lambda b, h, i: (b * nq + i, h)),
        out_shape=jax.ShapeDtypeStruct((n, ATTN_WIDTH), BF16),
        scratch_shapes=[pltpu.VMEM((ATTN_HB, ATTN_V_DIM + ONES_ROWS, 2 * TQ), F32)],
        compiler_params=_cparams(("parallel", "parallel", "arbitrary")),
        name="attn",
    )(qp, kp, v_t, lam_vecs, subw_col)


def _pack_halves(x):
    c = x.shape[1] // 2
    lo = pltpu.bitcast(x[:, :c].astype(BF16).astype(F32), U32) >> 16
    hi = pltpu.bitcast(x[:, c:].astype(BF16).astype(F32), U32) & jnp.uint32(0xFFFF0000)
    return hi | lo


TOKEN_ROWS = D_MODEL // 2 // LANES


def _store_token_rows(ref, first_piece, packed):
    tm = packed.shape[0]
    for j in range(packed.shape[1] // LANES):
        ref[pl.ds(first_piece + j, tm, stride=TOKEN_ROWS), :] = packed[:, j * LANES:(j + 1) * LANES]


def _load_token_rows(ref, tm):
    return jnp.concatenate([ref[pl.ds(s, tm, stride=TOKEN_ROWS), :] for s in range(TOKEN_ROWS)], axis=1)


def _unpack_halves(w, dtype=BF16):
    lo = pltpu.bitcast(w << 16, F32).astype(dtype)
    hi = pltpu.bitcast(w & jnp.uint32(0xFFFF0000), F32).astype(dtype)
    return lo, hi


def _outproj_kernel(x_ref, ys_ref, ya_ref, wos_ref, woa_ref, ln2_ref, wr_ref, br_ref,
                    h_ref, hnp_ref, meta_ref):
    tm = x_ref.shape[0]
    h = (x_ref[...]
         + jnp.dot(ys_ref[...], wos_ref[...], preferred_element_type=F32)
         + jnp.dot(ya_ref[...], woa_ref[...], preferred_element_type=F32))
    h_ref[...] = h
    ms = jnp.mean(h * h, axis=-1, keepdims=True)
    hn = h * lax.rsqrt(ms + EPS) * ln2_ref[...]
    _store_token_rows(hnp_ref, 0, _pack_halves(hn))

    lg_t = lax.dot_general(wr_ref[...], hn.astype(BF16), (((1,), (1,)), ((), ())),
                           preferred_element_type=F32)
    lg = lg_t[0:ROUTER_ROWS, :] + br_ref[0:ROUTER_ROWS, :]
    row = lax.broadcasted_iota(I32, (ROUTER_ROWS, tm), 0).astype(F32)
    big = float(LANES)
    gl = jnp.where(row < N_EXPERT_GROUPS, lg, NEG_INF)
    gmax = jnp.max(gl, axis=0, keepdims=True)
    gsel = jnp.min(jnp.where(gl == gmax, row, big), axis=0, keepdims=True)
    g_w = 1.0 / jnp.sum(jnp.exp(gl - gmax), axis=0, keepdims=True)
    eid = row - N_EXPERT_GROUPS
    lo = gsel * EXPERTS_PER_GROUP
    emask = (eid >= lo) & (eid < lo + EXPERTS_PER_GROUP)
    el = jnp.where(emask, lg, NEG_INF)
    m1 = jnp.max(el, axis=0, keepdims=True)
    i1 = jnp.min(jnp.where(el == m1, eid, big), axis=0, keepdims=True)
    el2 = jnp.where(eid == i1, NEG_INF, el)
    m2 = jnp.max(el2, axis=0, keepdims=True)
    i2 = jnp.min(jnp.where(el2 == m2, eid, big), axis=0, keepdims=True)
    e2 = jnp.exp(m2 - m1)
    w1 = g_w / (1.0 + e2)
    w2 = g_w * e2 / (1.0 + e2)
    mrow = lax.broadcasted_iota(I32, (LANES, tm), 0)
    meta_ref[...] = jnp.where(mrow == 0, i1, jnp.where(mrow == 1, i2, jnp.where(mrow == 2, w1, jnp.where(mrow == 3, w2, 0.0))))


def _out_proj(x2, y_ssd, y_att, wo_s, wo_a, ln2_w, w_router_t, b_router):
    n = x2.shape[0]
    full = lambda shape: pl.BlockSpec(shape, lambda i: (0, 0))
    return pl.pallas_call(
        _outproj_kernel,
        grid=(n // TM_OUT,),
        in_specs=[
            pl.BlockSpec((TM_OUT, D_MODEL), lambda i: (i, 0)),
            pl.BlockSpec((TM_OUT, SSD_WIDTH), lambda i: (i, 0)),
            pl.BlockSpec((TM_OUT, ATTN_WIDTH), lambda i: (i, 0)),
            full((SSD_WIDTH, D_MODEL)), full((ATTN_WIDTH, D_MODEL)),
            full((1, D_MODEL)), full((LANES, D_MODEL)), full((LANES, 1)),
        ],
        out_specs=[
            pl.BlockSpec((TM_OUT, D_MODEL), lambda i: (i, 0)),
            pl.BlockSpec((TM_OUT * TOKEN_ROWS, LANES), lambda i: (i, 0)),
            pl.BlockSpec((LANES, TM_OUT), lambda i: (0, i)),
        ],
        out_shape=[
            jax.ShapeDtypeStruct((n, D_MODEL), F32),
            jax.ShapeDtypeStruct((n * TOKEN_ROWS, LANES), U32),
            jax.ShapeDtypeStruct((LANES, n), F32),
        ],
        compiler_params=_cparams(("parallel",)),
        name="out_proj",
    )(x2, y_ssd, y_att, wo_s, wo_a, ln2_w, w_router_t, b_router)


def _rank_kernel(meta_ref, dest_ref, cnt_ref, run_ref, offs_ref):
    p = pl.program_id(0)
    i = pl.program_id(1)
    tm = meta_ref.shape[1]
    meta = meta_ref[...]
    row = lax.broadcasted_iota(I32, (N_EXPERTS, tm), 0).astype(F32)
    oh0 = (row == meta[0:1, :]).astype(F32)
    oh1 = (row == meta[1:2, :]).astype(F32)
    oh = oh0 + oh1
    rowsum = jnp.sum(oh, axis=1, keepdims=True)

    @pl.when((p == 0) & (i == 0))
    def _():
        run_ref[...] = jnp.zeros_like(run_ref)

    @pl.when(p == 0)
    def _():
        run_ref[...] = run_ref[...] + rowsum
        dest_ref[...] = jnp.zeros_like(dest_ref)
        cnt_ref[...] = run_ref[...]

    @pl.when((p == 1) & (i == 0))
    def _():
        cnt = run_ref[...]
        padded = jnp.ceil(cnt * (1.0 / TM_EXP)) * TM_EXP
        r = lax.broadcasted_iota(I32, (N_EXPERTS, N_EXPERTS), 0)
        c = lax.broadcasted_iota(I32, (N_EXPERTS, N_EXPERTS), 1)
        offs_ref[...] = jnp.dot((c < r).astype(F32), padded, precision=HIGHEST, preferred_element_type=F32)
        cnt_ref[...] = cnt
        run_ref[...] = jnp.zeros_like(run_ref)

    @pl.when(p == 1)
    def _():
        r = lax.broadcasted_iota(I32, (tm, tm), 0)
        c = lax.broadcasted_iota(I32, (tm, tm), 1)
        before = jnp.dot(oh.astype(BF16), (r < c).astype(BF16), preferred_element_type=F32)
        base = before + jnp.tile(run_ref[...] + offs_ref[...], (1, tm // LANES))
        d0 = jnp.sum(oh0 * base, axis=0, keepdims=True)
        d1 = jnp.sum(oh1 * base, axis=0, keepdims=True)
        drow = lax.broadcasted_iota(I32, (8, tm), 0)
        dest_ref[...] = jnp.where(drow == 0, d0, jnp.where(drow == 1, d1, 0.0)).astype(I32)
        run_ref[...] = run_ref[...] + rowsum


def _rank(meta_t):
    n = meta_t.shape[1]
    per_expert = lambda: pl.BlockSpec((N_EXPERTS, LANES), lambda p, i: (0, 0))
    return pl.pallas_call(
        _rank_kernel,
        grid=(2, n // TM_RANK),
        in_specs=[pl.BlockSpec((8, TM_RANK), lambda p, i: (0, i))],
        out_specs=[pl.BlockSpec((8, TM_RANK), lambda p, i: (0, i * p)), per_expert()],
        out_shape=[jax.ShapeDtypeStruct((8, n), I32), jax.ShapeDtypeStruct((N_EXPERTS, LANES), F32)],
        scratch_shapes=[pltpu.VMEM((N_EXPERTS, LANES), F32), pltpu.VMEM((N_EXPERTS, LANES), F32)],
        compiler_params=_cparams(("arbitrary", "arbitrary")),
        name="rank",
    )(meta_t)


def _token_copy(src_ref, src_token, dst_ref, dst_token, sem):
    src = src_ref.at[pl.ds(pl.multiple_of(src_token * TOKEN_ROWS, TOKEN_ROWS), TOKEN_ROWS)]
    dst = dst_ref.at[pl.ds(pl.multiple_of(dst_token * TOKEN_ROWS, TOKEN_ROWS), TOKEN_ROWS)]
    return pltpu.make_async_copy(src, dst, sem)


def _invert_kernel(d0_ref, d1_ref, valid_ref, inv_ref):
    n = d0_ref.shape[0]

    def clear_tile(tile, _):
        def clear(r, _):
            inv_ref[tile * TM_EXP + r] = 0
            return 0

        lax.fori_loop(valid_ref[tile], TM_EXP, clear, 0)
        return 0

    lax.fori_loop(0, valid_ref.shape[0], clear_tile, 0)

    def put(t, _):
        inv_ref[d0_ref[t]] = t
        inv_ref[d1_ref[t]] = n + t
        return 0

    lax.fori_loop(0, n, put, 0, unroll=8)


def _invert(dest0, dest1, tile_valid):
    smem = pl.BlockSpec(memory_space=pltpu.SMEM)
    return pl.pallas_call(
        _invert_kernel,
        in_specs=[smem, smem, smem],
        out_specs=smem,
        out_shape=jax.ShapeDtypeStruct((tile_valid.shape[0] * TM_EXP,), I32),
        name="invert",
    )(dest0, dest1, tile_valid)


CAST_ROWS = 256


def _cast_weight(src_ref, dst_ref):
    def body(i, _):
        rows = pl.ds(pl.multiple_of(i * CAST_ROWS, CAST_ROWS), CAST_ROWS)
        dst_ref[rows, :] = src_ref[rows, :].astype(BF16)
        return 0

    lax.fori_loop(0, src_ref.shape[0] // CAST_ROWS, body, 0)


S_EXPERT, S_FIRST, S_SLOT, S_NEXT, S_HAS_NEXT, S_USED, S_VALID = range(7)
WEIGHT_DMA_PRIORITY = 1


def _load_expert_weights(t, sched_ref, triples, sems):
    def copies(expert, slot):
        return [pltpu.make_async_copy(w.at[expert], stage.at[slot], sems.at[i, slot])
                for i, (w, stage, _) in enumerate(triples)]

    @pl.when(sched_ref[S_FIRST, t] == 1)
    def _():
        slot = sched_ref[S_SLOT, t]

        @pl.when(t == 0)
        def _():
            for cp in copies(sched_ref[S_EXPERT, t], slot):
                cp.start(priority=WEIGHT_DMA_PRIORITY)

        for cp in copies(sched_ref[S_EXPERT, t], slot):
            cp.wait()

        @pl.when(sched_ref[S_HAS_NEXT, t] == 1)
        def _():
            for cp in copies(sched_ref[S_NEXT, t], 1 - slot):
                cp.start(priority=WEIGHT_DMA_PRIORITY)

        for _, stage, dst in triples:
            _cast_weight(stage.at[slot], dst)


ROW_DMA_GROUPS = 4
GATHER_AHEAD = 2


def _tile_rows(sched_ref, tile, group):
    valid = sched_ref[S_VALID, tile]
    per = TM_EXP // ROW_DMA_GROUPS
    for r in range(group * per, (group + 1) * per):
        ok = r < valid
        yield r, jnp.where(ok, tile * TM_EXP + r, 0), ok


def _experts_up_kernel(sched_ref, inv_ref, hnp_hbm, wg_hbm, wu_hbm, hid_ref,
                       xbuf_ref, wg_stage, wu_stage, wgb_ref, wub_ref, wsems, gsem, *, n_tokens):
    t = pl.program_id(0)
    used = sched_ref[S_USED, 0]
    nbuf = GATHER_AHEAD + 1
    cur = t % nbuf
    ahead = (t + GATHER_AHEAD) % nbuf
    _load_expert_weights(t, sched_ref, [(wg_hbm, wg_stage, wgb_ref), (wu_hbm, wu_stage, wub_ref)], wsems)

    def start_rows(tile, buf, group):
        for r, slot, _ in _tile_rows(sched_ref, tile, group):
            token = inv_ref[slot] & (n_tokens - 1)
            _token_copy(hnp_hbm, token, xbuf_ref.at[buf], r, gsem.at[buf]).start()

    def wait_rows(buf):
        pltpu.make_async_copy(hnp_hbm.at[pl.ds(0, TM_EXP * TOKEN_ROWS)], xbuf_ref.at[buf], gsem.at[buf]).wait()

    for first in range(GATHER_AHEAD):
        @pl.when((t == 0) & (first < used))
        def _():
            for g in range(ROW_DMA_GROUPS):
                start_rows(first, first, g)

    def multiply(fetch_ahead):
        wait_rows(cur)
        c = D_MODEL // 2
        x_lo, x_hi = _unpack_halves(_load_token_rows(xbuf_ref.at[cur], TM_EXP))
        pieces = ((x_lo, wgb_ref, 0), (x_hi, wgb_ref, c), (x_lo, wub_ref, 0), (x_hi, wub_ref, c))
        acc = []
        for g, (x, w_ref, row0) in enumerate(pieces):
            if fetch_ahead:
                start_rows(t + GATHER_AHEAD, ahead, g)
            acc.append(jnp.dot(x, w_ref[row0:row0 + c, :], preferred_element_type=F32))
        hid_ref[...] = (_silu(acc[0] + acc[1]) * (acc[2] + acc[3])).astype(BF16)

    @pl.when(t + GATHER_AHEAD < used)
    def _():
        multiply(True)

    @pl.when((t + GATHER_AHEAD >= used) & (t < used))
    def _():
        multiply(False)

    @pl.when(t >= used)
    def _():
        hid_ref[...] = jnp.zeros_like(hid_ref)


def _experts_down_kernel(sched_ref, inv_ref, hid_ref, wd_hbm, y_hbm,
                         ybuf_ref, wd_stage, wdb_ref, wsems, ssem, *, n_tokens):
    t = pl.program_id(0)
    used = sched_ref[S_USED, 0]
    cur = t % 2
    _load_expert_weights(t, sched_ref, [(wd_hbm, wd_stage, wdb_ref)], wsems)

    def start_rows(tile, buf, group):
        for r, slot, ok in _tile_rows(sched_ref, tile, group):
            dst = jnp.where(ok, inv_ref[slot], TOP_K * n_tokens + r)
            _token_copy(ybuf_ref.at[buf], r, y_hbm, dst, ssem.at[buf]).start(priority=r % 2)

    def wait_rows(buf):
        pltpu.make_async_copy(ybuf_ref.at[buf], y_hbm.at[pl.ds(0, TM_EXP * TOKEN_ROWS)], ssem.at[buf]).wait()

    def multiply(send_previous):
        hid = hid_ref[...]
        q = D_MODEL // 4
        for half in range(2):
            if send_previous:
                start_rows(t - 1, 1 - cur, 2 * half)
            lo = jnp.dot(hid, wdb_ref[:, half * q:(half + 1) * q], preferred_element_type=F32)
            if send_previous:
                start_rows(t - 1, 1 - cur, 2 * half + 1)
            hi = jnp.dot(hid, wdb_ref[:, (half + 2) * q:(half + 3) * q], preferred_element_type=F32)
            _store_token_rows(ybuf_ref.at[cur], half * (q // LANES), _pack_halves(jnp.concatenate([lo, hi], axis=1)))

    @pl.when((t >= 2) & (t < used))
    def _():
        wait_rows(cur)

    @pl.when(t == 0)
    def _():
        ybuf_ref[1] = jnp.zeros(ybuf_ref.shape[1:], ybuf_ref.dtype)
        spare = pltpu.make_async_copy(ybuf_ref.at[1], y_hbm.at[pl.ds(TOP_K * n_tokens * TOKEN_ROWS, TM_EXP * TOKEN_ROWS)],
                                      ssem.at[1])
        spare.start()
        spare.wait()
        multiply(False)

    @pl.when((t > 0) & (t < used))
    def _():
        multiply(True)

    @pl.when(t == used - 1)
    def _():
        for g in range(ROW_DMA_GROUPS):
            start_rows(t, cur, g)

        @pl.when(t > 0)
        def _():
            wait_rows(1 - cur)

        wait_rows(cur)


def _tile_schedule(counts, n_tiles):
    tiles_per_expert = (counts + TM_EXP - 1) // TM_EXP
    tile_end = jnp.cumsum(tiles_per_expert)
    ids = jnp.arange(n_tiles, dtype=I32)
    raw = jnp.sum(ids[:, None] >= tile_end[None, :], axis=1)
    expert = jnp.minimum(raw, N_EXPERTS - 1).astype(I32)
    first = jnp.concatenate([jnp.ones((1,), I32), (expert[1:] != expert[:-1]).astype(I32)])
    slot = (jnp.cumsum(first) - 1) % 2
    start_idx = jnp.where(first == 1, ids, n_tiles)
    next_start = jnp.concatenate([lax.cummin(start_idx, reverse=True)[1:], jnp.full((1,), n_tiles, I32)])
    has_next = (next_start < n_tiles).astype(I32)
    next_expert = jnp.sum(jnp.where(ids[None, :] == next_start[:, None], expert[None, :], 0), axis=1)
    used = jnp.full((n_tiles,), tile_end[-1], I32)
    onehot = expert[:, None] == jnp.arange(N_EXPERTS)[None, :]
    tile_in_expert = ids - jnp.sum(jnp.where(onehot, (tile_end - tiles_per_expert)[None, :], 0), axis=1)
    rows_left = jnp.sum(jnp.where(onehot, counts[None, :], 0), axis=1) - tile_in_expert * TM_EXP
    valid = jnp.where(raw < N_EXPERTS, jnp.clip(rows_left, 0, TM_EXP), 0)
    return jnp.stack([expert, first, slot, next_expert, has_next, used, valid]).astype(I32)


def _experts(sched, inv, hnp, wg, wu, wd):
    n = hnp.shape[0] // TOKEN_ROWS
    n_slots = inv.shape[0]
    token_buf = lambda nbuf: pltpu.VMEM((nbuf, TM_EXP * TOKEN_ROWS, LANES), U32)
    row_spec = lambda width: pl.BlockSpec((TM_EXP, width), lambda t, sc, iv: (t, 0))
    hbm = pl.BlockSpec(memory_space=pl.ANY)
    big_vmem = pltpu.CompilerParams(dimension_semantics=("arbitrary",), vmem_limit_bytes=EXPERT_VMEM_LIMIT)
    hid = pl.pallas_call(
        functools.partial(_experts_up_kernel, n_tokens=n),
        grid_spec=pltpu.PrefetchScalarGridSpec(
            num_scalar_prefetch=2,
            grid=(n_slots // TM_EXP,),
            in_specs=[hbm, hbm, hbm],
            out_specs=row_spec(EXPERT_HIDDEN),
            scratch_shapes=[token_buf(GATHER_AHEAD + 1)]
                           + [pltpu.VMEM((2, D_MODEL, EXPERT_HIDDEN), F32)] * 2
                           + [pltpu.VMEM((D_MODEL, EXPERT_HIDDEN), BF16)] * 2
                           + [pltpu.SemaphoreType.DMA((2, 2)), pltpu.SemaphoreType.DMA((GATHER_AHEAD + 1,))],
        ),
        out_shape=jax.ShapeDtypeStruct((n_slots, EXPERT_HIDDEN), BF16),
        compiler_params=big_vmem,
        name="experts_up",
    )(sched, inv, hnp, wg, wu)
    return pl.pallas_call(
        functools.partial(_experts_down_kernel, n_tokens=n),
        grid_spec=pltpu.PrefetchScalarGridSpec(
            num_scalar_prefetch=2,
            grid=(n_slots // TM_EXP,),
            in_specs=[row_spec(EXPERT_HIDDEN), hbm],
            out_specs=hbm,
            scratch_shapes=[token_buf(2),
                            pltpu.VMEM((2, EXPERT_HIDDEN, D_MODEL), F32), pltpu.VMEM((EXPERT_HIDDEN, D_MODEL), BF16),
                            pltpu.SemaphoreType.DMA((1, 2)), pltpu.SemaphoreType.DMA((2,))],
        ),
        out_shape=jax.ShapeDtypeStruct(((TOP_K * n + TM_EXP) * TOKEN_ROWS, LANES), U32),
        compiler_params=big_vmem,
        name="experts_down",
    )(sched, inv, hid, wd)


def _combine_kernel(h_ref, meta_ref, y0_ref, y1_ref, o_ref):
    c = D_MODEL // 2
    meta = meta_ref[...].T
    w0, w1 = meta[:, 2:3], meta[:, 3:4]
    tm = h_ref.shape[0]
    lo0, hi0 = _unpack_halves(_load_token_rows(y0_ref, tm), F32)
    lo1, hi1 = _unpack_halves(_load_token_rows(y1_ref, tm), F32)
    o_ref[:, :c] = h_ref[:, :c] + (lo0 * w0 + lo1 * w1)
    o_ref[:, c:] = h_ref[:, c:] + (hi0 * w0 + hi1 * w1)


def _combine(h, meta, y):
    n = h.shape[0]
    nb = n // TM_COMB
    return pl.pallas_call(
        _combine_kernel,
        grid=(nb,),
        in_specs=[
            pl.BlockSpec((TM_COMB, D_MODEL), lambda i: (i, 0)),
            pl.BlockSpec((LANES, TM_COMB), lambda i: (0, i)),
            pl.BlockSpec((TM_COMB * TOKEN_ROWS, LANES), lambda i: (i, 0)),
            pl.BlockSpec((TM_COMB * TOKEN_ROWS, LANES), lambda i: (nb + i, 0)),
        ],
        out_specs=pl.BlockSpec((TM_COMB, D_MODEL), lambda i: (i, 0)),
        out_shape=jax.ShapeDtypeStruct((n, D_MODEL), F32),
        compiler_params=_cparams(("parallel",)),
        name="combine",
    )(h, meta, y, y)


def _lambda_init(layer_idx):
    return 0.8 - 0.6 * math.exp(-0.3 * layer_idx)


def _pad_lanes(v, width=LANES):
    return jnp.pad(v, ((0, 0), (0, width - v.shape[1])))


def _layer(l, x2, pos_rows, bsz, seq, ln1_w, w_in, conv_w, conv_b, dt_bias, a_log, d_skip, ssd_norm_w,
           q_norm_w, k_norm_w, lambda_q1, lambda_k1, lambda_q2, lambda_k2, subln_w, w_out, ln2_w,
           w_router_group, b_router_group, w_router_expert, b_router_expert, w_gate, w_up, w_down):
    n = x2.shape[0]
    c_z, c_xbc, c_dt = SSD_WIDTH, SSD_WIDTH + SSD_CONV_DIM, SSD_WIDTH + SSD_CONV_DIM + SSD_HEADS
    c_q, c_k = c_dt + ATTN_WIDTH, c_dt + 2 * ATTN_WIDTH
    w_main = jnp.concatenate([w_in[:, :c_z], w_in[:, c_dt:c_q], w_in[:, c_q:c_k], w_in[:, c_z:c_xbc]],
                             axis=1).astype(BF16)
    w_vt = w_in[:, c_k:].T.astype(BF16)
    w_dt = _pad_lanes(w_in[:, c_xbc:c_dt]).astype(BF16)

    u, v_t, dt_raw = _in_proj(x2, ln1_w[None, :], w_main, w_vt, w_dt)

    a_neg = _pad_lanes(-jnp.exp(a_log.astype(F32))[None, :])
    y_ssd = _ssd(u, dt_raw, conv_w, conv_b[None, :], _pad_lanes(dt_bias[None, :]), a_neg,
                 jnp.repeat(d_skip, SSD_HEAD_DIM)[None, :], ssd_norm_w[None, :], bsz, seq)

    inv_freq = jnp.power(ROPE_THETA, -jnp.arange(0, ROPE_DIM, 2, dtype=F32) / ROPE_DIM)
    invf_col = jnp.concatenate([inv_freq, jnp.zeros_like(inv_freq)])[:, None]
    seg_ones = (jnp.arange(LANES)[:, None] // ATTN_QK_DIM == jnp.arange(LANES)[None, :] // ATTN_QK_DIM).astype(BF16)
    q_scale = math.log2(math.e) / math.sqrt(ATTN_QK_DIM)
    qp, kp = _qk_prep(u, pos_rows, invf_col, jnp.tile(q_norm_w, 2)[None, :] * q_scale, jnp.tile(k_norm_w, 2)[None, :],
                      seg_ones, _rope_spread())

    lam_vecs = jnp.stack([lambda_q1, lambda_k1, lambda_q2, lambda_k2]).astype(F32)
    y_att = _attention(qp, kp, v_t, lam_vecs, subln_w[:, None], bsz, seq, _lambda_init(l))

    w_out_b = w_out.astype(BF16)
    w_router_t = _pad_lanes(jnp.concatenate([w_router_group, w_router_expert], axis=1)).T.astype(BF16)
    b_router = _pad_lanes(jnp.concatenate([b_router_group, b_router_expert])[None, :]).T
    h, hnp, meta = _out_proj(x2, y_ssd, y_att, w_out_b[:SSD_WIDTH], w_out_b[SSD_WIDTH:], ln2_w[None, :],
                             w_router_t, b_router)

    dest, cnt = _rank(meta)
    dest0, dest1 = dest[0], dest[1]
    counts = cnt[:, 0].astype(I32)
    n_slots = (n * TOP_K + N_EXPERTS * (TM_EXP - 1)) // TM_EXP * TM_EXP
    sched = _tile_schedule(counts, n_slots // TM_EXP)

    inv = _invert(dest0, dest1, sched[S_VALID])
    y = _experts(sched, inv, hnp, w_gate, w_up, w_down)
    return _combine(h, meta, y)


def kernel(x, positions, ln1_w, w_in, conv_w, conv_b, dt_bias, a_log, d_skip, ssd_norm_w, q_norm_w, k_norm_w,
           lambda_q1, lambda_k1, lambda_q2, lambda_k2, subln_w, w_out, ln2_w, w_router_group, b_router_group,
           w_router_expert, b_router_expert, w_gate, w_up, w_down):
    bsz, seq, d = x.shape
    assert d == D_MODEL and seq % TQ == 0 and (bsz * seq) % TM_IN == 0
    x2 = x.reshape(bsz * seq, d)
    pos_rows = jnp.broadcast_to(positions.astype(F32).reshape(1, bsz * seq), (8, bsz * seq))
    params = (ln1_w, w_in, conv_w, conv_b, dt_bias, a_log, d_skip, ssd_norm_w, q_norm_w, k_norm_w,
              lambda_q1, lambda_k1, lambda_q2, lambda_k2, subln_w, w_out, ln2_w, w_router_group, b_router_group,
              w_router_expert, b_router_expert, w_gate, w_up, w_down)
    for l in range(ln1_w.shape[0]):
        x2 = _layer(l, x2, pos_rows, bsz, seq, *[p[l] for p in params])
    return x2.reshape(bsz, seq, d)
```

```python
import functools
import math

import jax
import jax.numpy as jnp
from jax import lax
from jax.experimental import pallas as pl
from jax.experimental.pallas import tpu as pltpu

F32 = jnp.float32
BF16 = jnp.bfloat16
I32 = jnp.int32
U32 = jnp.uint32
HIGHEST = lax.Precision.HIGHEST

D_MODEL = 2048
SSD_WIDTH = 1024
ATTN_WIDTH = 1024
SSD_HEAD_DIM = 64
SSD_HEADS = 16
SSD_GROUPS = 2
SSD_HEADS_PER_GROUP = SSD_HEADS // SSD_GROUPS
SSD_STATE = 128
SSD_CONV = 4
SSD_CHUNK = 128
SSD_CONV_DIM = SSD_WIDTH + 2 * SSD_GROUPS * SSD_STATE
ATTN_V_DIM = 128
ATTN_HEADS = 8
ATTN_QK_DIM = 64
ROPE_THETA = 500000.0
ROPE_DIM = 16
N_EXPERT_GROUPS = 4
EXPERTS_PER_GROUP = 8
N_EXPERTS = 32
TOP_K = 2
EXPERT_HIDDEN = 1024
EPS = 1e-6

LANES = 128
NEG_INF = float("-inf")

TM_IN = 512
TN_IN = 1536
TM_QK = 512
TQ = 256
ATTN_HB = 8
ONES_ROWS = 16
TM_OUT = 512
TM_RANK = 512
TM_EXP = 256
TM_COMB = 256
ROUTER_ROWS = 40
U_COLS = SSD_WIDTH + 2 * ATTN_WIDTH + SSD_CONV_DIM
VMEM_LIMIT = 52 * 1024 * 1024
EXPERT_VMEM_LIMIT = 58 * 1024 * 1024


def _cparams(sem):
    return pltpu.CompilerParams(dimension_semantics=sem, vmem_limit_bytes=VMEM_LIMIT)


def _silu(x):
    return x * (1.0 / (1.0 + jnp.exp(-x)))


def _softplus(x):
    return jnp.maximum(x, 0.0) + jnp.log(1.0 + jnp.exp(-jnp.abs(x)))


def _inproj_kernel(x_ref, lnw_ref, w_ref, wvt_ref, wdt_ref, u_ref, vt_ref, dt_ref):
    x = x_ref[...]
    ms = jnp.mean(x * x, axis=-1, keepdims=True)
    xn = (x * lax.rsqrt(ms + EPS) * lnw_ref[...]).astype(BF16)
    for j in range(U_COLS // TN_IN):
        cols = slice(j * TN_IN, (j + 1) * TN_IN)
        u_ref[:, cols] = jnp.dot(xn, w_ref[:, cols], preferred_element_type=F32).astype(BF16)
    dt_ref[...] = jnp.dot(xn, wdt_ref[...], preferred_element_type=F32)
    vt_ref[...] = lax.dot_general(wvt_ref[...], xn, (((1,), (1,)), ((), ())),
                                  preferred_element_type=F32).astype(BF16)


def _in_proj(x2, ln_w, w_main, w_vt, w_dt):
    n = x2.shape[0]
    resident = lambda shape: pl.BlockSpec(shape, lambda i: (0, 0), pipeline_mode=pl.Buffered(1))
    return pl.pallas_call(
        _inproj_kernel,
        grid=(n // TM_IN,),
        in_specs=[
            pl.BlockSpec((TM_IN, D_MODEL), lambda i: (i, 0)),
            resident((1, D_MODEL)),
            resident((D_MODEL, U_COLS)),
            resident((ATTN_WIDTH, D_MODEL)),
            resident((D_MODEL, LANES)),
        ],
        out_specs=[
            pl.BlockSpec((TM_IN, U_COLS), lambda i: (i, 0)),
            pl.BlockSpec((ATTN_WIDTH, TM_IN), lambda i: (0, i)),
            pl.BlockSpec((TM_IN, LANES), lambda i: (i, 0)),
        ],
        out_shape=[
            jax.ShapeDtypeStruct((n, U_COLS), BF16),
            jax.ShapeDtypeStruct((ATTN_WIDTH, n), BF16),
            jax.ShapeDtypeStruct((n, LANES), F32),
        ],
        compiler_params=_cparams(("parallel",)),
        name="in_proj",
    )(x2, ln_w, w_main, w_vt, w_dt)


def _ssd_kernel(z_ref, xbc_ref, dt_ref, convw_ref, convb_ref, dtb_ref, aneg_ref, dskip_ref, normw_ref,
                y_ref, xp_ref, st_ref, yacc_ref):
    L = SSD_CHUNK
    P = SSD_HEAD_DIM

    @pl.when(pl.program_id(1) == 0)
    def _():
        xp_ref[0:8, :] = jnp.zeros((8, SSD_CONV_DIM), F32)
        st_ref[...] = jnp.zeros_like(st_ref)

    xp_ref[8:8 + L, :] = xbc_ref[...].astype(F32)
    acc = jnp.broadcast_to(convb_ref[...], (L, SSD_CONV_DIM))
    for k in range(SSD_CONV):
        acc = acc + xp_ref[5 + k:5 + k + L, :] * convw_ref[k:k + 1, :]
    xp_ref[0:8, :] = xp_ref[L:L + 8, :]
    xc = _silu(acc)

    dt = _softplus(dt_ref[...] + dtb_ref[...])
    a = dt * aneg_ref[...]
    row = lax.broadcasted_iota(I32, (L, L), 0)
    col = lax.broadcasted_iota(I32, (L, L), 1)
    causal = row >= col
    a_cs = jnp.dot(causal.astype(F32), a, precision=HIGHEST, preferred_element_type=F32)
    a_last = a_cs[L - 1:L, :]
    ea = jnp.exp(a_cs)
    dsdt = jnp.exp(a_last - a_cs) * dt
    cd = jnp.exp(a_last)
    a_cs_t = a_cs.T
    dt_t = dt.T
    dsdt_t = dsdt.T

    for g in range(SSD_GROUPS):
        b_g = xc[:, SSD_WIDTH + g * SSD_STATE:SSD_WIDTH + (g + 1) * SSD_STATE]
        c_off = SSD_WIDTH + SSD_GROUPS * SSD_STATE
        c_g = xc[:, c_off + g * SSD_STATE:c_off + (g + 1) * SSD_STATE]
        cb = lax.dot_general(c_g.astype(BF16), b_g.astype(BF16), (((1,), (1,)), ((), ())),
                             preferred_element_type=F32)
        b_gt = b_g.T
        for hh in range(SSD_HEADS_PER_GROUP):
            h = g * SSD_HEADS_PER_GROUP + hh
            xs_h = xc[:, h * P:(h + 1) * P].astype(BF16)
            seg = a_cs[:, h:h + 1] - a_cs_t[h:h + 1, :]
            dec = jnp.exp(jnp.where(causal, seg, NEG_INF))
            m = (cb * dec * dt_t[h:h + 1, :]).astype(BF16)
            c_s = (c_g * ea[:, h:h + 1]).astype(BF16)
            s_prev = st_ref[h]
            lhs = jnp.concatenate([m, c_s], axis=1)
            rhs = jnp.concatenate([xs_h, s_prev.astype(BF16)], axis=0)
            yacc_ref[:, h * P:(h + 1) * P] = jnp.dot(lhs, rhs, preferred_element_type=F32)
            bw = (b_gt * dsdt_t[h:h + 1, :]).astype(BF16)
            st_ref[h] = s_prev * cd[:, h:h + 1] + jnp.dot(bw, xs_h, preferred_element_type=F32)

    y = yacc_ref[...] + xc[:, :SSD_WIDTH] * dskip_ref[...]
    y = y * _silu(z_ref[...].astype(F32))
    gw = SSD_WIDTH // SSD_GROUPS
    for g in range(SSD_GROUPS):
        yg = y[:, g * gw:(g + 1) * gw]
        ms = jnp.mean(yg * yg, axis=-1, keepdims=True)
        y_ref[:, g * gw:(g + 1) * gw] = (yg * lax.rsqrt(ms + EPS) * normw_ref[:, g * gw:(g + 1) * gw]).astype(BF16)


def _ssd(u, dt_raw, conv_w, conv_b, dt_bias, a_neg, dskip_lanes, norm_w, bsz, seq):
    n = u.shape[0]
    nc = seq // SSD_CHUNK
    xbc_blk = (SSD_WIDTH + 2 * ATTN_WIDTH) // SSD_CONV_DIM
    full = lambda shape: pl.BlockSpec(shape, lambda b, c: (0, 0))
    return pl.pallas_call(
        _ssd_kernel,
        grid=(bsz, nc),
        in_specs=[
            pl.BlockSpec((SSD_CHUNK, SSD_WIDTH), lambda b, c: (b * nc + c, 0)),
            pl.BlockSpec((SSD_CHUNK, SSD_CONV_DIM), lambda b, c: (b * nc + c, xbc_blk)),
            pl.BlockSpec((SSD_CHUNK, LANES), lambda b, c: (b * nc + c, 0)),
            full((SSD_CONV, SSD_CONV_DIM)),
            full((1, SSD_CONV_DIM)),
            full((1, LANES)),
            full((1, LANES)),
            full((1, SSD_WIDTH)),
            full((1, SSD_WIDTH)),
        ],
        out_specs=pl.BlockSpec((SSD_CHUNK, SSD_WIDTH), lambda b, c: (b * nc + c, 0)),
        out_shape=jax.ShapeDtypeStruct((n, SSD_WIDTH), BF16),
        scratch_shapes=[
            pltpu.VMEM((SSD_CHUNK + 8, SSD_CONV_DIM), F32),
            pltpu.VMEM((SSD_HEADS, SSD_STATE, SSD_HEAD_DIM), F32),
            pltpu.VMEM((SSD_CHUNK, SSD_WIDTH), F32),
        ],
        compiler_params=_cparams(("parallel", "arbitrary")),
        name="ssd",
    )(u, u, dt_raw, conv_w, conv_b, dt_bias, a_neg, dskip_lanes, norm_w)


def _qkprep_kernel(q_ref, k_ref, pos_ref, invf_ref, qw_ref, kw_ref, ones_ref, expand_ref, qo_ref, ko_ref):
    tm = q_ref.shape[0]
    half = ROPE_DIM // 2
    ang_t = invf_ref[...] * pos_ref[0:1, :]
    trig = jnp.concatenate([jnp.cos(ang_t), jnp.sin(ang_t), jnp.zeros((LANES - 4 * half, tm), F32)], axis=0).T
    t1 = trig.astype(BF16)
    r1 = trig - t1.astype(F32)
    t2 = r1.astype(BF16)
    t3 = (r1 - t2.astype(F32)).astype(BF16)
    spread = expand_ref[...]
    tab = (jnp.dot(t1, spread, preferred_element_type=F32) + jnp.dot(t2, spread, preferred_element_type=F32)
           + jnp.dot(t3, spread, preferred_element_type=F32))
    cs = tab[:, :LANES]
    s_lo = tab[:, LANES:2 * LANES]
    s_hi = tab[:, 2 * LANES:]
    for src, w_ref, dst in ((q_ref, qw_ref, qo_ref), (k_ref, kw_ref, ko_ref)):
        for hb in range(ATTN_HEADS):
            x = src[:, hb * LANES:(hb + 1) * LANES].astype(F32)
            ss = jnp.dot((x * x).astype(BF16), ones_ref[...], preferred_element_type=F32)
            xn = x * lax.rsqrt(ss * (1.0 / ATTN_QK_DIM) + EPS) * w_ref[...]
            out = xn * cs + pltpu.roll(xn, LANES - half, 1) * s_lo + pltpu.roll(xn, half, 1) * s_hi
            dst[:, hb * LANES:(hb + 1) * LANES] = out.astype(BF16)


def _rope_spread():
    half = ROPE_DIM // 2
    lane = jnp.arange(LANES)
    d = lane % ATTN_QK_DIM
    src = jnp.arange(LANES)[:, None]
    cos_src = jnp.where(d < ROPE_DIM, d % half, half)
    cos_tab = (src == cos_src[None, :]).astype(F32)
    lo_tab = -((src == (2 * half + d)[None, :]) & (d < half)[None, :]).astype(F32)
    hi_tab = ((src == (2 * half + d - half)[None, :]) & ((d >= half) & (d < ROPE_DIM))[None, :]).astype(F32)
    return jnp.concatenate([cos_tab, lo_tab, hi_tab], axis=1).astype(BF16)


def _qk_prep(u, pos_rows, invf_col, qw_lanes, kw_lanes, seg_ones, spread):
    n = u.shape[0]
    full = lambda shape: pl.BlockSpec(shape, lambda i: (0, 0))
    return pl.pallas_call(
        _qkprep_kernel,
        grid=(n // TM_QK,),
        in_specs=[
            pl.BlockSpec((TM_QK, ATTN_WIDTH), lambda i: (i, 1)),
            pl.BlockSpec((TM_QK, ATTN_WIDTH), lambda i: (i, 2)),
            pl.BlockSpec((8, TM_QK), lambda i: (0, i)),
            full((ROPE_DIM, 1)), full((1, LANES)), full((1, LANES)), full((LANES, LANES)), full((LANES, 3 * LANES)),
        ],
        out_specs=[pl.BlockSpec((TM_QK, ATTN_WIDTH), lambda i: (i, 0))] * 2,
        out_shape=[jax.ShapeDtypeStruct((n, ATTN_WIDTH), BF16)] * 2,
        compiler_params=_cparams(("parallel",)),
        name="qk_prep",
    )(u, u, pos_rows, invf_col, qw_lanes, kw_lanes, seg_ones, spread)


def _attn_kernel(q_ref, k_ref, vt_ref, lamv_ref, subw_ref, o_ref, acc_ref, *, lam_init):
    qi = pl.program_id(2)
    lane = lax.broadcasted_iota(I32, (TQ, LANES), 1)
    qs = []
    for hb in range(ATTN_HB):
        q = q_ref[:, hb * LANES:(hb + 1) * LANES]
        zero = jnp.zeros_like(q)
        qs.append(jnp.concatenate([jnp.where(lane < ATTN_QK_DIM, q, zero),
                                   jnp.where(lane >= ATTN_QK_DIM, q, zero)], axis=0))
    acc_ref[...] = jnp.zeros_like(acc_ref)
    kv_idx = lax.broadcasted_iota(I32, (TQ, 2 * TQ), 0)
    q_idx = lax.broadcasted_iota(I32, (TQ, 2 * TQ), 1) & (TQ - 1)
    nt = (((1,), (1,)), ((), ()))

    def block(j, carry, masked):
        off = pl.multiple_of(j * TQ, TQ)
        ss = []
        for hb in range(ATTN_HB):
            kb = k_ref[pl.ds(off, TQ), hb * LANES:(hb + 1) * LANES]
            ss.append(lax.dot_general(kb, qs[hb], nt, preferred_element_type=F32))
        new, ps, alphas = [], [], []
        for hb in range(ATTN_HB):
            m_old = carry[hb]
            s = ss[hb]
            if masked:
                s = jnp.where(kv_idx <= q_idx, s, NEG_INF)
            m_new = jnp.maximum(m_old, jnp.max(s, axis=0, keepdims=True))
            alphas.append(jnp.exp2(m_old - m_new))
            ps.append(jnp.exp2(s - m_new).astype(BF16))
            new.append(m_new)
        pvs = []
        for hb in range(ATTN_HB):
            vb = jnp.concatenate([vt_ref[hb * LANES:(hb + 1) * LANES, pl.ds(off, TQ)], ones_rows], axis=0)
            pvs.append(jnp.dot(vb, ps[hb], preferred_element_type=F32))
        for hb in range(ATTN_HB):
            acc_ref[hb] = alphas[hb] * acc_ref[hb] + pvs[hb]
        return tuple(new)

    ones_rows = jnp.ones((ONES_ROWS, TQ), BF16)
    init = (jnp.full((1, 2 * TQ), NEG_INF, F32),) * ATTN_HB
    carry = lax.fori_loop(0, qi, lambda j, cr: block(j, cr, False), init)
    block(qi, carry, True)

    lv = lamv_ref[...]
    lam = (jnp.exp(jnp.sum(lv[0:1] * lv[1:2], axis=1, keepdims=True))
           - jnp.exp(jnp.sum(lv[2:3] * lv[3:4], axis=1, keepdims=True)) + lam_init)
    for hb in range(ATTN_HB):
        acc = acc_ref[hb]
        o2 = acc[:ATTN_V_DIM] * (1.0 / acc[ATTN_V_DIM:ATTN_V_DIM + 1])
        o_t = o2[:, :TQ] - lam * o2[:, TQ:]
        ms = jnp.mean(o_t * o_t, axis=0, keepdims=True)
        o_t = o_t * lax.rsqrt(ms + EPS) * subw_ref[...] * (1.0 - lam_init)
        o_ref[:, hb * LANES:(hb + 1) * LANES] = o_t.T.astype(BF16)


def _attention(qp, kp, v_t, lam_vecs, subw_col, bsz, seq, lam_init):
    n = qp.shape[0]
    nq = seq // TQ
    w = ATTN_HB * ATTN_V_DIM
    return pl.pallas_call(
        functools.partial(_attn_kernel, lam_init=lam_init),
        grid=(bsz, ATTN_HEADS // ATTN_HB, nq),
        in_specs=[
            pl.BlockSpec((TQ, w), lambda b, h, i: (b * nq + i, h)),
            pl.BlockSpec((seq, w), lambda b, h, i: (b, h)),
            pl.BlockSpec((w, seq), lambda b, h, i: (h, b)),
            pl.BlockSpec((4, ATTN_QK_DIM), lambda b, h, i: (0, 0)),
            pl.BlockSpec((ATTN_V_DIM, 1), lambda b, h, i: (0, 0)),
        ],
        out_specs=pl.BlockSpec((TQ, w), lambda b, h, i: (b * nq + i, h)),
        out_shape=jax.ShapeDtypeStruct((n, ATTN_WIDTH), BF16),
        scratch_shapes=[pltpu.VMEM((ATTN_HB, ATTN_V_DIM + ONES_ROWS, 2 * TQ), F32)],
        compiler_params=_cparams(("parallel", "parallel", "arbitrary")),
        name="attn",
    )(qp, kp, v_t, lam_vecs, subw_col)


def _pack_halves(x):
    c = x.shape[1] // 2
    lo = pltpu.bitcast(x[:, :c].astype(BF16).astype(F32), U32) >> 16
    hi = pltpu.bitcast(x[:, c:].astype(BF16).astype(F32), U32) & jnp.uint32(0xFFFF0000)
    return hi | lo


TOKEN_ROWS = D_MODEL // 2 // LANES


def _store_token_rows(ref, first_piece, packed):
    tm = packed.shape[0]
    for j in range(packed.shape[1] // LANES):
        ref[pl.ds(first_piece + j, tm, stride=TOKEN_ROWS), :] = packed[:, j * LANES:(j + 1) * LANES]


def _load_token_rows(ref, tm):
    return jnp.concatenate([ref[pl.ds(s, tm, stride=TOKEN_ROWS), :] for s in range(TOKEN_ROWS)], axis=1)


def _unpack_halves(w, dtype=BF16):
    lo = pltpu.bitcast(w << 16, F32).astype(dtype)
    hi = pltpu.bitcast(w & jnp.uint32(0xFFFF0000), F32).astype(dtype)
    return lo, hi


def _outproj_kernel(x_ref, ys_ref, ya_ref, wos_ref, woa_ref, ln2_ref, wr_ref, br_ref,
                    h_ref, hnp_ref, meta_ref):
    tm = x_ref.shape[0]
    h = (x_ref[...]
         + jnp.dot(ys_ref[...], wos_ref[...], preferred_element_type=F32)
         + jnp.dot(ya_ref[...], woa_ref[...], preferred_element_type=F32))
    h_ref[...] = h
    ms = jnp.mean(h * h, axis=-1, keepdims=True)
    hn = h * lax.rsqrt(ms + EPS) * ln2_ref[...]
    _store_token_rows(hnp_ref, 0, _pack_halves(hn))

    lg_t = lax.dot_general(wr_ref[...], hn.astype(BF16), (((1,), (1,)), ((), ())),
                           preferred_element_type=F32)
    lg = lg_t[0:ROUTER_ROWS, :] + br_ref[0:ROUTER_ROWS, :]
    row = lax.broadcasted_iota(I32, (ROUTER_ROWS, tm), 0).astype(F32)
    big = float(LANES)
    gl = jnp.where(row < N_EXPERT_GROUPS, lg, NEG_INF)
    gmax = jnp.max(gl, axis=0, keepdims=True)
    gsel = jnp.min(jnp.where(gl == gmax, row, big), axis=0, keepdims=True)
    g_w = 1.0 / jnp.sum(jnp.exp(gl - gmax), axis=0, keepdims=True)
    eid = row - N_EXPERT_GROUPS
    lo = gsel * EXPERTS_PER_GROUP
    emask = (eid >= lo) & (eid < lo + EXPERTS_PER_GROUP)
    el = jnp.where(emask, lg, NEG_INF)
    m1 = jnp.max(el, axis=0, keepdims=True)
    i1 = jnp.min(jnp.where(el == m1, eid, big), axis=0, keepdims=True)
    el2 = jnp.where(eid == i1, NEG_INF, el)
    m2 = jnp.max(el2, axis=0, keepdims=True)
    i2 = jnp.min(jnp.where(el2 == m2, eid, big), axis=0, keepdims=True)
    e2 = jnp.exp(m2 - m1)
    w1 = g_w / (1.0 + e2)
    w2 = g_w * e2 / (1.0 + e2)
    mrow = lax.broadcasted_iota(I32, (LANES, tm), 0)
    meta_ref[...] = jnp.where(mrow == 0, i1, jnp.where(mrow == 1, i2, jnp.where(mrow == 2, w1, jnp.where(mrow == 3, w2, 0.0))))


def _out_proj(x2, y_ssd, y_att, wo_s, wo_a, ln2_w, w_router_t, b_router):
    n = x2.shape[0]
    full = lambda shape: pl.BlockSpec(shape, lambda i: (0, 0), pipeline_mode=pl.Buffered(1))
    return pl.pallas_call(
        _outproj_kernel,
        grid=(n // TM_OUT,),
        in_specs=[
            pl.BlockSpec((TM_OUT, D_MODEL), lambda i: (i, 0)),
            pl.BlockSpec((TM_OUT, SSD_WIDTH), lambda i: (i, 0)),
            pl.BlockSpec((TM_OUT, ATTN_WIDTH), lambda i: (i, 0)),
            full((SSD_WIDTH, D_MODEL)), full((ATTN_WIDTH, D_MODEL)),
            full((1, D_MODEL)), full((LANES, D_MODEL)), full((LANES, 1)),
        ],
        out_specs=[
            pl.BlockSpec((TM_OUT, D_MODEL), lambda i: (i, 0)),
            pl.BlockSpec((TM_OUT * TOKEN_ROWS, LANES), lambda i: (i, 0)),
            pl.BlockSpec((LANES, TM_OUT), lambda i: (0, i)),
        ],
        out_shape=[
            jax.ShapeDtypeStruct((n, D_MODEL), F32),
            jax.ShapeDtypeStruct((n * TOKEN_ROWS, LANES), U32),
            jax.ShapeDtypeStruct((LANES, n), F32),
        ],
        compiler_params=_cparams(("parallel",)),
        name="out_proj",
    )(x2, y_ssd, y_att, wo_s, wo_a, ln2_w, w_router_t, b_router)


def _rank_kernel(meta_ref, dest_ref, cnt_ref, run_ref, offs_ref):
    p = pl.program_id(0)
    i = pl.program_id(1)
    tm = meta_ref.shape[1]
    meta = meta_ref[...]
    row = lax.broadcasted_iota(I32, (N_EXPERTS, tm), 0).astype(F32)
    oh0 = (row == meta[0:1, :]).astype(F32)
    oh1 = (row == meta[1:2, :]).astype(F32)
    oh = oh0 + oh1
    rowsum = jnp.sum(oh, axis=1, keepdims=True)

    @pl.when((p == 0) & (i == 0))
    def _():
        run_ref[...] = jnp.zeros_like(run_ref)

    @pl.when(p == 0)
    def _():
        run_ref[...] = run_ref[...] + rowsum
        dest_ref[...] = jnp.zeros_like(dest_ref)
        cnt_ref[...] = run_ref[...]

    @pl.when((p == 1) & (i == 0))
    def _():
        cnt = run_ref[...]
        padded = jnp.ceil(cnt * (1.0 / TM_EXP)) * TM_EXP
        r = lax.broadcasted_iota(I32, (N_EXPERTS, N_EXPERTS), 0)
        c = lax.broadcasted_iota(I32, (N_EXPERTS, N_EXPERTS), 1)
        offs_ref[...] = jnp.dot((c < r).astype(F32), padded, precision=HIGHEST, preferred_element_type=F32)
        cnt_ref[...] = cnt
        run_ref[...] = jnp.zeros_like(run_ref)

    @pl.when(p == 1)
    def _():
        r = lax.broadcasted_iota(I32, (tm, tm), 0)
        c = lax.broadcasted_iota(I32, (tm, tm), 1)
        before = jnp.dot(oh.astype(BF16), (r < c).astype(BF16), preferred_element_type=F32)
        base = before + jnp.tile(run_ref[...] + offs_ref[...], (1, tm // LANES))
        d0 = jnp.sum(oh0 * base, axis=0, keepdims=True)
        d1 = jnp.sum(oh1 * base, axis=0, keepdims=True)
        drow = lax.broadcasted_iota(I32, (8, tm), 0)
        dest_ref[...] = jnp.where(drow == 0, d0, jnp.where(drow == 1, d1, 0.0)).astype(I32)
        run_ref[...] = run_ref[...] + rowsum


def _rank(meta_t):
    n = meta_t.shape[1]
    per_expert = lambda: pl.BlockSpec((N_EXPERTS, LANES), lambda p, i: (0, 0))
    return pl.pallas_call(
        _rank_kernel,
        grid=(2, n // TM_RANK),
        in_specs=[pl.BlockSpec((8, TM_RANK), lambda p, i: (0, i))],
        out_specs=[pl.BlockSpec((8, TM_RANK), lambda p, i: (0, i * p)), per_expert()],
        out_shape=[jax.ShapeDtypeStruct((8, n), I32), jax.ShapeDtypeStruct((N_EXPERTS, LANES), F32)],
        scratch_shapes=[pltpu.VMEM((N_EXPERTS, LANES), F32), pltpu.VMEM((N_EXPERTS, LANES), F32)],
        compiler_params=_cparams(("arbitrary", "arbitrary")),
        name="rank",
    )(meta_t)


def _token_copy(src_ref, src_token, dst_ref, dst_token, sem):
    src = src_ref.at[pl.ds(pl.multiple_of(src_token * TOKEN_ROWS, TOKEN_ROWS), TOKEN_ROWS)]
    dst = dst_ref.at[pl.ds(pl.multiple_of(dst_token * TOKEN_ROWS, TOKEN_ROWS), TOKEN_ROWS)]
    return pltpu.make_async_copy(src, dst, sem)


def _invert_kernel(d0_ref, d1_ref, valid_ref, inv_ref):
    n = d0_ref.shape[0]

    def clear_tile(tile, _):
        def clear(r, _):
            inv_ref[tile * TM_EXP + r] = 0
            return 0

        lax.fori_loop(valid_ref[tile], TM_EXP, clear, 0)
        return 0

    lax.fori_loop(0, valid_ref.shape[0], clear_tile, 0)

    def put(t, _):
        inv_ref[d0_ref[t]] = t
        inv_ref[d1_ref[t]] = n + t
        return 0

    lax.fori_loop(0, n, put, 0, unroll=8)


def _invert(dest0, dest1, tile_valid):
    smem = pl.BlockSpec(memory_space=pltpu.SMEM)
    return pl.pallas_call(
        _invert_kernel,
        in_specs=[smem, smem, smem],
        out_specs=smem,
        out_shape=jax.ShapeDtypeStruct((tile_valid.shape[0] * TM_EXP,), I32),
        name="invert",
    )(dest0, dest1, tile_valid)


CAST_ROWS = 256


def _cast_weight(src_ref, dst_ref):
    def body(i, _):
        rows = pl.ds(pl.multiple_of(i * CAST_ROWS, CAST_ROWS), CAST_ROWS)
        dst_ref[rows, :] = src_ref[rows, :].astype(BF16)
        return 0

    lax.fori_loop(0, src_ref.shape[0] // CAST_ROWS, body, 0)


S_EXPERT, S_FIRST, S_SLOT, S_NEXT, S_HAS_NEXT, S_USED, S_VALID = range(7)
WEIGHT_DMA_PRIORITY = 1


def _load_expert_weights(t, sched_ref, triples, sems):
    def copies(expert, slot):
        return [pltpu.make_async_copy(w.at[expert], stage.at[slot], sems.at[i, slot])
                for i, (w, stage, _) in enumerate(triples)]

    @pl.when(sched_ref[S_FIRST, t] == 1)
    def _():
        slot = sched_ref[S_SLOT, t]

        @pl.when(t == 0)
        def _():
            for cp in copies(sched_ref[S_EXPERT, t], slot):
                cp.start(priority=WEIGHT_DMA_PRIORITY)

        for cp in copies(sched_ref[S_EXPERT, t], slot):
            cp.wait()

        @pl.when(sched_ref[S_HAS_NEXT, t] == 1)
        def _():
            for cp in copies(sched_ref[S_NEXT, t], 1 - slot):
                cp.start(priority=WEIGHT_DMA_PRIORITY)

        for _, stage, dst in triples:
            _cast_weight(stage.at[slot], dst)


ROW_DMA_GROUPS = 4
GATHER_AHEAD = 2


def _tile_rows(sched_ref, tile, group):
    valid = sched_ref[S_VALID, tile]
    per = TM_EXP // ROW_DMA_GROUPS
    for r in range(group * per, (group + 1) * per):
        ok = r < valid
        yield r, jnp.where(ok, tile * TM_EXP + r, 0), ok


def _experts_up_kernel(sched_ref, inv_ref, hnp_hbm, wg_hbm, wu_hbm, hid_ref,
                       xbuf_ref, wg_stage, wu_stage, wgb_ref, wub_ref, wsems, gsem, *, n_tokens):
    t = pl.program_id(0)
    used = sched_ref[S_USED, 0]
    nbuf = GATHER_AHEAD + 1
    cur = t % nbuf
    ahead = (t + GATHER_AHEAD) % nbuf
    _load_expert_weights(t, sched_ref, [(wg_hbm, wg_stage, wgb_ref), (wu_hbm, wu_stage, wub_ref)], wsems)

    def start_rows(tile, buf, group):
        for r, slot, _ in _tile_rows(sched_ref, tile, group):
            token = inv_ref[slot] & (n_tokens - 1)
            _token_copy(hnp_hbm, token, xbuf_ref.at[buf], r, gsem.at[buf]).start()

    def wait_rows(buf):
        pltpu.make_async_copy(hnp_hbm.at[pl.ds(0, TM_EXP * TOKEN_ROWS)], xbuf_ref.at[buf], gsem.at[buf]).wait()

    for first in range(GATHER_AHEAD):
        @pl.when((t == 0) & (first < used))
        def _():
            for g in range(ROW_DMA_GROUPS):
                start_rows(first, first, g)

    def multiply(fetch_ahead):
        wait_rows(cur)
        c = D_MODEL // 2
        x_lo, x_hi = _unpack_halves(_load_token_rows(xbuf_ref.at[cur], TM_EXP))
        pieces = ((x_lo, wgb_ref, 0), (x_hi, wgb_ref, c), (x_lo, wub_ref, 0), (x_hi, wub_ref, c))
        acc = []
        for g, (x, w_ref, row0) in enumerate(pieces):
            if fetch_ahead:
                start_rows(t + GATHER_AHEAD, ahead, g)
            acc.append(jnp.dot(x, w_ref[row0:row0 + c, :], preferred_element_type=F32))
        hid_ref[...] = (_silu(acc[0] + acc[1]) * (acc[2] + acc[3])).astype(BF16)

    @pl.when(t + GATHER_AHEAD < used)
    def _():
        multiply(True)

    @pl.when((t + GATHER_AHEAD >= used) & (t < used))
    def _():
        multiply(False)

    @pl.when(t >= used)
    def _():
        hid_ref[...] = jnp.zeros_like(hid_ref)


def _experts_down_kernel(sched_ref, inv_ref, hid_ref, wd_hbm, y_hbm,
                         ybuf_ref, wd_stage, wdb_ref, wsems, ssem, *, n_tokens):
    t = pl.program_id(0)
    used = sched_ref[S_USED, 0]
    cur = t % 2
    _load_expert_weights(t, sched_ref, [(wd_hbm, wd_stage, wdb_ref)], wsems)

    def start_rows(tile, buf, group):
        for r, slot, ok in _tile_rows(sched_ref, tile, group):
            dst = jnp.where(ok, inv_ref[slot], TOP_K * n_tokens + r)
            _token_copy(ybuf_ref.at[buf], r, y_hbm, dst, ssem.at[buf]).start(priority=r % 2)

    def wait_rows(buf):
        pltpu.make_async_copy(ybuf_ref.at[buf], y_hbm.at[pl.ds(0, TM_EXP * TOKEN_ROWS)], ssem.at[buf]).wait()

    def multiply(send_previous):
        hid = hid_ref[...]
        q = D_MODEL // 4
        for half in range(2):
            if send_previous:
                start_rows(t - 1, 1 - cur, 2 * half)
            lo = jnp.dot(hid, wdb_ref[:, half * q:(half + 1) * q], preferred_element_type=F32)
            if send_previous:
                start_rows(t - 1, 1 - cur, 2 * half + 1)
            hi = jnp.dot(hid, wdb_ref[:, (half + 2) * q:(half + 3) * q], preferred_element_type=F32)
            _store_token_rows(ybuf_ref.at[cur], half * (q // LANES), _pack_halves(jnp.concatenate([lo, hi], axis=1)))

    @pl.when((t >= 2) & (t < used))
    def _():
        wait_rows(cur)

    @pl.when(t == 0)
    def _():
        ybuf_ref[1] = jnp.zeros(ybuf_ref.shape[1:], ybuf_ref.dtype)
        spare = pltpu.make_async_copy(ybuf_ref.at[1], y_hbm.at[pl.ds(TOP_K * n_tokens * TOKEN_ROWS, TM_EXP * TOKEN_ROWS)],
                                      ssem.at[1])
        spare.start()
        spare.wait()
        multiply(False)

    @pl.when((t > 0) & (t < used))
    def _():
        multiply(True)

    @pl.when(t == used - 1)
    def _():
        for g in range(ROW_DMA_GROUPS):
            start_rows(t, cur, g)

        @pl.when(t > 0)
        def _():
            wait_rows(1 - cur)

        wait_rows(cur)


def _tile_schedule(counts, n_tiles):
    tiles_per_expert = (counts + TM_EXP - 1) // TM_EXP
    tile_end = jnp.cumsum(tiles_per_expert)
    ids = jnp.arange(n_tiles, dtype=I32)
    raw = jnp.sum(ids[:, None] >= tile_end[None, :], axis=1)
    expert = jnp.minimum(raw, N_EXPERTS - 1).astype(I32)
    first = jnp.concatenate([jnp.ones((1,), I32), (expert[1:] != expert[:-1]).astype(I32)])
    slot = (jnp.cumsum(first) - 1) % 2
    start_idx = jnp.where(first == 1, ids, n_tiles)
    next_start = jnp.concatenate([lax.cummin(start_idx, reverse=True)[1:], jnp.full((1,), n_tiles, I32)])
    has_next = (next_start < n_tiles).astype(I32)
    next_expert = jnp.sum(jnp.where(ids[None, :] == next_start[:, None], expert[None, :], 0), axis=1)
    used = jnp.full((n_tiles,), tile_end[-1], I32)
    onehot = expert[:, None] == jnp.arange(N_EXPERTS)[None, :]
    tile_in_expert = ids - jnp.sum(jnp.where(onehot, (tile_end - tiles_per_expert)[None, :], 0), axis=1)
    rows_left = jnp.sum(jnp.where(onehot, counts[None, :], 0), axis=1) - tile_in_expert * TM_EXP
    valid = jnp.where(raw < N_EXPERTS, jnp.clip(rows_left, 0, TM_EXP), 0)
    return jnp.stack([expert, first, slot, next_expert, has_next, used, valid]).astype(I32)


def _experts(sched, inv, hnp, wg, wu, wd):
    n = hnp.shape[0] // TOKEN_ROWS
    n_slots = inv.shape[0]
    token_buf = lambda nbuf: pltpu.VMEM((nbuf, TM_EXP * TOKEN_ROWS, LANES), U32)
    row_spec = lambda width: pl.BlockSpec((TM_EXP, width), lambda t, sc, iv: (t, 0))
    hbm = pl.BlockSpec(memory_space=pl.ANY)
    big_vmem = pltpu.CompilerParams(dimension_semantics=("arbitrary",), vmem_limit_bytes=EXPERT_VMEM_LIMIT)
    hid = pl.pallas_call(
        functools.partial(_experts_up_kernel, n_tokens=n),
        grid_spec=pltpu.PrefetchScalarGridSpec(
            num_scalar_prefetch=2,
            grid=(n_slots // TM_EXP,),
            in_specs=[hbm, hbm, hbm],
            out_specs=row_spec(EXPERT_HIDDEN),
            scratch_shapes=[token_buf(GATHER_AHEAD + 1)]
                           + [pltpu.VMEM((2, D_MODEL, EXPERT_HIDDEN), F32)] * 2
                           + [pltpu.VMEM((D_MODEL, EXPERT_HIDDEN), BF16)] * 2
                           + [pltpu.SemaphoreType.DMA((2, 2)), pltpu.SemaphoreType.DMA((GATHER_AHEAD + 1,))],
        ),
        out_shape=jax.ShapeDtypeStruct((n_slots, EXPERT_HIDDEN), BF16),
        compiler_params=big_vmem,
        name="experts_up",
    )(sched, inv, hnp, wg, wu)
    return pl.pallas_call(
        functools.partial(_experts_down_kernel, n_tokens=n),
        grid_spec=pltpu.PrefetchScalarGridSpec(
            num_scalar_prefetch=2,
            grid=(n_slots // TM_EXP,),
            in_specs=[row_spec(EXPERT_HIDDEN), hbm],
            out_specs=hbm,
            scratch_shapes=[token_buf(2),
                            pltpu.VMEM((2, EXPERT_HIDDEN, D_MODEL), F32), pltpu.VMEM((EXPERT_HIDDEN, D_MODEL), BF16),
                            pltpu.SemaphoreType.DMA((1, 2)), pltpu.SemaphoreType.DMA((2,))],
        ),
        out_shape=jax.ShapeDtypeStruct(((TOP_K * n + TM_EXP) * TOKEN_ROWS, LANES), U32),
        compiler_params=big_vmem,
        name="experts_down",
    )(sched, inv, hid, wd)


def _combine_kernel(h_ref, meta_ref, y0_ref, y1_ref, o_ref):
    c = D_MODEL // 2
    meta = meta_ref[...].T
    w0, w1 = meta[:, 2:3], meta[:, 3:4]
    tm = h_ref.shape[0]
    lo0, hi0 = _unpack_halves(_load_token_rows(y0_ref, tm), F32)
    lo1, hi1 = _unpack_halves(_load_token_rows(y1_ref, tm), F32)
    o_ref[:, :c] = h_ref[:, :c] + (lo0 * w0 + lo1 * w1)
    o_ref[:, c:] = h_ref[:, c:] + (hi0 * w0 + hi1 * w1)


def _combine(h, meta, y):
    n = h.shape[0]
    nb = n // TM_COMB
    return pl.pallas_call(
        _combine_kernel,
        grid=(nb,),
        in_specs=[
            pl.BlockSpec((TM_COMB, D_MODEL), lambda i: (i, 0)),
            pl.BlockSpec((LANES, TM_COMB), lambda i: (0, i)),
            pl.BlockSpec((TM_COMB * TOKEN_ROWS, LANES), lambda i: (i, 0)),
            pl.BlockSpec((TM_COMB * TOKEN_ROWS, LANES), lambda i: (nb + i, 0)),
        ],
        out_specs=pl.BlockSpec((TM_COMB, D_MODEL), lambda i: (i, 0)),
        out_shape=jax.ShapeDtypeStruct((n, D_MODEL), F32),
        compiler_params=_cparams(("parallel",)),
        name="combine",
    )(h, meta, y, y)


def _lambda_init(layer_idx):
    return 0.8 - 0.6 * math.exp(-0.3 * layer_idx)


def _pad_lanes(v, width=LANES):
    return jnp.pad(v, ((0, 0), (0, width - v.shape[1])))


def _layer(l, x2, pos_rows, bsz, seq, ln1_w, w_in, conv_w, conv_b, dt_bias, a_log, d_skip, ssd_norm_w,
           q_norm_w, k_norm_w, lambda_q1, lambda_k1, lambda_q2, lambda_k2, subln_w, w_out, ln2_w,
           w_router_group, b_router_group, w_router_expert, b_router_expert, w_gate, w_up, w_down):
    n = x2.shape[0]
    c_z, c_xbc, c_dt = SSD_WIDTH, SSD_WIDTH + SSD_CONV_DIM, SSD_WIDTH + SSD_CONV_DIM + SSD_HEADS
    c_q, c_k = c_dt + ATTN_WIDTH, c_dt + 2 * ATTN_WIDTH
    w_main = jnp.concatenate([w_in[:, :c_z], w_in[:, c_dt:c_q], w_in[:, c_q:c_k], w_in[:, c_z:c_xbc]],
                             axis=1).astype(BF16)
    w_vt = w_in[:, c_k:].T.astype(BF16)
    w_dt = _pad_lanes(w_in[:, c_xbc:c_dt]).astype(BF16)

    u, v_t, dt_raw = _in_proj(x2, ln1_w[None, :], w_main, w_vt, w_dt)

    a_neg = _pad_lanes(-jnp.exp(a_log.astype(F32))[None, :])
    y_ssd = _ssd(u, dt_raw, conv_w, conv_b[None, :], _pad_lanes(dt_bias[None, :]), a_neg,
                 jnp.repeat(d_skip, SSD_HEAD_DIM)[None, :], ssd_norm_w[None, :], bsz, seq)

    inv_freq = jnp.power(ROPE_THETA, -jnp.arange(0, ROPE_DIM, 2, dtype=F32) / ROPE_DIM)
    invf_col = jnp.concatenate([inv_freq, jnp.zeros_like(inv_freq)])[:, None]
    seg_ones = (jnp.arange(LANES)[:, None] // ATTN_QK_DIM == jnp.arange(LANES)[None, :] // ATTN_QK_DIM).astype(BF16)
    q_scale = math.log2(math.e) / math.sqrt(ATTN_QK_DIM)
    qp, kp = _qk_prep(u, pos_rows, invf_col, jnp.tile(q_norm_w, 2)[None, :] * q_scale, jnp.tile(k_norm_w, 2)[None, :],
                      seg_ones, _rope_spread())

    lam_vecs = jnp.stack([lambda_q1, lambda_k1, lambda_q2, lambda_k2]).astype(F32)
    y_att = _attention(qp, kp, v_t, lam_vecs, subln_w[:, None], bsz, seq, _lambda_init(l))

    w_out_b = w_out.astype(BF16)
    w_router_t = _pad_lanes(jnp.concatenate([w_router_group, w_router_expert], axis=1)).T.astype(BF16)
    b_router = _pad_lanes(jnp.concatenate([b_router_group, b_router_expert])[None, :]).T
    h, hnp, meta = _out_proj(x2, y_ssd, y_att, w_out_b[:SSD_WIDTH], w_out_b[SSD_WIDTH:], ln2_w[None, :],
                             w_router_t, b_router)

    dest, cnt = _rank(meta)
    dest0, dest1 = dest[0], dest[1]
    counts = cnt[:, 0].astype(I32)
    n_slots = (n * TOP_K + N_EXPERTS * (TM_EXP - 1)) // TM_EXP * TM_EXP
    sched = _tile_schedule(counts, n_slots // TM_EXP)

    inv = _invert(dest0, dest1, sched[S_VALID])
    y = _experts(sched, inv, hnp, w_gate, w_up, w_down)
    return _combine(h, meta, y)


def kernel(x, positions, ln1_w, w_in, conv_w, conv_b, dt_bias, a_log, d_skip, ssd_norm_w, q_norm_w, k_norm_w,
           lambda_q1, lambda_k1, lambda_q2, lambda_k2, subln_w, w_out, ln2_w, w_router_group, b_router_group,
           w_router_expert, b_router_expert, w_gate, w_up, w_down):
    bsz, seq, d = x.shape
    assert d == D_MODEL and seq % TQ == 0 and (bsz * seq) % TM_IN == 0
    x2 = x.reshape(bsz * seq, d)
    pos_rows = jnp.broadcast_to(positions.astype(F32).reshape(1, bsz * seq), (8, bsz * seq))
    params = (ln1_w, w_in, conv_w, conv_b, dt_bias, a_log, d_skip, ssd_norm_w, q_norm_w, k_norm_w,
              lambda_q1, lambda_k1, lambda_q2, lambda_k2, subln_w, w_out, ln2_w, w_router_group, b_router_group,
              w_router_expert, b_router_expert, w_gate, w_up, w_down)
    for l in range(ln1_w.shape[0]):
        x2 = _layer(l, x2, pos_rows, bsz, seq, *[p[l] for p in params])
    return x2.reshape(bsz, seq, d)
```

```python
import functools
import math

import jax
import jax.numpy as jnp
from jax import lax
from jax.experimental import pallas as pl
from jax.experimental.pallas import tpu as pltpu

F32 = jnp.float32
BF16 = jnp.bfloat16
I32 = jnp.int32
U32 = jnp.uint32
HIGHEST = lax.Precision.HIGHEST

D_MODEL = 2048
SSD_WIDTH = 1024
ATTN_WIDTH = 1024
SSD_HEAD_DIM = 64
SSD_HEADS = 16
SSD_GROUPS = 2
SSD_HEADS_PER_GROUP = SSD_HEADS // SSD_GROUPS
SSD_STATE = 128
SSD_CONV = 4
SSD_CHUNK = 128
SSD_CONV_DIM = SSD_WIDTH + 2 * SSD_GROUPS * SSD_STATE
ATTN_V_DIM = 128
ATTN_HEADS = 8
ATTN_QK_DIM = 64
ROPE_THETA = 500000.0
ROPE_DIM = 16
N_EXPERT_GROUPS = 4
EXPERTS_PER_GROUP = 8
N_EXPERTS = 32
TOP_K = 2
EXPERT_HIDDEN = 1024
EPS = 1e-6

LANES = 128
NEG_INF = float("-inf")

TM_IN = 512
TN_IN = 1536
TM_QK = 512
TQ = 256
ATTN_HB = 8
ONES_ROWS = 16
TM_OUT = 512
TM_RANK = 512
TM_EXP = 256
TM_COMB = 512
ROUTER_ROWS = 40
U_COLS = SSD_WIDTH + 2 * ATTN_WIDTH + SSD_CONV_DIM
VMEM_LIMIT = 52 * 1024 * 1024
EXPERT_VMEM_LIMIT = 58 * 1024 * 1024


def _cparams(sem):
    return pltpu.CompilerParams(dimension_semantics=sem, vmem_limit_bytes=VMEM_LIMIT)


def _silu(x):
    return x * (1.0 / (1.0 + jnp.exp(-x)))


def _softplus(x):
    return jnp.maximum(x, 0.0) + jnp.log(1.0 + jnp.exp(-jnp.abs(x)))


def _inproj_kernel(x_ref, lnw_ref, w_ref, wvt_ref, wdt_ref, u_ref, vt_ref, dt_ref):
    x = x_ref[...]
    ms = jnp.mean(x * x, axis=-1, keepdims=True)
    xn = (x * lax.rsqrt(ms + EPS) * lnw_ref[...]).astype(BF16)
    for j in range(U_COLS // TN_IN):
        cols = slice(j * TN_IN, (j + 1) * TN_IN)
        u_ref[:, cols] = jnp.dot(xn, w_ref[:, cols], preferred_element_type=F32).astype(BF16)
    dt_ref[...] = jnp.dot(xn, wdt_ref[...], preferred_element_type=F32)
    vt_ref[...] = lax.dot_general(wvt_ref[...], xn, (((1,), (1,)), ((), ())),
                                  preferred_element_type=F32).astype(BF16)


def _in_proj(x2, ln_w, w_main, w_vt, w_dt):
    n = x2.shape[0]
    resident = lambda shape: pl.BlockSpec(shape, lambda i: (0, 0), pipeline_mode=pl.Buffered(1))
    return pl.pallas_call(
        _inproj_kernel,
        grid=(n // TM_IN,),
        in_specs=[
            pl.BlockSpec((TM_IN, D_MODEL), lambda i: (i, 0)),
            resident((1, D_MODEL)),
            resident((D_MODEL, U_COLS)),
            resident((ATTN_WIDTH, D_MODEL)),
            resident((D_MODEL, LANES)),
        ],
        out_specs=[
            pl.BlockSpec((TM_IN, U_COLS), lambda i: (i, 0)),
            pl.BlockSpec((ATTN_WIDTH, TM_IN), lambda i: (0, i)),
            pl.BlockSpec((TM_IN, LANES), lambda i: (i, 0)),
        ],
        out_shape=[
            jax.ShapeDtypeStruct((n, U_COLS), BF16),
            jax.ShapeDtypeStruct((ATTN_WIDTH, n), BF16),
            jax.ShapeDtypeStruct((n, LANES), F32),
        ],
        compiler_params=_cparams(("parallel",)),
        name="in_proj",
    )(x2, ln_w, w_main, w_vt, w_dt)


def _ssd_kernel(z_ref, xbc_ref, dt_ref, convw_ref, convb_ref, dtb_ref, aneg_ref, dskip_ref, normw_ref,
                y_ref, xp_ref, st_ref, yacc_ref):
    L = SSD_CHUNK
    P = SSD_HEAD_DIM

    @pl.when(pl.program_id(1) == 0)
    def _():
        xp_ref[0:8, :] = jnp.zeros((8, SSD_CONV_DIM), F32)
        st_ref[...] = jnp.zeros_like(st_ref)

    xp_ref[8:8 + L, :] = xbc_ref[...].astype(F32)
    acc = jnp.broadcast_to(convb_ref[...], (L, SSD_CONV_DIM))
    for k in range(SSD_CONV):
        acc = acc + xp_ref[5 + k:5 + k + L, :] * convw_ref[k:k + 1, :]
    xp_ref[0:8, :] = xp_ref[L:L + 8, :]
    xc = _silu(acc)

    dt = _softplus(dt_ref[...] + dtb_ref[...])
    a = dt * aneg_ref[...]
    row = lax.broadcasted_iota(I32, (L, L), 0)
    col = lax.broadcasted_iota(I32, (L, L), 1)
    causal = row >= col
    a_cs = jnp.dot(causal.astype(F32), a, precision=HIGHEST, preferred_element_type=F32)
    a_last = a_cs[L - 1:L, :]
    ea = jnp.exp(a_cs)
    dsdt = jnp.exp(a_last - a_cs) * dt
    cd = jnp.exp(a_last)
    a_cs_t = a_cs.T
    dt_t = dt.T
    dsdt_t = dsdt.T

    for g in range(SSD_GROUPS):
        b_g = xc[:, SSD_WIDTH + g * SSD_STATE:SSD_WIDTH + (g + 1) * SSD_STATE]
        c_off = SSD_WIDTH + SSD_GROUPS * SSD_STATE
        c_g = xc[:, c_off + g * SSD_STATE:c_off + (g + 1) * SSD_STATE]
        cb = lax.dot_general(c_g.astype(BF16), b_g.astype(BF16), (((1,), (1,)), ((), ())),
                             preferred_element_type=F32)
        b_gt = b_g.T
        for hh in range(SSD_HEADS_PER_GROUP):
            h = g * SSD_HEADS_PER_GROUP + hh
            xs_h = xc[:, h * P:(h + 1) * P].astype(BF16)
            seg = a_cs[:, h:h + 1] - a_cs_t[h:h + 1, :]
            dec = jnp.exp(jnp.where(causal, seg, NEG_INF))
            m = (cb * dec * dt_t[h:h + 1, :]).astype(BF16)
            c_s = (c_g * ea[:, h:h + 1]).astype(BF16)
            s_prev = st_ref[h]
            lhs = jnp.concatenate([m, c_s], axis=1)
            rhs = jnp.concatenate([xs_h, s_prev.astype(BF16)], axis=0)
            yacc_ref[:, h * P:(h + 1) * P] = jnp.dot(lhs, rhs, preferred_element_type=F32)
            bw = (b_gt * dsdt_t[h:h + 1, :]).astype(BF16)
            st_ref[h] = s_prev * cd[:, h:h + 1] + jnp.dot(bw, xs_h, preferred_element_type=F32)

    y = yacc_ref[...] + xc[:, :SSD_WIDTH] * dskip_ref[...]
    y = y * _silu(z_ref[...].astype(F32))
    gw = SSD_WIDTH // SSD_GROUPS
    for g in range(SSD_GROUPS):
        yg = y[:, g * gw:(g + 1) * gw]
        ms = jnp.mean(yg * yg, axis=-1, keepdims=True)
        y_ref[:, g * gw:(g + 1) * gw] = (yg * lax.rsqrt(ms + EPS) * normw_ref[:, g * gw:(g + 1) * gw]).astype(BF16)


def _ssd(u, dt_raw, conv_w, conv_b, dt_bias, a_neg, dskip_lanes, norm_w, bsz, seq):
    n = u.shape[0]
    nc = seq // SSD_CHUNK
    xbc_blk = (SSD_WIDTH + 2 * ATTN_WIDTH) // SSD_CONV_DIM
    full = lambda shape: pl.BlockSpec(shape, lambda b, c: (0, 0))
    return pl.pallas_call(
        _ssd_kernel,
        grid=(bsz, nc),
        in_specs=[
            pl.BlockSpec((SSD_CHUNK, SSD_WIDTH), lambda b, c: (b * nc + c, 0)),
            pl.BlockSpec((SSD_CHUNK, SSD_CONV_DIM), lambda b, c: (b * nc + c, xbc_blk)),
            pl.BlockSpec((SSD_CHUNK, LANES), lambda b, c: (b * nc + c, 0)),
            full((SSD_CONV, SSD_CONV_DIM)),
            full((1, SSD_CONV_DIM)),
            full((1, LANES)),
            full((1, LANES)),
            full((1, SSD_WIDTH)),
            full((1, SSD_WIDTH)),
        ],
        out_specs=pl.BlockSpec((SSD_CHUNK, SSD_WIDTH), lambda b, c: (b * nc + c, 0)),
        out_shape=jax.ShapeDtypeStruct((n, SSD_WIDTH), BF16),
        scratch_shapes=[
            pltpu.VMEM((SSD_CHUNK + 8, SSD_CONV_DIM), F32),
            pltpu.VMEM((SSD_HEADS, SSD_STATE, SSD_HEAD_DIM), F32),
            pltpu.VMEM((SSD_CHUNK, SSD_WIDTH), F32),
        ],
        compiler_params=_cparams(("parallel", "arbitrary")),
        name="ssd",
    )(u, u, dt_raw, conv_w, conv_b, dt_bias, a_neg, dskip_lanes, norm_w)


def _qkprep_kernel(q_ref, k_ref, pos_ref, invf_ref, qw_ref, kw_ref, ones_ref, expand_ref, qo_ref, ko_ref):
    tm = q_ref.shape[0]
    half = ROPE_DIM // 2
    ang_t = invf_ref[...] * pos_ref[0:1, :]
    trig = jnp.concatenate([jnp.cos(ang_t), jnp.sin(ang_t), jnp.zeros((LANES - 4 * half, tm), F32)], axis=0).T
    t1 = trig.astype(BF16)
    r1 = trig - t1.astype(F32)
    t2 = r1.astype(BF16)
    t3 = (r1 - t2.astype(F32)).astype(BF16)
    spread = expand_ref[...]
    tab = (jnp.dot(t1, spread, preferred_element_type=F32) + jnp.dot(t2, spread, preferred_element_type=F32)
           + jnp.dot(t3, spread, preferred_element_type=F32))
    cs = tab[:, :LANES]
    s_lo = tab[:, LANES:2 * LANES]
    s_hi = tab[:, 2 * LANES:]
    for src, w_ref, dst in ((q_ref, qw_ref, qo_ref), (k_ref, kw_ref, ko_ref)):
        for hb in range(ATTN_HEADS):
            x = src[:, hb * LANES:(hb + 1) * LANES].astype(F32)
            ss = jnp.dot((x * x).astype(BF16), ones_ref[...], preferred_element_type=F32)
            xn = x * lax.rsqrt(ss * (1.0 / ATTN_QK_DIM) + EPS) * w_ref[...]
            out = xn * cs + pltpu.roll(xn, LANES - half, 1) * s_lo + pltpu.roll(xn, half, 1) * s_hi
            dst[:, hb * LANES:(hb + 1) * LANES] = out.astype(BF16)


def _rope_spread():
    half = ROPE_DIM // 2
    lane = jnp.arange(LANES)
    d = lane % ATTN_QK_DIM
    src = jnp.arange(LANES)[:, None]
    cos_src = jnp.where(d < ROPE_DIM, d % half, half)
    cos_tab = (src == cos_src[None, :]).astype(F32)
    lo_tab = -((src == (2 * half + d)[None, :]) & (d < half)[None, :]).astype(F32)
    hi_tab = ((src == (2 * half + d - half)[None, :]) & ((d >= half) & (d < ROPE_DIM))[None, :]).astype(F32)
    return jnp.concatenate([cos_tab, lo_tab, hi_tab], axis=1).astype(BF16)


def _qk_prep(u, pos_rows, invf_col, qw_lanes, kw_lanes, seg_ones, spread):
    n = u.shape[0]
    full = lambda shape: pl.BlockSpec(shape, lambda i: (0, 0))
    return pl.pallas_call(
        _qkprep_kernel,
        grid=(n // TM_QK,),
        in_specs=[
            pl.BlockSpec((TM_QK, ATTN_WIDTH), lambda i: (i, 1)),
            pl.BlockSpec((TM_QK, ATTN_WIDTH), lambda i: (i, 2)),
            pl.BlockSpec((8, TM_QK), lambda i: (0, i)),
            full((ROPE_DIM, 1)), full((1, LANES)), full((1, LANES)), full((LANES, LANES)), full((LANES, 3 * LANES)),
        ],
        out_specs=[pl.BlockSpec((TM_QK, ATTN_WIDTH), lambda i: (i, 0))] * 2,
        out_shape=[jax.ShapeDtypeStruct((n, ATTN_WIDTH), BF16)] * 2,
        compiler_params=_cparams(("parallel",)),
        name="qk_prep",
    )(u, u, pos_rows, invf_col, qw_lanes, kw_lanes, seg_ones, spread)


def _attn_kernel(q_ref, k_ref, vt_ref, lamv_ref, subw_ref, o_ref, acc_ref, *, lam_init):
    qi = pl.program_id(2)
    lane = lax.broadcasted_iota(I32, (TQ, LANES), 1)
    qs = []
    for hb in range(ATTN_HB):
        q = q_ref[:, hb * LANES:(hb + 1) * LANES]
        zero = jnp.zeros_like(q)
        qs.append(jnp.concatenate([jnp.where(lane < ATTN_QK_DIM, q, zero),
                                   jnp.where(lane >= ATTN_QK_DIM, q, zero)], axis=0))
    acc_ref[...] = jnp.zeros_like(acc_ref)
    kv_idx = lax.broadcasted_iota(I32, (TQ, 2 * TQ), 0)
    q_idx = lax.broadcasted_iota(I32, (TQ, 2 * TQ), 1) & (TQ - 1)
    nt = (((1,), (1,)), ((), ()))

    def block(j, carry, masked):
        off = pl.multiple_of(j * TQ, TQ)
        ss = []
        for hb in range(ATTN_HB):
            kb = k_ref[pl.ds(off, TQ), hb * LANES:(hb + 1) * LANES]
            ss.append(lax.dot_general(kb, qs[hb], nt, preferred_element_type=F32))
        new, ps, alphas = [], [], []
        for hb in range(ATTN_HB):
            m_old = carry[hb]
            s = ss[hb]
            if masked:
                s = jnp.where(kv_idx <= q_idx, s, NEG_INF)
            m_new = jnp.maximum(m_old, jnp.max(s, axis=0, keepdims=True))
            alphas.append(jnp.exp2(m_old - m_new))
            ps.append(jnp.exp2(s - m_new).astype(BF16))
            new.append(m_new)
        pvs = []
        for hb in range(ATTN_HB):
            vb = jnp.concatenate([vt_ref[hb * LANES:(hb + 1) * LANES, pl.ds(off, TQ)], ones_rows], axis=0)
            pvs.append(jnp.dot(vb, ps[hb], preferred_element_type=F32))
        for hb in range(ATTN_HB):
            acc_ref[hb] = alphas[hb] * acc_ref[hb] + pvs[hb]
        return tuple(new)

    ones_rows = jnp.ones((ONES_ROWS, TQ), BF16)
    init = (jnp.full((1, 2 * TQ), NEG_INF, F32),) * ATTN_HB
    carry = lax.fori_loop(0, qi, lambda j, cr: block(j, cr, False), init)
    block(qi, carry, True)

    lv = lamv_ref[...]
    lam = (jnp.exp(jnp.sum(lv[0:1] * lv[1:2], axis=1, keepdims=True))
           - jnp.exp(jnp.sum(lv[2:3] * lv[3:4], axis=1, keepdims=True)) + lam_init)
    for hb in range(ATTN_HB):
        acc = acc_ref[hb]
        o2 = acc[:ATTN_V_DIM] * (1.0 / acc[ATTN_V_DIM:ATTN_V_DIM + 1])
        o_t = o2[:, :TQ] - lam * o2[:, TQ:]
        ms = jnp.mean(o_t * o_t, axis=0, keepdims=True)
        o_t = o_t * lax.rsqrt(ms + EPS) * subw_ref[...] * (1.0 - lam_init)
        o_ref[:, hb * LANES:(hb + 1) * LANES] = o_t.T.astype(BF16)


def _attention(qp, kp, v_t, lam_vecs, subw_col, bsz, seq, lam_init):
    n = qp.shape[0]
    nq = seq // TQ
    w = ATTN_HB * ATTN_V_DIM
    return pl.pallas_call(
        functools.partial(_attn_kernel, lam_init=lam_init),
        grid=(bsz, ATTN_HEADS // ATTN_HB, nq),
        in_specs=[
            pl.BlockSpec((TQ, w), lambda b, h, i: (b * nq + i, h)),
            pl.BlockSpec((seq, w), lambda b, h, i: (b, h)),
            pl.BlockSpec((w, seq), lambda b, h, i: (h, b)),
            pl.BlockSpec((4, ATTN_QK_DIM), lambda b, h, i: (0, 0)),
            pl.BlockSpec((ATTN_V_DIM, 1), lambda b, h, i: (0, 0)),
        ],
        out_specs=pl.BlockSpec((TQ, w), lambda b, h, i: (b * nq + i, h)),
        out_shape=jax.ShapeDtypeStruct((n, ATTN_WIDTH), BF16),
        scratch_shapes=[pltpu.VMEM((ATTN_HB, ATTN_V_DIM + ONES_ROWS, 2 * TQ), F32)],
        compiler_params=_cparams(("parallel", "parallel", "arbitrary")),
        name="attn",
    )(qp, kp, v_t, lam_vecs, subw_col)


def _pack_halves(x):
    c = x.shape[1] // 2
    lo = pltpu.bitcast(x[:, :c].astype(BF16).astype(F32), U32) >> 16
    hi = pltpu.bitcast(x[:, c:].astype(BF16).astype(F32), U32) & jnp.uint32(0xFFFF0000)
    return hi | lo


TOKEN_ROWS = D_MODEL // 2 // LANES


def _store_token_rows(ref, first_piece, packed):
    tm = packed.shape[0]
    for j in range(packed.shape[1] // LANES):
        ref[pl.ds(first_piece + j, tm, stride=TOKEN_ROWS), :] = packed[:, j * LANES:(j + 1) * LANES]


def _load_token_rows(ref, tm):
    return jnp.concatenate([ref[pl.ds(s, tm, stride=TOKEN_ROWS), :] for s in range(TOKEN_ROWS)], axis=1)


def _unpack_halves(w, dtype=BF16):
    lo = pltpu.bitcast(w << 16, F32).astype(dtype)
    hi = pltpu.bitcast(w & jnp.uint32(0xFFFF0000), F32).astype(dtype)
    return lo, hi


def _outproj_kernel(x_ref, ys_ref, ya_ref, wos_ref, woa_ref, ln2_ref, wr_ref, br_ref,
                    h_ref, hnp_ref, meta_ref):
    tm = x_ref.shape[0]
    h = (x_ref[...]
         + jnp.dot(ys_ref[...], wos_ref[...], preferred_element_type=F32)
         + jnp.dot(ya_ref[...], woa_ref[...], preferred_element_type=F32))
    h_ref[...] = h
    ms = jnp.mean(h * h, axis=-1, keepdims=True)
    hn = h * lax.rsqrt(ms + EPS) * ln2_ref[...]
    _store_token_rows(hnp_ref, 0, _pack_halves(hn))

    lg_t = lax.dot_general(wr_ref[...], hn.astype(BF16), (((1,), (1,)), ((), ())),
                           preferred_element_type=F32)
    lg = lg_t[0:ROUTER_ROWS, :] + br_ref[0:ROUTER_ROWS, :]
    row = lax.broadcasted_iota(I32, (ROUTER_ROWS, tm), 0).astype(F32)
    big = float(LANES)
    gl = jnp.where(row < N_EXPERT_GROUPS, lg, NEG_INF)
    gmax = jnp.max(gl, axis=0, keepdims=True)
    gsel = jnp.min(jnp.where(gl == gmax, row, big), axis=0, keepdims=True)
    g_w = 1.0 / jnp.sum(jnp.exp(gl - gmax), axis=0, keepdims=True)
    eid = row - N_EXPERT_GROUPS
    lo = gsel * EXPERTS_PER_GROUP
    emask = (eid >= lo) & (eid < lo + EXPERTS_PER_GROUP)
    el = jnp.where(emask, lg, NEG_INF)
    m1 = jnp.max(el, axis=0, keepdims=True)
    i1 = jnp.min(jnp.where(el == m1, eid, big), axis=0, keepdims=True)
    el2 = jnp.where(eid == i1, NEG_INF, el)
    m2 = jnp.max(el2, axis=0, keepdims=True)
    i2 = jnp.min(jnp.where(el2 == m2, eid, big), axis=0, keepdims=True)
    e2 = jnp.exp(m2 - m1)
    w1 = g_w / (1.0 + e2)
    w2 = g_w * e2 / (1.0 + e2)
    mrow = lax.broadcasted_iota(I32, (LANES, tm), 0)
    meta_ref[...] = jnp.where(mrow == 0, i1, jnp.where(mrow == 1, i2, jnp.where(mrow == 2, w1, jnp.where(mrow == 3, w2, 0.0))))


def _out_proj(x2, y_ssd, y_att, wo_s, wo_a, ln2_w, w_router_t, b_router):
    n = x2.shape[0]
    full = lambda shape: pl.BlockSpec(shape, lambda i: (0, 0), pipeline_mode=pl.Buffered(1))
    return pl.pallas_call(
        _outproj_kernel,
        grid=(n // TM_OUT,),
        in_specs=[
            pl.BlockSpec((TM_OUT, D_MODEL), lambda i: (i, 0)),
            pl.BlockSpec((TM_OUT, SSD_WIDTH), lambda i: (i, 0)),
            pl.BlockSpec((TM_OUT, ATTN_WIDTH), lambda i: (i, 0)),
            full((SSD_WIDTH, D_MODEL)), full((ATTN_WIDTH, D_MODEL)),
            full((1, D_MODEL)), full((LANES, D_MODEL)), full((LANES, 1)),
        ],
        out_specs=[
            pl.BlockSpec((TM_OUT, D_MODEL), lambda i: (i, 0)),
            pl.BlockSpec((TM_OUT * TOKEN_ROWS, LANES), lambda i: (i, 0)),
            pl.BlockSpec((LANES, TM_OUT), lambda i: (0, i)),
        ],
        out_shape=[
            jax.ShapeDtypeStruct((n, D_MODEL), F32),
            jax.ShapeDtypeStruct((n * TOKEN_ROWS, LANES), U32),
            jax.ShapeDtypeStruct((LANES, n), F32),
        ],
        compiler_params=_cparams(("parallel",)),
        name="out_proj",
    )(x2, y_ssd, y_att, wo_s, wo_a, ln2_w, w_router_t, b_router)


def _rank_kernel(meta_ref, dest_ref, cnt_ref, run_ref, offs_ref):
    p = pl.program_id(0)
    i = pl.program_id(1)
    tm = meta_ref.shape[1]
    meta = meta_ref[...]
    row = lax.broadcasted_iota(I32, (N_EXPERTS, tm), 0).astype(F32)
    oh0 = (row == meta[0:1, :]).astype(F32)
    oh1 = (row == meta[1:2, :]).astype(F32)
    oh = oh0 + oh1
    rowsum = jnp.sum(oh, axis=1, keepdims=True)

    @pl.when((p == 0) & (i == 0))
    def _():
        run_ref[...] = jnp.zeros_like(run_ref)

    @pl.when(p == 0)
    def _():
        run_ref[...] = run_ref[...] + rowsum
        dest_ref[...] = jnp.zeros_like(dest_ref)
        cnt_ref[...] = run_ref[...]

    @pl.when((p == 1) & (i == 0))
    def _():
        cnt = run_ref[...]
        padded = jnp.ceil(cnt * (1.0 / TM_EXP)) * TM_EXP
        r = lax.broadcasted_iota(I32, (N_EXPERTS, N_EXPERTS), 0)
        c = lax.broadcasted_iota(I32, (N_EXPERTS, N_EXPERTS), 1)
        offs_ref[...] = jnp.dot((c < r).astype(F32), padded, precision=HIGHEST, preferred_element_type=F32)
        cnt_ref[...] = cnt
        run_ref[...] = jnp.zeros_like(run_ref)

    @pl.when(p == 1)
    def _():
        r = lax.broadcasted_iota(I32, (tm, tm), 0)
        c = lax.broadcasted_iota(I32, (tm, tm), 1)
        before = jnp.dot(oh.astype(BF16), (r < c).astype(BF16), preferred_element_type=F32)
        base = before + jnp.tile(run_ref[...] + offs_ref[...], (1, tm // LANES))
        d0 = jnp.sum(oh0 * base, axis=0, keepdims=True)
        d1 = jnp.sum(oh1 * base, axis=0, keepdims=True)
        drow = lax.broadcasted_iota(I32, (8, tm), 0)
        dest_ref[...] = jnp.where(drow == 0, d0, jnp.where(drow == 1, d1, 0.0)).astype(I32)
        run_ref[...] = run_ref[...] + rowsum


def _rank(meta_t):
    n = meta_t.shape[1]
    per_expert = lambda: pl.BlockSpec((N_EXPERTS, LANES), lambda p, i: (0, 0))
    return pl.pallas_call(
        _rank_kernel,
        grid=(2, n // TM_RANK),
        in_specs=[pl.BlockSpec((8, TM_RANK), lambda p, i: (0, i))],
        out_specs=[pl.BlockSpec((8, TM_RANK), lambda p, i: (0, i * p)), per_expert()],
        out_shape=[jax.ShapeDtypeStruct((8, n), I32), jax.ShapeDtypeStruct((N_EXPERTS, LANES), F32)],
        scratch_shapes=[pltpu.VMEM((N_EXPERTS, LANES), F32), pltpu.VMEM((N_EXPERTS, LANES), F32)],
        compiler_params=_cparams(("arbitrary", "arbitrary")),
        name="rank",
    )(meta_t)


def _token_copy(src_ref, src_token, dst_ref, dst_token, sem):
    src = src_ref.at[pl.ds(pl.multiple_of(src_token * TOKEN_ROWS, TOKEN_ROWS), TOKEN_ROWS)]
    dst = dst_ref.at[pl.ds(pl.multiple_of(dst_token * TOKEN_ROWS, TOKEN_ROWS), TOKEN_ROWS)]
    return pltpu.make_async_copy(src, dst, sem)


def _invert_kernel(d0_ref, d1_ref, valid_ref, inv_ref):
    n = d0_ref.shape[0]

    def clear_tile(tile, _):
        def clear(r, _):
            inv_ref[tile * TM_EXP + r] = 0
            return 0

        lax.fori_loop(valid_ref[tile], TM_EXP, clear, 0)
        return 0

    lax.fori_loop(0, valid_ref.shape[0], clear_tile, 0)

    def put(t, _):
        inv_ref[d0_ref[t]] = t
        inv_ref[d1_ref[t]] = n + t
        return 0

    lax.fori_loop(0, n, put, 0, unroll=8)


def _invert(dest0, dest1, tile_valid):
    smem = pl.BlockSpec(memory_space=pltpu.SMEM)
    return pl.pallas_call(
        _invert_kernel,
        in_specs=[smem, smem, smem],
        out_specs=smem,
        out_shape=jax.ShapeDtypeStruct((tile_valid.shape[0] * TM_EXP,), I32),
        name="invert",
    )(dest0, dest1, tile_valid)


CAST_ROWS = 256


def _cast_weight(src_ref, dst_ref):
    def body(i, _):
        rows = pl.ds(pl.multiple_of(i * CAST_ROWS, CAST_ROWS), CAST_ROWS)
        dst_ref[rows, :] = src_ref[rows, :].astype(BF16)
        return 0

    lax.fori_loop(0, src_ref.shape[0] // CAST_ROWS, body, 0)


S_EXPERT, S_FIRST, S_SLOT, S_NEXT, S_HAS_NEXT, S_USED, S_VALID = range(7)
WEIGHT_DMA_PRIORITY = 1


def _load_expert_weights(t, sched_ref, triples, sems):
    def copies(expert, slot):
        return [pltpu.make_async_copy(w.at[expert], stage.at[slot], sems.at[i, slot])
                for i, (w, stage, _) in enumerate(triples)]

    @pl.when(sched_ref[S_FIRST, t] == 1)
    def _():
        slot = sched_ref[S_SLOT, t]

        @pl.when(t == 0)
        def _():
            for cp in copies(sched_ref[S_EXPERT, t], slot):
                cp.start(priority=WEIGHT_DMA_PRIORITY)

        for cp in copies(sched_ref[S_EXPERT, t], slot):
            cp.wait()

        @pl.when(sched_ref[S_HAS_NEXT, t] == 1)
        def _():
            for cp in copies(sched_ref[S_NEXT, t], 1 - slot):
                cp.start(priority=WEIGHT_DMA_PRIORITY)

        for _, stage, dst in triples:
            _cast_weight(stage.at[slot], dst)


ROW_DMA_GROUPS = 4
GATHER_AHEAD = 3


def _tile_rows(sched_ref, tile, group):
    valid = sched_ref[S_VALID, tile]
    per = TM_EXP // ROW_DMA_GROUPS
    for r in range(group * per, (group + 1) * per):
        ok = r < valid
        yield r, jnp.where(ok, tile * TM_EXP + r, 0), ok


def _experts_up_kernel(sched_ref, inv_ref, hnp_hbm, wg_hbm, wu_hbm, hid_ref,
                       xbuf_ref, wg_stage, wu_stage, wgb_ref, wub_ref, wsems, gsem, *, n_tokens):
    t = pl.program_id(0)
    used = sched_ref[S_USED, 0]
    nbuf = GATHER_AHEAD + 1
    cur = t % nbuf
    ahead = (t + GATHER_AHEAD) % nbuf
    _load_expert_weights(t, sched_ref, [(wg_hbm, wg_stage, wgb_ref), (wu_hbm, wu_stage, wub_ref)], wsems)

    def start_rows(tile, buf, group):
        for r, slot, _ in _tile_rows(sched_ref, tile, group):
            token = inv_ref[slot] & (n_tokens - 1)
            _token_copy(hnp_hbm, token, xbuf_ref.at[buf], r, gsem.at[buf]).start()

    def wait_rows(buf):
        pltpu.make_async_copy(hnp_hbm.at[pl.ds(0, TM_EXP * TOKEN_ROWS)], xbuf_ref.at[buf], gsem.at[buf]).wait()

    for first in range(GATHER_AHEAD):
        @pl.when((t == 0) & (first < used))
        def _():
            for g in range(ROW_DMA_GROUPS):
                start_rows(first, first, g)

    def multiply(fetch_ahead):
        wait_rows(cur)
        c = D_MODEL // 2
        x_lo, x_hi = _unpack_halves(_load_token_rows(xbuf_ref.at[cur], TM_EXP))
        pieces = ((x_lo, wgb_ref, 0), (x_hi, wgb_ref, c), (x_lo, wub_ref, 0), (x_hi, wub_ref, c))
        acc = []
        for g, (x, w_ref, row0) in enumerate(pieces):
            if fetch_ahead:
                start_rows(t + GATHER_AHEAD, ahead, g)
            acc.append(jnp.dot(x, w_ref[row0:row0 + c, :], preferred_element_type=F32))
        hid_ref[...] = (_silu(acc[0] + acc[1]) * (acc[2] + acc[3])).astype(BF16)

    @pl.when(t + GATHER_AHEAD < used)
    def _():
        multiply(True)

    @pl.when((t + GATHER_AHEAD >= used) & (t < used))
    def _():
        multiply(False)

    @pl.when(t >= used)
    def _():
        hid_ref[...] = jnp.zeros_like(hid_ref)


def _experts_down_kernel(sched_ref, inv_ref, hid_ref, wd_hbm, y_hbm,
                         ybuf_ref, wd_stage, wdb_ref, wsems, ssem, *, n_tokens):
    t = pl.program_id(0)
    used = sched_ref[S_USED, 0]
    cur = t % 2
    _load_expert_weights(t, sched_ref, [(wd_hbm, wd_stage, wdb_ref)], wsems)

    def start_rows(tile, buf, group):
        for r, slot, ok in _tile_rows(sched_ref, tile, group):
            dst = jnp.where(ok, inv_ref[slot], TOP_K * n_tokens + r)
            _token_copy(ybuf_ref.at[buf], r, y_hbm, dst, ssem.at[buf]).start(priority=r % 2)

    def wait_rows(buf):
        pltpu.make_async_copy(ybuf_ref.at[buf], y_hbm.at[pl.ds(0, TM_EXP * TOKEN_ROWS)], ssem.at[buf]).wait()

    def multiply(send_previous):
        hid = hid_ref[...]
        q = D_MODEL // 4
        for half in range(2):
            if send_previous:
                start_rows(t - 1, 1 - cur, 2 * half)
            lo = jnp.dot(hid, wdb_ref[:, half * q:(half + 1) * q], preferred_element_type=F32)
            if send_previous:
                start_rows(t - 1, 1 - cur, 2 * half + 1)
            hi = jnp.dot(hid, wdb_ref[:, (half + 2) * q:(half + 3) * q], preferred_element_type=F32)
            _store_token_rows(ybuf_ref.at[cur], half * (q // LANES), _pack_halves(jnp.concatenate([lo, hi], axis=1)))

    @pl.when((t >= 2) & (t < used))
    def _():
        wait_rows(cur)

    @pl.when(t == 0)
    def _():
        ybuf_ref[1] = jnp.zeros(ybuf_ref.shape[1:], ybuf_ref.dtype)
        spare = pltpu.make_async_copy(ybuf_ref.at[1], y_hbm.at[pl.ds(TOP_K * n_tokens * TOKEN_ROWS, TM_EXP * TOKEN_ROWS)],
                                      ssem.at[1])
        spare.start()
        spare.wait()
        multiply(False)

    @pl.when((t > 0) & (t < used))
    def _():
        multiply(True)

    @pl.when(t == used - 1)
    def _():
        for g in range(ROW_DMA_GROUPS):
            start_rows(t, cur, g)

        @pl.when(t > 0)
        def _():
            wait_rows(1 - cur)

        wait_rows(cur)


def _tile_schedule(counts, n_tiles):
    tiles_per_expert = (counts + TM_EXP - 1) // TM_EXP
    tile_end = jnp.cumsum(tiles_per_expert)
    ids = jnp.arange(n_tiles, dtype=I32)
    raw = jnp.sum(ids[:, None] >= tile_end[None, :], axis=1)
    expert = jnp.minimum(raw, N_EXPERTS - 1).astype(I32)
    first = jnp.concatenate([jnp.ones((1,), I32), (expert[1:] != expert[:-1]).astype(I32)])
    slot = (jnp.cumsum(first) - 1) % 2
    start_idx = jnp.where(first == 1, ids, n_tiles)
    next_start = jnp.concatenate([lax.cummin(start_idx, reverse=True)[1:], jnp.full((1,), n_tiles, I32)])
    has_next = (next_start < n_tiles).astype(I32)
    next_expert = jnp.sum(jnp.where(ids[None, :] == next_start[:, None], expert[None, :], 0), axis=1)
    used = jnp.full((n_tiles,), tile_end[-1], I32)
    onehot = expert[:, None] == jnp.arange(N_EXPERTS)[None, :]
    tile_in_expert = ids - jnp.sum(jnp.where(onehot, (tile_end - tiles_per_expert)[None, :], 0), axis=1)
    rows_left = jnp.sum(jnp.where(onehot, counts[None, :], 0), axis=1) - tile_in_expert * TM_EXP
    valid = jnp.where(raw < N_EXPERTS, jnp.clip(rows_left, 0, TM_EXP), 0)
    return jnp.stack([expert, first, slot, next_expert, has_next, used, valid]).astype(I32)


def _experts(sched, inv, hnp, wg, wu, wd):
    n = hnp.shape[0] // TOKEN_ROWS
    n_slots = inv.shape[0]
    token_buf = lambda nbuf: pltpu.VMEM((nbuf, TM_EXP * TOKEN_ROWS, LANES), U32)
    row_spec = lambda width: pl.BlockSpec((TM_EXP, width), lambda t, sc, iv: (t, 0))
    hbm = pl.BlockSpec(memory_space=pl.ANY)
    big_vmem = pltpu.CompilerParams(dimension_semantics=("arbitrary",), vmem_limit_bytes=EXPERT_VMEM_LIMIT)
    hid = pl.pallas_call(
        functools.partial(_experts_up_kernel, n_tokens=n),
        grid_spec=pltpu.PrefetchScalarGridSpec(
            num_scalar_prefetch=2,
            grid=(n_slots // TM_EXP,),
            in_specs=[hbm, hbm, hbm],
            out_specs=row_spec(EXPERT_HIDDEN),
            scratch_shapes=[token_buf(GATHER_AHEAD + 1)]
                           + [pltpu.VMEM((2, D_MODEL, EXPERT_HIDDEN), F32)] * 2
                           + [pltpu.VMEM((D_MODEL, EXPERT_HIDDEN), BF16)] * 2
                           + [pltpu.SemaphoreType.DMA((2, 2)), pltpu.SemaphoreType.DMA((GATHER_AHEAD + 1,))],
        ),
        out_shape=jax.ShapeDtypeStruct((n_slots, EXPERT_HIDDEN), BF16),
        compiler_params=big_vmem,
        name="experts_up",
    )(sched, inv, hnp, wg, wu)
    return pl.pallas_call(
        functools.partial(_experts_down_kernel, n_tokens=n),
        grid_spec=pltpu.PrefetchScalarGridSpec(
            num_scalar_prefetch=2,
            grid=(n_slots // TM_EXP,),
            in_specs=[row_spec(EXPERT_HIDDEN), hbm],
            out_specs=hbm,
            scratch_shapes=[token_buf(2),
                            pltpu.VMEM((2, EXPERT_HIDDEN, D_MODEL), F32), pltpu.VMEM((EXPERT_HIDDEN, D_MODEL), BF16),
                            pltpu.SemaphoreType.DMA((1, 2)), pltpu.SemaphoreType.DMA((2,))],
        ),
        out_shape=jax.ShapeDtypeStruct(((TOP_K * n + TM_EXP) * TOKEN_ROWS, LANES), U32),
        compiler_params=big_vmem,
        name="experts_down",
    )(sched, inv, hid, wd)


def _combine_kernel(h_ref, meta_ref, y0_ref, y1_ref, o_ref):
    c = D_MODEL // 2
    meta = meta_ref[...].T
    w0, w1 = meta[:, 2:3], meta[:, 3:4]
    tm = h_ref.shape[0]
    lo0, hi0 = _unpack_halves(_load_token_rows(y0_ref, tm), F32)
    lo1, hi1 = _unpack_halves(_load_token_rows(y1_ref, tm), F32)
    o_ref[:, :c] = h_ref[:, :c] + (lo0 * w0 + lo1 * w1)
    o_ref[:, c:] = h_ref[:, c:] + (hi0 * w0 + hi1 * w1)


def _combine(h, meta, y):
    n = h.shape[0]
    nb = n // TM_COMB
    return pl.pallas_call(
        _combine_kernel,
        grid=(nb,),
        in_specs=[
            pl.BlockSpec((TM_COMB, D_MODEL), lambda i: (i, 0)),
            pl.BlockSpec((LANES, TM_COMB), lambda i: (0, i)),
            pl.BlockSpec((TM_COMB * TOKEN_ROWS, LANES), lambda i: (i, 0)),
            pl.BlockSpec((TM_COMB * TOKEN_ROWS, LANES), lambda i: (nb + i, 0)),
        ],
        out_specs=pl.BlockSpec((TM_COMB, D_MODEL), lambda i: (i, 0)),
        out_shape=jax.ShapeDtypeStruct((n, D_MODEL), F32),
        compiler_params=_cparams(("parallel",)),
        name="combine",
    )(h, meta, y, y)


def _lambda_init(layer_idx):
    return 0.8 - 0.6 * math.exp(-0.3 * layer_idx)


def _pad_lanes(v, width=LANES):
    return jnp.pad(v, ((0, 0), (0, width - v.shape[1])))


def _layer(l, x2, pos_rows, bsz, seq, ln1_w, w_in, conv_w, conv_b, dt_bias, a_log, d_skip, ssd_norm_w,
           q_norm_w, k_norm_w, lambda_q1, lambda_k1, lambda_q2, lambda_k2, subln_w, w_out, ln2_w,
           w_router_group, b_router_group, w_router_expert, b_router_expert, w_gate, w_up, w_down):
    n = x2.shape[0]
    c_z, c_xbc, c_dt = SSD_WIDTH, SSD_WIDTH + SSD_CONV_DIM, SSD_WIDTH + SSD_CONV_DIM + SSD_HEADS
    c_q, c_k = c_dt + ATTN_WIDTH, c_dt + 2 * ATTN_WIDTH
    w_main = jnp.concatenate([w_in[:, :c_z], w_in[:, c_dt:c_q], w_in[:, c_q:c_k], w_in[:, c_z:c_xbc]],
                             axis=1).astype(BF16)
    w_vt = w_in[:, c_k:].T.astype(BF16)
    w_dt = _pad_lanes(w_in[:, c_xbc:c_dt]).astype(BF16)

    u, v_t, dt_raw = _in_proj(x2, ln1_w[None, :], w_main, w_vt, w_dt)

    a_neg = _pad_lanes(-jnp.exp(a_log.astype(F32))[None, :])
    y_ssd = _ssd(u, dt_raw, conv_w, conv_b[None, :], _pad_lanes(dt_bias[None, :]), a_neg,
                 jnp.repeat(d_skip, SSD_HEAD_DIM)[None, :], ssd_norm_w[None, :], bsz, seq)

    inv_freq = jnp.power(ROPE_THETA, -jnp.arange(0, ROPE_DIM, 2, dtype=F32) / ROPE_DIM)
    invf_col = jnp.concatenate([inv_freq, jnp.zeros_like(inv_freq)])[:, None]
    seg_ones = (jnp.arange(LANES)[:, None] // ATTN_QK_DIM == jnp.arange(LANES)[None, :] // ATTN_QK_DIM).astype(BF16)
    q_scale = math.log2(math.e) / math.sqrt(ATTN_QK_DIM)
    qp, kp = _qk_prep(u, pos_rows, invf_col, jnp.tile(q_norm_w, 2)[None, :] * q_scale, jnp.tile(k_norm_w, 2)[None, :],
                      seg_ones, _rope_spread())

    lam_vecs = jnp.stack([lambda_q1, lambda_k1, lambda_q2, lambda_k2]).astype(F32)
    y_att = _attention(qp, kp, v_t, lam_vecs, subln_w[:, None], bsz, seq, _lambda_init(l))

    w_out_b = w_out.astype(BF16)
    w_router_t = _pad_lanes(jnp.concatenate([w_router_group, w_router_expert], axis=1)).T.astype(BF16)
    b_router = _pad_lanes(jnp.concatenate([b_router_group, b_router_expert])[None, :]).T
    h, hnp, meta = _out_proj(x2, y_ssd, y_att, w_out_b[:SSD_WIDTH], w_out_b[SSD_WIDTH:], ln2_w[None, :],
                             w_router_t, b_router)

    dest, cnt = _rank(meta)
    dest0, dest1 = dest[0], dest[1]
    counts = cnt[:, 0].astype(I32)
    n_slots = (n * TOP_K + N_EXPERTS * (TM_EXP - 1)) // TM_EXP * TM_EXP
    sched = _tile_schedule(counts, n_slots // TM_EXP)

    inv = _invert(dest0, dest1, sched[S_VALID])
    y = _experts(sched, inv, hnp, w_gate, w_up, w_down)
    return _combine(h, meta, y)


def kernel(x, positions, ln1_w, w_in, conv_w, conv_b, dt_bias, a_log, d_skip, ssd_norm_w, q_norm_w, k_norm_w,
           lambda_q1, lambda_k1, lambda_q2, lambda_k2, subln_w, w_out, ln2_w, w_router_group, b_router_group,
           w_router_expert, b_router_expert, w_gate, w_up, w_down):
    bsz, seq, d = x.shape
    assert d == D_MODEL and seq % TQ == 0 and (bsz * seq) % TM_IN == 0
    x2 = x.reshape(bsz * seq, d)
    pos_rows = jnp.broadcast_to(positions.astype(F32).reshape(1, bsz * seq), (8, bsz * seq))
    params = (ln1_w, w_in, conv_w, conv_b, dt_bias, a_log, d_skip, ssd_norm_w, q_norm_w, k_norm_w,
              lambda_q1, lambda_k1, lambda_q2, lambda_k2, subln_w, w_out, ln2_w, w_router_group, b_router_group,
              w_router_expert, b_router_expert, w_gate, w_up, w_down)
    for l in range(ln1_w.shape[0]):
        x2 = _layer(l, x2, pos_rows, bsz, seq, *[p[l] for p in params])
    return x2.reshape(bsz, seq, d)
```

```python
import functools
import math

import jax
import jax.numpy as jnp
from jax import lax
from jax.experimental import pallas as pl
from jax.experimental.pallas import tpu as pltpu

F32 = jnp.float32
BF16 = jnp.bfloat16
I32 = jnp.int32
U32 = jnp.uint32
HIGHEST = lax.Precision.HIGHEST

D_MODEL = 2048
SSD_WIDTH = 1024
ATTN_WIDTH = 1024
SSD_HEAD_DIM = 64
SSD_HEADS = 16
SSD_GROUPS = 2
SSD_HEADS_PER_GROUP = SSD_HEADS // SSD_GROUPS
SSD_STATE = 128
SSD_CONV = 4
SSD_CHUNK = 128
SSD_CONV_DIM = SSD_WIDTH + 2 * SSD_GROUPS * SSD_STATE
ATTN_V_DIM = 128
ATTN_HEADS = 8
ATTN_QK_DIM = 64
ROPE_THETA = 500000.0
ROPE_DIM = 16
N_EXPERT_GROUPS = 4
EXPERTS_PER_GROUP = 8
N_EXPERTS = 32
TOP_K = 2
EXPERT_HIDDEN = 1024
EPS = 1e-6

LANES = 128
NEG_INF = float("-inf")

TM_IN = 512
TQ = 256
ATTN_HB = 8
ONES_ROWS = 16
TM_OUT = 512
TM_RANK = 512
TM_EXP = 256
TM_COMB = 512
ROUTER_ROWS = 40
Q_COL = SSD_WIDTH
K_COL = Q_COL + ATTN_WIDTH
XBC_COL = K_COL + ATTN_WIDTH
U_COLS = XBC_COL + SSD_CONV_DIM
VMEM_LIMIT = 52 * 1024 * 1024
EXPERT_VMEM_LIMIT = 58 * 1024 * 1024


def _cparams(sem):
    return pltpu.CompilerParams(dimension_semantics=sem, vmem_limit_bytes=VMEM_LIMIT)


def _silu(x):
    return x * (1.0 / (1.0 + jnp.exp(-x)))


def _softplus(x):
    return jnp.maximum(x, 0.0) + jnp.log(1.0 + jnp.exp(-jnp.abs(x)))


def _rope_tables(pos_ref, invf_ref, spread_ref):
    tm = pos_ref.shape[1]
    ang_t = invf_ref[...] * pos_ref[0:1, :]
    trig = jnp.concatenate([jnp.cos(ang_t), jnp.sin(ang_t), jnp.zeros((LANES - 2 * ROPE_DIM, tm), F32)], axis=0).T
    t1 = trig.astype(BF16)
    r1 = trig - t1.astype(F32)
    t2 = r1.astype(BF16)
    t3 = (r1 - t2.astype(F32)).astype(BF16)
    spread = spread_ref[...]
    tab = (jnp.dot(t1, spread, preferred_element_type=F32) + jnp.dot(t2, spread, preferred_element_type=F32)
           + jnp.dot(t3, spread, preferred_element_type=F32))
    return tab[:, :LANES], tab[:, LANES:2 * LANES], tab[:, 2 * LANES:]


def _norm_rope(x, w, tables, seg_ones):
    cs, s_lo, s_hi = tables
    half = ROPE_DIM // 2
    ss = jnp.dot((x * x).astype(BF16), seg_ones, preferred_element_type=F32)
    xn = x * lax.rsqrt(ss * (1.0 / ATTN_QK_DIM) + EPS) * w
    return xn * cs + pltpu.roll(xn, LANES - half, 1) * s_lo + pltpu.roll(xn, half, 1) * s_hi


def _inproj_kernel(x_ref, lnw_ref, w_ref, wvt_ref, wdt_ref, pos_ref, invf_ref, qw_ref, kw_ref, ones_ref, spread_ref,
                   u_ref, vt_ref, dt_ref):
    x = x_ref[...]
    ms = jnp.mean(x * x, axis=-1, keepdims=True)
    xn = (x * lax.rsqrt(ms + EPS) * lnw_ref[...]).astype(BF16)
    tables = _rope_tables(pos_ref, invf_ref, spread_ref)
    for c0, c1, head_w in ((0, SSD_WIDTH, None), (Q_COL, K_COL, qw_ref), (K_COL, XBC_COL, kw_ref), (XBC_COL, U_COLS, None)):
        acc = jnp.dot(xn, w_ref[:, c0:c1], preferred_element_type=F32)
        if head_w is None:
            u_ref[:, c0:c1] = acc.astype(BF16)
        else:
            for hb in range(ATTN_HEADS):
                blk = _norm_rope(acc[:, hb * LANES:(hb + 1) * LANES], head_w[...], tables, ones_ref[...])
                u_ref[:, c0 + hb * LANES:c0 + (hb + 1) * LANES] = blk.astype(BF16)
    dt_ref[...] = jnp.dot(xn, wdt_ref[...], preferred_element_type=F32)
    vt_ref[...] = lax.dot_general(wvt_ref[...], xn, (((1,), (1,)), ((), ())),
                                  preferred_element_type=F32).astype(BF16)


def _in_proj(x2, ln_w, w_main, w_vt, w_dt, pos_rows, invf_col, qw_lanes, kw_lanes, seg_ones, spread):
    n = x2.shape[0]
    resident = lambda shape: pl.BlockSpec(shape, lambda i: (0, 0), pipeline_mode=pl.Buffered(1))
    return pl.pallas_call(
        _inproj_kernel,
        grid=(n // TM_IN,),
        in_specs=[
            pl.BlockSpec((TM_IN, D_MODEL), lambda i: (i, 0)),
            resident((1, D_MODEL)),
            resident((D_MODEL, U_COLS)),
            resident((ATTN_WIDTH, D_MODEL)),
            resident((D_MODEL, LANES)),
            pl.BlockSpec((8, TM_IN), lambda i: (0, i)),
            resident((ROPE_DIM, 1)), resident((1, LANES)), resident((1, LANES)), resident((LANES, LANES)),
            resident((LANES, 3 * LANES)),
        ],
        out_specs=[
            pl.BlockSpec((TM_IN, U_COLS), lambda i: (i, 0)),
            pl.BlockSpec((ATTN_WIDTH, TM_IN), lambda i: (0, i)),
            pl.BlockSpec((TM_IN, LANES), lambda i: (i, 0)),
        ],
        out_shape=[
            jax.ShapeDtypeStruct((n, U_COLS), BF16),
            jax.ShapeDtypeStruct((ATTN_WIDTH, n), BF16),
            jax.ShapeDtypeStruct((n, LANES), F32),
        ],
        compiler_params=_cparams(("parallel",)),
        name="in_proj",
    )(x2, ln_w, w_main, w_vt, w_dt, pos_rows, invf_col, qw_lanes, kw_lanes, seg_ones, spread)


def _ssd_kernel(z_ref, xbc_ref, dt_ref, convw_ref, convb_ref, dtb_ref, aneg_ref, dskip_ref, normw_ref,
                y_ref, xp_ref, st_ref, yacc_ref):
    L = SSD_CHUNK
    P = SSD_HEAD_DIM

    @pl.when(pl.program_id(1) == 0)
    def _():
        xp_ref[0:8, :] = jnp.zeros((8, SSD_CONV_DIM), F32)
        st_ref[...] = jnp.zeros_like(st_ref)

    xp_ref[8:8 + L, :] = xbc_ref[...].astype(F32)
    acc = jnp.broadcast_to(convb_ref[...], (L, SSD_CONV_DIM))
    for k in range(SSD_CONV):
        acc = acc + xp_ref[5 + k:5 + k + L, :] * convw_ref[k:k + 1, :]
    xp_ref[0:8, :] = xp_ref[L:L + 8, :]
    xc = _silu(acc)

    dt = _softplus(dt_ref[...] + dtb_ref[...])
    a = dt * aneg_ref[...]
    row = lax.broadcasted_iota(I32, (L, L), 0)
    col = lax.broadcasted_iota(I32, (L, L), 1)
    causal = row >= col
    a_cs = jnp.dot(causal.astype(F32), a, precision=HIGHEST, preferred_element_type=F32)
    a_last = a_cs[L - 1:L, :]
    ea = jnp.exp(a_cs)
    dsdt = jnp.exp(a_last - a_cs) * dt
    cd = jnp.exp(a_last)
    a_cs_t = a_cs.T
    dt_t = dt.T
    dsdt_t = dsdt.T

    for g in range(SSD_GROUPS):
        b_g = xc[:, SSD_WIDTH + g * SSD_STATE:SSD_WIDTH + (g + 1) * SSD_STATE]
        c_off = SSD_WIDTH + SSD_GROUPS * SSD_STATE
        c_g = xc[:, c_off + g * SSD_STATE:c_off + (g + 1) * SSD_STATE]
        cb = lax.dot_general(c_g.astype(BF16), b_g.astype(BF16), (((1,), (1,)), ((), ())),
                             preferred_element_type=F32)
        b_gt = b_g.T
        for hh in range(SSD_HEADS_PER_GROUP):
            h = g * SSD_HEADS_PER_GROUP + hh
            xs_h = xc[:, h * P:(h + 1) * P].astype(BF16)
            seg = a_cs[:, h:h + 1] - a_cs_t[h:h + 1, :]
            dec = jnp.exp(jnp.where(causal, seg, NEG_INF))
            m = (cb * dec * dt_t[h:h + 1, :]).astype(BF16)
            c_s = (c_g * ea[:, h:h + 1]).astype(BF16)
            s_prev = st_ref[h]
            lhs = jnp.concatenate([m, c_s], axis=1)
            rhs = jnp.concatenate([xs_h, s_prev.astype(BF16)], axis=0)
            yacc_ref[:, h * P:(h + 1) * P] = jnp.dot(lhs, rhs, preferred_element_type=F32)
            bw = (b_gt * dsdt_t[h:h + 1, :]).astype(BF16)
            st_ref[h] = s_prev * cd[:, h:h + 1] + jnp.dot(bw, xs_h, preferred_element_type=F32)

    y = yacc_ref[...] + xc[:, :SSD_WIDTH] * dskip_ref[...]
    y = y * _silu(z_ref[...].astype(F32))
    gw = SSD_WIDTH // SSD_GROUPS
    for g in range(SSD_GROUPS):
        yg = y[:, g * gw:(g + 1) * gw]
        ms = jnp.mean(yg * yg, axis=-1, keepdims=True)
        y_ref[:, g * gw:(g + 1) * gw] = (yg * lax.rsqrt(ms + EPS) * normw_ref[:, g * gw:(g + 1) * gw]).astype(BF16)


def _ssd(u, dt_raw, conv_w, conv_b, dt_bias, a_neg, dskip_lanes, norm_w, bsz, seq):
    n = u.shape[0]
    nc = seq // SSD_CHUNK
    xbc_blk = (SSD_WIDTH + 2 * ATTN_WIDTH) // SSD_CONV_DIM
    full = lambda shape: pl.BlockSpec(shape, lambda b, c: (0, 0))
    return pl.pallas_call(
        _ssd_kernel,
        grid=(bsz, nc),
        in_specs=[
            pl.BlockSpec((SSD_CHUNK, SSD_WIDTH), lambda b, c: (b * nc + c, 0)),
            pl.BlockSpec((SSD_CHUNK, SSD_CONV_DIM), lambda b, c: (b * nc + c, xbc_blk)),
            pl.BlockSpec((SSD_CHUNK, LANES), lambda b, c: (b * nc + c, 0)),
            full((SSD_CONV, SSD_CONV_DIM)),
            full((1, SSD_CONV_DIM)),
            full((1, LANES)),
            full((1, LANES)),
            full((1, SSD_WIDTH)),
            full((1, SSD_WIDTH)),
        ],
        out_specs=pl.BlockSpec((SSD_CHUNK, SSD_WIDTH), lambda b, c: (b * nc + c, 0)),
        out_shape=jax.ShapeDtypeStruct((n, SSD_WIDTH), BF16),
        scratch_shapes=[
            pltpu.VMEM((SSD_CHUNK + 8, SSD_CONV_DIM), F32),
            pltpu.VMEM((SSD_HEADS, SSD_STATE, SSD_HEAD_DIM), F32),
            pltpu.VMEM((SSD_CHUNK, SSD_WIDTH), F32),
        ],
        compiler_params=_cparams(("parallel", "arbitrary")),
        name="ssd",
    )(u, u, dt_raw, conv_w, conv_b, dt_bias, a_neg, dskip_lanes, norm_w)


def _rope_spread():
    half = ROPE_DIM // 2
    lane = jnp.arange(LANES)
    d = lane % ATTN_QK_DIM
    src = jnp.arange(LANES)[:, None]
    cos_src = jnp.where(d < ROPE_DIM, d % half, half)
    cos_tab = (src == cos_src[None, :]).astype(F32)
    lo_tab = -((src == (2 * half + d)[None, :]) & (d < half)[None, :]).astype(F32)
    hi_tab = ((src == (2 * half + d - half)[None, :]) & ((d >= half) & (d < ROPE_DIM))[None, :]).astype(F32)
    return jnp.concatenate([cos_tab, lo_tab, hi_tab], axis=1).astype(BF16)


def _attn_kernel(q_ref, k_ref, vt_ref, lamv_ref, subw_ref, o_ref, acc_ref, *, lam_init):
    qi = pl.program_id(2)
    lane = lax.broadcasted_iota(I32, (TQ, LANES), 1)
    qs = []
    for hb in range(ATTN_HB):
        q = q_ref[:, hb * LANES:(hb + 1) * LANES]
        zero = jnp.zeros_like(q)
        qs.append(jnp.concatenate([jnp.where(lane < ATTN_QK_DIM, q, zero),
                                   jnp.where(lane >= ATTN_QK_DIM, q, zero)], axis=0))
    acc_ref[...] = jnp.zeros_like(acc_ref)
    kv_idx = lax.broadcasted_iota(I32, (TQ, 2 * TQ), 0)
    q_idx = lax.broadcasted_iota(I32, (TQ, 2 * TQ), 1) & (TQ - 1)
    nt = (((1,), (1,)), ((), ()))

    def block(j, carry, masked):
        off = pl.multiple_of(j * TQ, TQ)
        ss = []
        for hb in range(ATTN_HB):
            kb = k_ref[pl.ds(off, TQ), hb * LANES:(hb + 1) * LANES]
            ss.append(lax.dot_general(kb, qs[hb], nt, preferred_element_type=F32))
        new, ps, alphas = [], [], []
        for hb in range(ATTN_HB):
            m_old = carry[hb]
            s = ss[hb]
            if masked:
                s = jnp.where(kv_idx <= q_idx, s, NEG_INF)
            m_new = jnp.maximum(m_old, jnp.max(s, axis=0, keepdims=True))
            alphas.append(jnp.exp2(m_old - m_new))
            ps.append(jnp.exp2(s - m_new).astype(BF16))
            new.append(m_new)
        pvs = []
        for hb in range(ATTN_HB):
            vb = jnp.concatenate([vt_ref[hb * LANES:(hb + 1) * LANES, pl.ds(off, TQ)], ones_rows], axis=0)
            pvs.append(jnp.dot(vb, ps[hb], preferred_element_type=F32))
        for hb in range(ATTN_HB):
            acc_ref[hb] = alphas[hb] * acc_ref[hb] + pvs[hb]
        return tuple(new)

    ones_rows = jnp.ones((ONES_ROWS, TQ), BF16)
    init = (jnp.full((1, 2 * TQ), NEG_INF, F32),) * ATTN_HB
    carry = lax.fori_loop(0, qi, lambda j, cr: block(j, cr, False), init)
    block(qi, carry, True)

    lv = lamv_ref[...]
    lam = (jnp.exp(jnp.sum(lv[0:1] * lv[1:2], axis=1, keepdims=True))
           - jnp.exp(jnp.sum(lv[2:3] * lv[3:4], axis=1, keepdims=True)) + lam_init)
    for hb in range(ATTN_HB):
        acc = acc_ref[hb]
        o2 = acc[:ATTN_V_DIM] * (1.0 / acc[ATTN_V_DIM:ATTN_V_DIM + 1])
        o_t = o2[:, :TQ] - lam * o2[:, TQ:]
        ms = jnp.mean(o_t * o_t, axis=0, keepdims=True)
        o_t = o_t * lax.rsqrt(ms + EPS) * subw_ref[...] * (1.0 - lam_init)
        o_ref[:, hb * LANES:(hb + 1) * LANES] = o_t.T.astype(BF16)


def _attention(u, v_t, lam_vecs, subw_col, bsz, seq, lam_init):
    n = u.shape[0]
    nq = seq // TQ
    w = ATTN_HB * ATTN_V_DIM
    q_blk, k_blk = Q_COL // w, K_COL // w
    return pl.pallas_call(
        functools.partial(_attn_kernel, lam_init=lam_init),
        grid=(bsz, ATTN_HEADS // ATTN_HB, nq),
        in_specs=[
            pl.BlockSpec((TQ, w), lambda b, h, i: (b * nq + i, q_blk + h)),
            pl.BlockSpec((seq, w), lambda b, h, i: (b, k_blk + h)),
            pl.BlockSpec((w, seq), lambda b, h, i: (h, b)),
            pl.BlockSpec((4, ATTN_QK_DIM), lambda b, h, i: (0, 0)),
            pl.BlockSpec((ATTN_V_DIM, 1), lambda b, h, i: (0, 0)),
        ],
        out_specs=pl.BlockSpec((TQ, w), lambda b, h, i: (b * nq + i, h)),
        out_shape=jax.ShapeDtypeStruct((n, ATTN_WIDTH), BF16),
        scratch_shapes=[pltpu.VMEM((ATTN_HB, ATTN_V_DIM + ONES_ROWS, 2 * TQ), F32)],
        compiler_params=_cparams(("parallel", "parallel", "arbitrary")),
        name="attn",
    )(u, u, v_t, lam_vecs, subw_col)


def _pack_halves(x):
    c = x.shape[1] // 2
    lo = pltpu.bitcast(x[:, :c].astype(BF16).astype(F32), U32) >> 16
    hi = pltpu.bitcast(x[:, c:].astype(BF16).astype(F32), U32) & jnp.uint32(0xFFFF0000)
    return hi | lo


TOKEN_ROWS = D_MODEL // 2 // LANES


def _store_token_rows(ref, first_piece, packed):
    tm = packed.shape[0]
    for j in range(packed.shape[1] // LANES):
        ref[pl.ds(first_piece + j, tm, stride=TOKEN_ROWS), :] = packed[:, j * LANES:(j + 1) * LANES]


def _load_token_rows(ref, tm):
    return jnp.concatenate([ref[pl.ds(s, tm, stride=TOKEN_ROWS), :] for s in range(TOKEN_ROWS)], axis=1)


def _unpack_halves(w, dtype=BF16):
    lo = pltpu.bitcast(w << 16, F32).astype(dtype)
    hi = pltpu.bitcast(w & jnp.uint32(0xFFFF0000), F32).astype(dtype)
    return lo, hi


def _outproj_kernel(x_ref, ys_ref, ya_ref, wos_ref, woa_ref, ln2_ref, wr_ref, br_ref,
                    h_ref, hnp_ref, meta_ref):
    tm = x_ref.shape[0]
    h = (x_ref[...]
         + jnp.dot(ys_ref[...], wos_ref[...], preferred_element_type=F32)
         + jnp.dot(ya_ref[...], woa_ref[...], preferred_element_type=F32))
    h_ref[...] = h
    ms = jnp.mean(h * h, axis=-1, keepdims=True)
    hn = h * lax.rsqrt(ms + EPS) * ln2_ref[...]
    _store_token_rows(hnp_ref, 0, _pack_halves(hn))

    lg_t = lax.dot_general(wr_ref[...], hn.astype(BF16), (((1,), (1,)), ((), ())),
                           preferred_element_type=F32)
    lg = lg_t[0:ROUTER_ROWS, :] + br_ref[0:ROUTER_ROWS, :]
    row = lax.broadcasted_iota(I32, (ROUTER_ROWS, tm), 0).astype(F32)
    big = float(LANES)
    gl = jnp.where(row < N_EXPERT_GROUPS, lg, NEG_INF)
    gmax = jnp.max(gl, axis=0, keepdims=True)
    gsel = jnp.min(jnp.where(gl == gmax, row, big), axis=0, keepdims=True)
    g_w = 1.0 / jnp.sum(jnp.exp(gl - gmax), axis=0, keepdims=True)
    eid = row - N_EXPERT_GROUPS
    lo = gsel * EXPERTS_PER_GROUP
    emask = (eid >= lo) & (eid < lo + EXPERTS_PER_GROUP)
    el = jnp.where(emask, lg, NEG_INF)
    m1 = jnp.max(el, axis=0, keepdims=True)
    i1 = jnp.min(jnp.where(el == m1, eid, big), axis=0, keepdims=True)
    el2 = jnp.where(eid == i1, NEG_INF, el)
    m2 = jnp.max(el2, axis=0, keepdims=True)
    i2 = jnp.min(jnp.where(el2 == m2, eid, big), axis=0, keepdims=True)
    e2 = jnp.exp(m2 - m1)
    w1 = g_w / (1.0 + e2)
    w2 = g_w * e2 / (1.0 + e2)
    mrow = lax.broadcasted_iota(I32, (LANES, tm), 0)
    meta_ref[...] = jnp.where(mrow == 0, i1, jnp.where(mrow == 1, i2, jnp.where(mrow == 2, w1, jnp.where(mrow == 3, w2, 0.0))))


def _out_proj(x2, y_ssd, y_att, wo_s, wo_a, ln2_w, w_router_t, b_router):
    n = x2.shape[0]
    full = lambda shape: pl.BlockSpec(shape, lambda i: (0, 0), pipeline_mode=pl.Buffered(1))
    return pl.pallas_call(
        _outproj_kernel,
        grid=(n // TM_OUT,),
        in_specs=[
            pl.BlockSpec((TM_OUT, D_MODEL), lambda i: (i, 0)),
            pl.BlockSpec((TM_OUT, SSD_WIDTH), lambda i: (i, 0)),
            pl.BlockSpec((TM_OUT, ATTN_WIDTH), lambda i: (i, 0)),
            full((SSD_WIDTH, D_MODEL)), full((ATTN_WIDTH, D_MODEL)),
            full((1, D_MODEL)), full((LANES, D_MODEL)), full((LANES, 1)),
        ],
        out_specs=[
            pl.BlockSpec((TM_OUT, D_MODEL), lambda i: (i, 0)),
            pl.BlockSpec((TM_OUT * TOKEN_ROWS, LANES), lambda i: (i, 0)),
            pl.BlockSpec((LANES, TM_OUT), lambda i: (0, i)),
        ],
        out_shape=[
            jax.ShapeDtypeStruct((n, D_MODEL), F32),
            jax.ShapeDtypeStruct((n * TOKEN_ROWS, LANES), U32),
            jax.ShapeDtypeStruct((LANES, n), F32),
        ],
        compiler_params=_cparams(("parallel",)),
        name="out_proj",
    )(x2, y_ssd, y_att, wo_s, wo_a, ln2_w, w_router_t, b_router)


def _rank_kernel(meta_ref, dest_ref, cnt_ref, run_ref, offs_ref):
    p = pl.program_id(0)
    i = pl.program_id(1)
    tm = meta_ref.shape[1]
    meta = meta_ref[...]
    row = lax.broadcasted_iota(I32, (N_EXPERTS, tm), 0).astype(F32)
    oh0 = (row == meta[0:1, :]).astype(F32)
    oh1 = (row == meta[1:2, :]).astype(F32)
    oh = oh0 + oh1
    rowsum = jnp.sum(oh, axis=1, keepdims=True)

    @pl.when((p == 0) & (i == 0))
    def _():
        run_ref[...] = jnp.zeros_like(run_ref)

    @pl.when(p == 0)
    def _():
        run_ref[...] = run_ref[...] + rowsum
        dest_ref[...] = jnp.zeros_like(dest_ref)
        cnt_ref[...] = run_ref[...]

    @pl.when((p == 1) & (i == 0))
    def _():
        cnt = run_ref[...]
        padded = jnp.ceil(cnt * (1.0 / TM_EXP)) * TM_EXP
        r = lax.broadcasted_iota(I32, (N_EXPERTS, N_EXPERTS), 0)
        c = lax.broadcasted_iota(I32, (N_EXPERTS, N_EXPERTS), 1)
        offs_ref[...] = jnp.dot((c < r).astype(F32), padded, precision=HIGHEST, preferred_element_type=F32)
        cnt_ref[...] = cnt
        run_ref[...] = jnp.zeros_like(run_ref)

    @pl.when(p == 1)
    def _():
        r = lax.broadcasted_iota(I32, (tm, tm), 0)
        c = lax.broadcasted_iota(I32, (tm, tm), 1)
        before = jnp.dot(oh.astype(BF16), (r < c).astype(BF16), preferred_element_type=F32)
        base = before + jnp.tile(run_ref[...] + offs_ref[...], (1, tm // LANES))
        d0 = jnp.sum(oh0 * base, axis=0, keepdims=True)
        d1 = jnp.sum(oh1 * base, axis=0, keepdims=True)
        drow = lax.broadcasted_iota(I32, (8, tm), 0)
        dest_ref[...] = jnp.where(drow == 0, d0, jnp.where(drow == 1, d1, 0.0)).astype(I32)
        run_ref[...] = run_ref[...] + rowsum


def _rank(meta_t):
    n = meta_t.shape[1]
    per_expert = lambda: pl.BlockSpec((N_EXPERTS, LANES), lambda p, i: (0, 0))
    return pl.pallas_call(
        _rank_kernel,
        grid=(2, n // TM_RANK),
        in_specs=[pl.BlockSpec((8, TM_RANK), lambda p, i: (0, i))],
        out_specs=[pl.BlockSpec((8, TM_RANK), lambda p, i: (0, i * p)), per_expert()],
        out_shape=[jax.ShapeDtypeStruct((8, n), I32), jax.ShapeDtypeStruct((N_EXPERTS, LANES), F32)],
        scratch_shapes=[pltpu.VMEM((N_EXPERTS, LANES), F32), pltpu.VMEM((N_EXPERTS, LANES), F32)],
        compiler_params=_cparams(("arbitrary", "arbitrary")),
        name="rank",
    )(meta_t)


def _token_copy(src_ref, src_token, dst_ref, dst_token, sem):
    src = src_ref.at[pl.ds(pl.multiple_of(src_token * TOKEN_ROWS, TOKEN_ROWS), TOKEN_ROWS)]
    dst = dst_ref.at[pl.ds(pl.multiple_of(dst_token * TOKEN_ROWS, TOKEN_ROWS), TOKEN_ROWS)]
    return pltpu.make_async_copy(src, dst, sem)


def _invert_kernel(d0_ref, d1_ref, valid_ref, inv_ref):
    n = d0_ref.shape[0]

    def clear_tile(tile, _):
        def clear(r, _):
            inv_ref[tile * TM_EXP + r] = 0
            return 0

        lax.fori_loop(valid_ref[tile], TM_EXP, clear, 0)
        return 0

    lax.fori_loop(0, valid_ref.shape[0], clear_tile, 0)

    def put(t, _):
        inv_ref[d0_ref[t]] = t
        inv_ref[d1_ref[t]] = n + t
        return 0

    lax.fori_loop(0, n, put, 0, unroll=8)


def _invert(dest0, dest1, tile_valid):
    smem = pl.BlockSpec(memory_space=pltpu.SMEM)
    return pl.pallas_call(
        _invert_kernel,
        in_specs=[smem, smem, smem],
        out_specs=smem,
        out_shape=jax.ShapeDtypeStruct((tile_valid.shape[0] * TM_EXP,), I32),
        name="invert",
    )(dest0, dest1, tile_valid)


CAST_ROWS = 256


def _cast_weight(src_ref, dst_ref):
    def body(i, _):
        rows = pl.ds(pl.multiple_of(i * CAST_ROWS, CAST_ROWS), CAST_ROWS)
        dst_ref[rows, :] = src_ref[rows, :].astype(BF16)
        return 0

    lax.fori_loop(0, src_ref.shape[0] // CAST_ROWS, body, 0)


S_EXPERT, S_FIRST, S_SLOT, S_NEXT, S_HAS_NEXT, S_USED, S_VALID = range(7)
WEIGHT_DMA_PRIORITY = 1


def _load_expert_weights(t, sched_ref, triples, sems):
    def copies(expert, slot):
        return [pltpu.make_async_copy(w.at[expert], stage.at[slot], sems.at[i, slot])
                for i, (w, stage, _) in enumerate(triples)]

    @pl.when(sched_ref[S_FIRST, t] == 1)
    def _():
        slot = sched_ref[S_SLOT, t]

        @pl.when(t == 0)
        def _():
            for cp in copies(sched_ref[S_EXPERT, t], slot):
                cp.start(priority=WEIGHT_DMA_PRIORITY)

        for cp in copies(sched_ref[S_EXPERT, t], slot):
            cp.wait()

        @pl.when(sched_ref[S_HAS_NEXT, t] == 1)
        def _():
            for cp in copies(sched_ref[S_NEXT, t], 1 - slot):
                cp.start(priority=WEIGHT_DMA_PRIORITY)

        for _, stage, dst in triples:
            _cast_weight(stage.at[slot], dst)


ROW_DMA_GROUPS = 4
GATHER_AHEAD = 3


def _tile_rows(sched_ref, tile, group):
    valid = sched_ref[S_VALID, tile]
    per = TM_EXP // ROW_DMA_GROUPS
    for r in range(group * per, (group + 1) * per):
        ok = r < valid
        yield r, jnp.where(ok, tile * TM_EXP + r, 0), ok


def _experts_up_kernel(sched_ref, inv_ref, hnp_hbm, wg_hbm, wu_hbm, hid_ref,
                       xbuf_ref, wg_stage, wu_stage, wgb_ref, wub_ref, wsems, gsem, *, n_tokens):
    t = pl.program_id(0)
    used = sched_ref[S_USED, 0]
    nbuf = GATHER_AHEAD + 1
    cur = t % nbuf
    ahead = (t + GATHER_AHEAD) % nbuf
    _load_expert_weights(t, sched_ref, [(wg_hbm, wg_stage, wgb_ref), (wu_hbm, wu_stage, wub_ref)], wsems)

    def start_rows(tile, buf, group):
        for r, slot, _ in _tile_rows(sched_ref, tile, group):
            token = inv_ref[slot] & (n_tokens - 1)
            _token_copy(hnp_hbm, token, xbuf_ref.at[buf], r, gsem.at[buf]).start()

    def wait_rows(buf):
        pltpu.make_async_copy(hnp_hbm.at[pl.ds(0, TM_EXP * TOKEN_ROWS)], xbuf_ref.at[buf], gsem.at[buf]).wait()

    for first in range(GATHER_AHEAD):
        @pl.when((t == 0) & (first < used))
        def _():
            for g in range(ROW_DMA_GROUPS):
                start_rows(first, first, g)

    def multiply(fetch_ahead):
        wait_rows(cur)
        c = D_MODEL // 2
        x_lo, x_hi = _unpack_halves(_load_token_rows(xbuf_ref.at[cur], TM_EXP))
        pieces = ((x_lo, wgb_ref, 0), (x_hi, wgb_ref, c), (x_lo, wub_ref, 0), (x_hi, wub_ref, c))
        acc = []
        for g, (x, w_ref, row0) in enumerate(pieces):
            if fetch_ahead:
                start_rows(t + GATHER_AHEAD, ahead, g)
            acc.append(jnp.dot(x, w_ref[row0:row0 + c, :], preferred_element_type=F32))
        hid_ref[...] = (_silu(acc[0] + acc[1]) * (acc[2] + acc[3])).astype(BF16)

    @pl.when(t + GATHER_AHEAD < used)
    def _():
        multiply(True)

    @pl.when((t + GATHER_AHEAD >= used) & (t < used))
    def _():
        multiply(False)

    @pl.when(t >= used)
    def _():
        hid_ref[...] = jnp.zeros_like(hid_ref)


def _experts_down_kernel(sched_ref, inv_ref, hid_ref, wd_hbm, y_hbm,
                         ybuf_ref, wd_stage, wdb_ref, wsems, ssem, *, n_tokens):
    t = pl.program_id(0)
    used = sched_ref[S_USED, 0]
    cur = t % 2
    _load_expert_weights(t, sched_ref, [(wd_hbm, wd_stage, wdb_ref)], wsems)

    def start_rows(tile, buf, group):
        for r, slot, ok in _tile_rows(sched_ref, tile, group):
            dst = jnp.where(ok, inv_ref[slot], TOP_K * n_tokens + r)
            _token_copy(ybuf_ref.at[buf], r, y_hbm, dst, ssem.at[buf]).start(priority=r % 2)

    def wait_rows(buf):
        pltpu.make_async_copy(ybuf_ref.at[buf], y_hbm.at[pl.ds(0, TM_EXP * TOKEN_ROWS)], ssem.at[buf]).wait()

    def multiply(send_previous):
        hid = hid_ref[...]
        q = D_MODEL // 4
        for half in range(2):
            if send_previous:
                start_rows(t - 1, 1 - cur, 2 * half)
            lo = jnp.dot(hid, wdb_ref[:, half * q:(half + 1) * q], preferred_element_type=F32)
            if send_previous:
                start_rows(t - 1, 1 - cur, 2 * half + 1)
            hi = jnp.dot(hid, wdb_ref[:, (half + 2) * q:(half + 3) * q], preferred_element_type=F32)
            _store_token_rows(ybuf_ref.at[cur], half * (q // LANES), _pack_halves(jnp.concatenate([lo, hi], axis=1)))

    @pl.when((t >= 2) & (t < used))
    def _():
        wait_rows(cur)

    @pl.when(t == 0)
    def _():
        ybuf_ref[1] = jnp.zeros(ybuf_ref.shape[1:], ybuf_ref.dtype)
        spare = pltpu.make_async_copy(ybuf_ref.at[1], y_hbm.at[pl.ds(TOP_K * n_tokens * TOKEN_ROWS, TM_EXP * TOKEN_ROWS)],
                                      ssem.at[1])
        spare.start()
        spare.wait()
        multiply(False)

    @pl.when((t > 0) & (t < used))
    def _():
        multiply(True)

    @pl.when(t == used - 1)
    def _():
        for g in range(ROW_DMA_GROUPS):
            start_rows(t, cur, g)

        @pl.when(t > 0)
        def _():
            wait_rows(1 - cur)

        wait_rows(cur)


def _tile_schedule(counts, n_tiles):
    tiles_per_expert = (counts + TM_EXP - 1) // TM_EXP
    tile_end = jnp.cumsum(tiles_per_expert)
    ids = jnp.arange(n_tiles, dtype=I32)
    raw = jnp.sum(ids[:, None] >= tile_end[None, :], axis=1)
    expert = jnp.minimum(raw, N_EXPERTS - 1).astype(I32)
    first = jnp.concatenate([jnp.ones((1,), I32), (expert[1:] != expert[:-1]).astype(I32)])
    slot = (jnp.cumsum(first) - 1) % 2
    start_idx = jnp.where(first == 1, ids, n_tiles)
    next_start = jnp.concatenate([lax.cummin(start_idx, reverse=True)[1:], jnp.full((1,), n_tiles, I32)])
    has_next = (next_start < n_tiles).astype(I32)
    next_expert = jnp.sum(jnp.where(ids[None, :] == next_start[:, None], expert[None, :], 0), axis=1)
    used = jnp.full((n_tiles,), tile_end[-1], I32)
    onehot = expert[:, None] == jnp.arange(N_EXPERTS)[None, :]
    tile_in_expert = ids - jnp.sum(jnp.where(onehot, (tile_end - tiles_per_expert)[None, :], 0), axis=1)
    rows_left = jnp.sum(jnp.where(onehot, counts[None, :], 0), axis=1) - tile_in_expert * TM_EXP
    valid = jnp.where(raw < N_EXPERTS, jnp.clip(rows_left, 0, TM_EXP), 0)
    return jnp.stack([expert, first, slot, next_expert, has_next, used, valid]).astype(I32)


def _experts(sched, inv, hnp, wg, wu, wd):
    n = hnp.shape[0] // TOKEN_ROWS
    n_slots = inv.shape[0]
    token_buf = lambda nbuf: pltpu.VMEM((nbuf, TM_EXP * TOKEN_ROWS, LANES), U32)
    row_spec = lambda width: pl.BlockSpec((TM_EXP, width), lambda t, sc, iv: (t, 0))
    hbm = pl.BlockSpec(memory_space=pl.ANY)
    big_vmem = pltpu.CompilerParams(dimension_semantics=("arbitrary",), vmem_limit_bytes=EXPERT_VMEM_LIMIT)
    hid = pl.pallas_call(
        functools.partial(_experts_up_kernel, n_tokens=n),
        grid_spec=pltpu.PrefetchScalarGridSpec(
            num_scalar_prefetch=2,
            grid=(n_slots // TM_EXP,),
            in_specs=[hbm, hbm, hbm],
            out_specs=row_spec(EXPERT_HIDDEN),
            scratch_shapes=[token_buf(GATHER_AHEAD + 1)]
                           + [pltpu.VMEM((2, D_MODEL, EXPERT_HIDDEN), F32)] * 2
                           + [pltpu.VMEM((D_MODEL, EXPERT_HIDDEN), BF16)] * 2
                           + [pltpu.SemaphoreType.DMA((2, 2)), pltpu.SemaphoreType.DMA((GATHER_AHEAD + 1,))],
        ),
        out_shape=jax.ShapeDtypeStruct((n_slots, EXPERT_HIDDEN), BF16),
        compiler_params=big_vmem,
        name="experts_up",
    )(sched, inv, hnp, wg, wu)
    return pl.pallas_call(
        functools.partial(_experts_down_kernel, n_tokens=n),
        grid_spec=pltpu.PrefetchScalarGridSpec(
            num_scalar_prefetch=2,
            grid=(n_slots // TM_EXP,),
            in_specs=[row_spec(EXPERT_HIDDEN), hbm],
            out_specs=hbm,
            scratch_shapes=[token_buf(2),
                            pltpu.VMEM((2, EXPERT_HIDDEN, D_MODEL), F32), pltpu.VMEM((EXPERT_HIDDEN, D_MODEL), BF16),
                            pltpu.SemaphoreType.DMA((1, 2)), pltpu.SemaphoreType.DMA((2,))],
        ),
        out_shape=jax.ShapeDtypeStruct(((TOP_K * n + TM_EXP) * TOKEN_ROWS, LANES), U32),
        compiler_params=big_vmem,
        name="experts_down",
    )(sched, inv, hid, wd)


def _combine_kernel(h_ref, meta_ref, y0_ref, y1_ref, o_ref):
    c = D_MODEL // 2
    meta = meta_ref[...].T
    w0, w1 = meta[:, 2:3], meta[:, 3:4]
    tm = h_ref.shape[0]
    lo0, hi0 = _unpack_halves(_load_token_rows(y0_ref, tm), F32)
    lo1, hi1 = _unpack_halves(_load_token_rows(y1_ref, tm), F32)
    o_ref[:, :c] = h_ref[:, :c] + (lo0 * w0 + lo1 * w1)
    o_ref[:, c:] = h_ref[:, c:] + (hi0 * w0 + hi1 * w1)


def _combine(h, meta, y):
    n = h.shape[0]
    nb = n // TM_COMB
    return pl.pallas_call(
        _combine_kernel,
        grid=(nb,),
        in_specs=[
            pl.BlockSpec((TM_COMB, D_MODEL), lambda i: (i, 0)),
            pl.BlockSpec((LANES, TM_COMB), lambda i: (0, i)),
            pl.BlockSpec((TM_COMB * TOKEN_ROWS, LANES), lambda i: (i, 0)),
            pl.BlockSpec((TM_COMB * TOKEN_ROWS, LANES), lambda i: (nb + i, 0)),
        ],
        out_specs=pl.BlockSpec((TM_COMB, D_MODEL), lambda i: (i, 0)),
        out_shape=jax.ShapeDtypeStruct((n, D_MODEL), F32),
        compiler_params=_cparams(("parallel",)),
        name="combine",
    )(h, meta, y, y)


def _lambda_init(layer_idx):
    return 0.8 - 0.6 * math.exp(-0.3 * layer_idx)


def _pad_lanes(v, width=LANES):
    return jnp.pad(v, ((0, 0), (0, width - v.shape[1])))


def _layer(l, x2, pos_rows, bsz, seq, ln1_w, w_in, conv_w, conv_b, dt_bias, a_log, d_skip, ssd_norm_w,
           q_norm_w, k_norm_w, lambda_q1, lambda_k1, lambda_q2, lambda_k2, subln_w, w_out, ln2_w,
           w_router_group, b_router_group, w_router_expert, b_router_expert, w_gate, w_up, w_down):
    n = x2.shape[0]
    c_z, c_xbc, c_dt = SSD_WIDTH, SSD_WIDTH + SSD_CONV_DIM, SSD_WIDTH + SSD_CONV_DIM + SSD_HEADS
    c_q, c_k = c_dt + ATTN_WIDTH, c_dt + 2 * ATTN_WIDTH
    w_main = jnp.concatenate([w_in[:, :c_z], w_in[:, c_dt:c_q], w_in[:, c_q:c_k], w_in[:, c_z:c_xbc]],
                             axis=1).astype(BF16)
    w_vt = w_in[:, c_k:].T.astype(BF16)
    w_dt = _pad_lanes(w_in[:, c_xbc:c_dt]).astype(BF16)

    inv_freq = jnp.power(ROPE_THETA, -jnp.arange(0, ROPE_DIM, 2, dtype=F32) / ROPE_DIM)
    invf_col = jnp.concatenate([inv_freq, jnp.zeros_like(inv_freq)])[:, None]
    seg_ones = (jnp.arange(LANES)[:, None] // ATTN_QK_DIM == jnp.arange(LANES)[None, :] // ATTN_QK_DIM).astype(BF16)
    q_scale = math.log2(math.e) / math.sqrt(ATTN_QK_DIM)
    u, v_t, dt_raw = _in_proj(x2, ln1_w[None, :], w_main, w_vt, w_dt, pos_rows, invf_col,
                              jnp.tile(q_norm_w, 2)[None, :] * q_scale, jnp.tile(k_norm_w, 2)[None, :],
                              seg_ones, _rope_spread())

    a_neg = _pad_lanes(-jnp.exp(a_log.astype(F32))[None, :])
    y_ssd = _ssd(u, dt_raw, conv_w, conv_b[None, :], _pad_lanes(dt_bias[None, :]), a_neg,
                 jnp.repeat(d_skip, SSD_HEAD_DIM)[None, :], ssd_norm_w[None, :], bsz, seq)

    lam_vecs = jnp.stack([lambda_q1, lambda_k1, lambda_q2, lambda_k2]).astype(F32)
    y_att = _attention(u, v_t, lam_vecs, subln_w[:, None], bsz, seq, _lambda_init(l))

    w_out_b = w_out.astype(BF16)
    w_router_t = _pad_lanes(jnp.concatenate([w_router_group, w_router_expert], axis=1)).T.astype(BF16)
    b_router = _pad_lanes(jnp.concatenate([b_router_group, b_router_expert])[None, :]).T
    h, hnp, meta = _out_proj(x2, y_ssd, y_att, w_out_b[:SSD_WIDTH], w_out_b[SSD_WIDTH:], ln2_w[None, :],
                             w_router_t, b_router)

    dest, cnt = _rank(meta)
    dest0, dest1 = dest[0], dest[1]
    counts = cnt[:, 0].astype(I32)
    n_slots = (n * TOP_K + N_EXPERTS * (TM_EXP - 1)) // TM_EXP * TM_EXP
    sched = _tile_schedule(counts, n_slots // TM_EXP)

    inv = _invert(dest0, dest1, sched[S_VALID])
    y = _experts(sched, inv, hnp, w_gate, w_up, w_down)
    return _combine(h, meta, y)


def kernel(x, positions, ln1_w, w_in, conv_w, conv_b, dt_bias, a_log, d_skip, ssd_norm_w, q_norm_w, k_norm_w,
           lambda_q1, lambda_k1, lambda_q2, lambda_k2, subln_w, w_out, ln2_w, w_router_group, b_router_group,
           w_router_expert, b_router_expert, w_gate, w_up, w_down):
    bsz, seq, d = x.shape
    assert d == D_MODEL and seq % TQ == 0 and (bsz * seq) % TM_IN == 0
    x2 = x.reshape(bsz * seq, d)
    pos_rows = jnp.broadcast_to(positions.astype(F32).reshape(1, bsz * seq), (8, bsz * seq))
    params = (ln1_w, w_in, conv_w, conv_b, dt_bias, a_log, d_skip, ssd_norm_w, q_norm_w, k_norm_w,
              lambda_q1, lambda_k1, lambda_q2, lambda_k2, subln_w, w_out, ln2_w, w_router_group, b_router_group,
              w_router_expert, b_router_expert, w_gate, w_up, w_down)
    for l in range(ln1_w.shape[0]):
        x2 = _layer(l, x2, pos_rows, bsz, seq, *[p[l] for p in params])
    return x2.reshape(bsz, seq, d)
```

```python
import functools
import math

import jax
import jax.numpy as jnp
from jax import lax
from jax.experimental import pallas as pl
from jax.experimental.pallas import tpu as pltpu

F32 = jnp.float32
BF16 = jnp.bfloat16
I32 = jnp.int32
U32 = jnp.uint32
HIGHEST = lax.Precision.HIGHEST

D_MODEL = 2048
SSD_WIDTH = 1024
ATTN_WIDTH = 1024
SSD_HEAD_DIM = 64
SSD_HEADS = 16
SSD_GROUPS = 2
SSD_HEADS_PER_GROUP = SSD_HEADS // SSD_GROUPS
SSD_STATE = 128
SSD_CONV = 4
SSD_CHUNK = 128
SSD_CONV_DIM = SSD_WIDTH + 2 * SSD_GROUPS * SSD_STATE
ATTN_V_DIM = 128
ATTN_HEADS = 8
ATTN_QK_DIM = 64
ROPE_THETA = 500000.0
ROPE_DIM = 16
N_EXPERT_GROUPS = 4
EXPERTS_PER_GROUP = 8
N_EXPERTS = 32
TOP_K = 2
EXPERT_HIDDEN = 1024
EPS = 1e-6

LANES = 128
NEG_INF = float("-inf")

TM_IN = 512
TQ = 256
ATTN_HB = 8
ONES_ROWS = 16
TM_OUT = 512
TM_RANK = 512
TM_EXP = 256
TM_COMB = 512
ROUTER_ROWS = 40
Q_COL = SSD_WIDTH
K_COL = Q_COL + ATTN_WIDTH
XBC_COL = K_COL + ATTN_WIDTH
U_COLS = XBC_COL + SSD_CONV_DIM
VMEM_LIMIT = 52 * 1024 * 1024
EXPERT_VMEM_LIMIT = 58 * 1024 * 1024


def _cparams(sem):
    return pltpu.CompilerParams(dimension_semantics=sem, vmem_limit_bytes=VMEM_LIMIT)


def _silu(x):
    return x * (1.0 / (1.0 + jnp.exp(-x)))


def _softplus(x):
    return jnp.maximum(x, 0.0) + jnp.log(1.0 + jnp.exp(-jnp.abs(x)))


def _rope_tables(pos_ref, invf_ref, spread_ref):
    tm = pos_ref.shape[1]
    ang_t = invf_ref[...] * pos_ref[0:1, :]
    trig = jnp.concatenate([jnp.cos(ang_t), jnp.sin(ang_t), jnp.zeros((LANES - 2 * ROPE_DIM, tm), F32)], axis=0).T
    t1 = trig.astype(BF16)
    r1 = trig - t1.astype(F32)
    t2 = r1.astype(BF16)
    t3 = (r1 - t2.astype(F32)).astype(BF16)
    spread = spread_ref[...]
    tab = (jnp.dot(t1, spread, preferred_element_type=F32) + jnp.dot(t2, spread, preferred_element_type=F32)
           + jnp.dot(t3, spread, preferred_element_type=F32))
    return tab[:, :LANES], tab[:, LANES:2 * LANES], tab[:, 2 * LANES:]


def _norm_rope(x, w, tables, seg_ones):
    cs, s_lo, s_hi = tables
    half = ROPE_DIM // 2
    ss = jnp.dot((x * x).astype(BF16), seg_ones, preferred_element_type=F32)
    xn = x * lax.rsqrt(ss * (1.0 / ATTN_QK_DIM) + EPS) * w
    return xn * cs + pltpu.roll(xn, LANES - half, 1) * s_lo + pltpu.roll(xn, half, 1) * s_hi


def _inproj_kernel(x_ref, lnw_ref, w_ref, wvt_ref, wdt_ref, pos_ref, invf_ref, qw_ref, kw_ref, ones_ref, spread_ref,
                   u_ref, vt_ref, dt_ref):
    x = x_ref[...]
    ms = jnp.mean(x * x, axis=-1, keepdims=True)
    xn = (x * lax.rsqrt(ms + EPS) * lnw_ref[...]).astype(BF16)
    tables = _rope_tables(pos_ref, invf_ref, spread_ref)
    for c0, c1, head_w in ((0, SSD_WIDTH, None), (Q_COL, K_COL, qw_ref), (K_COL, XBC_COL, kw_ref), (XBC_COL, U_COLS, None)):
        acc = jnp.dot(xn, w_ref[:, c0:c1], preferred_element_type=F32)
        if head_w is None:
            u_ref[:, c0:c1] = acc.astype(BF16)
        else:
            for hb in range(ATTN_HEADS):
                blk = _norm_rope(acc[:, hb * LANES:(hb + 1) * LANES], head_w[...], tables, ones_ref[...])
                u_ref[:, c0 + hb * LANES:c0 + (hb + 1) * LANES] = blk.astype(BF16)
    dt_ref[...] = jnp.dot(xn, wdt_ref[...], preferred_element_type=F32)
    vt_ref[...] = lax.dot_general(wvt_ref[...], xn, (((1,), (1,)), ((), ())),
                                  preferred_element_type=F32).astype(BF16)


def _in_proj(x2, ln_w, w_main, w_vt, w_dt, pos_rows, invf_col, qw_lanes, kw_lanes, seg_ones, spread):
    n = x2.shape[0]
    resident = lambda shape: pl.BlockSpec(shape, lambda i: (0, 0), pipeline_mode=pl.Buffered(1))
    return pl.pallas_call(
        _inproj_kernel,
        grid=(n // TM_IN,),
        in_specs=[
            pl.BlockSpec((TM_IN, D_MODEL), lambda i: (i, 0)),
            resident((1, D_MODEL)),
            resident((D_MODEL, U_COLS)),
            resident((ATTN_WIDTH, D_MODEL)),
            resident((D_MODEL, LANES)),
            pl.BlockSpec((8, TM_IN), lambda i: (0, i)),
            resident((ROPE_DIM, 1)), resident((1, LANES)), resident((1, LANES)), resident((LANES, LANES)),
            resident((LANES, 3 * LANES)),
        ],
        out_specs=[
            pl.BlockSpec((TM_IN, U_COLS), lambda i: (i, 0)),
            pl.BlockSpec((ATTN_WIDTH, TM_IN), lambda i: (0, i)),
            pl.BlockSpec((TM_IN, LANES), lambda i: (i, 0)),
        ],
        out_shape=[
            jax.ShapeDtypeStruct((n, U_COLS), BF16),
            jax.ShapeDtypeStruct((ATTN_WIDTH, n), BF16),
            jax.ShapeDtypeStruct((n, LANES), F32),
        ],
        compiler_params=_cparams(("parallel",)),
        name="in_proj",
    )(x2, ln_w, w_main, w_vt, w_dt, pos_rows, invf_col, qw_lanes, kw_lanes, seg_ones, spread)


def _ssd_kernel(z_ref, xbc_ref, dt_ref, convw_ref, convb_ref, dtb_ref, aneg_ref, dskip_ref, normw_ref,
                y_ref, xp_ref, st_ref, yacc_ref):
    L = SSD_CHUNK
    P = SSD_HEAD_DIM

    @pl.when(pl.program_id(1) == 0)
    def _():
        xp_ref[0:8, :] = jnp.zeros((8, SSD_CONV_DIM), F32)
        st_ref[...] = jnp.zeros_like(st_ref)

    xp_ref[8:8 + L, :] = xbc_ref[...].astype(F32)
    acc = jnp.broadcast_to(convb_ref[...], (L, SSD_CONV_DIM))
    for k in range(SSD_CONV):
        acc = acc + xp_ref[5 + k:5 + k + L, :] * convw_ref[k:k + 1, :]
    xp_ref[0:8, :] = xp_ref[L:L + 8, :]
    xc = _silu(acc)

    dt = _softplus(dt_ref[...] + dtb_ref[...])
    a = dt * aneg_ref[...]
    row = lax.broadcasted_iota(I32, (L, L), 0)
    col = lax.broadcasted_iota(I32, (L, L), 1)
    causal = row >= col
    a_cs = jnp.dot(causal.astype(F32), a, precision=HIGHEST, preferred_element_type=F32)
    a_last = a_cs[L - 1:L, :]
    ea = jnp.exp(a_cs)
    dsdt = jnp.exp(a_last - a_cs) * dt
    cd = jnp.exp(a_last)
    a_cs_t = a_cs.T
    dt_t = dt.T
    dsdt_t = dsdt.T

    for g in range(SSD_GROUPS):
        b_g = xc[:, SSD_WIDTH + g * SSD_STATE:SSD_WIDTH + (g + 1) * SSD_STATE]
        c_off = SSD_WIDTH + SSD_GROUPS * SSD_STATE
        c_g = xc[:, c_off + g * SSD_STATE:c_off + (g + 1) * SSD_STATE]
        cb = lax.dot_general(c_g.astype(BF16), b_g.astype(BF16), (((1,), (1,)), ((), ())),
                             preferred_element_type=F32)
        b_gt = b_g.T
        for hh in range(SSD_HEADS_PER_GROUP):
            h = g * SSD_HEADS_PER_GROUP + hh
            xs_h = xc[:, h * P:(h + 1) * P].astype(BF16)
            seg = a_cs[:, h:h + 1] - a_cs_t[h:h + 1, :]
            dec = jnp.exp(jnp.where(causal, seg, NEG_INF))
            m = (cb * dec * dt_t[h:h + 1, :]).astype(BF16)
            c_s = (c_g * ea[:, h:h + 1]).astype(BF16)
            s_prev = st_ref[h]
            lhs = jnp.concatenate([m, c_s], axis=1)
            rhs = jnp.concatenate([xs_h, s_prev.astype(BF16)], axis=0)
            yacc_ref[:, h * P:(h + 1) * P] = jnp.dot(lhs, rhs, preferred_element_type=F32)
            bw = (b_gt * dsdt_t[h:h + 1, :]).astype(BF16)
            st_ref[h] = s_prev * cd[:, h:h + 1] + jnp.dot(bw, xs_h, preferred_element_type=F32)

    y = yacc_ref[...] + xc[:, :SSD_WIDTH] * dskip_ref[...]
    y = y * _silu(z_ref[...].astype(F32))
    gw = SSD_WIDTH // SSD_GROUPS
    for g in range(SSD_GROUPS):
        yg = y[:, g * gw:(g + 1) * gw]
        ms = jnp.mean(yg * yg, axis=-1, keepdims=True)
        y_ref[:, g * gw:(g + 1) * gw] = (yg * lax.rsqrt(ms + EPS) * normw_ref[:, g * gw:(g + 1) * gw]).astype(BF16)


def _ssd(u, dt_raw, conv_w, conv_b, dt_bias, a_neg, dskip_lanes, norm_w, bsz, seq):
    n = u.shape[0]
    nc = seq // SSD_CHUNK
    xbc_blk = (SSD_WIDTH + 2 * ATTN_WIDTH) // SSD_CONV_DIM
    full = lambda shape: pl.BlockSpec(shape, lambda b, c: (0, 0))
    return pl.pallas_call(
        _ssd_kernel,
        grid=(bsz, nc),
        in_specs=[
            pl.BlockSpec((SSD_CHUNK, SSD_WIDTH), lambda b, c: (b * nc + c, 0)),
            pl.BlockSpec((SSD_CHUNK, SSD_CONV_DIM), lambda b, c: (b * nc + c, xbc_blk)),
            pl.BlockSpec((SSD_CHUNK, LANES), lambda b, c: (b * nc + c, 0)),
            full((SSD_CONV, SSD_CONV_DIM)),
            full((1, SSD_CONV_DIM)),
            full((1, LANES)),
            full((1, LANES)),
            full((1, SSD_WIDTH)),
            full((1, SSD_WIDTH)),
        ],
        out_specs=pl.BlockSpec((SSD_CHUNK, SSD_WIDTH), lambda b, c: (b * nc + c, 0)),
        out_shape=jax.ShapeDtypeStruct((n, SSD_WIDTH), BF16),
        scratch_shapes=[
            pltpu.VMEM((SSD_CHUNK + 8, SSD_CONV_DIM), F32),
            pltpu.VMEM((SSD_HEADS, SSD_STATE, SSD_HEAD_DIM), F32),
            pltpu.VMEM((SSD_CHUNK, SSD_WIDTH), F32),
        ],
        compiler_params=_cparams(("parallel", "arbitrary")),
        name="ssd",
    )(u, u, dt_raw, conv_w, conv_b, dt_bias, a_neg, dskip_lanes, norm_w)


def _rope_spread():
    half = ROPE_DIM // 2
    lane = jnp.arange(LANES)
    d = lane % ATTN_QK_DIM
    src = jnp.arange(LANES)[:, None]
    cos_src = jnp.where(d < ROPE_DIM, d % half, half)
    cos_tab = (src == cos_src[None, :]).astype(F32)
    lo_tab = -((src == (2 * half + d)[None, :]) & (d < half)[None, :]).astype(F32)
    hi_tab = ((src == (2 * half + d - half)[None, :]) & ((d >= half) & (d < ROPE_DIM))[None, :]).astype(F32)
    return jnp.concatenate([cos_tab, lo_tab, hi_tab], axis=1).astype(BF16)


def _attn_kernel(q_ref, k_ref, vt_ref, lamv_ref, subw_ref, o_ref, acc_ref, *, lam_init):
    qi = pl.program_id(2)
    lane = lax.broadcasted_iota(I32, (TQ, LANES), 1)
    qs = []
    for hb in range(ATTN_HB):
        q = q_ref[:, hb * LANES:(hb + 1) * LANES]
        zero = jnp.zeros_like(q)
        qs.append(jnp.concatenate([jnp.where(lane < ATTN_QK_DIM, q, zero),
                                   jnp.where(lane >= ATTN_QK_DIM, q, zero)], axis=0))
    acc_ref[...] = jnp.zeros_like(acc_ref)
    kv_idx = lax.broadcasted_iota(I32, (TQ, 2 * TQ), 0)
    q_idx = lax.broadcasted_iota(I32, (TQ, 2 * TQ), 1) & (TQ - 1)
    nt = (((1,), (1,)), ((), ()))

    def block(j, carry, masked):
        off = pl.multiple_of(j * TQ, TQ)
        ss = []
        for hb in range(ATTN_HB):
            kb = k_ref[pl.ds(off, TQ), hb * LANES:(hb + 1) * LANES]
            ss.append(lax.dot_general(kb, qs[hb], nt, preferred_element_type=F32))
        new, ps, alphas = [], [], []
        for hb in range(ATTN_HB):
            m_old = carry[hb]
            s = ss[hb]
            if masked:
                s = jnp.where(kv_idx <= q_idx, s, NEG_INF)
            m_new = jnp.maximum(m_old, jnp.max(s, axis=0, keepdims=True))
            alphas.append(jnp.exp2(m_old - m_new))
            ps.append(jnp.exp2(s - m_new).astype(BF16))
            new.append(m_new)
        pvs = []
        for hb in range(ATTN_HB):
            vb = jnp.concatenate([vt_ref[hb * LANES:(hb + 1) * LANES, pl.ds(off, TQ)], ones_rows], axis=0)
            pvs.append(jnp.dot(vb, ps[hb], preferred_element_type=F32))
        for hb in range(ATTN_HB):
            acc_ref[hb] = alphas[hb] * acc_ref[hb] + pvs[hb]
        return tuple(new)

    ones_rows = jnp.ones((ONES_ROWS, TQ), BF16)
    init = (jnp.full((1, 2 * TQ), NEG_INF, F32),) * ATTN_HB
    carry = lax.fori_loop(0, qi, lambda j, cr: block(j, cr, False), init)
    block(qi, carry, True)

    lv = lamv_ref[...]
    lam = (jnp.exp(jnp.sum(lv[0:1] * lv[1:2], axis=1, keepdims=True))
           - jnp.exp(jnp.sum(lv[2:3] * lv[3:4], axis=1, keepdims=True)) + lam_init)
    for hb in range(ATTN_HB):
        acc = acc_ref[hb]
        o2 = acc[:ATTN_V_DIM] * (1.0 / acc[ATTN_V_DIM:ATTN_V_DIM + 1])
        o_t = o2[:, :TQ] - lam * o2[:, TQ:]
        ms = jnp.mean(o_t * o_t, axis=0, keepdims=True)
        o_t = o_t * lax.rsqrt(ms + EPS) * subw_ref[...] * (1.0 - lam_init)
        o_ref[:, hb * LANES:(hb + 1) * LANES] = o_t.T.astype(BF16)


def _attention(u, v_t, lam_vecs, subw_col, bsz, seq, lam_init):
    n = u.shape[0]
    nq = seq // TQ
    w = ATTN_HB * ATTN_V_DIM
    q_blk, k_blk = Q_COL // w, K_COL // w
    return pl.pallas_call(
        functools.partial(_attn_kernel, lam_init=lam_init),
        grid=(bsz, ATTN_HEADS // ATTN_HB, nq),
        in_specs=[
            pl.BlockSpec((TQ, w), lambda b, h, i: (b * nq + i, q_blk + h)),
            pl.BlockSpec((seq, w), lambda b, h, i: (b, k_blk + h)),
            pl.BlockSpec((w, seq), lambda b, h, i: (h, b)),
            pl.BlockSpec((4, ATTN_QK_DIM), lambda b, h, i: (0, 0)),
            pl.BlockSpec((ATTN_V_DIM, 1), lambda b, h, i: (0, 0)),
        ],
        out_specs=pl.BlockSpec((TQ, w), lambda b, h, i: (b * nq + i, h)),
        out_shape=jax.ShapeDtypeStruct((n, ATTN_WIDTH), BF16),
        scratch_shapes=[pltpu.VMEM((ATTN_HB, ATTN_V_DIM + ONES_ROWS, 2 * TQ), F32)],
        compiler_params=_cparams(("parallel", "parallel", "arbitrary")),
        name="attn",
    )(u, u, v_t, lam_vecs, subw_col)


def _pack_halves(x):
    c = x.shape[1] // 2
    lo = pltpu.bitcast(x[:, :c].astype(BF16).astype(F32), U32) >> 16
    hi = pltpu.bitcast(x[:, c:].astype(BF16).astype(F32), U32) & jnp.uint32(0xFFFF0000)
    return hi | lo


TOKEN_ROWS = D_MODEL // 2 // LANES


def _store_token_rows(ref, first_piece, packed):
    tm = packed.shape[0]
    for j in range(packed.shape[1] // LANES):
        ref[pl.ds(first_piece + j, tm, stride=TOKEN_ROWS), :] = packed[:, j * LANES:(j + 1) * LANES]


def _load_token_rows(ref, tm):
    return jnp.concatenate([ref[pl.ds(s, tm, stride=TOKEN_ROWS), :] for s in range(TOKEN_ROWS)], axis=1)


def _unpack_halves(w, dtype=BF16):
    lo = pltpu.bitcast(w << 16, F32).astype(dtype)
    hi = pltpu.bitcast(w & jnp.uint32(0xFFFF0000), F32).astype(dtype)
    return lo, hi


def _outproj_kernel(x_ref, ys_ref, ya_ref, wos_ref, woa_ref, ln2_ref, wr_ref, br_ref,
                    h_ref, hnp_ref, meta_ref):
    tm = x_ref.shape[0]
    h = (x_ref[...]
         + jnp.dot(ys_ref[...], wos_ref[...], preferred_element_type=F32)
         + jnp.dot(ya_ref[...], woa_ref[...], preferred_element_type=F32))
    h_ref[...] = h
    ms = jnp.mean(h * h, axis=-1, keepdims=True)
    hn = h * lax.rsqrt(ms + EPS) * ln2_ref[...]
    _store_token_rows(hnp_ref, 0, _pack_halves(hn))

    lg_t = lax.dot_general(wr_ref[...], hn.astype(BF16), (((1,), (1,)), ((), ())),
                           preferred_element_type=F32)
    lg = lg_t[0:ROUTER_ROWS, :] + br_ref[0:ROUTER_ROWS, :]
    row = lax.broadcasted_iota(I32, (ROUTER_ROWS, tm), 0).astype(F32)
    big = float(LANES)
    gl = jnp.where(row < N_EXPERT_GROUPS, lg, NEG_INF)
    gmax = jnp.max(gl, axis=0, keepdims=True)
    gsel = jnp.min(jnp.where(gl == gmax, row, big), axis=0, keepdims=True)
    g_w = 1.0 / jnp.sum(jnp.exp(gl - gmax), axis=0, keepdims=True)
    eid = row - N_EXPERT_GROUPS
    lo = gsel * EXPERTS_PER_GROUP
    emask = (eid >= lo) & (eid < lo + EXPERTS_PER_GROUP)
    el = jnp.where(emask, lg, NEG_INF)
    m1 = jnp.max(el, axis=0, keepdims=True)
    i1 = jnp.min(jnp.where(el == m1, eid, big), axis=0, keepdims=True)
    el2 = jnp.where(eid == i1, NEG_INF, el)
    m2 = jnp.max(el2, axis=0, keepdims=True)
    i2 = jnp.min(jnp.where(el2 == m2, eid, big), axis=0, keepdims=True)
    e2 = jnp.exp(m2 - m1)
    w1 = g_w / (1.0 + e2)
    w2 = g_w * e2 / (1.0 + e2)
    mrow = lax.broadcasted_iota(I32, (LANES, tm), 0)
    meta_ref[...] = jnp.where(mrow == 0, i1, jnp.where(mrow == 1, i2, jnp.where(mrow == 2, w1, jnp.where(mrow == 3, w2, 0.0))))


def _out_proj(x2, y_ssd, y_att, wo_s, wo_a, ln2_w, w_router_t, b_router):
    n = x2.shape[0]
    full = lambda shape: pl.BlockSpec(shape, lambda i: (0, 0), pipeline_mode=pl.Buffered(1))
    return pl.pallas_call(
        _outproj_kernel,
        grid=(n // TM_OUT,),
        in_specs=[
            pl.BlockSpec((TM_OUT, D_MODEL), lambda i: (i, 0)),
            pl.BlockSpec((TM_OUT, SSD_WIDTH), lambda i: (i, 0)),
            pl.BlockSpec((TM_OUT, ATTN_WIDTH), lambda i: (i, 0)),
            full((SSD_WIDTH, D_MODEL)), full((ATTN_WIDTH, D_MODEL)),
            full((1, D_MODEL)), full((LANES, D_MODEL)), full((LANES, 1)),
        ],
        out_specs=[
            pl.BlockSpec((TM_OUT, D_MODEL), lambda i: (i, 0)),
            pl.BlockSpec((TM_OUT * TOKEN_ROWS, LANES), lambda i: (i, 0)),
            pl.BlockSpec((LANES, TM_OUT), lambda i: (0, i)),
        ],
        out_shape=[
            jax.ShapeDtypeStruct((n, D_MODEL), F32),
            jax.ShapeDtypeStruct((n * TOKEN_ROWS, LANES), U32),
            jax.ShapeDtypeStruct((LANES, n), F32),
        ],
        compiler_params=_cparams(("parallel",)),
        name="out_proj",
    )(x2, y_ssd, y_att, wo_s, wo_a, ln2_w, w_router_t, b_router)


def _rank_kernel(meta_ref, dest_ref, cnt_ref, run_ref, offs_ref):
    p = pl.program_id(0)
    i = pl.program_id(1)
    tm = meta_ref.shape[1]
    meta = meta_ref[...]
    row = lax.broadcasted_iota(I32, (N_EXPERTS, tm), 0).astype(F32)
    oh0 = (row == meta[0:1, :]).astype(F32)
    oh1 = (row == meta[1:2, :]).astype(F32)
    oh = oh0 + oh1
    rowsum = jnp.sum(oh, axis=1, keepdims=True)

    @pl.when((p == 0) & (i == 0))
    def _():
        run_ref[...] = jnp.zeros_like(run_ref)

    @pl.when(p == 0)
    def _():
        run_ref[...] = run_ref[...] + rowsum
        dest_ref[...] = jnp.zeros_like(dest_ref)
        cnt_ref[...] = run_ref[...]

    @pl.when((p == 1) & (i == 0))
    def _():
        cnt = run_ref[...]
        padded = jnp.ceil(cnt * (1.0 / TM_EXP)) * TM_EXP
        r = lax.broadcasted_iota(I32, (N_EXPERTS, N_EXPERTS), 0)
        c = lax.broadcasted_iota(I32, (N_EXPERTS, N_EXPERTS), 1)
        offs_ref[...] = jnp.dot((c < r).astype(F32), padded, precision=HIGHEST, preferred_element_type=F32)
        cnt_ref[...] = cnt
        run_ref[...] = jnp.zeros_like(run_ref)

    @pl.when(p == 1)
    def _():
        r = lax.broadcasted_iota(I32, (tm, tm), 0)
        c = lax.broadcasted_iota(I32, (tm, tm), 1)
        before = jnp.dot(oh.astype(BF16), (r < c).astype(BF16), preferred_element_type=F32)
        base = before + jnp.tile(run_ref[...] + offs_ref[...], (1, tm // LANES))
        d0 = jnp.sum(oh0 * base, axis=0, keepdims=True)
        d1 = jnp.sum(oh1 * base, axis=0, keepdims=True)
        drow = lax.broadcasted_iota(I32, (8, tm), 0)
        dest_ref[...] = jnp.where(drow == 0, d0, jnp.where(drow == 1, d1, 0.0)).astype(I32)
        run_ref[...] = run_ref[...] + rowsum


def _rank(meta_t):
    n = meta_t.shape[1]
    per_expert = lambda: pl.BlockSpec((N_EXPERTS, LANES), lambda p, i: (0, 0))
    return pl.pallas_call(
        _rank_kernel,
        grid=(2, n // TM_RANK),
        in_specs=[pl.BlockSpec((8, TM_RANK), lambda p, i: (0, i))],
        out_specs=[pl.BlockSpec((8, TM_RANK), lambda p, i: (0, i * p)), per_expert()],
        out_shape=[jax.ShapeDtypeStruct((8, n), I32), jax.ShapeDtypeStruct((N_EXPERTS, LANES), F32)],
        scratch_shapes=[pltpu.VMEM((N_EXPERTS, LANES), F32), pltpu.VMEM((N_EXPERTS, LANES), F32)],
        compiler_params=_cparams(("arbitrary", "arbitrary")),
        name="rank",
    )(meta_t)


def _token_copy(src_ref, src_token, dst_ref, dst_token, sem):
    src = src_ref.at[pl.ds(pl.multiple_of(src_token * TOKEN_ROWS, TOKEN_ROWS), TOKEN_ROWS)]
    dst = dst_ref.at[pl.ds(pl.multiple_of(dst_token * TOKEN_ROWS, TOKEN_ROWS), TOKEN_ROWS)]
    return pltpu.make_async_copy(src, dst, sem)


def _invert_kernel(d0_ref, d1_ref, valid_ref, inv_ref):
    n = d0_ref.shape[0]

    def clear_tile(tile, _):
        def clear(r, _):
            inv_ref[tile * TM_EXP + r] = 0
            return 0

        lax.fori_loop(valid_ref[tile], TM_EXP, clear, 0)
        return 0

    lax.fori_loop(0, valid_ref.shape[0], clear_tile, 0)

    def put(t, _):
        inv_ref[d0_ref[t]] = t
        inv_ref[d1_ref[t]] = n + t
        return 0

    lax.fori_loop(0, n, put, 0, unroll=8)


def _invert(dest0, dest1, tile_valid):
    smem = pl.BlockSpec(memory_space=pltpu.SMEM)
    return pl.pallas_call(
        _invert_kernel,
        in_specs=[smem, smem, smem],
        out_specs=smem,
        out_shape=jax.ShapeDtypeStruct((tile_valid.shape[0] * TM_EXP,), I32),
        name="invert",
    )(dest0, dest1, tile_valid)


CAST_ROWS = 256


def _cast_weight(src_ref, dst_ref):
    def body(i, _):
        rows = pl.ds(pl.multiple_of(i * CAST_ROWS, CAST_ROWS), CAST_ROWS)
        dst_ref[rows, :] = src_ref[rows, :].astype(BF16)
        return 0

    lax.fori_loop(0, src_ref.shape[0] // CAST_ROWS, body, 0)


S_EXPERT, S_FIRST, S_SLOT, S_NEXT, S_HAS_NEXT, S_USED, S_VALID = range(7)
WEIGHT_DMA_PRIORITY = 1


def _load_expert_weights(t, sched_ref, triples, sems):
    def copies(expert, slot):
        return [pltpu.make_async_copy(w.at[expert], stage.at[slot], sems.at[i, slot])
                for i, (w, stage, _) in enumerate(triples)]

    @pl.when(sched_ref[S_FIRST, t] == 1)
    def _():
        slot = sched_ref[S_SLOT, t]

        @pl.when(t == 0)
        def _():
            for i, cp in enumerate(copies(sched_ref[S_EXPERT, t], slot)):
                cp.start(priority=(WEIGHT_DMA_PRIORITY + i) % 2)

        for cp in copies(sched_ref[S_EXPERT, t], slot):
            cp.wait()

        @pl.when(sched_ref[S_HAS_NEXT, t] == 1)
        def _():
            for i, cp in enumerate(copies(sched_ref[S_NEXT, t], 1 - slot)):
                cp.start(priority=(WEIGHT_DMA_PRIORITY + i) % 2)

        for _, stage, dst in triples:
            _cast_weight(stage.at[slot], dst)


ROW_DMA_GROUPS = 4
GATHER_AHEAD = 3


def _tile_rows(sched_ref, tile, group):
    valid = sched_ref[S_VALID, tile]
    per = TM_EXP // ROW_DMA_GROUPS
    for r in range(group * per, (group + 1) * per):
        ok = r < valid
        yield r, jnp.where(ok, tile * TM_EXP + r, 0), ok


def _experts_up_kernel(sched_ref, inv_ref, hnp_hbm, wg_hbm, wu_hbm, hid_ref,
                       xbuf_ref, wg_stage, wu_stage, wgb_ref, wub_ref, wsems, gsem, *, n_tokens):
    t = pl.program_id(0)
    used = sched_ref[S_USED, 0]
    nbuf = GATHER_AHEAD + 1
    cur = t % nbuf
    ahead = (t + GATHER_AHEAD) % nbuf
    _load_expert_weights(t, sched_ref, [(wg_hbm, wg_stage, wgb_ref), (wu_hbm, wu_stage, wub_ref)], wsems)

    def start_rows(tile, buf, group):
        for r, slot, _ in _tile_rows(sched_ref, tile, group):
            token = inv_ref[slot] & (n_tokens - 1)
            _token_copy(hnp_hbm, token, xbuf_ref.at[buf], r, gsem.at[buf]).start()

    def wait_rows(buf):
        pltpu.make_async_copy(hnp_hbm.at[pl.ds(0, TM_EXP * TOKEN_ROWS)], xbuf_ref.at[buf], gsem.at[buf]).wait()

    for first in range(GATHER_AHEAD):
        @pl.when((t == 0) & (first < used))
        def _():
            for g in range(ROW_DMA_GROUPS):
                start_rows(first, first, g)

    def multiply(fetch_ahead):
        wait_rows(cur)
        c = D_MODEL // 2
        x_lo, x_hi = _unpack_halves(_load_token_rows(xbuf_ref.at[cur], TM_EXP))
        pieces = ((x_lo, wgb_ref, 0), (x_hi, wgb_ref, c), (x_lo, wub_ref, 0), (x_hi, wub_ref, c))
        acc = []
        for g, (x, w_ref, row0) in enumerate(pieces):
            if fetch_ahead:
                start_rows(t + GATHER_AHEAD, ahead, g)
            acc.append(jnp.dot(x, w_ref[row0:row0 + c, :], preferred_element_type=F32))
        hid_ref[...] = (_silu(acc[0] + acc[1]) * (acc[2] + acc[3])).astype(BF16)

    @pl.when(t + GATHER_AHEAD < used)
    def _():
        multiply(True)

    @pl.when((t + GATHER_AHEAD >= used) & (t < used))
    def _():
        multiply(False)

    @pl.when(t >= used)
    def _():
        hid_ref[...] = jnp.zeros_like(hid_ref)


def _experts_down_kernel(sched_ref, inv_ref, hid_ref, wd_hbm, y_hbm,
                         ybuf_ref, wd_stage, wdb_ref, wsems, ssem, *, n_tokens):
    t = pl.program_id(0)
    used = sched_ref[S_USED, 0]
    cur = t % 2
    _load_expert_weights(t, sched_ref, [(wd_hbm, wd_stage, wdb_ref)], wsems)

    def start_rows(tile, buf, group):
        for r, slot, ok in _tile_rows(sched_ref, tile, group):
            dst = jnp.where(ok, inv_ref[slot], TOP_K * n_tokens + r)
            _token_copy(ybuf_ref.at[buf], r, y_hbm, dst, ssem.at[buf]).start(priority=r % 2)

    def wait_rows(buf):
        pltpu.make_async_copy(ybuf_ref.at[buf], y_hbm.at[pl.ds(0, TM_EXP * TOKEN_ROWS)], ssem.at[buf]).wait()

    def multiply(send_previous):
        hid = hid_ref[...]
        q = D_MODEL // 4
        for half in range(2):
            if send_previous:
                start_rows(t - 1, 1 - cur, 2 * half)
            lo = jnp.dot(hid, wdb_ref[:, half * q:(half + 1) * q], preferred_element_type=F32)
            if send_previous:
                start_rows(t - 1, 1 - cur, 2 * half + 1)
            hi = jnp.dot(hid, wdb_ref[:, (half + 2) * q:(half + 3) * q], preferred_element_type=F32)
            _store_token_rows(ybuf_ref.at[cur], half * (q // LANES), _pack_halves(jnp.concatenate([lo, hi], axis=1)))

    @pl.when((t >= 2) & (t < used))
    def _():
        wait_rows(cur)

    @pl.when(t == 0)
    def _():
        ybuf_ref[1] = jnp.zeros(ybuf_ref.shape[1:], ybuf_ref.dtype)
        spare = pltpu.make_async_copy(ybuf_ref.at[1], y_hbm.at[pl.ds(TOP_K * n_tokens * TOKEN_ROWS, TM_EXP * TOKEN_ROWS)],
                                      ssem.at[1])
        spare.start()
        spare.wait()
        multiply(False)

    @pl.when((t > 0) & (t < used))
    def _():
        multiply(True)

    @pl.when(t == used - 1)
    def _():
        for g in range(ROW_DMA_GROUPS):
            start_rows(t, cur, g)

        @pl.when(t > 0)
        def _():
            wait_rows(1 - cur)

        wait_rows(cur)


def _tile_schedule(counts, n_tiles):
    tiles_per_expert = (counts + TM_EXP - 1) // TM_EXP
    tile_end = jnp.cumsum(tiles_per_expert)
    ids = jnp.arange(n_tiles, dtype=I32)
    raw = jnp.sum(ids[:, None] >= tile_end[None, :], axis=1)
    expert = jnp.minimum(raw, N_EXPERTS - 1).astype(I32)
    first = jnp.concatenate([jnp.ones((1,), I32), (expert[1:] != expert[:-1]).astype(I32)])
    slot = (jnp.cumsum(first) - 1) % 2
    start_idx = jnp.where(first == 1, ids, n_tiles)
    next_start = jnp.concatenate([lax.cummin(start_idx, reverse=True)[1:], jnp.full((1,), n_tiles, I32)])
    has_next = (next_start < n_tiles).astype(I32)
    next_expert = jnp.sum(jnp.where(ids[None, :] == next_start[:, None], expert[None, :], 0), axis=1)
    used = jnp.full((n_tiles,), tile_end[-1], I32)
    onehot = expert[:, None] == jnp.arange(N_EXPERTS)[None, :]
    tile_in_expert = ids - jnp.sum(jnp.where(onehot, (tile_end - tiles_per_expert)[None, :], 0), axis=1)
    rows_left = jnp.sum(jnp.where(onehot, counts[None, :], 0), axis=1) - tile_in_expert * TM_EXP
    valid = jnp.where(raw < N_EXPERTS, jnp.clip(rows_left, 0, TM_EXP), 0)
    return jnp.stack([expert, first, slot, next_expert, has_next, used, valid]).astype(I32)


def _experts(sched, inv, hnp, wg, wu, wd):
    n = hnp.shape[0] // TOKEN_ROWS
    n_slots = inv.shape[0]
    token_buf = lambda nbuf: pltpu.VMEM((nbuf, TM_EXP * TOKEN_ROWS, LANES), U32)
    row_spec = lambda width: pl.BlockSpec((TM_EXP, width), lambda t, sc, iv: (t, 0))
    hbm = pl.BlockSpec(memory_space=pl.ANY)
    big_vmem = pltpu.CompilerParams(dimension_semantics=("arbitrary",), vmem_limit_bytes=EXPERT_VMEM_LIMIT)
    hid = pl.pallas_call(
        functools.partial(_experts_up_kernel, n_tokens=n),
        grid_spec=pltpu.PrefetchScalarGridSpec(
            num_scalar_prefetch=2,
            grid=(n_slots // TM_EXP,),
            in_specs=[hbm, hbm, hbm],
            out_specs=row_spec(EXPERT_HIDDEN),
            scratch_shapes=[token_buf(GATHER_AHEAD + 1)]
                           + [pltpu.VMEM((2, D_MODEL, EXPERT_HIDDEN), F32)] * 2
                           + [pltpu.VMEM((D_MODEL, EXPERT_HIDDEN), BF16)] * 2
                           + [pltpu.SemaphoreType.DMA((2, 2)), pltpu.SemaphoreType.DMA((GATHER_AHEAD + 1,))],
        ),
        out_shape=jax.ShapeDtypeStruct((n_slots, EXPERT_HIDDEN), BF16),
        compiler_params=big_vmem,
        name="experts_up",
    )(sched, inv, hnp, wg, wu)
    return pl.pallas_call(
        functools.partial(_experts_down_kernel, n_tokens=n),
        grid_spec=pltpu.PrefetchScalarGridSpec(
            num_scalar_prefetch=2,
            grid=(n_slots // TM_EXP,),
            in_specs=[row_spec(EXPERT_HIDDEN), hbm],
            out_specs=hbm,
            scratch_shapes=[token_buf(2),
                            pltpu.VMEM((2, EXPERT_HIDDEN, D_MODEL), F32), pltpu.VMEM((EXPERT_HIDDEN, D_MODEL), BF16),
                            pltpu.SemaphoreType.DMA((1, 2)), pltpu.SemaphoreType.DMA((2,))],
        ),
        out_shape=jax.ShapeDtypeStruct(((TOP_K * n + TM_EXP) * TOKEN_ROWS, LANES), U32),
        compiler_params=big_vmem,
        name="experts_down",
    )(sched, inv, hid, wd)


def _combine_kernel(h_ref, meta_ref, y0_ref, y1_ref, o_ref):
    c = D_MODEL // 2
    meta = meta_ref[...].T
    w0, w1 = meta[:, 2:3], meta[:, 3:4]
    tm = h_ref.shape[0]
    lo0, hi0 = _unpack_halves(_load_token_rows(y0_ref, tm), F32)
    lo1, hi1 = _unpack_halves(_load_token_rows(y1_ref, tm), F32)
    o_ref[:, :c] = h_ref[:, :c] + (lo0 * w0 + lo1 * w1)
    o_ref[:, c:] = h_ref[:, c:] + (hi0 * w0 + hi1 * w1)


def _combine(h, meta, y):
    n = h.shape[0]
    nb = n // TM_COMB
    return pl.pallas_call(
        _combine_kernel,
        grid=(nb,),
        in_specs=[
            pl.BlockSpec((TM_COMB, D_MODEL), lambda i: (i, 0)),
            pl.BlockSpec((LANES, TM_COMB), lambda i: (0, i)),
            pl.BlockSpec((TM_COMB * TOKEN_ROWS, LANES), lambda i: (i, 0)),
            pl.BlockSpec((TM_COMB * TOKEN_ROWS, LANES), lambda i: (nb + i, 0)),
        ],
        out_specs=pl.BlockSpec((TM_COMB, D_MODEL), lambda i: (i, 0)),
        out_shape=jax.ShapeDtypeStruct((n, D_MODEL), F32),
        compiler_params=_cparams(("parallel",)),
        name="combine",
    )(h, meta, y, y)


def _lambda_init(layer_idx):
    return 0.8 - 0.6 * math.exp(-0.3 * layer_idx)


def _pad_lanes(v, width=LANES):
    return jnp.pad(v, ((0, 0), (0, width - v.shape[1])))


def _layer(l, x2, pos_rows, bsz, seq, ln1_w, w_in, conv_w, conv_b, dt_bias, a_log, d_skip, ssd_norm_w,
           q_norm_w, k_norm_w, lambda_q1, lambda_k1, lambda_q2, lambda_k2, subln_w, w_out, ln2_w,
           w_router_group, b_router_group, w_router_expert, b_router_expert, w_gate, w_up, w_down):
    n = x2.shape[0]
    c_z, c_xbc, c_dt = SSD_WIDTH, SSD_WIDTH + SSD_CONV_DIM, SSD_WIDTH + SSD_CONV_DIM + SSD_HEADS
    c_q, c_k = c_dt + ATTN_WIDTH, c_dt + 2 * ATTN_WIDTH
    w_main = jnp.concatenate([w_in[:, :c_z], w_in[:, c_dt:c_q], w_in[:, c_q:c_k], w_in[:, c_z:c_xbc]],
                             axis=1).astype(BF16)
    w_vt = w_in[:, c_k:].T.astype(BF16)
    w_dt = _pad_lanes(w_in[:, c_xbc:c_dt]).astype(BF16)

    inv_freq = jnp.power(ROPE_THETA, -jnp.arange(0, ROPE_DIM, 2, dtype=F32) / ROPE_DIM)
    invf_col = jnp.concatenate([inv_freq, jnp.zeros_like(inv_freq)])[:, None]
    seg_ones = (jnp.arange(LANES)[:, None] // ATTN_QK_DIM == jnp.arange(LANES)[None, :] // ATTN_QK_DIM).astype(BF16)
    q_scale = math.log2(math.e) / math.sqrt(ATTN_QK_DIM)
    u, v_t, dt_raw = _in_proj(x2, ln1_w[None, :], w_main, w_vt, w_dt, pos_rows, invf_col,
                              jnp.tile(q_norm_w, 2)[None, :] * q_scale, jnp.tile(k_norm_w, 2)[None, :],
                              seg_ones, _rope_spread())

    a_neg = _pad_lanes(-jnp.exp(a_log.astype(F32))[None, :])
    y_ssd = _ssd(u, dt_raw, conv_w, conv_b[None, :], _pad_lanes(dt_bias[None, :]), a_neg,
                 jnp.repeat(d_skip, SSD_HEAD_DIM)[None, :], ssd_norm_w[None, :], bsz, seq)

    lam_vecs = jnp.stack([lambda_q1, lambda_k1, lambda_q2, lambda_k2]).astype(F32)
    y_att = _attention(u, v_t, lam_vecs, subln_w[:, None], bsz, seq, _lambda_init(l))

    w_out_b = w_out.astype(BF16)
    w_router_t = _pad_lanes(jnp.concatenate([w_router_group, w_router_expert], axis=1)).T.astype(BF16)
    b_router = _pad_lanes(jnp.concatenate([b_router_group, b_router_expert])[None, :]).T
    h, hnp, meta = _out_proj(x2, y_ssd, y_att, w_out_b[:SSD_WIDTH], w_out_b[SSD_WIDTH:], ln2_w[None, :],
                             w_router_t, b_router)

    dest, cnt = _rank(meta)
    dest0, dest1 = dest[0], dest[1]
    counts = cnt[:, 0].astype(I32)
    n_slots = (n * TOP_K + N_EXPERTS * (TM_EXP - 1)) // TM_EXP * TM_EXP
    sched = _tile_schedule(counts, n_slots // TM_EXP)

    inv = _invert(dest0, dest1, sched[S_VALID])
    y = _experts(sched, inv, hnp, w_gate, w_up, w_down)
    return _combine(h, meta, y)


def kernel(x, positions, ln1_w, w_in, conv_w, conv_b, dt_bias, a_log, d_skip, ssd_norm_w, q_norm_w, k_norm_w,
           lambda_q1, lambda_k1, lambda_q2, lambda_k2, subln_w, w_out, ln2_w, w_router_group, b_router_group,
           w_router_expert, b_router_expert, w_gate, w_up, w_down):
    bsz, seq, d = x.shape
    assert d == D_MODEL and seq % TQ == 0 and (bsz * seq) % TM_IN == 0
    x2 = x.reshape(bsz * seq, d)
    pos_rows = jnp.broadcast_to(positions.astype(F32).reshape(1, bsz * seq), (8, bsz * seq))
    params = (ln1_w, w_in, conv_w, conv_b, dt_bias, a_log, d_skip, ssd_norm_w, q_norm_w, k_norm_w,
              lambda_q1, lambda_k1, lambda_q2, lambda_k2, subln_w, w_out, ln2_w, w_router_group, b_router_group,
              w_router_expert, b_router_expert, w_gate, w_up, w_down)
    for l in range(ln1_w.shape[0]):
        x2 = _layer(l, x2, pos_rows, bsz, seq, *[p[l] for p in params])
    return x2.reshape(bsz, seq, d)
```

```python
import functools
import math

import jax
import jax.numpy as jnp
from jax import lax
from jax.experimental import pallas as pl
from jax.experimental.pallas import tpu as pltpu

F32 = jnp.float32
BF16 = jnp.bfloat16
I32 = jnp.int32
U32 = jnp.uint32
HIGHEST = lax.Precision.HIGHEST

D_MODEL = 2048
SSD_WIDTH = 1024
ATTN_WIDTH = 1024
SSD_HEAD_DIM = 64
SSD_HEADS = 16
SSD_GROUPS = 2
SSD_HEADS_PER_GROUP = SSD_HEADS // SSD_GROUPS
SSD_STATE = 128
SSD_CONV = 4
SSD_CHUNK = 128
SSD_CONV_DIM = SSD_WIDTH + 2 * SSD_GROUPS * SSD_STATE
ATTN_V_DIM = 128
ATTN_HEADS = 8
ATTN_QK_DIM = 64
ROPE_THETA = 500000.0
ROPE_DIM = 16
N_EXPERT_GROUPS = 4
EXPERTS_PER_GROUP = 8
N_EXPERTS = 32
TOP_K = 2
EXPERT_HIDDEN = 1024
EPS = 1e-6

LANES = 128
NEG_INF = float("-inf")

TM_IN = 512
TQ = 256
ATTN_HB = 8
ONES_ROWS = 16
TM_OUT = 512
TM_RANK = 512
TM_EXP = 256
TM_COMB = 512
ROUTER_ROWS = 40
Q_COL = SSD_WIDTH
K_COL = Q_COL + ATTN_WIDTH
XBC_COL = K_COL + ATTN_WIDTH
U_COLS = XBC_COL + SSD_CONV_DIM
VMEM_LIMIT = 52 * 1024 * 1024
EXPERT_VMEM_LIMIT = 58 * 1024 * 1024


def _cparams(sem):
    return pltpu.CompilerParams(dimension_semantics=sem, vmem_limit_bytes=VMEM_LIMIT)


def _silu(x):
    return x * (1.0 / (1.0 + jnp.exp(-x)))


def _softplus(x):
    return jnp.maximum(x, 0.0) + jnp.log(1.0 + jnp.exp(-jnp.abs(x)))


def _rope_tables(pos_ref, invf_ref, spread_ref):
    tm = pos_ref.shape[1]
    ang_t = invf_ref[...] * pos_ref[0:1, :]
    trig = jnp.concatenate([jnp.cos(ang_t), jnp.sin(ang_t), jnp.zeros((LANES - 2 * ROPE_DIM, tm), F32)], axis=0).T
    t1 = trig.astype(BF16)
    r1 = trig - t1.astype(F32)
    t2 = r1.astype(BF16)
    t3 = (r1 - t2.astype(F32)).astype(BF16)
    spread = spread_ref[...]
    tab = (jnp.dot(t1, spread, preferred_element_type=F32) + jnp.dot(t2, spread, preferred_element_type=F32)
           + jnp.dot(t3, spread, preferred_element_type=F32))
    return tab[:, :LANES], tab[:, LANES:2 * LANES], tab[:, 2 * LANES:]


def _norm_rope(x, w, tables, seg_ones):
    cs, s_lo, s_hi = tables
    half = ROPE_DIM // 2
    ss = jnp.dot((x * x).astype(BF16), seg_ones, preferred_element_type=F32)
    xn = x * lax.rsqrt(ss * (1.0 / ATTN_QK_DIM) + EPS) * w
    return xn * cs + pltpu.roll(xn, LANES - half, 1) * s_lo + pltpu.roll(xn, half, 1) * s_hi


def _inproj_kernel(x_ref, lnw_ref, w_ref, wvt_ref, wdt_ref, pos_ref, invf_ref, qw_ref, kw_ref, ones_ref, spread_ref,
                   u_ref, vt_ref, dt_ref):
    x = x_ref[...]
    ms = jnp.mean(x * x, axis=-1, keepdims=True)
    xn = (x * lax.rsqrt(ms + EPS) * lnw_ref[...]).astype(BF16)
    tables = _rope_tables(pos_ref, invf_ref, spread_ref)
    for c0, c1, head_w in ((0, SSD_WIDTH, None), (Q_COL, K_COL, qw_ref), (K_COL, XBC_COL, kw_ref), (XBC_COL, U_COLS, None)):
        acc = jnp.dot(xn, w_ref[:, c0:c1], preferred_element_type=F32)
        if head_w is None:
            u_ref[:, c0:c1] = acc.astype(BF16)
        else:
            for hb in range(ATTN_HEADS):
                blk = _norm_rope(acc[:, hb * LANES:(hb + 1) * LANES], head_w[...], tables, ones_ref[...])
                u_ref[:, c0 + hb * LANES:c0 + (hb + 1) * LANES] = blk.astype(BF16)
    dt_ref[...] = jnp.dot(xn, wdt_ref[...], preferred_element_type=F32)
    vt_ref[...] = lax.dot_general(wvt_ref[...], xn, (((1,), (1,)), ((), ())),
                                  preferred_element_type=F32).astype(BF16)


def _in_proj(x2, ln_w, w_main, w_vt, w_dt, pos_rows, invf_col, qw_lanes, kw_lanes, seg_ones, spread):
    n = x2.shape[0]
    resident = lambda shape: pl.BlockSpec(shape, lambda i: (0, 0), pipeline_mode=pl.Buffered(1))
    return pl.pallas_call(
        _inproj_kernel,
        grid=(n // TM_IN,),
        in_specs=[
            pl.BlockSpec((TM_IN, D_MODEL), lambda i: (i, 0)),
            resident((1, D_MODEL)),
            resident((D_MODEL, U_COLS)),
            resident((ATTN_WIDTH, D_MODEL)),
            resident((D_MODEL, LANES)),
            pl.BlockSpec((8, TM_IN), lambda i: (0, i)),
            resident((ROPE_DIM, 1)), resident((1, LANES)), resident((1, LANES)), resident((LANES, LANES)),
            resident((LANES, 3 * LANES)),
        ],
        out_specs=[
            pl.BlockSpec((TM_IN, U_COLS), lambda i: (i, 0)),
            pl.BlockSpec((ATTN_WIDTH, TM_IN), lambda i: (0, i)),
            pl.BlockSpec((TM_IN, LANES), lambda i: (i, 0)),
        ],
        out_shape=[
            jax.ShapeDtypeStruct((n, U_COLS), BF16),
            jax.ShapeDtypeStruct((ATTN_WIDTH, n), BF16),
            jax.ShapeDtypeStruct((n, LANES), F32),
        ],
        compiler_params=_cparams(("parallel",)),
        name="in_proj",
    )(x2, ln_w, w_main, w_vt, w_dt, pos_rows, invf_col, qw_lanes, kw_lanes, seg_ones, spread)


def _ssd_kernel(z_ref, xbc_ref, dt_ref, convw_ref, convb_ref, dtb_ref, aneg_ref, dskip_ref, normw_ref,
                y_ref, xp_ref, st_ref, yacc_ref):
    L = SSD_CHUNK
    P = SSD_HEAD_DIM

    @pl.when(pl.program_id(1) == 0)
    def _():
        xp_ref[0:8, :] = jnp.zeros((8, SSD_CONV_DIM), F32)
        st_ref[...] = jnp.zeros_like(st_ref)

    xp_ref[8:8 + L, :] = xbc_ref[...].astype(F32)
    acc = jnp.broadcast_to(convb_ref[...], (L, SSD_CONV_DIM))
    for k in range(SSD_CONV):
        acc = acc + xp_ref[5 + k:5 + k + L, :] * convw_ref[k:k + 1, :]
    xp_ref[0:8, :] = xp_ref[L:L + 8, :]
    xc = _silu(acc)

    dt = _softplus(dt_ref[...] + dtb_ref[...])
    a = dt * aneg_ref[...]
    row = lax.broadcasted_iota(I32, (L, L), 0)
    col = lax.broadcasted_iota(I32, (L, L), 1)
    causal = row >= col
    a_cs = jnp.dot(causal.astype(F32), a, precision=HIGHEST, preferred_element_type=F32)
    a_last = a_cs[L - 1:L, :]
    ea = jnp.exp(a_cs)
    dsdt = jnp.exp(a_last - a_cs) * dt
    cd = jnp.exp(a_last)
    a_cs_t = a_cs.T
    dt_t = dt.T
    dsdt_t = dsdt.T

    for g in range(SSD_GROUPS):
        b_g = xc[:, SSD_WIDTH + g * SSD_STATE:SSD_WIDTH + (g + 1) * SSD_STATE]
        c_off = SSD_WIDTH + SSD_GROUPS * SSD_STATE
        c_g = xc[:, c_off + g * SSD_STATE:c_off + (g + 1) * SSD_STATE]
        cb = lax.dot_general(c_g.astype(BF16), b_g.astype(BF16), (((1,), (1,)), ((), ())),
                             preferred_element_type=F32)
        b_gt = b_g.T
        low = lax.broadcasted_iota(I32, (L, LANES), 1) < P
        for pair in range(g * SSD_HEADS_PER_GROUP // 2, (g + 1) * SSD_HEADS_PER_GROUP // 2):
            h0 = 2 * pair
            xs_p = xc[:, h0 * P:(h0 + 2) * P].astype(BF16)
            s_prev = st_ref[pair]
            rhs = jnp.concatenate([xs_p, s_prev.astype(BF16)], axis=0)
            ys, news = [], []
            for h in (h0, h0 + 1):
                seg = a_cs[:, h:h + 1] - a_cs_t[h:h + 1, :]
                dec = jnp.exp(jnp.where(causal, seg, NEG_INF))
                m = (cb * dec * dt_t[h:h + 1, :]).astype(BF16)
                c_s = (c_g * ea[:, h:h + 1]).astype(BF16)
                lhs = jnp.concatenate([m, c_s], axis=1)
                ys.append(jnp.dot(lhs, rhs, preferred_element_type=F32))
                bw = (b_gt * dsdt_t[h:h + 1, :]).astype(BF16)
                news.append(jnp.dot(bw, xs_p, preferred_element_type=F32))
            yacc_ref[:, h0 * P:(h0 + 2) * P] = jnp.where(low, ys[0], ys[1])
            cd_p = jnp.where(low[0:1, :], cd[:, h0:h0 + 1], cd[:, h0 + 1:h0 + 2])
            st_ref[pair] = s_prev * cd_p + jnp.where(low, news[0], news[1])

    y = yacc_ref[...] + xc[:, :SSD_WIDTH] * dskip_ref[...]
    y = y * _silu(z_ref[...].astype(F32))
    gw = SSD_WIDTH // SSD_GROUPS
    for g in range(SSD_GROUPS):
        yg = y[:, g * gw:(g + 1) * gw]
        ms = jnp.mean(yg * yg, axis=-1, keepdims=True)
        y_ref[:, g * gw:(g + 1) * gw] = (yg * lax.rsqrt(ms + EPS) * normw_ref[:, g * gw:(g + 1) * gw]).astype(BF16)


def _ssd(u, dt_raw, conv_w, conv_b, dt_bias, a_neg, dskip_lanes, norm_w, bsz, seq):
    n = u.shape[0]
    nc = seq // SSD_CHUNK
    xbc_blk = (SSD_WIDTH + 2 * ATTN_WIDTH) // SSD_CONV_DIM
    full = lambda shape: pl.BlockSpec(shape, lambda b, c: (0, 0))
    return pl.pallas_call(
        _ssd_kernel,
        grid=(bsz, nc),
        in_specs=[
            pl.BlockSpec((SSD_CHUNK, SSD_WIDTH), lambda b, c: (b * nc + c, 0)),
            pl.BlockSpec((SSD_CHUNK, SSD_CONV_DIM), lambda b, c: (b * nc + c, xbc_blk)),
            pl.BlockSpec((SSD_CHUNK, LANES), lambda b, c: (b * nc + c, 0)),
            full((SSD_CONV, SSD_CONV_DIM)),
            full((1, SSD_CONV_DIM)),
            full((1, LANES)),
            full((1, LANES)),
            full((1, SSD_WIDTH)),
            full((1, SSD_WIDTH)),
        ],
        out_specs=pl.BlockSpec((SSD_CHUNK, SSD_WIDTH), lambda b, c: (b * nc + c, 0)),
        out_shape=jax.ShapeDtypeStruct((n, SSD_WIDTH), BF16),
        scratch_shapes=[
            pltpu.VMEM((SSD_CHUNK + 8, SSD_CONV_DIM), F32),
            pltpu.VMEM((SSD_HEADS // 2, SSD_STATE, 2 * SSD_HEAD_DIM), F32),
            pltpu.VMEM((SSD_CHUNK, SSD_WIDTH), F32),
        ],
        compiler_params=_cparams(("parallel", "arbitrary")),
        name="ssd",
    )(u, u, dt_raw, conv_w, conv_b, dt_bias, a_neg, dskip_lanes, norm_w)


def _rope_spread():
    half = ROPE_DIM // 2
    lane = jnp.arange(LANES)
    d = lane % ATTN_QK_DIM
    src = jnp.arange(LANES)[:, None]
    cos_src = jnp.where(d < ROPE_DIM, d % half, half)
    cos_tab = (src == cos_src[None, :]).astype(F32)
    lo_tab = -((src == (2 * half + d)[None, :]) & (d < half)[None, :]).astype(F32)
    hi_tab = ((src == (2 * half + d - half)[None, :]) & ((d >= half) & (d < ROPE_DIM))[None, :]).astype(F32)
    return jnp.concatenate([cos_tab, lo_tab, hi_tab], axis=1).astype(BF16)


def _attn_kernel(q_ref, k_ref, vt_ref, lamv_ref, subw_ref, o_ref, acc_ref, *, lam_init):
    qi = pl.program_id(2)
    lane = lax.broadcasted_iota(I32, (TQ, LANES), 1)
    qs = []
    for hb in range(ATTN_HB):
        q = q_ref[:, hb * LANES:(hb + 1) * LANES]
        zero = jnp.zeros_like(q)
        qs.append(jnp.concatenate([jnp.where(lane < ATTN_QK_DIM, q, zero),
                                   jnp.where(lane >= ATTN_QK_DIM, q, zero)], axis=0))
    acc_ref[...] = jnp.zeros_like(acc_ref)
    kv_idx = lax.broadcasted_iota(I32, (TQ, 2 * TQ), 0)
    q_idx = lax.broadcasted_iota(I32, (TQ, 2 * TQ), 1) & (TQ - 1)
    nt = (((1,), (1,)), ((), ()))

    def block(j, carry, masked):
        off = pl.multiple_of(j * TQ, TQ)
        ss = []
        for hb in range(ATTN_HB):
            kb = k_ref[pl.ds(off, TQ), hb * LANES:(hb + 1) * LANES]
            ss.append(lax.dot_general(kb, qs[hb], nt, preferred_element_type=F32))
        new, ps, alphas = [], [], []
        for hb in range(ATTN_HB):
            m_old = carry[hb]
            s = ss[hb]
            if masked:
                s = jnp.where(kv_idx <= q_idx, s, NEG_INF)
            m_new = jnp.maximum(m_old, jnp.max(s, axis=0, keepdims=True))
            alphas.append(jnp.exp2(m_old - m_new))
            ps.append(jnp.exp2(s - m_new).astype(BF16))
            new.append(m_new)
        pvs = []
        for hb in range(ATTN_HB):
            vb = jnp.concatenate([vt_ref[hb * LANES:(hb + 1) * LANES, pl.ds(off, TQ)], ones_rows], axis=0)
            pvs.append(jnp.dot(vb, ps[hb], preferred_element_type=F32))
        for hb in range(ATTN_HB):
            acc_ref[hb] = alphas[hb] * acc_ref[hb] + pvs[hb]
        return tuple(new)

    ones_rows = jnp.ones((ONES_ROWS, TQ), BF16)
    init = (jnp.full((1, 2 * TQ), NEG_INF, F32),) * ATTN_HB
    carry = lax.fori_loop(0, qi, lambda j, cr: block(j, cr, False), init)
    block(qi, carry, True)

    lv = lamv_ref[...]
    lam = (jnp.exp(jnp.sum(lv[0:1] * lv[1:2], axis=1, keepdims=True))
           - jnp.exp(jnp.sum(lv[2:3] * lv[3:4], axis=1, keepdims=True)) + lam_init)
    for hb in range(ATTN_HB):
        acc = acc_ref[hb]
        o2 = acc[:ATTN_V_DIM] * (1.0 / acc[ATTN_V_DIM:ATTN_V_DIM + 1])
        o_t = o2[:, :TQ] - lam * o2[:, TQ:]
        ms = jnp.mean(o_t * o_t, axis=0, keepdims=True)
        o_t = o_t * lax.rsqrt(ms + EPS) * subw_ref[...] * (1.0 - lam_init)
        o_ref[:, hb * LANES:(hb + 1) * LANES] = o_t.T.astype(BF16)


def _attention(u, v_t, lam_vecs, subw_col, bsz, seq, lam_init):
    n = u.shape[0]
    nq = seq // TQ
    w = ATTN_HB * ATTN_V_DIM
    q_blk, k_blk = Q_COL // w, K_COL // w
    return pl.pallas_call(
        functools.partial(_attn_kernel, lam_init=lam_init),
        grid=(bsz, ATTN_HEADS // ATTN_HB, nq),
        in_specs=[
            pl.BlockSpec((TQ, w), lambda b, h, i: (b * nq + i, q_blk + h)),
            pl.BlockSpec((seq, w), lambda b, h, i: (b, k_blk + h)),
            pl.BlockSpec((w, seq), lambda b, h, i: (h, b)),
            pl.BlockSpec((4, ATTN_QK_DIM), lambda b, h, i: (0, 0)),
            pl.BlockSpec((ATTN_V_DIM, 1), lambda b, h, i: (0, 0)),
        ],
        out_specs=pl.BlockSpec((TQ, w), lambda b, h, i: (b * nq + i, h)),
        out_shape=jax.ShapeDtypeStruct((n, ATTN_WIDTH), BF16),
        scratch_shapes=[pltpu.VMEM((ATTN_HB, ATTN_V_DIM + ONES_ROWS, 2 * TQ), F32)],
        compiler_params=_cparams(("parallel", "parallel", "arbitrary")),
        name="attn",
    )(u, u, v_t, lam_vecs, subw_col)


def _pack_halves(x):
    c = x.shape[1] // 2
    lo = pltpu.bitcast(x[:, :c].astype(BF16).astype(F32), U32) >> 16
    hi = pltpu.bitcast(x[:, c:].astype(BF16).astype(F32), U32) & jnp.uint32(0xFFFF0000)
    return hi | lo


TOKEN_ROWS = D_MODEL // 2 // LANES


def _store_token_rows(ref, first_piece, packed):
    tm = packed.shape[0]
    for j in range(packed.shape[1] // LANES):
        ref[pl.ds(first_piece + j, tm, stride=TOKEN_ROWS), :] = packed[:, j * LANES:(j + 1) * LANES]


def _load_token_rows(ref, tm):
    return jnp.concatenate([ref[pl.ds(s, tm, stride=TOKEN_ROWS), :] for s in range(TOKEN_ROWS)], axis=1)


def _unpack_halves(w, dtype=BF16):
    lo = pltpu.bitcast(w << 16, F32).astype(dtype)
    hi = pltpu.bitcast(w & jnp.uint32(0xFFFF0000), F32).astype(dtype)
    return lo, hi


def _outproj_kernel(x_ref, ys_ref, ya_ref, wos_ref, woa_ref, ln2_ref, wr_ref, br_ref,
                    h_ref, hnp_ref, meta_ref):
    tm = x_ref.shape[0]
    h = (x_ref[...]
         + jnp.dot(ys_ref[...], wos_ref[...], preferred_element_type=F32)
         + jnp.dot(ya_ref[...], woa_ref[...], preferred_element_type=F32))
    h_ref[...] = h
    ms = jnp.mean(h * h, axis=-1, keepdims=True)
    hn = h * lax.rsqrt(ms + EPS) * ln2_ref[...]
    _store_token_rows(hnp_ref, 0, _pack_halves(hn))

    lg_t = lax.dot_general(wr_ref[...], hn.astype(BF16), (((1,), (1,)), ((), ())),
                           preferred_element_type=F32)
    lg = lg_t[0:ROUTER_ROWS, :] + br_ref[0:ROUTER_ROWS, :]
    row = lax.broadcasted_iota(I32, (ROUTER_ROWS, tm), 0).astype(F32)
    big = float(LANES)
    gl = jnp.where(row < N_EXPERT_GROUPS, lg, NEG_INF)
    gmax = jnp.max(gl, axis=0, keepdims=True)
    gsel = jnp.min(jnp.where(gl == gmax, row, big), axis=0, keepdims=True)
    g_w = 1.0 / jnp.sum(jnp.exp(gl - gmax), axis=0, keepdims=True)
    eid = row - N_EXPERT_GROUPS
    lo = gsel * EXPERTS_PER_GROUP
    emask = (eid >= lo) & (eid < lo + EXPERTS_PER_GROUP)
    el = jnp.where(emask, lg, NEG_INF)
    m1 = jnp.max(el, axis=0, keepdims=True)
    i1 = jnp.min(jnp.where(el == m1, eid, big), axis=0, keepdims=True)
    el2 = jnp.where(eid == i1, NEG_INF, el)
    m2 = jnp.max(el2, axis=0, keepdims=True)
    i2 = jnp.min(jnp.where(el2 == m2, eid, big), axis=0, keepdims=True)
    e2 = jnp.exp(m2 - m1)
    w1 = g_w / (1.0 + e2)
    w2 = g_w * e2 / (1.0 + e2)
    mrow = lax.broadcasted_iota(I32, (LANES, tm), 0)
    meta_ref[...] = jnp.where(mrow == 0, i1, jnp.where(mrow == 1, i2, jnp.where(mrow == 2, w1, jnp.where(mrow == 3, w2, 0.0))))


def _out_proj(x2, y_ssd, y_att, wo_s, wo_a, ln2_w, w_router_t, b_router):
    n = x2.shape[0]
    full = lambda shape: pl.BlockSpec(shape, lambda i: (0, 0), pipeline_mode=pl.Buffered(1))
    return pl.pallas_call(
        _outproj_kernel,
        grid=(n // TM_OUT,),
        in_specs=[
            pl.BlockSpec((TM_OUT, D_MODEL), lambda i: (i, 0)),
            pl.BlockSpec((TM_OUT, SSD_WIDTH), lambda i: (i, 0)),
            pl.BlockSpec((TM_OUT, ATTN_WIDTH), lambda i: (i, 0)),
            full((SSD_WIDTH, D_MODEL)), full((ATTN_WIDTH, D_MODEL)),
            full((1, D_MODEL)), full((LANES, D_MODEL)), full((LANES, 1)),
        ],
        out_specs=[
            pl.BlockSpec((TM_OUT, D_MODEL), lambda i: (i, 0)),
            pl.BlockSpec((TM_OUT * TOKEN_ROWS, LANES), lambda i: (i, 0)),
            pl.BlockSpec((LANES, TM_OUT), lambda i: (0, i)),
        ],
        out_shape=[
            jax.ShapeDtypeStruct((n, D_MODEL), F32),
            jax.ShapeDtypeStruct((n * TOKEN_ROWS, LANES), U32),
            jax.ShapeDtypeStruct((LANES, n), F32),
        ],
        compiler_params=_cparams(("parallel",)),
        name="out_proj",
    )(x2, y_ssd, y_att, wo_s, wo_a, ln2_w, w_router_t, b_router)


def _rank_kernel(meta_ref, dest_ref, cnt_ref, run_ref, offs_ref):
    p = pl.program_id(0)
    i = pl.program_id(1)
    tm = meta_ref.shape[1]
    meta = meta_ref[...]
    row = lax.broadcasted_iota(I32, (N_EXPERTS, tm), 0).astype(F32)
    oh0 = (row == meta[0:1, :]).astype(F32)
    oh1 = (row == meta[1:2, :]).astype(F32)
    oh = oh0 + oh1
    rowsum = jnp.sum(oh, axis=1, keepdims=True)

    @pl.when((p == 0) & (i == 0))
    def _():
        run_ref[...] = jnp.zeros_like(run_ref)

    @pl.when(p == 0)
    def _():
        run_ref[...] = run_ref[...] + rowsum
        dest_ref[...] = jnp.zeros_like(dest_ref)
        cnt_ref[...] = run_ref[...]

    @pl.when((p == 1) & (i == 0))
    def _():
        cnt = run_ref[...]
        padded = jnp.ceil(cnt * (1.0 / TM_EXP)) * TM_EXP
        r = lax.broadcasted_iota(I32, (N_EXPERTS, N_EXPERTS), 0)
        c = lax.broadcasted_iota(I32, (N_EXPERTS, N_EXPERTS), 1)
        offs_ref[...] = jnp.dot((c < r).astype(F32), padded, precision=HIGHEST, preferred_element_type=F32)
        cnt_ref[...] = cnt
        run_ref[...] = jnp.zeros_like(run_ref)

    @pl.when(p == 1)
    def _():
        r = lax.broadcasted_iota(I32, (tm, tm), 0)
        c = lax.broadcasted_iota(I32, (tm, tm), 1)
        before = jnp.dot(oh.astype(BF16), (r < c).astype(BF16), preferred_element_type=F32)
        base = before + jnp.tile(run_ref[...] + offs_ref[...], (1, tm // LANES))
        d0 = jnp.sum(oh0 * base, axis=0, keepdims=True)
        d1 = jnp.sum(oh1 * base, axis=0, keepdims=True)
        drow = lax.broadcasted_iota(I32, (8, tm), 0)
        dest_ref[...] = jnp.where(drow == 0, d0, jnp.where(drow == 1, d1, 0.0)).astype(I32)
        run_ref[...] = run_ref[...] + rowsum


def _rank(meta_t):
    n = meta_t.shape[1]
    per_expert = lambda: pl.BlockSpec((N_EXPERTS, LANES), lambda p, i: (0, 0))
    return pl.pallas_call(
        _rank_kernel,
        grid=(2, n // TM_RANK),
        in_specs=[pl.BlockSpec((8, TM_RANK), lambda p, i: (0, i))],
        out_specs=[pl.BlockSpec((8, TM_RANK), lambda p, i: (0, i * p)), per_expert()],
        out_shape=[jax.ShapeDtypeStruct((8, n), I32), jax.ShapeDtypeStruct((N_EXPERTS, LANES), F32)],
        scratch_shapes=[pltpu.VMEM((N_EXPERTS, LANES), F32), pltpu.VMEM((N_EXPERTS, LANES), F32)],
        compiler_params=_cparams(("arbitrary", "arbitrary")),
        name="rank",
    )(meta_t)


def _token_copy(src_ref, src_token, dst_ref, dst_token, sem):
    src = src_ref.at[pl.ds(pl.multiple_of(src_token * TOKEN_ROWS, TOKEN_ROWS), TOKEN_ROWS)]
    dst = dst_ref.at[pl.ds(pl.multiple_of(dst_token * TOKEN_ROWS, TOKEN_ROWS), TOKEN_ROWS)]
    return pltpu.make_async_copy(src, dst, sem)


def _invert_kernel(d0_ref, d1_ref, valid_ref, inv_ref):
    n = d0_ref.shape[0]

    def clear_tile(tile, _):
        def clear(r, _):
            inv_ref[tile * TM_EXP + r] = 0
            return 0

        lax.fori_loop(valid_ref[tile], TM_EXP, clear, 0)
        return 0

    lax.fori_loop(0, valid_ref.shape[0], clear_tile, 0)

    def put(t, _):
        inv_ref[d0_ref[t]] = t
        inv_ref[d1_ref[t]] = n + t
        return 0

    lax.fori_loop(0, n, put, 0, unroll=8)


def _invert(dest0, dest1, tile_valid):
    smem = pl.BlockSpec(memory_space=pltpu.SMEM)
    return pl.pallas_call(
        _invert_kernel,
        in_specs=[smem, smem, smem],
        out_specs=smem,
        out_shape=jax.ShapeDtypeStruct((tile_valid.shape[0] * TM_EXP,), I32),
        name="invert",
    )(dest0, dest1, tile_valid)


CAST_ROWS = 256


def _cast_weight(src_ref, dst_ref):
    def body(i, _):
        rows = pl.ds(pl.multiple_of(i * CAST_ROWS, CAST_ROWS), CAST_ROWS)
        dst_ref[rows, :] = src_ref[rows, :].astype(BF16)
        return 0

    lax.fori_loop(0, src_ref.shape[0] // CAST_ROWS, body, 0)


S_EXPERT, S_FIRST, S_SLOT, S_NEXT, S_HAS_NEXT, S_USED, S_VALID = range(7)
WEIGHT_DMA_PRIORITY = 1


def _load_expert_weights(t, sched_ref, triples, sems):
    def copies(expert, slot):
        return [pltpu.make_async_copy(w.at[expert], stage.at[slot], sems.at[i, slot])
                for i, (w, stage, _) in enumerate(triples)]

    @pl.when(sched_ref[S_FIRST, t] == 1)
    def _():
        slot = sched_ref[S_SLOT, t]

        @pl.when(t == 0)
        def _():
            for cp in copies(sched_ref[S_EXPERT, t], slot):
                cp.start(priority=WEIGHT_DMA_PRIORITY)

        for cp in copies(sched_ref[S_EXPERT, t], slot):
            cp.wait()

        @pl.when(sched_ref[S_HAS_NEXT, t] == 1)
        def _():
            for cp in copies(sched_ref[S_NEXT, t], 1 - slot):
                cp.start(priority=WEIGHT_DMA_PRIORITY)

        for _, stage, dst in triples:
            _cast_weight(stage.at[slot], dst)


ROW_DMA_GROUPS = 4
GATHER_AHEAD = 3


def _tile_rows(sched_ref, tile, group):
    valid = sched_ref[S_VALID, tile]
    per = TM_EXP // ROW_DMA_GROUPS
    for r in range(group * per, (group + 1) * per):
        ok = r < valid
        yield r, jnp.where(ok, tile * TM_EXP + r, 0), ok


def _experts_up_kernel(sched_ref, inv_ref, hnp_hbm, wg_hbm, wu_hbm, hid_ref,
                       xbuf_ref, wg_stage, wu_stage, wgb_ref, wub_ref, wsems, gsem, *, n_tokens):
    t = pl.program_id(0)
    used = sched_ref[S_USED, 0]
    nbuf = GATHER_AHEAD + 1
    cur = t % nbuf
    ahead = (t + GATHER_AHEAD) % nbuf
    _load_expert_weights(t, sched_ref, [(wg_hbm, wg_stage, wgb_ref), (wu_hbm, wu_stage, wub_ref)], wsems)

    def start_rows(tile, buf, group):
        for r, slot, _ in _tile_rows(sched_ref, tile, group):
            token = inv_ref[slot] & (n_tokens - 1)
            _token_copy(hnp_hbm, token, xbuf_ref.at[buf], r, gsem.at[buf]).start()

    def wait_rows(buf):
        pltpu.make_async_copy(hnp_hbm.at[pl.ds(0, TM_EXP * TOKEN_ROWS)], xbuf_ref.at[buf], gsem.at[buf]).wait()

    for first in range(GATHER_AHEAD):
        @pl.when((t == 0) & (first < used))
        def _():
            for g in range(ROW_DMA_GROUPS):
                start_rows(first, first, g)

    def multiply(fetch_ahead):
        wait_rows(cur)
        c = D_MODEL // 2
        x_lo, x_hi = _unpack_halves(_load_token_rows(xbuf_ref.at[cur], TM_EXP))
        pieces = ((x_lo, wgb_ref, 0), (x_hi, wgb_ref, c), (x_lo, wub_ref, 0), (x_hi, wub_ref, c))
        acc = []
        for g, (x, w_ref, row0) in enumerate(pieces):
            if fetch_ahead:
                start_rows(t + GATHER_AHEAD, ahead, g)
            acc.append(jnp.dot(x, w_ref[row0:row0 + c, :], preferred_element_type=F32))
        hid_ref[...] = (_silu(acc[0] + acc[1]) * (acc[2] + acc[3])).astype(BF16)

    @pl.when(t + GATHER_AHEAD < used)
    def _():
        multiply(True)

    @pl.when((t + GATHER_AHEAD >= used) & (t < used))
    def _():
        multiply(False)

    @pl.when(t >= used)
    def _():
        hid_ref[...] = jnp.zeros_like(hid_ref)


def _experts_down_kernel(sched_ref, inv_ref, hid_ref, wd_hbm, y_hbm,
                         ybuf_ref, wd_stage, wdb_ref, wsems, ssem, *, n_tokens):
    t = pl.program_id(0)
    used = sched_ref[S_USED, 0]
    cur = t % 2
    _load_expert_weights(t, sched_ref, [(wd_hbm, wd_stage, wdb_ref)], wsems)

    def start_rows(tile, buf, group):
        for r, slot, ok in _tile_rows(sched_ref, tile, group):
            dst = jnp.where(ok, inv_ref[slot], TOP_K * n_tokens + r)
            _token_copy(ybuf_ref.at[buf], r, y_hbm, dst, ssem.at[buf]).start(priority=r % 2)

    def wait_rows(buf):
        pltpu.make_async_copy(ybuf_ref.at[buf], y_hbm.at[pl.ds(0, TM_EXP * TOKEN_ROWS)], ssem.at[buf]).wait()

    def multiply(send_previous):
        hid = hid_ref[...]
        q = D_MODEL // 4
        for half in range(2):
            if send_previous:
                start_rows(t - 1, 1 - cur, 2 * half)
            lo = jnp.dot(hid, wdb_ref[:, half * q:(half + 1) * q], preferred_element_type=F32)
            if send_previous:
                start_rows(t - 1, 1 - cur, 2 * half + 1)
            hi = jnp.dot(hid, wdb_ref[:, (half + 2) * q:(half + 3) * q], preferred_element_type=F32)
            _store_token_rows(ybuf_ref.at[cur], half * (q // LANES), _pack_halves(jnp.concatenate([lo, hi], axis=1)))

    @pl.when((t >= 2) & (t < used))
    def _():
        wait_rows(cur)

    @pl.when(t == 0)
    def _():
        ybuf_ref[1] = jnp.zeros(ybuf_ref.shape[1:], ybuf_ref.dtype)
        spare = pltpu.make_async_copy(ybuf_ref.at[1], y_hbm.at[pl.ds(TOP_K * n_tokens * TOKEN_ROWS, TM_EXP * TOKEN_ROWS)],
                                      ssem.at[1])
        spare.start()
        spare.wait()
        multiply(False)

    @pl.when((t > 0) & (t < used))
    def _():
        multiply(True)

    @pl.when(t == used - 1)
    def _():
        for g in range(ROW_DMA_GROUPS):
            start_rows(t, cur, g)

        @pl.when(t > 0)
        def _():
            wait_rows(1 - cur)

        wait_rows(cur)


def _tile_schedule(counts, n_tiles):
    tiles_per_expert = (counts + TM_EXP - 1) // TM_EXP
    tile_end = jnp.cumsum(tiles_per_expert)
    ids = jnp.arange(n_tiles, dtype=I32)
    raw = jnp.sum(ids[:, None] >= tile_end[None, :], axis=1)
    expert = jnp.minimum(raw, N_EXPERTS - 1).astype(I32)
    first = jnp.concatenate([jnp.ones((1,), I32), (expert[1:] != expert[:-1]).astype(I32)])
    slot = (jnp.cumsum(first) - 1) % 2
    start_idx = jnp.where(first == 1, ids, n_tiles)
    next_start = jnp.concatenate([lax.cummin(start_idx, reverse=True)[1:], jnp.full((1,), n_tiles, I32)])
    has_next = (next_start < n_tiles).astype(I32)
    next_expert = jnp.sum(jnp.where(ids[None, :] == next_start[:, None], expert[None, :], 0), axis=1)
    used = jnp.full((n_tiles,), tile_end[-1], I32)
    onehot = expert[:, None] == jnp.arange(N_EXPERTS)[None, :]
    tile_in_expert = ids - jnp.sum(jnp.where(onehot, (tile_end - tiles_per_expert)[None, :], 0), axis=1)
    rows_left = jnp.sum(jnp.where(onehot, counts[None, :], 0), axis=1) - tile_in_expert * TM_EXP
    valid = jnp.where(raw < N_EXPERTS, jnp.clip(rows_left, 0, TM_EXP), 0)
    return jnp.stack([expert, first, slot, next_expert, has_next, used, valid]).astype(I32)


def _experts(sched, inv, hnp, wg, wu, wd):
    n = hnp.shape[0] // TOKEN_ROWS
    n_slots = inv.shape[0]
    token_buf = lambda nbuf: pltpu.VMEM((nbuf, TM_EXP * TOKEN_ROWS, LANES), U32)
    row_spec = lambda width: pl.BlockSpec((TM_EXP, width), lambda t, sc, iv: (t, 0))
    hbm = pl.BlockSpec(memory_space=pl.ANY)
    big_vmem = pltpu.CompilerParams(dimension_semantics=("arbitrary",), vmem_limit_bytes=EXPERT_VMEM_LIMIT)
    hid = pl.pallas_call(
        functools.partial(_experts_up_kernel, n_tokens=n),
        grid_spec=pltpu.PrefetchScalarGridSpec(
            num_scalar_prefetch=2,
            grid=(n_slots // TM_EXP,),
            in_specs=[hbm, hbm, hbm],
            out_specs=row_spec(EXPERT_HIDDEN),
            scratch_shapes=[token_buf(GATHER_AHEAD + 1)]
                           + [pltpu.VMEM((2, D_MODEL, EXPERT_HIDDEN), F32)] * 2
                           + [pltpu.VMEM((D_MODEL, EXPERT_HIDDEN), BF16)] * 2
                           + [pltpu.SemaphoreType.DMA((2, 2)), pltpu.SemaphoreType.DMA((GATHER_AHEAD + 1,))],
        ),
        out_shape=jax.ShapeDtypeStruct((n_slots, EXPERT_HIDDEN), BF16),
        compiler_params=big_vmem,
        name="experts_up",
    )(sched, inv, hnp, wg, wu)
    return pl.pallas_call(
        functools.partial(_experts_down_kernel, n_tokens=n),
        grid_spec=pltpu.PrefetchScalarGridSpec(
            num_scalar_prefetch=2,
            grid=(n_slots // TM_EXP,),
            in_specs=[row_spec(EXPERT_HIDDEN), hbm],
            out_specs=hbm,
            scratch_shapes=[token_buf(2),
                            pltpu.VMEM((2, EXPERT_HIDDEN, D_MODEL), F32), pltpu.VMEM((EXPERT_HIDDEN, D_MODEL), BF16),
                            pltpu.SemaphoreType.DMA((1, 2)), pltpu.SemaphoreType.DMA((2,))],
        ),
        out_shape=jax.ShapeDtypeStruct(((TOP_K * n + TM_EXP) * TOKEN_ROWS, LANES), U32),
        compiler_params=big_vmem,
        name="experts_down",
    )(sched, inv, hid, wd)


def _combine_kernel(h_ref, meta_ref, y0_ref, y1_ref, o_ref):
    c = D_MODEL // 2
    meta = meta_ref[...].T
    w0, w1 = meta[:, 2:3], meta[:, 3:4]
    tm = h_ref.shape[0]
    lo0, hi0 = _unpack_halves(_load_token_rows(y0_ref, tm), F32)
    lo1, hi1 = _unpack_halves(_load_token_rows(y1_ref, tm), F32)
    o_ref[:, :c] = h_ref[:, :c] + (lo0 * w0 + lo1 * w1)
    o_ref[:, c:] = h_ref[:, c:] + (hi0 * w0 + hi1 * w1)


def _combine(h, meta, y):
    n = h.shape[0]
    nb = n // TM_COMB
    return pl.pallas_call(
        _combine_kernel,
        grid=(nb,),
        in_specs=[
            pl.BlockSpec((TM_COMB, D_MODEL), lambda i: (i, 0)),
            pl.BlockSpec((LANES, TM_COMB), lambda i: (0, i)),
            pl.BlockSpec((TM_COMB * TOKEN_ROWS, LANES), lambda i: (i, 0)),
            pl.BlockSpec((TM_COMB * TOKEN_ROWS, LANES), lambda i: (nb + i, 0)),
        ],
        out_specs=pl.BlockSpec((TM_COMB, D_MODEL), lambda i: (i, 0)),
        out_shape=jax.ShapeDtypeStruct((n, D_MODEL), F32),
        compiler_params=_cparams(("parallel",)),
        name="combine",
    )(h, meta, y, y)


def _lambda_init(layer_idx):
    return 0.8 - 0.6 * math.exp(-0.3 * layer_idx)


def _pad_lanes(v, width=LANES):
    return jnp.pad(v, ((0, 0), (0, width - v.shape[1])))


def _layer(l, x2, pos_rows, bsz, seq, ln1_w, w_in, conv_w, conv_b, dt_bias, a_log, d_skip, ssd_norm_w,
           q_norm_w, k_norm_w, lambda_q1, lambda_k1, lambda_q2, lambda_k2, subln_w, w_out, ln2_w,
           w_router_group, b_router_group, w_router_expert, b_router_expert, w_gate, w_up, w_down):
    n = x2.shape[0]
    c_z, c_xbc, c_dt = SSD_WIDTH, SSD_WIDTH + SSD_CONV_DIM, SSD_WIDTH + SSD_CONV_DIM + SSD_HEADS
    c_q, c_k = c_dt + ATTN_WIDTH, c_dt + 2 * ATTN_WIDTH
    w_main = jnp.concatenate([w_in[:, :c_z], w_in[:, c_dt:c_q], w_in[:, c_q:c_k], w_in[:, c_z:c_xbc]],
                             axis=1).astype(BF16)
    w_vt = w_in[:, c_k:].T.astype(BF16)
    w_dt = _pad_lanes(w_in[:, c_xbc:c_dt]).astype(BF16)

    inv_freq = jnp.power(ROPE_THETA, -jnp.arange(0, ROPE_DIM, 2, dtype=F32) / ROPE_DIM)
    invf_col = jnp.concatenate([inv_freq, jnp.zeros_like(inv_freq)])[:, None]
    seg_ones = (jnp.arange(LANES)[:, None] // ATTN_QK_DIM == jnp.arange(LANES)[None, :] // ATTN_QK_DIM).astype(BF16)
    q_scale = math.log2(math.e) / math.sqrt(ATTN_QK_DIM)
    u, v_t, dt_raw = _in_proj(x2, ln1_w[None, :], w_main, w_vt, w_dt, pos_rows, invf_col,
                              jnp.tile(q_norm_w, 2)[None, :] * q_scale, jnp.tile(k_norm_w, 2)[None, :],
                              seg_ones, _rope_spread())

    a_neg = _pad_lanes(-jnp.exp(a_log.astype(F32))[None, :])
    y_ssd = _ssd(u, dt_raw, conv_w, conv_b[None, :], _pad_lanes(dt_bias[None, :]), a_neg,
                 jnp.repeat(d_skip, SSD_HEAD_DIM)[None, :], ssd_norm_w[None, :], bsz, seq)

    lam_vecs = jnp.stack([lambda_q1, lambda_k1, lambda_q2, lambda_k2]).astype(F32)
    y_att = _attention(u, v_t, lam_vecs, subln_w[:, None], bsz, seq, _lambda_init(l))

    w_out_b = w_out.astype(BF16)
    w_router_t = _pad_lanes(jnp.concatenate([w_router_group, w_router_expert], axis=1)).T.astype(BF16)
    b_router = _pad_lanes(jnp.concatenate([b_router_group, b_router_expert])[None, :]).T
    h, hnp, meta = _out_proj(x2, y_ssd, y_att, w_out_b[:SSD_WIDTH], w_out_b[SSD_WIDTH:], ln2_w[None, :],
                             w_router_t, b_router)

    dest, cnt = _rank(meta)
    dest0, dest1 = dest[0], dest[1]
    counts = cnt[:, 0].astype(I32)
    n_slots = (n * TOP_K + N_EXPERTS * (TM_EXP - 1)) // TM_EXP * TM_EXP
    sched = _tile_schedule(counts, n_slots // TM_EXP)

    inv = _invert(dest0, dest1, sched[S_VALID])
    y = _experts(sched, inv, hnp, w_gate, w_up, w_down)
    return _combine(h, meta, y)


def kernel(x, positions, ln1_w, w_in, conv_w, conv_b, dt_bias, a_log, d_skip, ssd_norm_w, q_norm_w, k_norm_w,
           lambda_q1, lambda_k1, lambda_q2, lambda_k2, subln_w, w_out, ln2_w, w_router_group, b_router_group,
           w_router_expert, b_router_expert, w_gate, w_up, w_down):
    bsz, seq, d = x.shape
    assert d == D_MODEL and seq % TQ == 0 and (bsz * seq) % TM_IN == 0
    x2 = x.reshape(bsz * seq, d)
    pos_rows = jnp.broadcast_to(positions.astype(F32).reshape(1, bsz * seq), (8, bsz * seq))
    params = (ln1_w, w_in, conv_w, conv_b, dt_bias, a_log, d_skip, ssd_norm_w, q_norm_w, k_norm_w,
              lambda_q1, lambda_k1, lambda_q2, lambda_k2, subln_w, w_out, ln2_w, w_router_group, b_router_group,
              w_router_expert, b_router_expert, w_gate, w_up, w_down)
    for l in range(ln1_w.shape[0]):
        x2 = _layer(l, x2, pos_rows, bsz, seq, *[p[l] for p in params])
    return x2.reshape(bsz, seq, d)
```

```python
import functools
import math

import jax
import jax.numpy as jnp
from jax import lax
from jax.experimental import pallas as pl
from jax.experimental.pallas import tpu as pltpu

F32 = jnp.float32
BF16 = jnp.bfloat16
I32 = jnp.int32
U32 = jnp.uint32
HIGHEST = lax.Precision.HIGHEST

D_MODEL = 2048
SSD_WIDTH = 1024
ATTN_WIDTH = 1024
SSD_HEAD_DIM = 64
SSD_HEADS = 16
SSD_GROUPS = 2
SSD_HEADS_PER_GROUP = SSD_HEADS // SSD_GROUPS
SSD_STATE = 128
SSD_CONV = 4
SSD_CHUNK = 128
SSD_CONV_DIM = SSD_WIDTH + 2 * SSD_GROUPS * SSD_STATE
ATTN_V_DIM = 128
ATTN_HEADS = 8
ATTN_QK_DIM = 64
ROPE_THETA = 500000.0
ROPE_DIM = 16
N_EXPERT_GROUPS = 4
EXPERTS_PER_GROUP = 8
N_EXPERTS = 32
TOP_K = 2
EXPERT_HIDDEN = 1024
EPS = 1e-6

LANES = 128
NEG_INF = float("-inf")

TM_IN = 512
TQ = 256
ATTN_HB = 8
ONES_ROWS = 16
TM_OUT = 512
TM_RANK = 512
TM_EXP = 256
TM_COMB = 512
ROUTER_ROWS = 40
Q_COL = SSD_WIDTH
K_COL = Q_COL + ATTN_WIDTH
XBC_COL = K_COL + ATTN_WIDTH
U_COLS = XBC_COL + SSD_CONV_DIM
VMEM_LIMIT = 52 * 1024 * 1024
EXPERT_VMEM_LIMIT = 58 * 1024 * 1024


def _cparams(sem):
    return pltpu.CompilerParams(dimension_semantics=sem, vmem_limit_bytes=VMEM_LIMIT)


def _silu(x):
    return x * (1.0 / (1.0 + jnp.exp(-x)))


def _softplus(x):
    return jnp.maximum(x, 0.0) + jnp.log(1.0 + jnp.exp(-jnp.abs(x)))


def _rope_tables(pos_ref, invf_ref, spread_ref):
    tm = pos_ref.shape[1]
    ang_t = invf_ref[...] * pos_ref[0:1, :]
    trig = jnp.concatenate([jnp.cos(ang_t), jnp.sin(ang_t), jnp.zeros((LANES - 2 * ROPE_DIM, tm), F32)], axis=0).T
    t1 = trig.astype(BF16)
    r1 = trig - t1.astype(F32)
    t2 = r1.astype(BF16)
    t3 = (r1 - t2.astype(F32)).astype(BF16)
    spread = spread_ref[...]
    tab = (jnp.dot(t1, spread, preferred_element_type=F32) + jnp.dot(t2, spread, preferred_element_type=F32)
           + jnp.dot(t3, spread, preferred_element_type=F32))
    return tab[:, :LANES], tab[:, LANES:2 * LANES], tab[:, 2 * LANES:]


def _norm_rope(x, w, tables, seg_ones):
    cs, s_lo, s_hi = tables
    half = ROPE_DIM // 2
    ss = jnp.dot((x * x).astype(BF16), seg_ones, preferred_element_type=F32)
    xn = x * lax.rsqrt(ss * (1.0 / ATTN_QK_DIM) + EPS) * w
    return xn * cs + pltpu.roll(xn, LANES - half, 1) * s_lo + pltpu.roll(xn, half, 1) * s_hi


def _inproj_kernel(x_ref, lnw_ref, w_ref, wvt_ref, wdt_ref, pos_ref, invf_ref, qw_ref, kw_ref, ones_ref, spread_ref,
                   u_ref, vt_ref, dt_ref):
    x = x_ref[...]
    ms = jnp.mean(x * x, axis=-1, keepdims=True)
    xn = (x * lax.rsqrt(ms + EPS) * lnw_ref[...]).astype(BF16)
    tables = _rope_tables(pos_ref, invf_ref, spread_ref)
    for c0, c1, head_w in ((0, SSD_WIDTH, None), (Q_COL, K_COL, qw_ref), (K_COL, XBC_COL, kw_ref), (XBC_COL, U_COLS, None)):
        acc = jnp.dot(xn, w_ref[:, c0:c1], preferred_element_type=F32)
        if head_w is None:
            u_ref[:, c0:c1] = acc.astype(BF16)
        else:
            for hb in range(ATTN_HEADS):
                blk = _norm_rope(acc[:, hb * LANES:(hb + 1) * LANES], head_w[...], tables, ones_ref[...])
                u_ref[:, c0 + hb * LANES:c0 + (hb + 1) * LANES] = blk.astype(BF16)
    dt_ref[...] = jnp.dot(xn, wdt_ref[...], preferred_element_type=F32)
    vt_ref[...] = lax.dot_general(wvt_ref[...], xn, (((1,), (1,)), ((), ())),
                                  preferred_element_type=F32).astype(BF16)


def _in_proj(x2, ln_w, w_main, w_vt, w_dt, pos_rows, invf_col, qw_lanes, kw_lanes, seg_ones, spread):
    n = x2.shape[0]
    resident = lambda shape: pl.BlockSpec(shape, lambda i: (0, 0), pipeline_mode=pl.Buffered(1))
    return pl.pallas_call(
        _inproj_kernel,
        grid=(n // TM_IN,),
        in_specs=[
            pl.BlockSpec((TM_IN, D_MODEL), lambda i: (i, 0)),
            resident((1, D_MODEL)),
            resident((D_MODEL, U_COLS)),
            resident((ATTN_WIDTH, D_MODEL)),
            resident((D_MODEL, LANES)),
            pl.BlockSpec((8, TM_IN), lambda i: (0, i)),
            resident((ROPE_DIM, 1)), resident((1, LANES)), resident((1, LANES)), resident((LANES, LANES)),
            resident((LANES, 3 * LANES)),
        ],
        out_specs=[
            pl.BlockSpec((TM_IN, U_COLS), lambda i: (i, 0)),
            pl.BlockSpec((ATTN_WIDTH, TM_IN), lambda i: (0, i)),
            pl.BlockSpec((TM_IN, LANES), lambda i: (i, 0)),
        ],
        out_shape=[
            jax.ShapeDtypeStruct((n, U_COLS), BF16),
            jax.ShapeDtypeStruct((ATTN_WIDTH, n), BF16),
            jax.ShapeDtypeStruct((n, LANES), F32),
        ],
        compiler_params=_cparams(("parallel",)),
        name="in_proj",
    )(x2, ln_w, w_main, w_vt, w_dt, pos_rows, invf_col, qw_lanes, kw_lanes, seg_ones, spread)


def _ssd_kernel(z_ref, xbc_ref, dt_ref, convw_ref, convb_ref, dtb_ref, aneg_ref, dskip_ref, normw_ref,
                y_ref, xp_ref, st_ref, yacc_ref):
    L = SSD_CHUNK
    P = SSD_HEAD_DIM

    @pl.when(pl.program_id(1) == 0)
    def _():
        xp_ref[0:8, :] = jnp.zeros((8, SSD_CONV_DIM), F32)
        st_ref[...] = jnp.zeros_like(st_ref)

    xp_ref[8:8 + L, :] = xbc_ref[...].astype(F32)
    acc = jnp.broadcast_to(convb_ref[...], (L, SSD_CONV_DIM))
    for k in range(SSD_CONV):
        acc = acc + xp_ref[5 + k:5 + k + L, :] * convw_ref[k:k + 1, :]
    xp_ref[0:8, :] = xp_ref[L:L + 8, :]
    xc = _silu(acc)

    dt = _softplus(dt_ref[...] + dtb_ref[...])
    a = dt * aneg_ref[...]
    row = lax.broadcasted_iota(I32, (L, L), 0)
    col = lax.broadcasted_iota(I32, (L, L), 1)
    causal = row >= col
    a_cs = jnp.dot(causal.astype(F32), a, precision=HIGHEST, preferred_element_type=F32)
    a_last = a_cs[L - 1:L, :]
    ea = jnp.exp(a_cs)
    dsdt = jnp.exp(a_last - a_cs) * dt
    cd = jnp.exp(a_last)
    a_cs_t = a_cs.T
    dt_t = dt.T
    dsdt_t = dsdt.T

    for g in range(SSD_GROUPS):
        b_g = xc[:, SSD_WIDTH + g * SSD_STATE:SSD_WIDTH + (g + 1) * SSD_STATE]
        c_off = SSD_WIDTH + SSD_GROUPS * SSD_STATE
        c_g = xc[:, c_off + g * SSD_STATE:c_off + (g + 1) * SSD_STATE]
        cb = lax.dot_general(c_g.astype(BF16), b_g.astype(BF16), (((1,), (1,)), ((), ())),
                             preferred_element_type=F32)
        b_gt = b_g.T
        low = lax.broadcasted_iota(I32, (L, LANES), 1) < P
        for pair in range(g * SSD_HEADS_PER_GROUP // 2, (g + 1) * SSD_HEADS_PER_GROUP // 2):
            h0 = 2 * pair
            xs_p = xc[:, h0 * P:(h0 + 2) * P].astype(BF16)
            s_prev = st_ref[pair]
            rhs = jnp.concatenate([xs_p, s_prev.astype(BF16)], axis=0)
            ys, news = [], []
            for h in (h0, h0 + 1):
                seg = a_cs[:, h:h + 1] - a_cs_t[h:h + 1, :]
                dec = jnp.exp(jnp.where(causal, seg, NEG_INF))
                m = (cb * dec * dt_t[h:h + 1, :]).astype(BF16)
                c_s = (c_g * ea[:, h:h + 1]).astype(BF16)
                lhs = jnp.concatenate([m, c_s], axis=1)
                ys.append(jnp.dot(lhs, rhs, preferred_element_type=F32))
                bw = (b_gt * dsdt_t[h:h + 1, :]).astype(BF16)
                news.append(jnp.dot(bw, xs_p, preferred_element_type=F32))
            yacc_ref[:, h0 * P:(h0 + 2) * P] = jnp.where(low, ys[0], ys[1])
            cd_p = jnp.where(low[0:1, :], cd[:, h0:h0 + 1], cd[:, h0 + 1:h0 + 2])
            st_ref[pair] = s_prev * cd_p + jnp.where(low, news[0], news[1])

    y = yacc_ref[...] + xc[:, :SSD_WIDTH] * dskip_ref[...]
    y = y * _silu(z_ref[...].astype(F32))
    gw = SSD_WIDTH // SSD_GROUPS
    for g in range(SSD_GROUPS):
        yg = y[:, g * gw:(g + 1) * gw]
        ms = jnp.mean(yg * yg, axis=-1, keepdims=True)
        y_ref[:, g * gw:(g + 1) * gw] = (yg * lax.rsqrt(ms + EPS) * normw_ref[:, g * gw:(g + 1) * gw]).astype(BF16)


def _ssd(u, dt_raw, conv_w, conv_b, dt_bias, a_neg, dskip_lanes, norm_w, bsz, seq):
    n = u.shape[0]
    nc = seq // SSD_CHUNK
    xbc_blk = (SSD_WIDTH + 2 * ATTN_WIDTH) // SSD_CONV_DIM
    full = lambda shape: pl.BlockSpec(shape, lambda b, c: (0, 0))
    return pl.pallas_call(
        _ssd_kernel,
        grid=(bsz, nc),
        in_specs=[
            pl.BlockSpec((SSD_CHUNK, SSD_WIDTH), lambda b, c: (b * nc + c, 0)),
            pl.BlockSpec((SSD_CHUNK, SSD_CONV_DIM), lambda b, c: (b * nc + c, xbc_blk)),
            pl.BlockSpec((SSD_CHUNK, LANES), lambda b, c: (b * nc + c, 0)),
            full((SSD_CONV, SSD_CONV_DIM)),
            full((1, SSD_CONV_DIM)),
            full((1, LANES)),
            full((1, LANES)),
            full((1, SSD_WIDTH)),
            full((1, SSD_WIDTH)),
        ],
        out_specs=pl.BlockSpec((SSD_CHUNK, SSD_WIDTH), lambda b, c: (b * nc + c, 0)),
        out_shape=jax.ShapeDtypeStruct((n, SSD_WIDTH), BF16),
        scratch_shapes=[
            pltpu.VMEM((SSD_CHUNK + 8, SSD_CONV_DIM), F32),
            pltpu.VMEM((SSD_HEADS // 2, SSD_STATE, 2 * SSD_HEAD_DIM), F32),
            pltpu.VMEM((SSD_CHUNK, SSD_WIDTH), F32),
        ],
        compiler_params=_cparams(("parallel", "arbitrary")),
        name="ssd",
    )(u, u, dt_raw, conv_w, conv_b, dt_bias, a_neg, dskip_lanes, norm_w)


def _rope_spread():
    half = ROPE_DIM // 2
    lane = jnp.arange(LANES)
    d = lane % ATTN_QK_DIM
    src = jnp.arange(LANES)[:, None]
    cos_src = jnp.where(d < ROPE_DIM, d % half, half)
    cos_tab = (src == cos_src[None, :]).astype(F32)
    lo_tab = -((src == (2 * half + d)[None, :]) & (d < half)[None, :]).astype(F32)
    hi_tab = ((src == (2 * half + d - half)[None, :]) & ((d >= half) & (d < ROPE_DIM))[None, :]).astype(F32)
    return jnp.concatenate([cos_tab, lo_tab, hi_tab], axis=1).astype(BF16)


def _attn_kernel(q_ref, k_ref, vt_ref, lamv_ref, subw_ref, o_ref, acc_ref, *, lam_init):
    qi = pl.program_id(2)
    lane = lax.broadcasted_iota(I32, (TQ, LANES), 1)
    qs = []
    for hb in range(ATTN_HB):
        q = q_ref[:, hb * LANES:(hb + 1) * LANES]
        zero = jnp.zeros_like(q)
        qs.append(jnp.concatenate([jnp.where(lane < ATTN_QK_DIM, q, zero),
                                   jnp.where(lane >= ATTN_QK_DIM, q, zero)], axis=0))
    acc_ref[...] = jnp.zeros_like(acc_ref)
    kv_idx = lax.broadcasted_iota(I32, (TQ, 2 * TQ), 0)
    q_idx = lax.broadcasted_iota(I32, (TQ, 2 * TQ), 1) & (TQ - 1)
    nt = (((1,), (1,)), ((), ()))

    def block(j, carry, masked):
        off = pl.multiple_of(j * TQ, TQ)
        ss = []
        for hb in range(ATTN_HB):
            kb = k_ref[pl.ds(off, TQ), hb * LANES:(hb + 1) * LANES]
            ss.append(lax.dot_general(kb, qs[hb], nt, preferred_element_type=F32))
        new, ps, alphas = [], [], []
        for hb in range(ATTN_HB):
            m_old = carry[hb]
            s = ss[hb]
            if masked:
                s = jnp.where(kv_idx <= q_idx, s, NEG_INF)
            m_new = jnp.maximum(m_old, jnp.max(s, axis=0, keepdims=True))
            alphas.append(jnp.exp2(m_old - m_new))
            ps.append(jnp.exp2(s - m_new).astype(BF16))
            new.append(m_new)
        pvs = []
        for hb in range(ATTN_HB):
            vb = jnp.concatenate([vt_ref[hb * LANES:(hb + 1) * LANES, pl.ds(off, TQ)], ones_rows], axis=0)
            pvs.append(jnp.dot(vb, ps[hb], preferred_element_type=F32))
        for hb in range(ATTN_HB):
            acc_ref[hb] = alphas[hb] * acc_ref[hb] + pvs[hb]
        return tuple(new)

    ones_rows = jnp.ones((ONES_ROWS, TQ), BF16)
    init = (jnp.full((1, 2 * TQ), NEG_INF, F32),) * ATTN_HB
    carry = lax.fori_loop(0, qi, lambda j, cr: block(j, cr, False), init)
    block(qi, carry, True)

    lv = lamv_ref[...]
    lam = (jnp.exp(jnp.sum(lv[0:1] * lv[1:2], axis=1, keepdims=True))
           - jnp.exp(jnp.sum(lv[2:3] * lv[3:4], axis=1, keepdims=True)) + lam_init)
    for hb in range(ATTN_HB):
        acc = acc_ref[hb]
        o2 = acc[:ATTN_V_DIM] * (1.0 / acc[ATTN_V_DIM:ATTN_V_DIM + 1])
        o_t = o2[:, :TQ] - lam * o2[:, TQ:]
        ms = jnp.mean(o_t * o_t, axis=0, keepdims=True)
        o_t = o_t * lax.rsqrt(ms + EPS) * subw_ref[...] * (1.0 - lam_init)
        o_ref[:, hb * LANES:(hb + 1) * LANES] = o_t.T.astype(BF16)


def _attention(u, v_t, lam_vecs, subw_col, bsz, seq, lam_init):
    n = u.shape[0]
    nq = seq // TQ
    w = ATTN_HB * ATTN_V_DIM
    q_blk, k_blk = Q_COL // w, K_COL // w
    return pl.pallas_call(
        functools.partial(_attn_kernel, lam_init=lam_init),
        grid=(bsz, ATTN_HEADS // ATTN_HB, nq),
        in_specs=[
            pl.BlockSpec((TQ, w), lambda b, h, i: (b * nq + i, q_blk + h)),
            pl.BlockSpec((seq, w), lambda b, h, i: (b, k_blk + h)),
            pl.BlockSpec((w, seq), lambda b, h, i: (h, b)),
            pl.BlockSpec((4, ATTN_QK_DIM), lambda b, h, i: (0, 0)),
            pl.BlockSpec((ATTN_V_DIM, 1), lambda b, h, i: (0, 0)),
        ],
        out_specs=pl.BlockSpec((TQ, w), lambda b, h, i: (b * nq + i, h)),
        out_shape=jax.ShapeDtypeStruct((n, ATTN_WIDTH), BF16),
        scratch_shapes=[pltpu.VMEM((ATTN_HB, ATTN_V_DIM + ONES_ROWS, 2 * TQ), F32)],
        compiler_params=_cparams(("parallel", "parallel", "arbitrary")),
        name="attn",
    )(u, u, v_t, lam_vecs, subw_col)


def _pack_halves(x):
    c = x.shape[1] // 2
    lo = pltpu.bitcast(x[:, :c].astype(BF16).astype(F32), U32) >> 16
    hi = pltpu.bitcast(x[:, c:].astype(BF16).astype(F32), U32) & jnp.uint32(0xFFFF0000)
    return hi | lo


TOKEN_ROWS = D_MODEL // 2 // LANES


def _store_token_rows(ref, first_piece, packed):
    tm = packed.shape[0]
    for j in range(packed.shape[1] // LANES):
        ref[pl.ds(first_piece + j, tm, stride=TOKEN_ROWS), :] = packed[:, j * LANES:(j + 1) * LANES]


def _load_token_rows(ref, tm):
    return jnp.concatenate([ref[pl.ds(s, tm, stride=TOKEN_ROWS), :] for s in range(TOKEN_ROWS)], axis=1)


def _unpack_halves(w, dtype=BF16):
    lo = pltpu.bitcast(w << 16, F32).astype(dtype)
    hi = pltpu.bitcast(w & jnp.uint32(0xFFFF0000), F32).astype(dtype)
    return lo, hi


def _outproj_kernel(x_ref, ys_ref, ya_ref, wos_ref, woa_ref, ln2_ref, wr_ref, br_ref,
                    h_ref, hnp_ref, meta_ref, cnt_ref):
    tm = x_ref.shape[0]
    h = (x_ref[...]
         + jnp.dot(ys_ref[...], wos_ref[...], preferred_element_type=F32)
         + jnp.dot(ya_ref[...], woa_ref[...], preferred_element_type=F32))
    h_ref[...] = h
    ms = jnp.mean(h * h, axis=-1, keepdims=True)
    hn = h * lax.rsqrt(ms + EPS) * ln2_ref[...]
    _store_token_rows(hnp_ref, 0, _pack_halves(hn))

    lg_t = lax.dot_general(wr_ref[...], hn.astype(BF16), (((1,), (1,)), ((), ())),
                           preferred_element_type=F32)
    lg = lg_t[0:ROUTER_ROWS, :] + br_ref[0:ROUTER_ROWS, :]
    row = lax.broadcasted_iota(I32, (ROUTER_ROWS, tm), 0).astype(F32)
    big = float(LANES)
    gl = jnp.where(row < N_EXPERT_GROUPS, lg, NEG_INF)
    gmax = jnp.max(gl, axis=0, keepdims=True)
    gsel = jnp.min(jnp.where(gl == gmax, row, big), axis=0, keepdims=True)
    g_w = 1.0 / jnp.sum(jnp.exp(gl - gmax), axis=0, keepdims=True)
    eid = row - N_EXPERT_GROUPS
    lo = gsel * EXPERTS_PER_GROUP
    emask = (eid >= lo) & (eid < lo + EXPERTS_PER_GROUP)
    el = jnp.where(emask, lg, NEG_INF)
    m1 = jnp.max(el, axis=0, keepdims=True)
    i1 = jnp.min(jnp.where(el == m1, eid, big), axis=0, keepdims=True)
    el2 = jnp.where(eid == i1, NEG_INF, el)
    m2 = jnp.max(el2, axis=0, keepdims=True)
    i2 = jnp.min(jnp.where(el2 == m2, eid, big), axis=0, keepdims=True)
    e2 = jnp.exp(m2 - m1)
    w1 = g_w / (1.0 + e2)
    w2 = g_w * e2 / (1.0 + e2)
    mrow = lax.broadcasted_iota(I32, (LANES, tm), 0)
    meta_ref[...] = jnp.where(mrow == 0, i1, jnp.where(mrow == 1, i2, jnp.where(mrow == 2, w1, jnp.where(mrow == 3, w2, 0.0))))
    hits = (eid == i1).astype(F32) + (eid == i2).astype(F32)
    cnt_ref[...] = jnp.broadcast_to(jnp.sum(hits, axis=1, keepdims=True), cnt_ref.shape)


def _out_proj(x2, y_ssd, y_att, wo_s, wo_a, ln2_w, w_router_t, b_router):
    n = x2.shape[0]
    full = lambda shape: pl.BlockSpec(shape, lambda i: (0, 0), pipeline_mode=pl.Buffered(1))
    return pl.pallas_call(
        _outproj_kernel,
        grid=(n // TM_OUT,),
        in_specs=[
            pl.BlockSpec((TM_OUT, D_MODEL), lambda i: (i, 0)),
            pl.BlockSpec((TM_OUT, SSD_WIDTH), lambda i: (i, 0)),
            pl.BlockSpec((TM_OUT, ATTN_WIDTH), lambda i: (i, 0)),
            full((SSD_WIDTH, D_MODEL)), full((ATTN_WIDTH, D_MODEL)),
            full((1, D_MODEL)), full((LANES, D_MODEL)), full((LANES, 1)),
        ],
        out_specs=[
            pl.BlockSpec((TM_OUT, D_MODEL), lambda i: (i, 0)),
            pl.BlockSpec((TM_OUT * TOKEN_ROWS, LANES), lambda i: (i, 0)),
            pl.BlockSpec((LANES, TM_OUT), lambda i: (0, i)),
            pl.BlockSpec((None, ROUTER_ROWS, LANES), lambda i: (i, 0, 0)),
        ],
        out_shape=[
            jax.ShapeDtypeStruct((n, D_MODEL), F32),
            jax.ShapeDtypeStruct((n * TOKEN_ROWS, LANES), U32),
            jax.ShapeDtypeStruct((LANES, n), F32),
            jax.ShapeDtypeStruct((n // TM_OUT, ROUTER_ROWS, LANES), F32),
        ],
        compiler_params=_cparams(("parallel",)),
        name="out_proj",
    )(x2, y_ssd, y_att, wo_s, wo_a, ln2_w, w_router_t, b_router)


def _rank_kernel(meta_ref, offs_ref, dest_ref, run_ref):
    tm = meta_ref.shape[1]
    meta = meta_ref[...]
    row = lax.broadcasted_iota(I32, (N_EXPERTS, tm), 0).astype(F32)
    oh0 = (row == meta[0:1, :]).astype(F32)
    oh1 = (row == meta[1:2, :]).astype(F32)
    oh = oh0 + oh1

    @pl.when(pl.program_id(0) == 0)
    def _():
        run_ref[...] = jnp.zeros_like(run_ref)

    r = lax.broadcasted_iota(I32, (tm, tm), 0)
    c = lax.broadcasted_iota(I32, (tm, tm), 1)
    before = jnp.dot(oh.astype(BF16), (r < c).astype(BF16), preferred_element_type=F32)
    base = before + jnp.tile(run_ref[...] + offs_ref[...], (1, tm // LANES))
    d0 = jnp.sum(oh0 * base, axis=0, keepdims=True)
    d1 = jnp.sum(oh1 * base, axis=0, keepdims=True)
    drow = lax.broadcasted_iota(I32, (8, tm), 0)
    dest_ref[...] = jnp.where(drow == 0, d0, jnp.where(drow == 1, d1, 0.0)).astype(I32)
    run_ref[...] = run_ref[...] + jnp.sum(oh, axis=1, keepdims=True)


def _rank(meta_t, offsets):
    n = meta_t.shape[1]
    return pl.pallas_call(
        _rank_kernel,
        grid=(n // TM_RANK,),
        in_specs=[pl.BlockSpec((8, TM_RANK), lambda i: (0, i)), pl.BlockSpec((N_EXPERTS, LANES), lambda i: (0, 0))],
        out_specs=pl.BlockSpec((8, TM_RANK), lambda i: (0, i)),
        out_shape=jax.ShapeDtypeStruct((8, n), I32),
        scratch_shapes=[pltpu.VMEM((N_EXPERTS, LANES), F32)],
        compiler_params=_cparams(("arbitrary",)),
        name="rank",
    )(meta_t, offsets)


def _token_copy(src_ref, src_token, dst_ref, dst_token, sem):
    src = src_ref.at[pl.ds(pl.multiple_of(src_token * TOKEN_ROWS, TOKEN_ROWS), TOKEN_ROWS)]
    dst = dst_ref.at[pl.ds(pl.multiple_of(dst_token * TOKEN_ROWS, TOKEN_ROWS), TOKEN_ROWS)]
    return pltpu.make_async_copy(src, dst, sem)


def _invert_kernel(d0_ref, d1_ref, valid_ref, inv_ref):
    n = d0_ref.shape[0]

    def clear_tile(tile, _):
        def clear(r, _):
            inv_ref[tile * TM_EXP + r] = 0
            return 0

        lax.fori_loop(valid_ref[tile], TM_EXP, clear, 0)
        return 0

    lax.fori_loop(0, valid_ref.shape[0], clear_tile, 0)

    def put(t, _):
        inv_ref[d0_ref[t]] = t
        inv_ref[d1_ref[t]] = n + t
        return 0

    lax.fori_loop(0, n, put, 0, unroll=8)


def _invert(dest0, dest1, tile_valid):
    smem = pl.BlockSpec(memory_space=pltpu.SMEM)
    return pl.pallas_call(
        _invert_kernel,
        in_specs=[smem, smem, smem],
        out_specs=smem,
        out_shape=jax.ShapeDtypeStruct((tile_valid.shape[0] * TM_EXP,), I32),
        name="invert",
    )(dest0, dest1, tile_valid)


CAST_ROWS = 256


def _cast_weight(src_ref, dst_ref):
    def body(i, _):
        rows = pl.ds(pl.multiple_of(i * CAST_ROWS, CAST_ROWS), CAST_ROWS)
        dst_ref[rows, :] = src_ref[rows, :].astype(BF16)
        return 0

    lax.fori_loop(0, src_ref.shape[0] // CAST_ROWS, body, 0)


S_EXPERT, S_FIRST, S_SLOT, S_NEXT, S_HAS_NEXT, S_USED, S_VALID = range(7)
WEIGHT_DMA_PRIORITY = 1


def _load_expert_weights(t, sched_ref, triples, sems):
    def copies(expert, slot):
        return [pltpu.make_async_copy(w.at[expert], stage.at[slot], sems.at[i, slot])
                for i, (w, stage, _) in enumerate(triples)]

    @pl.when(sched_ref[S_FIRST, t] == 1)
    def _():
        slot = sched_ref[S_SLOT, t]

        @pl.when(t == 0)
        def _():
            for cp in copies(sched_ref[S_EXPERT, t], slot):
                cp.start(priority=WEIGHT_DMA_PRIORITY)

        for cp in copies(sched_ref[S_EXPERT, t], slot):
            cp.wait()

        @pl.when(sched_ref[S_HAS_NEXT, t] == 1)
        def _():
            for cp in copies(sched_ref[S_NEXT, t], 1 - slot):
                cp.start(priority=WEIGHT_DMA_PRIORITY)

        for _, stage, dst in triples:
            _cast_weight(stage.at[slot], dst)


ROW_DMA_GROUPS = 4
GATHER_AHEAD = 3


def _tile_rows(sched_ref, tile, group):
    valid = sched_ref[S_VALID, tile]
    per = TM_EXP // ROW_DMA_GROUPS
    for r in range(group * per, (group + 1) * per):
        ok = r < valid
        yield r, jnp.where(ok, tile * TM_EXP + r, 0), ok


def _experts_up_kernel(sched_ref, inv_ref, hnp_hbm, wg_hbm, wu_hbm, hid_ref,
                       xbuf_ref, wg_stage, wu_stage, wgb_ref, wub_ref, wsems, gsem, *, n_tokens):
    t = pl.program_id(0)
    used = sched_ref[S_USED, 0]
    nbuf = GATHER_AHEAD + 1
    cur = t % nbuf
    ahead = (t + GATHER_AHEAD) % nbuf
    _load_expert_weights(t, sched_ref, [(wg_hbm, wg_stage, wgb_ref), (wu_hbm, wu_stage, wub_ref)], wsems)

    def start_rows(tile, buf, group):
        for r, slot, _ in _tile_rows(sched_ref, tile, group):
            token = inv_ref[slot] & (n_tokens - 1)
            _token_copy(hnp_hbm, token, xbuf_ref.at[buf], r, gsem.at[buf]).start()

    def wait_rows(buf):
        pltpu.make_async_copy(hnp_hbm.at[pl.ds(0, TM_EXP * TOKEN_ROWS)], xbuf_ref.at[buf], gsem.at[buf]).wait()

    for first in range(GATHER_AHEAD):
        @pl.when((t == 0) & (first < used))
        def _():
            for g in range(ROW_DMA_GROUPS):
                start_rows(first, first, g)

    def multiply(fetch_ahead):
        wait_rows(cur)
        c = D_MODEL // 2
        x_lo, x_hi = _unpack_halves(_load_token_rows(xbuf_ref.at[cur], TM_EXP))
        pieces = ((x_lo, wgb_ref, 0), (x_hi, wgb_ref, c), (x_lo, wub_ref, 0), (x_hi, wub_ref, c))
        acc = []
        for g, (x, w_ref, row0) in enumerate(pieces):
            if fetch_ahead:
                start_rows(t + GATHER_AHEAD, ahead, g)
            acc.append(jnp.dot(x, w_ref[row0:row0 + c, :], preferred_element_type=F32))
        hid_ref[...] = (_silu(acc[0] + acc[1]) * (acc[2] + acc[3])).astype(BF16)

    @pl.when(t + GATHER_AHEAD < used)
    def _():
        multiply(True)

    @pl.when((t + GATHER_AHEAD >= used) & (t < used))
    def _():
        multiply(False)

    @pl.when(t >= used)
    def _():
        hid_ref[...] = jnp.zeros_like(hid_ref)


def _experts_down_kernel(sched_ref, inv_ref, hid_ref, wd_hbm, y_hbm,
                         ybuf_ref, wd_stage, wdb_ref, wsems, ssem, *, n_tokens):
    t = pl.program_id(0)
    used = sched_ref[S_USED, 0]
    cur = t % 2
    _load_expert_weights(t, sched_ref, [(wd_hbm, wd_stage, wdb_ref)], wsems)

    def start_rows(tile, buf, group):
        for r, slot, ok in _tile_rows(sched_ref, tile, group):
            dst = jnp.where(ok, inv_ref[slot], TOP_K * n_tokens + r)
            _token_copy(ybuf_ref.at[buf], r, y_hbm, dst, ssem.at[buf]).start(priority=r % 2)

    def wait_rows(buf):
        pltpu.make_async_copy(ybuf_ref.at[buf], y_hbm.at[pl.ds(0, TM_EXP * TOKEN_ROWS)], ssem.at[buf]).wait()

    def multiply(send_previous):
        hid = hid_ref[...]
        q = D_MODEL // 4
        for half in range(2):
            if send_previous:
                start_rows(t - 1, 1 - cur, 2 * half)
            lo = jnp.dot(hid, wdb_ref[:, half * q:(half + 1) * q], preferred_element_type=F32)
            if send_previous:
                start_rows(t - 1, 1 - cur, 2 * half + 1)
            hi = jnp.dot(hid, wdb_ref[:, (half + 2) * q:(half + 3) * q], preferred_element_type=F32)
            _store_token_rows(ybuf_ref.at[cur], half * (q // LANES), _pack_halves(jnp.concatenate([lo, hi], axis=1)))

    @pl.when((t >= 2) & (t < used))
    def _():
        wait_rows(cur)

    @pl.when(t == 0)
    def _():
        ybuf_ref[1] = jnp.zeros(ybuf_ref.shape[1:], ybuf_ref.dtype)
        spare = pltpu.make_async_copy(ybuf_ref.at[1], y_hbm.at[pl.ds(TOP_K * n_tokens * TOKEN_ROWS, TM_EXP * TOKEN_ROWS)],
                                      ssem.at[1])
        spare.start()
        spare.wait()
        multiply(False)

    @pl.when((t > 0) & (t < used))
    def _():
        multiply(True)

    @pl.when(t == used - 1)
    def _():
        for g in range(ROW_DMA_GROUPS):
            start_rows(t, cur, g)

        @pl.when(t > 0)
        def _():
            wait_rows(1 - cur)

        wait_rows(cur)


def _tile_schedule(counts, n_tiles):
    tiles_per_expert = (counts + TM_EXP - 1) // TM_EXP
    tile_end = jnp.cumsum(tiles_per_expert)
    ids = jnp.arange(n_tiles, dtype=I32)
    raw = jnp.sum(ids[:, None] >= tile_end[None, :], axis=1)
    expert = jnp.minimum(raw, N_EXPERTS - 1).astype(I32)
    first = jnp.concatenate([jnp.ones((1,), I32), (expert[1:] != expert[:-1]).astype(I32)])
    slot = (jnp.cumsum(first) - 1) % 2
    start_idx = jnp.where(first == 1, ids, n_tiles)
    next_start = jnp.concatenate([lax.cummin(start_idx, reverse=True)[1:], jnp.full((1,), n_tiles, I32)])
    has_next = (next_start < n_tiles).astype(I32)
    next_expert = jnp.sum(jnp.where(ids[None, :] == next_start[:, None], expert[None, :], 0), axis=1)
    used = jnp.full((n_tiles,), tile_end[-1], I32)
    onehot = expert[:, None] == jnp.arange(N_EXPERTS)[None, :]
    tile_in_expert = ids - jnp.sum(jnp.where(onehot, (tile_end - tiles_per_expert)[None, :], 0), axis=1)
    rows_left = jnp.sum(jnp.where(onehot, counts[None, :], 0), axis=1) - tile_in_expert * TM_EXP
    valid = jnp.where(raw < N_EXPERTS, jnp.clip(rows_left, 0, TM_EXP), 0)
    return jnp.stack([expert, first, slot, next_expert, has_next, used, valid]).astype(I32)


def _experts(sched, inv, hnp, wg, wu, wd):
    n = hnp.shape[0] // TOKEN_ROWS
    n_slots = inv.shape[0]
    token_buf = lambda nbuf: pltpu.VMEM((nbuf, TM_EXP * TOKEN_ROWS, LANES), U32)
    row_spec = lambda width: pl.BlockSpec((TM_EXP, width), lambda t, sc, iv: (t, 0))
    hbm = pl.BlockSpec(memory_space=pl.ANY)
    big_vmem = pltpu.CompilerParams(dimension_semantics=("arbitrary",), vmem_limit_bytes=EXPERT_VMEM_LIMIT)
    hid = pl.pallas_call(
        functools.partial(_experts_up_kernel, n_tokens=n),
        grid_spec=pltpu.PrefetchScalarGridSpec(
            num_scalar_prefetch=2,
            grid=(n_slots // TM_EXP,),
            in_specs=[hbm, hbm, hbm],
            out_specs=row_spec(EXPERT_HIDDEN),
            scratch_shapes=[token_buf(GATHER_AHEAD + 1)]
                           + [pltpu.VMEM((2, D_MODEL, EXPERT_HIDDEN), F32)] * 2
                           + [pltpu.VMEM((D_MODEL, EXPERT_HIDDEN), BF16)] * 2
                           + [pltpu.SemaphoreType.DMA((2, 2)), pltpu.SemaphoreType.DMA((GATHER_AHEAD + 1,))],
        ),
        out_shape=jax.ShapeDtypeStruct((n_slots, EXPERT_HIDDEN), BF16),
        compiler_params=big_vmem,
        name="experts_up",
    )(sched, inv, hnp, wg, wu)
    return pl.pallas_call(
        functools.partial(_experts_down_kernel, n_tokens=n),
        grid_spec=pltpu.PrefetchScalarGridSpec(
            num_scalar_prefetch=2,
            grid=(n_slots // TM_EXP,),
            in_specs=[row_spec(EXPERT_HIDDEN), hbm],
            out_specs=hbm,
            scratch_shapes=[token_buf(2),
                            pltpu.VMEM((2, EXPERT_HIDDEN, D_MODEL), F32), pltpu.VMEM((EXPERT_HIDDEN, D_MODEL), BF16),
                            pltpu.SemaphoreType.DMA((1, 2)), pltpu.SemaphoreType.DMA((2,))],
        ),
        out_shape=jax.ShapeDtypeStruct(((TOP_K * n + TM_EXP) * TOKEN_ROWS, LANES), U32),
        compiler_params=big_vmem,
        name="experts_down",
    )(sched, inv, hid, wd)


def _combine_kernel(h_ref, meta_ref, y0_ref, y1_ref, o_ref):
    c = D_MODEL // 2
    meta = meta_ref[...].T
    w0, w1 = meta[:, 2:3], meta[:, 3:4]
    tm = h_ref.shape[0]
    lo0, hi0 = _unpack_halves(_load_token_rows(y0_ref, tm), F32)
    lo1, hi1 = _unpack_halves(_load_token_rows(y1_ref, tm), F32)
    o_ref[:, :c] = h_ref[:, :c] + (lo0 * w0 + lo1 * w1)
    o_ref[:, c:] = h_ref[:, c:] + (hi0 * w0 + hi1 * w1)


def _combine(h, meta, y):
    n = h.shape[0]
    nb = n // TM_COMB
    return pl.pallas_call(
        _combine_kernel,
        grid=(nb,),
        in_specs=[
            pl.BlockSpec((TM_COMB, D_MODEL), lambda i: (i, 0)),
            pl.BlockSpec((LANES, TM_COMB), lambda i: (0, i)),
            pl.BlockSpec((TM_COMB * TOKEN_ROWS, LANES), lambda i: (i, 0)),
            pl.BlockSpec((TM_COMB * TOKEN_ROWS, LANES), lambda i: (nb + i, 0)),
        ],
        out_specs=pl.BlockSpec((TM_COMB, D_MODEL), lambda i: (i, 0)),
        out_shape=jax.ShapeDtypeStruct((n, D_MODEL), F32),
        compiler_params=_cparams(("parallel",)),
        name="combine",
    )(h, meta, y, y)


def _lambda_init(layer_idx):
    return 0.8 - 0.6 * math.exp(-0.3 * layer_idx)


def _pad_lanes(v, width=LANES):
    return jnp.pad(v, ((0, 0), (0, width - v.shape[1])))


def _layer(l, x2, pos_rows, bsz, seq, ln1_w, w_in, conv_w, conv_b, dt_bias, a_log, d_skip, ssd_norm_w,
           q_norm_w, k_norm_w, lambda_q1, lambda_k1, lambda_q2, lambda_k2, subln_w, w_out, ln2_w,
           w_router_group, b_router_group, w_router_expert, b_router_expert, w_gate, w_up, w_down):
    n = x2.shape[0]
    c_z, c_xbc, c_dt = SSD_WIDTH, SSD_WIDTH + SSD_CONV_DIM, SSD_WIDTH + SSD_CONV_DIM + SSD_HEADS
    c_q, c_k = c_dt + ATTN_WIDTH, c_dt + 2 * ATTN_WIDTH
    w_main = jnp.concatenate([w_in[:, :c_z], w_in[:, c_dt:c_q], w_in[:, c_q:c_k], w_in[:, c_z:c_xbc]],
                             axis=1).astype(BF16)
    w_vt = w_in[:, c_k:].T.astype(BF16)
    w_dt = _pad_lanes(w_in[:, c_xbc:c_dt]).astype(BF16)

    inv_freq = jnp.power(ROPE_THETA, -jnp.arange(0, ROPE_DIM, 2, dtype=F32) / ROPE_DIM)
    invf_col = jnp.concatenate([inv_freq, jnp.zeros_like(inv_freq)])[:, None]
    seg_ones = (jnp.arange(LANES)[:, None] // ATTN_QK_DIM == jnp.arange(LANES)[None, :] // ATTN_QK_DIM).astype(BF16)
    q_scale = math.log2(math.e) / math.sqrt(ATTN_QK_DIM)
    u, v_t, dt_raw = _in_proj(x2, ln1_w[None, :], w_main, w_vt, w_dt, pos_rows, invf_col,
                              jnp.tile(q_norm_w, 2)[None, :] * q_scale, jnp.tile(k_norm_w, 2)[None, :],
                              seg_ones, _rope_spread())

    a_neg = _pad_lanes(-jnp.exp(a_log.astype(F32))[None, :])
    y_ssd = _ssd(u, dt_raw, conv_w, conv_b[None, :], _pad_lanes(dt_bias[None, :]), a_neg,
                 jnp.repeat(d_skip, SSD_HEAD_DIM)[None, :], ssd_norm_w[None, :], bsz, seq)

    lam_vecs = jnp.stack([lambda_q1, lambda_k1, lambda_q2, lambda_k2]).astype(F32)
    y_att = _attention(u, v_t, lam_vecs, subln_w[:, None], bsz, seq, _lambda_init(l))

    w_out_b = w_out.astype(BF16)
    w_router_t = _pad_lanes(jnp.concatenate([w_router_group, w_router_expert], axis=1)).T.astype(BF16)
    b_router = _pad_lanes(jnp.concatenate([b_router_group, b_router_expert])[None, :]).T
    h, hnp, meta, tile_counts = _out_proj(x2, y_ssd, y_att, w_out_b[:SSD_WIDTH], w_out_b[SSD_WIDTH:], ln2_w[None, :],
                                          w_router_t, b_router)

    counts = jnp.sum(tile_counts[:, N_EXPERT_GROUPS:N_EXPERT_GROUPS + N_EXPERTS, 0], axis=0).astype(I32)
    padded = (counts + TM_EXP - 1) // TM_EXP * TM_EXP
    offsets = jnp.broadcast_to((jnp.cumsum(padded) - padded).astype(F32)[:, None], (N_EXPERTS, LANES))
    dest = _rank(meta, offsets)
    dest0, dest1 = dest[0], dest[1]
    n_slots = (n * TOP_K + N_EXPERTS * (TM_EXP - 1)) // TM_EXP * TM_EXP
    sched = _tile_schedule(counts, n_slots // TM_EXP)

    inv = _invert(dest0, dest1, sched[S_VALID])
    y = _experts(sched, inv, hnp, w_gate, w_up, w_down)
    return _combine(h, meta, y)


def kernel(x, positions, ln1_w, w_in, conv_w, conv_b, dt_bias, a_log, d_skip, ssd_norm_w, q_norm_w, k_norm_w,
           lambda_q1, lambda_k1, lambda_q2, lambda_k2, subln_w, w_out, ln2_w, w_router_group, b_router_group,
           w_router_expert, b_router_expert, w_gate, w_up, w_down):
    bsz, seq, d = x.shape
    assert d == D_MODEL and seq % TQ == 0 and (bsz * seq) % TM_IN == 0
    x2 = x.reshape(bsz * seq, d)
    pos_rows = jnp.broadcast_to(positions.astype(F32).reshape(1, bsz * seq), (8, bsz * seq))
    params = (ln1_w, w_in, conv_w, conv_b, dt_bias, a_log, d_skip, ssd_norm_w, q_norm_w, k_norm_w,
              lambda_q1, lambda_k1, lambda_q2, lambda_k2, subln_w, w_out, ln2_w, w_router_group, b_router_group,
              w_router_expert, b_router_expert, w_gate, w_up, w_down)
    for l in range(ln1_w.shape[0]):
        x2 = _layer(l, x2, pos_rows, bsz, seq, *[p[l] for p in params])
    return x2.reshape(bsz, seq, d)
```

```python
import functools
import math

import jax
import jax.numpy as jnp
from jax import lax
from jax.experimental import pallas as pl
from jax.experimental.pallas import tpu as pltpu

F32 = jnp.float32
BF16 = jnp.bfloat16
I32 = jnp.int32
U32 = jnp.uint32
HIGHEST = lax.Precision.HIGHEST

D_MODEL = 2048
SSD_WIDTH = 1024
ATTN_WIDTH = 1024
SSD_HEAD_DIM = 64
SSD_HEADS = 16
SSD_GROUPS = 2
SSD_HEADS_PER_GROUP = SSD_HEADS // SSD_GROUPS
SSD_STATE = 128
SSD_CONV = 4
SSD_CHUNK = 128
SSD_CONV_DIM = SSD_WIDTH + 2 * SSD_GROUPS * SSD_STATE
ATTN_V_DIM = 128
ATTN_HEADS = 8
ATTN_QK_DIM = 64
ROPE_THETA = 500000.0
ROPE_DIM = 16
N_EXPERT_GROUPS = 4
EXPERTS_PER_GROUP = 8
N_EXPERTS = 32
TOP_K = 2
EXPERT_HIDDEN = 1024
EPS = 1e-6

LANES = 128
NEG_INF = float("-inf")

TM_IN = 512
TQ = 256
ATTN_HB = 8
ONES_ROWS = 16
TM_OUT = 512
TM_RANK = 512
TM_EXP = 256
TM_COMB = 512
ROUTER_ROWS = 40
Q_COL = SSD_WIDTH
K_COL = Q_COL + ATTN_WIDTH
XBC_COL = K_COL + ATTN_WIDTH
U_COLS = XBC_COL + SSD_CONV_DIM
VMEM_LIMIT = 52 * 1024 * 1024
EXPERT_VMEM_LIMIT = 58 * 1024 * 1024


def _cparams(sem):
    return pltpu.CompilerParams(dimension_semantics=sem, vmem_limit_bytes=VMEM_LIMIT)


def _silu(x):
    return x * (1.0 / (1.0 + jnp.exp(-x)))


def _softplus(x):
    return jnp.maximum(x, 0.0) + jnp.log(1.0 + jnp.exp(-jnp.abs(x)))


def _rope_tables(pos_ref, invf_ref, spread_ref):
    tm = pos_ref.shape[1]
    ang_t = invf_ref[...] * pos_ref[0:1, :]
    trig = jnp.concatenate([jnp.cos(ang_t), jnp.sin(ang_t), jnp.zeros((LANES - 2 * ROPE_DIM, tm), F32)], axis=0).T
    t1 = trig.astype(BF16)
    r1 = trig - t1.astype(F32)
    t2 = r1.astype(BF16)
    t3 = (r1 - t2.astype(F32)).astype(BF16)
    spread = spread_ref[...]
    tab = (jnp.dot(t1, spread, preferred_element_type=F32) + jnp.dot(t2, spread, preferred_element_type=F32)
           + jnp.dot(t3, spread, preferred_element_type=F32))
    return tab[:, :LANES], tab[:, LANES:2 * LANES], tab[:, 2 * LANES:]


def _norm_rope(x, w, tables, seg_ones):
    cs, s_lo, s_hi = tables
    half = ROPE_DIM // 2
    ss = jnp.dot((x * x).astype(BF16), seg_ones, preferred_element_type=F32)
    xn = x * lax.rsqrt(ss * (1.0 / ATTN_QK_DIM) + EPS) * w
    return xn * cs + pltpu.roll(xn, LANES - half, 1) * s_lo + pltpu.roll(xn, half, 1) * s_hi


def _inproj_kernel(x_ref, lnw_ref, w_ref, wvt_ref, wdt_ref, pos_ref, invf_ref, qw_ref, kw_ref, ones_ref, spread_ref,
                   u_ref, vt_ref, dt_ref):
    x = x_ref[...]
    ms = jnp.mean(x * x, axis=-1, keepdims=True)
    xn = (x * lax.rsqrt(ms + EPS) * lnw_ref[...]).astype(BF16)
    tables = _rope_tables(pos_ref, invf_ref, spread_ref)
    for c0, c1, head_w in ((0, SSD_WIDTH, None), (Q_COL, K_COL, qw_ref), (K_COL, XBC_COL, kw_ref), (XBC_COL, U_COLS, None)):
        acc = jnp.dot(xn, w_ref[:, c0:c1], preferred_element_type=F32)
        if head_w is None:
            u_ref[:, c0:c1] = acc.astype(BF16)
        else:
            for hb in range(ATTN_HEADS):
                blk = _norm_rope(acc[:, hb * LANES:(hb + 1) * LANES], head_w[...], tables, ones_ref[...])
                u_ref[:, c0 + hb * LANES:c0 + (hb + 1) * LANES] = blk.astype(BF16)
    dt_ref[...] = jnp.dot(xn, wdt_ref[...], preferred_element_type=F32)
    vt_ref[...] = lax.dot_general(wvt_ref[...], xn, (((1,), (1,)), ((), ())),
                                  preferred_element_type=F32).astype(BF16)


def _in_proj(x2, ln_w, w_main, w_vt, w_dt, pos_rows, invf_col, qw_lanes, kw_lanes, seg_ones, spread):
    n = x2.shape[0]
    resident = lambda shape: pl.BlockSpec(shape, lambda i: (0, 0), pipeline_mode=pl.Buffered(1))
    return pl.pallas_call(
        _inproj_kernel,
        grid=(n // TM_IN,),
        in_specs=[
            pl.BlockSpec((TM_IN, D_MODEL), lambda i: (i, 0)),
            resident((1, D_MODEL)),
            resident((D_MODEL, U_COLS)),
            resident((ATTN_WIDTH, D_MODEL)),
            resident((D_MODEL, LANES)),
            pl.BlockSpec((8, TM_IN), lambda i: (0, i)),
            resident((ROPE_DIM, 1)), resident((1, LANES)), resident((1, LANES)), resident((LANES, LANES)),
            resident((LANES, 3 * LANES)),
        ],
        out_specs=[
            pl.BlockSpec((TM_IN, U_COLS), lambda i: (i, 0)),
            pl.BlockSpec((ATTN_WIDTH, TM_IN), lambda i: (0, i)),
            pl.BlockSpec((TM_IN, LANES), lambda i: (i, 0)),
        ],
        out_shape=[
            jax.ShapeDtypeStruct((n, U_COLS), BF16),
            jax.ShapeDtypeStruct((ATTN_WIDTH, n), BF16),
            jax.ShapeDtypeStruct((n, LANES), F32),
        ],
        compiler_params=_cparams(("parallel",)),
        name="in_proj",
    )(x2, ln_w, w_main, w_vt, w_dt, pos_rows, invf_col, qw_lanes, kw_lanes, seg_ones, spread)


def _ssd_kernel(z_ref, xbc_ref, dt_ref, convw_ref, convb_ref, dtb_ref, aneg_ref, dskip_ref, normw_ref,
                y_ref, xp_ref, st_ref, yacc_ref):
    L = SSD_CHUNK
    P = SSD_HEAD_DIM

    @pl.when(pl.program_id(1) == 0)
    def _():
        xp_ref[0:8, :] = jnp.zeros((8, SSD_CONV_DIM), F32)
        st_ref[...] = jnp.zeros_like(st_ref)

    xp_ref[8:8 + L, :] = xbc_ref[...].astype(F32)
    acc = jnp.broadcast_to(convb_ref[...], (L, SSD_CONV_DIM))
    for k in range(SSD_CONV):
        acc = acc + xp_ref[5 + k:5 + k + L, :] * convw_ref[k:k + 1, :]
    xp_ref[0:8, :] = xp_ref[L:L + 8, :]
    xc = _silu(acc)

    dt = _softplus(dt_ref[...] + dtb_ref[...])
    a = dt * aneg_ref[...]
    row = lax.broadcasted_iota(I32, (L, L), 0)
    col = lax.broadcasted_iota(I32, (L, L), 1)
    causal = row >= col
    a_cs = jnp.dot(causal.astype(F32), a, precision=HIGHEST, preferred_element_type=F32)
    a_last = a_cs[L - 1:L, :]
    ea = jnp.exp(a_cs)
    dsdt = jnp.exp(a_last - a_cs) * dt
    cd = jnp.exp(a_last)
    a_cs_t = a_cs.T
    dt_t = dt.T
    dsdt_t = dsdt.T

    for g in range(SSD_GROUPS):
        b_g = xc[:, SSD_WIDTH + g * SSD_STATE:SSD_WIDTH + (g + 1) * SSD_STATE]
        c_off = SSD_WIDTH + SSD_GROUPS * SSD_STATE
        c_g = xc[:, c_off + g * SSD_STATE:c_off + (g + 1) * SSD_STATE]
        cb = lax.dot_general(c_g.astype(BF16), b_g.astype(BF16), (((1,), (1,)), ((), ())),
                             preferred_element_type=F32)
        b_gt = b_g.T
        low = lax.broadcasted_iota(I32, (L, LANES), 1) < P
        for pair in range(g * SSD_HEADS_PER_GROUP // 2, (g + 1) * SSD_HEADS_PER_GROUP // 2):
            h0 = 2 * pair
            xs_p = xc[:, h0 * P:(h0 + 2) * P].astype(BF16)
            s_prev = st_ref[pair]
            rhs = jnp.concatenate([xs_p, s_prev.astype(BF16)], axis=0)
            ys, news = [], []
            for h in (h0, h0 + 1):
                seg = a_cs[:, h:h + 1] - a_cs_t[h:h + 1, :]
                dec = jnp.exp(jnp.where(causal, seg, NEG_INF))
                m = (cb * dec * dt_t[h:h + 1, :]).astype(BF16)
                c_s = (c_g * ea[:, h:h + 1]).astype(BF16)
                lhs = jnp.concatenate([m, c_s], axis=1)
                ys.append(jnp.dot(lhs, rhs, preferred_element_type=F32))
                bw = (b_gt * dsdt_t[h:h + 1, :]).astype(BF16)
                news.append(jnp.dot(bw, xs_p, preferred_element_type=F32))
            yacc_ref[:, h0 * P:(h0 + 2) * P] = jnp.where(low, ys[0], ys[1])
            cd_p = jnp.where(low[0:1, :], cd[:, h0:h0 + 1], cd[:, h0 + 1:h0 + 2])
            st_ref[pair] = s_prev * cd_p + jnp.where(low, news[0], news[1])

    y = yacc_ref[...] + xc[:, :SSD_WIDTH] * dskip_ref[...]
    y = y * _silu(z_ref[...].astype(F32))
    gw = SSD_WIDTH // SSD_GROUPS
    for g in range(SSD_GROUPS):
        yg = y[:, g * gw:(g + 1) * gw]
        ms = jnp.mean(yg * yg, axis=-1, keepdims=True)
        y_ref[:, g * gw:(g + 1) * gw] = (yg * lax.rsqrt(ms + EPS) * normw_ref[:, g * gw:(g + 1) * gw]).astype(BF16)


def _ssd(u, dt_raw, conv_w, conv_b, dt_bias, a_neg, dskip_lanes, norm_w, bsz, seq):
    n = u.shape[0]
    nc = seq // SSD_CHUNK
    xbc_blk = (SSD_WIDTH + 2 * ATTN_WIDTH) // SSD_CONV_DIM
    full = lambda shape: pl.BlockSpec(shape, lambda b, c: (0, 0))
    return pl.pallas_call(
        _ssd_kernel,
        grid=(bsz, nc),
        in_specs=[
            pl.BlockSpec((SSD_CHUNK, SSD_WIDTH), lambda b, c: (b * nc + c, 0)),
            pl.BlockSpec((SSD_CHUNK, SSD_CONV_DIM), lambda b, c: (b * nc + c, xbc_blk)),
            pl.BlockSpec((SSD_CHUNK, LANES), lambda b, c: (b * nc + c, 0)),
            full((SSD_CONV, SSD_CONV_DIM)),
            full((1, SSD_CONV_DIM)),
            full((1, LANES)),
            full((1, LANES)),
            full((1, SSD_WIDTH)),
            full((1, SSD_WIDTH)),
        ],
        out_specs=pl.BlockSpec((SSD_CHUNK, SSD_WIDTH), lambda b, c: (b * nc + c, 0)),
        out_shape=jax.ShapeDtypeStruct((n, SSD_WIDTH), BF16),
        scratch_shapes=[
            pltpu.VMEM((SSD_CHUNK + 8, SSD_CONV_DIM), F32),
            pltpu.VMEM((SSD_HEADS // 2, SSD_STATE, 2 * SSD_HEAD_DIM), F32),
            pltpu.VMEM((SSD_CHUNK, SSD_WIDTH), F32),
        ],
        compiler_params=_cparams(("parallel", "arbitrary")),
        name="ssd",
    )(u, u, dt_raw, conv_w, conv_b, dt_bias, a_neg, dskip_lanes, norm_w)


def _rope_spread():
    half = ROPE_DIM // 2
    lane = jnp.arange(LANES)
    d = lane % ATTN_QK_DIM
    src = jnp.arange(LANES)[:, None]
    cos_src = jnp.where(d < ROPE_DIM, d % half, half)
    cos_tab = (src == cos_src[None, :]).astype(F32)
    lo_tab = -((src == (2 * half + d)[None, :]) & (d < half)[None, :]).astype(F32)
    hi_tab = ((src == (2 * half + d - half)[None, :]) & ((d >= half) & (d < ROPE_DIM))[None, :]).astype(F32)
    return jnp.concatenate([cos_tab, lo_tab, hi_tab], axis=1).astype(BF16)


def _attn_kernel(q_ref, k_ref, vt_ref, lamv_ref, subw_ref, o_ref, acc_ref, *, lam_init):
    qi = pl.program_id(2)
    lane = lax.broadcasted_iota(I32, (TQ, LANES), 1)
    qs = []
    for hb in range(ATTN_HB):
        q = q_ref[:, hb * LANES:(hb + 1) * LANES]
        zero = jnp.zeros_like(q)
        qs.append(jnp.concatenate([jnp.where(lane < ATTN_QK_DIM, q, zero),
                                   jnp.where(lane >= ATTN_QK_DIM, q, zero)], axis=0))
    acc_ref[...] = jnp.zeros_like(acc_ref)
    kv_idx = lax.broadcasted_iota(I32, (TQ, 2 * TQ), 0)
    q_idx = lax.broadcasted_iota(I32, (TQ, 2 * TQ), 1) & (TQ - 1)
    nt = (((1,), (1,)), ((), ()))

    def block(j, carry, masked):
        off = pl.multiple_of(j * TQ, TQ)
        ss = []
        for hb in range(ATTN_HB):
            kb = k_ref[pl.ds(off, TQ), hb * LANES:(hb + 1) * LANES]
            ss.append(lax.dot_general(kb, qs[hb], nt, preferred_element_type=F32))
        new, ps, alphas = [], [], []
        for hb in range(ATTN_HB):
            m_old = carry[hb]
            s = ss[hb]
            if masked:
                s = jnp.where(kv_idx <= q_idx, s, NEG_INF)
            m_new = jnp.maximum(m_old, jnp.max(s, axis=0, keepdims=True))
            alphas.append(jnp.exp2(m_old - m_new))
            ps.append(jnp.exp2(s - m_new).astype(BF16))
            new.append(m_new)
        pvs = []
        for hb in range(ATTN_HB):
            vb = jnp.concatenate([vt_ref[hb * LANES:(hb + 1) * LANES, pl.ds(off, TQ)], ones_rows], axis=0)
            pvs.append(jnp.dot(vb, ps[hb], preferred_element_type=F32))
        for hb in range(ATTN_HB):
            acc_ref[hb] = alphas[hb] * acc_ref[hb] + pvs[hb]
        return tuple(new)

    ones_rows = jnp.ones((ONES_ROWS, TQ), BF16)
    init = (jnp.full((1, 2 * TQ), NEG_INF, F32),) * ATTN_HB
    carry = lax.fori_loop(0, qi, lambda j, cr: block(j, cr, False), init)
    block(qi, carry, True)

    lv = lamv_ref[...]
    lam = (jnp.exp(jnp.sum(lv[0:1] * lv[1:2], axis=1, keepdims=True))
           - jnp.exp(jnp.sum(lv[2:3] * lv[3:4], axis=1, keepdims=True)) + lam_init)
    for hb in range(ATTN_HB):
        acc = acc_ref[hb]
        o2 = acc[:ATTN_V_DIM] * (1.0 / acc[ATTN_V_DIM:ATTN_V_DIM + 1])
        o_t = o2[:, :TQ] - lam * o2[:, TQ:]
        ms = jnp.mean(o_t * o_t, axis=0, keepdims=True)
        o_t = o_t * lax.rsqrt(ms + EPS) * subw_ref[...] * (1.0 - lam_init)
        o_ref[:, hb * LANES:(hb + 1) * LANES] = o_t.T.astype(BF16)


def _attention(u, v_t, lam_vecs, subw_col, bsz, seq, lam_init):
    n = u.shape[0]
    nq = seq // TQ
    w = ATTN_HB * ATTN_V_DIM
    q_blk, k_blk = Q_COL // w, K_COL // w
    return pl.pallas_call(
        functools.partial(_attn_kernel, lam_init=lam_init),
        grid=(bsz, ATTN_HEADS // ATTN_HB, nq),
        in_specs=[
            pl.BlockSpec((TQ, w), lambda b, h, i: (b * nq + i, q_blk + h)),
            pl.BlockSpec((seq, w), lambda b, h, i: (b, k_blk + h)),
            pl.BlockSpec((w, seq), lambda b, h, i: (h, b)),
            pl.BlockSpec((4, ATTN_QK_DIM), lambda b, h, i: (0, 0)),
            pl.BlockSpec((ATTN_V_DIM, 1), lambda b, h, i: (0, 0)),
        ],
        out_specs=pl.BlockSpec((TQ, w), lambda b, h, i: (b * nq + i, h)),
        out_shape=jax.ShapeDtypeStruct((n, ATTN_WIDTH), BF16),
        scratch_shapes=[pltpu.VMEM((ATTN_HB, ATTN_V_DIM + ONES_ROWS, 2 * TQ), F32)],
        compiler_params=_cparams(("parallel", "parallel", "arbitrary")),
        name="attn",
    )(u, u, v_t, lam_vecs, subw_col)


def _pack_halves(x):
    c = x.shape[1] // 2
    lo = pltpu.bitcast(x[:, :c].astype(BF16).astype(F32), U32) >> 16
    hi = pltpu.bitcast(x[:, c:].astype(BF16).astype(F32), U32) & jnp.uint32(0xFFFF0000)
    return hi | lo


TOKEN_ROWS = D_MODEL // 2 // LANES


def _store_token_rows(ref, first_piece, packed):
    tm = packed.shape[0]
    for j in range(packed.shape[1] // LANES):
        ref[pl.ds(first_piece + j, tm, stride=TOKEN_ROWS), :] = packed[:, j * LANES:(j + 1) * LANES]


def _load_token_rows(ref, tm):
    return jnp.concatenate([ref[pl.ds(s, tm, stride=TOKEN_ROWS), :] for s in range(TOKEN_ROWS)], axis=1)


def _unpack_halves(w, dtype=BF16):
    lo = pltpu.bitcast(w << 16, F32).astype(dtype)
    hi = pltpu.bitcast(w & jnp.uint32(0xFFFF0000), F32).astype(dtype)
    return lo, hi


def _outproj_kernel(x_ref, ys_ref, ya_ref, wos_ref, woa_ref, ln2_ref, wr_ref, br_ref,
                    h_ref, hnp_ref, meta_ref, cnt_ref):
    tm = x_ref.shape[0]
    h = (x_ref[...]
         + jnp.dot(ys_ref[...], wos_ref[...], preferred_element_type=F32)
         + jnp.dot(ya_ref[...], woa_ref[...], preferred_element_type=F32))
    h_ref[...] = h
    ms = jnp.mean(h * h, axis=-1, keepdims=True)
    hn = h * lax.rsqrt(ms + EPS) * ln2_ref[...]
    _store_token_rows(hnp_ref, 0, _pack_halves(hn))

    lg_t = lax.dot_general(wr_ref[...], hn.astype(BF16), (((1,), (1,)), ((), ())),
                           preferred_element_type=F32)
    lg = lg_t[0:ROUTER_ROWS, :] + br_ref[0:ROUTER_ROWS, :]
    row = lax.broadcasted_iota(I32, (ROUTER_ROWS, tm), 0).astype(F32)
    big = float(LANES)
    gl = jnp.where(row < N_EXPERT_GROUPS, lg, NEG_INF)
    gmax = jnp.max(gl, axis=0, keepdims=True)
    gsel = jnp.min(jnp.where(gl == gmax, row, big), axis=0, keepdims=True)
    g_w = 1.0 / jnp.sum(jnp.exp(gl - gmax), axis=0, keepdims=True)
    eid = row - N_EXPERT_GROUPS
    lo = gsel * EXPERTS_PER_GROUP
    emask = (eid >= lo) & (eid < lo + EXPERTS_PER_GROUP)
    el = jnp.where(emask, lg, NEG_INF)
    m1 = jnp.max(el, axis=0, keepdims=True)
    i1 = jnp.min(jnp.where(el == m1, eid, big), axis=0, keepdims=True)
    el2 = jnp.where(eid == i1, NEG_INF, el)
    m2 = jnp.max(el2, axis=0, keepdims=True)
    i2 = jnp.min(jnp.where(el2 == m2, eid, big), axis=0, keepdims=True)
    e2 = jnp.exp(m2 - m1)
    w1 = g_w / (1.0 + e2)
    w2 = g_w * e2 / (1.0 + e2)
    mrow = lax.broadcasted_iota(I32, (LANES, tm), 0)
    meta_ref[...] = jnp.where(mrow == 0, i1, jnp.where(mrow == 1, i2, jnp.where(mrow == 2, w1, jnp.where(mrow == 3, w2, 0.0))))
    hits = (eid == i1).astype(F32) + (eid == i2).astype(F32)
    cnt_ref[...] = jnp.broadcast_to(jnp.sum(hits, axis=1, keepdims=True), cnt_ref.shape)


def _out_proj(x2, y_ssd, y_att, wo_s, wo_a, ln2_w, w_router_t, b_router):
    n = x2.shape[0]
    full = lambda shape: pl.BlockSpec(shape, lambda i: (0, 0), pipeline_mode=pl.Buffered(1))
    return pl.pallas_call(
        _outproj_kernel,
        grid=(n // TM_OUT,),
        in_specs=[
            pl.BlockSpec((TM_OUT, D_MODEL), lambda i: (i, 0)),
            pl.BlockSpec((TM_OUT, SSD_WIDTH), lambda i: (i, 0)),
            pl.BlockSpec((TM_OUT, ATTN_WIDTH), lambda i: (i, 0)),
            full((SSD_WIDTH, D_MODEL)), full((ATTN_WIDTH, D_MODEL)),
            full((1, D_MODEL)), full((LANES, D_MODEL)), full((LANES, 1)),
        ],
        out_specs=[
            pl.BlockSpec((TM_OUT, D_MODEL), lambda i: (i, 0)),
            pl.BlockSpec((TM_OUT * TOKEN_ROWS, LANES), lambda i: (i, 0)),
            pl.BlockSpec((LANES, TM_OUT), lambda i: (0, i)),
            pl.BlockSpec((None, ROUTER_ROWS, LANES), lambda i: (i, 0, 0)),
        ],
        out_shape=[
            jax.ShapeDtypeStruct((n, D_MODEL), F32),
            jax.ShapeDtypeStruct((n * TOKEN_ROWS, LANES), U32),
            jax.ShapeDtypeStruct((LANES, n), F32),
            jax.ShapeDtypeStruct((n // TM_OUT, ROUTER_ROWS, LANES), F32),
        ],
        compiler_params=_cparams(("parallel",)),
        name="out_proj",
    )(x2, y_ssd, y_att, wo_s, wo_a, ln2_w, w_router_t, b_router)


def _rank_kernel(meta_ref, offs_ref, dest_ref, run_ref):
    tm = meta_ref.shape[1]
    meta = meta_ref[...]
    row = lax.broadcasted_iota(I32, (N_EXPERTS, tm), 0).astype(F32)
    oh0 = (row == meta[0:1, :]).astype(F32)
    oh1 = (row == meta[1:2, :]).astype(F32)
    oh = oh0 + oh1

    @pl.when(pl.program_id(0) == 0)
    def _():
        run_ref[...] = jnp.zeros_like(run_ref)

    r = lax.broadcasted_iota(I32, (tm, tm), 0)
    c = lax.broadcasted_iota(I32, (tm, tm), 1)
    before = jnp.dot(oh.astype(BF16), (r < c).astype(BF16), preferred_element_type=F32)
    base = before + jnp.tile(run_ref[...] + offs_ref[...], (1, tm // LANES))
    d0 = jnp.sum(oh0 * base, axis=0, keepdims=True)
    d1 = jnp.sum(oh1 * base, axis=0, keepdims=True)
    drow = lax.broadcasted_iota(I32, (8, tm), 0)
    dest_ref[...] = jnp.where(drow == 0, d0, jnp.where(drow == 1, d1, 0.0)).astype(I32)
    run_ref[...] = run_ref[...] + jnp.sum(oh, axis=1, keepdims=True)


def _rank(meta_t, offsets):
    n = meta_t.shape[1]
    return pl.pallas_call(
        _rank_kernel,
        grid=(n // TM_RANK,),
        in_specs=[pl.BlockSpec((8, TM_RANK), lambda i: (0, i)), pl.BlockSpec((N_EXPERTS, LANES), lambda i: (0, 0))],
        out_specs=pl.BlockSpec((8, TM_RANK), lambda i: (0, i)),
        out_shape=jax.ShapeDtypeStruct((8, n), I32),
        scratch_shapes=[pltpu.VMEM((N_EXPERTS, LANES), F32)],
        compiler_params=_cparams(("arbitrary",)),
        name="rank",
    )(meta_t, offsets)


def _token_copy(src_ref, src_token, dst_ref, dst_token, sem):
    src = src_ref.at[pl.ds(pl.multiple_of(src_token * TOKEN_ROWS, TOKEN_ROWS), TOKEN_ROWS)]
    dst = dst_ref.at[pl.ds(pl.multiple_of(dst_token * TOKEN_ROWS, TOKEN_ROWS), TOKEN_ROWS)]
    return pltpu.make_async_copy(src, dst, sem)


CLEAR_GROUP = 8


def _invert_kernel(d0_ref, d1_ref, valid_ref, inv_ref):
    n = d0_ref.shape[0]

    def clear_tile(tile, _):
        def clear(c, _):
            for k in range(CLEAR_GROUP):
                inv_ref[tile * TM_EXP + c * CLEAR_GROUP + k] = 0
            return 0

        lax.fori_loop(valid_ref[tile] // CLEAR_GROUP, TM_EXP // CLEAR_GROUP, clear, 0)
        return 0

    lax.fori_loop(0, valid_ref.shape[0], clear_tile, 0)

    def put(t, _):
        inv_ref[d0_ref[t]] = t
        inv_ref[d1_ref[t]] = n + t
        return 0

    lax.fori_loop(0, n, put, 0, unroll=8)


def _invert(dest0, dest1, tile_valid):
    smem = pl.BlockSpec(memory_space=pltpu.SMEM)
    return pl.pallas_call(
        _invert_kernel,
        in_specs=[smem, smem, smem],
        out_specs=smem,
        out_shape=jax.ShapeDtypeStruct((tile_valid.shape[0] * TM_EXP,), I32),
        name="invert",
    )(dest0, dest1, tile_valid)


CAST_ROWS = 256


def _cast_weight(src_ref, dst_ref):
    def body(i, _):
        rows = pl.ds(pl.multiple_of(i * CAST_ROWS, CAST_ROWS), CAST_ROWS)
        dst_ref[rows, :] = src_ref[rows, :].astype(BF16)
        return 0

    lax.fori_loop(0, src_ref.shape[0] // CAST_ROWS, body, 0)


S_EXPERT, S_FIRST, S_SLOT, S_NEXT, S_HAS_NEXT, S_USED, S_VALID = range(7)
WEIGHT_DMA_PRIORITY = 1


def _load_expert_weights(t, sched_ref, triples, sems):
    def copies(expert, slot):
        return [pltpu.make_async_copy(w.at[expert], stage.at[slot], sems.at[i, slot])
                for i, (w, stage, _) in enumerate(triples)]

    @pl.when(sched_ref[S_FIRST, t] == 1)
    def _():
        slot = sched_ref[S_SLOT, t]

        @pl.when(t == 0)
        def _():
            for cp in copies(sched_ref[S_EXPERT, t], slot):
                cp.start(priority=WEIGHT_DMA_PRIORITY)

        for cp in copies(sched_ref[S_EXPERT, t], slot):
            cp.wait()

        @pl.when(sched_ref[S_HAS_NEXT, t] == 1)
        def _():
            for cp in copies(sched_ref[S_NEXT, t], 1 - slot):
                cp.start(priority=WEIGHT_DMA_PRIORITY)

        for _, stage, dst in triples:
            _cast_weight(stage.at[slot], dst)


ROW_DMA_GROUPS = 4
GATHER_AHEAD = 3


def _tile_rows(sched_ref, tile, group):
    valid = sched_ref[S_VALID, tile]
    per = TM_EXP // ROW_DMA_GROUPS
    for r in range(group * per, (group + 1) * per):
        ok = r < valid
        yield r, jnp.where(ok, tile * TM_EXP + r, 0), ok


def _experts_up_kernel(sched_ref, inv_ref, hnp_hbm, wg_hbm, wu_hbm, hid_ref,
                       xbuf_ref, wg_stage, wu_stage, wgb_ref, wub_ref, wsems, gsem, *, n_tokens):
    t = pl.program_id(0)
    used = sched_ref[S_USED, 0]
    nbuf = GATHER_AHEAD + 1
    cur = t % nbuf
    ahead = (t + GATHER_AHEAD) % nbuf
    _load_expert_weights(t, sched_ref, [(wg_hbm, wg_stage, wgb_ref), (wu_hbm, wu_stage, wub_ref)], wsems)

    def start_rows(tile, buf, group):
        for r, slot, _ in _tile_rows(sched_ref, tile, group):
            token = inv_ref[slot] & (n_tokens - 1)
            _token_copy(hnp_hbm, token, xbuf_ref.at[buf], r, gsem.at[buf]).start()

    def wait_rows(buf):
        pltpu.make_async_copy(hnp_hbm.at[pl.ds(0, TM_EXP * TOKEN_ROWS)], xbuf_ref.at[buf], gsem.at[buf]).wait()

    for first in range(GATHER_AHEAD):
        @pl.when((t == 0) & (first < used))
        def _():
            for g in range(ROW_DMA_GROUPS):
                start_rows(first, first, g)

    def multiply(fetch_ahead):
        wait_rows(cur)
        c = D_MODEL // 2
        x_lo, x_hi = _unpack_halves(_load_token_rows(xbuf_ref.at[cur], TM_EXP))
        pieces = ((x_lo, wgb_ref, 0), (x_hi, wgb_ref, c), (x_lo, wub_ref, 0), (x_hi, wub_ref, c))
        acc = []
        for g, (x, w_ref, row0) in enumerate(pieces):
            if fetch_ahead:
                start_rows(t + GATHER_AHEAD, ahead, g)
            acc.append(jnp.dot(x, w_ref[row0:row0 + c, :], preferred_element_type=F32))
        hid_ref[...] = (_silu(acc[0] + acc[1]) * (acc[2] + acc[3])).astype(BF16)

    @pl.when(t + GATHER_AHEAD < used)
    def _():
        multiply(True)

    @pl.when((t + GATHER_AHEAD >= used) & (t < used))
    def _():
        multiply(False)

    @pl.when(t >= used)
    def _():
        hid_ref[...] = jnp.zeros_like(hid_ref)


def _experts_down_kernel(sched_ref, inv_ref, hid_ref, wd_hbm, y_hbm,
                         ybuf_ref, wd_stage, wdb_ref, wsems, ssem, *, n_tokens):
    t = pl.program_id(0)
    used = sched_ref[S_USED, 0]
    cur = t % 2
    _load_expert_weights(t, sched_ref, [(wd_hbm, wd_stage, wdb_ref)], wsems)

    def start_rows(tile, buf, group):
        for r, slot, ok in _tile_rows(sched_ref, tile, group):
            dst = jnp.where(ok, inv_ref[slot], TOP_K * n_tokens + r)
            _token_copy(ybuf_ref.at[buf], r, y_hbm, dst, ssem.at[buf]).start(priority=r % 2)

    def wait_rows(buf):
        pltpu.make_async_copy(ybuf_ref.at[buf], y_hbm.at[pl.ds(0, TM_EXP * TOKEN_ROWS)], ssem.at[buf]).wait()

    def multiply(send_previous):
        hid = hid_ref[...]
        q = D_MODEL // 4
        for half in range(2):
            if send_previous:
                start_rows(t - 1, 1 - cur, 2 * half)
            lo = jnp.dot(hid, wdb_ref[:, half * q:(half + 1) * q], preferred_element_type=F32)
            if send_previous:
                start_rows(t - 1, 1 - cur, 2 * half + 1)
            hi = jnp.dot(hid, wdb_ref[:, (half + 2) * q:(half + 3) * q], preferred_element_type=F32)
            _store_token_rows(ybuf_ref.at[cur], half * (q // LANES), _pack_halves(jnp.concatenate([lo, hi], axis=1)))

    @pl.when((t >= 2) & (t < used))
    def _():
        wait_rows(cur)

    @pl.when(t == 0)
    def _():
        ybuf_ref[1] = jnp.zeros(ybuf_ref.shape[1:], ybuf_ref.dtype)
        spare = pltpu.make_async_copy(ybuf_ref.at[1], y_hbm.at[pl.ds(TOP_K * n_tokens * TOKEN_ROWS, TM_EXP * TOKEN_ROWS)],
                                      ssem.at[1])
        spare.start()
        spare.wait()
        multiply(False)

    @pl.when((t > 0) & (t < used))
    def _():
        multiply(True)

    @pl.when(t == used - 1)
    def _():
        for g in range(ROW_DMA_GROUPS):
            start_rows(t, cur, g)

        @pl.when(t > 0)
        def _():
            wait_rows(1 - cur)

        wait_rows(cur)


def _tile_schedule(counts, n_tiles):
    tiles_per_expert = (counts + TM_EXP - 1) // TM_EXP
    tile_end = jnp.cumsum(tiles_per_expert)
    ids = jnp.arange(n_tiles, dtype=I32)
    raw = jnp.sum(ids[:, None] >= tile_end[None, :], axis=1)
    expert = jnp.minimum(raw, N_EXPERTS - 1).astype(I32)
    first = jnp.concatenate([jnp.ones((1,), I32), (expert[1:] != expert[:-1]).astype(I32)])
    slot = (jnp.cumsum(first) - 1) % 2
    start_idx = jnp.where(first == 1, ids, n_tiles)
    next_start = jnp.concatenate([lax.cummin(start_idx, reverse=True)[1:], jnp.full((1,), n_tiles, I32)])
    has_next = (next_start < n_tiles).astype(I32)
    next_expert = jnp.sum(jnp.where(ids[None, :] == next_start[:, None], expert[None, :], 0), axis=1)
    used = jnp.full((n_tiles,), tile_end[-1], I32)
    onehot = expert[:, None] == jnp.arange(N_EXPERTS)[None, :]
    tile_in_expert = ids - jnp.sum(jnp.where(onehot, (tile_end - tiles_per_expert)[None, :], 0), axis=1)
    rows_left = jnp.sum(jnp.where(onehot, counts[None, :], 0), axis=1) - tile_in_expert * TM_EXP
    valid = jnp.where(raw < N_EXPERTS, jnp.clip(rows_left, 0, TM_EXP), 0)
    return jnp.stack([expert, first, slot, next_expert, has_next, used, valid]).astype(I32)


def _experts(sched, inv, hnp, wg, wu, wd):
    n = hnp.shape[0] // TOKEN_ROWS
    n_slots = inv.shape[0]
    token_buf = lambda nbuf: pltpu.VMEM((nbuf, TM_EXP * TOKEN_ROWS, LANES), U32)
    row_spec = lambda width: pl.BlockSpec((TM_EXP, width), lambda t, sc, iv: (t, 0))
    hbm = pl.BlockSpec(memory_space=pl.ANY)
    big_vmem = pltpu.CompilerParams(dimension_semantics=("arbitrary",), vmem_limit_bytes=EXPERT_VMEM_LIMIT)
    hid = pl.pallas_call(
        functools.partial(_experts_up_kernel, n_tokens=n),
        grid_spec=pltpu.PrefetchScalarGridSpec(
            num_scalar_prefetch=2,
            grid=(n_slots // TM_EXP,),
            in_specs=[hbm, hbm, hbm],
            out_specs=row_spec(EXPERT_HIDDEN),
            scratch_shapes=[token_buf(GATHER_AHEAD + 1)]
                           + [pltpu.VMEM((2, D_MODEL, EXPERT_HIDDEN), F32)] * 2
                           + [pltpu.VMEM((D_MODEL, EXPERT_HIDDEN), BF16)] * 2
                           + [pltpu.SemaphoreType.DMA((2, 2)), pltpu.SemaphoreType.DMA((GATHER_AHEAD + 1,))],
        ),
        out_shape=jax.ShapeDtypeStruct((n_slots, EXPERT_HIDDEN), BF16),
        compiler_params=big_vmem,
        name="experts_up",
    )(sched, inv, hnp, wg, wu)
    return pl.pallas_call(
        functools.partial(_experts_down_kernel, n_tokens=n),
        grid_spec=pltpu.PrefetchScalarGridSpec(
            num_scalar_prefetch=2,
            grid=(n_slots // TM_EXP,),
            in_specs=[row_spec(EXPERT_HIDDEN), hbm],
            out_specs=hbm,
            scratch_shapes=[token_buf(2),
                            pltpu.VMEM((2, EXPERT_HIDDEN, D_MODEL), F32), pltpu.VMEM((EXPERT_HIDDEN, D_MODEL), BF16),
                            pltpu.SemaphoreType.DMA((1, 2)), pltpu.SemaphoreType.DMA((2,))],
        ),
        out_shape=jax.ShapeDtypeStruct(((TOP_K * n + TM_EXP) * TOKEN_ROWS, LANES), U32),
        compiler_params=big_vmem,
        name="experts_down",
    )(sched, inv, hid, wd)


def _combine_kernel(h_ref, meta_ref, y0_ref, y1_ref, o_ref):
    c = D_MODEL // 2
    meta = meta_ref[...].T
    w0, w1 = meta[:, 2:3], meta[:, 3:4]
    tm = h_ref.shape[0]
    lo0, hi0 = _unpack_halves(_load_token_rows(y0_ref, tm), F32)
    lo1, hi1 = _unpack_halves(_load_token_rows(y1_ref, tm), F32)
    o_ref[:, :c] = h_ref[:, :c] + (lo0 * w0 + lo1 * w1)
    o_ref[:, c:] = h_ref[:, c:] + (hi0 * w0 + hi1 * w1)


def _combine(h, meta, y):
    n = h.shape[0]
    nb = n // TM_COMB
    return pl.pallas_call(
        _combine_kernel,
        grid=(nb,),
        in_specs=[
            pl.BlockSpec((TM_COMB, D_MODEL), lambda i: (i, 0)),
            pl.BlockSpec((LANES, TM_COMB), lambda i: (0, i)),
            pl.BlockSpec((TM_COMB * TOKEN_ROWS, LANES), lambda i: (i, 0)),
            pl.BlockSpec((TM_COMB * TOKEN_ROWS, LANES), lambda i: (nb + i, 0)),
        ],
        out_specs=pl.BlockSpec((TM_COMB, D_MODEL), lambda i: (i, 0)),
        out_shape=jax.ShapeDtypeStruct((n, D_MODEL), F32),
        compiler_params=_cparams(("parallel",)),
        name="combine",
    )(h, meta, y, y)


def _lambda_init(layer_idx):
    return 0.8 - 0.6 * math.exp(-0.3 * layer_idx)


def _pad_lanes(v, width=LANES):
    return jnp.pad(v, ((0, 0), (0, width - v.shape[1])))


def _layer(l, x2, pos_rows, bsz, seq, ln1_w, w_in, conv_w, conv_b, dt_bias, a_log, d_skip, ssd_norm_w,
           q_norm_w, k_norm_w, lambda_q1, lambda_k1, lambda_q2, lambda_k2, subln_w, w_out, ln2_w,
           w_router_group, b_router_group, w_router_expert, b_router_expert, w_gate, w_up, w_down):
    n = x2.shape[0]
    c_z, c_xbc, c_dt = SSD_WIDTH, SSD_WIDTH + SSD_CONV_DIM, SSD_WIDTH + SSD_CONV_DIM + SSD_HEADS
    c_q, c_k = c_dt + ATTN_WIDTH, c_dt + 2 * ATTN_WIDTH
    w_main = jnp.concatenate([w_in[:, :c_z], w_in[:, c_dt:c_q], w_in[:, c_q:c_k], w_in[:, c_z:c_xbc]],
                             axis=1).astype(BF16)
    w_vt = w_in[:, c_k:].T.astype(BF16)
    w_dt = _pad_lanes(w_in[:, c_xbc:c_dt]).astype(BF16)

    inv_freq = jnp.power(ROPE_THETA, -jnp.arange(0, ROPE_DIM, 2, dtype=F32) / ROPE_DIM)
    invf_col = jnp.concatenate([inv_freq, jnp.zeros_like(inv_freq)])[:, None]
    seg_ones = (jnp.arange(LANES)[:, None] // ATTN_QK_DIM == jnp.arange(LANES)[None, :] // ATTN_QK_DIM).astype(BF16)
    q_scale = math.log2(math.e) / math.sqrt(ATTN_QK_DIM)
    u, v_t, dt_raw = _in_proj(x2, ln1_w[None, :], w_main, w_vt, w_dt, pos_rows, invf_col,
                              jnp.tile(q_norm_w, 2)[None, :] * q_scale, jnp.tile(k_norm_w, 2)[None, :],
                              seg_ones, _rope_spread())

    a_neg = _pad_lanes(-jnp.exp(a_log.astype(F32))[None, :])
    y_ssd = _ssd(u, dt_raw, conv_w, conv_b[None, :], _pad_lanes(dt_bias[None, :]), a_neg,
                 jnp.repeat(d_skip, SSD_HEAD_DIM)[None, :], ssd_norm_w[None, :], bsz, seq)

    lam_vecs = jnp.stack([lambda_q1, lambda_k1, lambda_q2, lambda_k2]).astype(F32)
    y_att = _attention(u, v_t, lam_vecs, subln_w[:, None], bsz, seq, _lambda_init(l))

    w_out_b = w_out.astype(BF16)
    w_router_t = _pad_lanes(jnp.concatenate([w_router_group, w_router_expert], axis=1)).T.astype(BF16)
    b_router = _pad_lanes(jnp.concatenate([b_router_group, b_router_expert])[None, :]).T
    h, hnp, meta, tile_counts = _out_proj(x2, y_ssd, y_att, w_out_b[:SSD_WIDTH], w_out_b[SSD_WIDTH:], ln2_w[None, :],
                                          w_router_t, b_router)

    counts = jnp.sum(tile_counts[:, N_EXPERT_GROUPS:N_EXPERT_GROUPS + N_EXPERTS, 0], axis=0).astype(I32)
    padded = (counts + TM_EXP - 1) // TM_EXP * TM_EXP
    offsets = jnp.broadcast_to((jnp.cumsum(padded) - padded).astype(F32)[:, None], (N_EXPERTS, LANES))
    dest = _rank(meta, offsets)
    dest0, dest1 = dest[0], dest[1]
    n_slots = (n * TOP_K + N_EXPERTS * (TM_EXP - 1)) // TM_EXP * TM_EXP
    sched = _tile_schedule(counts, n_slots // TM_EXP)

    inv = _invert(dest0, dest1, sched[S_VALID])
    y = _experts(sched, inv, hnp, w_gate, w_up, w_down)
    return _combine(h, meta, y)


def kernel(x, positions, ln1_w, w_in, conv_w, conv_b, dt_bias, a_log, d_skip, ssd_norm_w, q_norm_w, k_norm_w,
           lambda_q1, lambda_k1, lambda_q2, lambda_k2, subln_w, w_out, ln2_w, w_router_group, b_router_group,
           w_router_expert, b_router_expert, w_gate, w_up, w_down):
    bsz, seq, d = x.shape
    assert d == D_MODEL and seq % TQ == 0 and (bsz * seq) % TM_IN == 0
    x2 = x.reshape(bsz * seq, d)
    pos_rows = jnp.broadcast_to(positions.astype(F32).reshape(1, bsz * seq), (8, bsz * seq))
    params = (ln1_w, w_in, conv_w, conv_b, dt_bias, a_log, d_skip, ssd_norm_w, q_norm_w, k_norm_w,
              lambda_q1, lambda_k1, lambda_q2, lambda_k2, subln_w, w_out, ln2_w, w_router_group, b_router_group,
              w_router_expert, b_router_expert, w_gate, w_up, w_down)
    for l in range(ln1_w.shape[0]):
        x2 = _layer(l, x2, pos_rows, bsz, seq, *[p[l] for p in params])
    return x2.reshape(bsz, seq, d)
```

```python
import functools
import math

import jax
import jax.numpy as jnp
from jax import lax
from jax.experimental import pallas as pl
from jax.experimental.pallas import tpu as pltpu

F32 = jnp.float32
BF16 = jnp.bfloat16
I32 = jnp.int32
U32 = jnp.uint32
HIGHEST = lax.Precision.HIGHEST

D_MODEL = 2048
SSD_WIDTH = 1024
ATTN_WIDTH = 1024
SSD_HEAD_DIM = 64
SSD_HEADS = 16
SSD_GROUPS = 2
SSD_HEADS_PER_GROUP = SSD_HEADS // SSD_GROUPS
SSD_STATE = 128
SSD_CONV = 4
SSD_CHUNK = 128
SSD_CONV_DIM = SSD_WIDTH + 2 * SSD_GROUPS * SSD_STATE
ATTN_V_DIM = 128
ATTN_HEADS = 8
ATTN_QK_DIM = 64
ROPE_THETA = 500000.0
ROPE_DIM = 16
N_EXPERT_GROUPS = 4
EXPERTS_PER_GROUP = 8
N_EXPERTS = 32
TOP_K = 2
EXPERT_HIDDEN = 1024
EPS = 1e-6

LANES = 128
NEG_INF = float("-inf")

TM_IN = 512
TQ = 256
ATTN_HB = 8
ONES_ROWS = 16
TM_OUT = 512
TM_RANK = 512
TM_EXP = 256
TM_COMB = 512
ROUTER_ROWS = 40
Q_COL = SSD_WIDTH
K_COL = Q_COL + ATTN_WIDTH
XBC_COL = K_COL + ATTN_WIDTH
U_COLS = XBC_COL + SSD_CONV_DIM
VMEM_LIMIT = 52 * 1024 * 1024
EXPERT_VMEM_LIMIT = 58 * 1024 * 1024


def _cparams(sem):
    return pltpu.CompilerParams(dimension_semantics=sem, vmem_limit_bytes=VMEM_LIMIT)


def _silu(x):
    return x * (1.0 / (1.0 + jnp.exp(-x)))


def _softplus(x):
    return jnp.maximum(x, 0.0) + jnp.log(1.0 + jnp.exp(-jnp.abs(x)))


def _rope_tables(pos_ref, invf_ref, spread_ref):
    tm = pos_ref.shape[1]
    ang_t = invf_ref[...] * pos_ref[0:1, :]
    trig = jnp.concatenate([jnp.cos(ang_t), jnp.sin(ang_t), jnp.zeros((LANES - 2 * ROPE_DIM, tm), F32)], axis=0).T
    t1 = trig.astype(BF16)
    r1 = trig - t1.astype(F32)
    t2 = r1.astype(BF16)
    t3 = (r1 - t2.astype(F32)).astype(BF16)
    spread = spread_ref[...]
    tab = (jnp.dot(t1, spread, preferred_element_type=F32) + jnp.dot(t2, spread, preferred_element_type=F32)
           + jnp.dot(t3, spread, preferred_element_type=F32))
    return tab[:, :LANES], tab[:, LANES:2 * LANES], tab[:, 2 * LANES:]


def _norm_rope(x, w, tables, seg_ones):
    cs, s_lo, s_hi = tables
    half = ROPE_DIM // 2
    ss = jnp.dot((x * x).astype(BF16), seg_ones, preferred_element_type=F32)
    xn = x * lax.rsqrt(ss * (1.0 / ATTN_QK_DIM) + EPS) * w
    return xn * cs + pltpu.roll(xn, LANES - half, 1) * s_lo + pltpu.roll(xn, half, 1) * s_hi


def _inproj_kernel(x_ref, lnw_ref, wz_ref, wq_ref, wk_ref, wxbc_ref, wvt_ref, wdt_ref, pos_ref, invf_ref, qw_ref, kw_ref,
                   ones_ref, spread_ref, u_ref, vt_ref, dt_ref):
    x = x_ref[...]
    ms = jnp.mean(x * x, axis=-1, keepdims=True)
    xn = (x * lax.rsqrt(ms + EPS) * lnw_ref[...]).astype(BF16)
    tables = _rope_tables(pos_ref, invf_ref, spread_ref)
    for c0, c1, w_ref, head_w in ((0, SSD_WIDTH, wz_ref, None), (Q_COL, K_COL, wq_ref, qw_ref),
                                  (K_COL, XBC_COL, wk_ref, kw_ref), (XBC_COL, U_COLS, wxbc_ref, None)):
        acc = jnp.dot(xn, w_ref[...], preferred_element_type=F32)
        if head_w is None:
            u_ref[:, c0:c1] = acc.astype(BF16)
        else:
            for hb in range(ATTN_HEADS):
                blk = _norm_rope(acc[:, hb * LANES:(hb + 1) * LANES], head_w[...], tables, ones_ref[...])
                u_ref[:, c0 + hb * LANES:c0 + (hb + 1) * LANES] = blk.astype(BF16)
    dt_ref[...] = jnp.dot(xn, wdt_ref[...], preferred_element_type=F32)
    vt_ref[...] = lax.dot_general(wvt_ref[...], xn, (((1,), (1,)), ((), ())),
                                  preferred_element_type=F32).astype(BF16)


def _in_proj(x2, ln_w, w_cols, w_vt, w_dt, pos_rows, invf_col, qw_lanes, kw_lanes, seg_ones, spread):
    n = x2.shape[0]
    resident = lambda shape: pl.BlockSpec(shape, lambda i: (0, 0), pipeline_mode=pl.Buffered(1))
    return pl.pallas_call(
        _inproj_kernel,
        grid=(n // TM_IN,),
        in_specs=[
            pl.BlockSpec((TM_IN, D_MODEL), lambda i: (i, 0)),
            resident((1, D_MODEL)),
            *[resident(w.shape) for w in w_cols],
            resident((ATTN_WIDTH, D_MODEL)),
            resident((D_MODEL, LANES)),
            pl.BlockSpec((8, TM_IN), lambda i: (0, i)),
            resident((ROPE_DIM, 1)), resident((1, LANES)), resident((1, LANES)), resident((LANES, LANES)),
            resident((LANES, 3 * LANES)),
        ],
        out_specs=[
            pl.BlockSpec((TM_IN, U_COLS), lambda i: (i, 0)),
            pl.BlockSpec((ATTN_WIDTH, TM_IN), lambda i: (0, i)),
            pl.BlockSpec((TM_IN, LANES), lambda i: (i, 0)),
        ],
        out_shape=[
            jax.ShapeDtypeStruct((n, U_COLS), BF16),
            jax.ShapeDtypeStruct((ATTN_WIDTH, n), BF16),
            jax.ShapeDtypeStruct((n, LANES), F32),
        ],
        compiler_params=_cparams(("parallel",)),
        name="in_proj",
    )(x2, ln_w, *w_cols, w_vt, w_dt, pos_rows, invf_col, qw_lanes, kw_lanes, seg_ones, spread)


def _ssd_kernel(z_ref, xbc_ref, dt_ref, convw_ref, convb_ref, dtb_ref, aneg_ref, dskip_ref, normw_ref,
                y_ref, xp_ref, st_ref, yacc_ref):
    L = SSD_CHUNK
    P = SSD_HEAD_DIM

    @pl.when(pl.program_id(1) == 0)
    def _():
        xp_ref[0:8, :] = jnp.zeros((8, SSD_CONV_DIM), F32)
        st_ref[...] = jnp.zeros_like(st_ref)

    xp_ref[8:8 + L, :] = xbc_ref[...].astype(F32)
    acc = jnp.broadcast_to(convb_ref[...], (L, SSD_CONV_DIM))
    for k in range(SSD_CONV):
        acc = acc + xp_ref[5 + k:5 + k + L, :] * convw_ref[k:k + 1, :]
    xp_ref[0:8, :] = xp_ref[L:L + 8, :]
    xc = _silu(acc)

    dt = _softplus(dt_ref[...] + dtb_ref[...])
    a = dt * aneg_ref[...]
    row = lax.broadcasted_iota(I32, (L, L), 0)
    col = lax.broadcasted_iota(I32, (L, L), 1)
    causal = row >= col
    a_cs = jnp.dot(causal.astype(F32), a, precision=HIGHEST, preferred_element_type=F32)
    a_last = a_cs[L - 1:L, :]
    ea = jnp.exp(a_cs)
    dsdt = jnp.exp(a_last - a_cs) * dt
    cd = jnp.exp(a_last)
    a_cs_t = a_cs.T
    dt_t = dt.T
    dsdt_t = dsdt.T

    for g in range(SSD_GROUPS):
        b_g = xc[:, SSD_WIDTH + g * SSD_STATE:SSD_WIDTH + (g + 1) * SSD_STATE]
        c_off = SSD_WIDTH + SSD_GROUPS * SSD_STATE
        c_g = xc[:, c_off + g * SSD_STATE:c_off + (g + 1) * SSD_STATE]
        cb = lax.dot_general(c_g.astype(BF16), b_g.astype(BF16), (((1,), (1,)), ((), ())),
                             preferred_element_type=F32)
        b_gt = b_g.T
        low = lax.broadcasted_iota(I32, (L, LANES), 1) < P
        for pair in range(g * SSD_HEADS_PER_GROUP // 2, (g + 1) * SSD_HEADS_PER_GROUP // 2):
            h0 = 2 * pair
            xs_p = xc[:, h0 * P:(h0 + 2) * P].astype(BF16)
            s_prev = st_ref[pair]
            rhs = jnp.concatenate([xs_p, s_prev.astype(BF16)], axis=0)
            ys, news = [], []
            for h in (h0, h0 + 1):
                seg = a_cs[:, h:h + 1] - a_cs_t[h:h + 1, :]
                dec = jnp.exp(jnp.where(causal, seg, NEG_INF))
                m = (cb * dec * dt_t[h:h + 1, :]).astype(BF16)
                c_s = (c_g * ea[:, h:h + 1]).astype(BF16)
                lhs = jnp.concatenate([m, c_s], axis=1)
                ys.append(jnp.dot(lhs, rhs, preferred_element_type=F32))
                bw = (b_gt * dsdt_t[h:h + 1, :]).astype(BF16)
                news.append(jnp.dot(bw, xs_p, preferred_element_type=F32))
            yacc_ref[:, h0 * P:(h0 + 2) * P] = jnp.where(low, ys[0], ys[1])
            cd_p = jnp.where(low[0:1, :], cd[:, h0:h0 + 1], cd[:, h0 + 1:h0 + 2])
            st_ref[pair] = s_prev * cd_p + jnp.where(low, news[0], news[1])

    y = yacc_ref[...] + xc[:, :SSD_WIDTH] * dskip_ref[...]
    y = y * _silu(z_ref[...].astype(F32))
    gw = SSD_WIDTH // SSD_GROUPS
    for g in range(SSD_GROUPS):
        yg = y[:, g * gw:(g + 1) * gw]
        ms = jnp.mean(yg * yg, axis=-1, keepdims=True)
        y_ref[:, g * gw:(g + 1) * gw] = (yg * lax.rsqrt(ms + EPS) * normw_ref[:, g * gw:(g + 1) * gw]).astype(BF16)


def _ssd(u, dt_raw, conv_w, conv_b, dt_bias, a_neg, dskip_lanes, norm_w, bsz, seq):
    n = u.shape[0]
    nc = seq // SSD_CHUNK
    xbc_blk = (SSD_WIDTH + 2 * ATTN_WIDTH) // SSD_CONV_DIM
    full = lambda shape: pl.BlockSpec(shape, lambda b, c: (0, 0))
    return pl.pallas_call(
        _ssd_kernel,
        grid=(bsz, nc),
        in_specs=[
            pl.BlockSpec((SSD_CHUNK, SSD_WIDTH), lambda b, c: (b * nc + c, 0)),
            pl.BlockSpec((SSD_CHUNK, SSD_CONV_DIM), lambda b, c: (b * nc + c, xbc_blk)),
            pl.BlockSpec((SSD_CHUNK, LANES), lambda b, c: (b * nc + c, 0)),
            full((SSD_CONV, SSD_CONV_DIM)),
            full((1, SSD_CONV_DIM)),
            full((1, LANES)),
            full((1, LANES)),
            full((1, SSD_WIDTH)),
            full((1, SSD_WIDTH)),
        ],
        out_specs=pl.BlockSpec((SSD_CHUNK, SSD_WIDTH), lambda b, c: (b * nc + c, 0)),
        out_shape=jax.ShapeDtypeStruct((n, SSD_WIDTH), BF16),
        scratch_shapes=[
            pltpu.VMEM((SSD_CHUNK + 8, SSD_CONV_DIM), F32),
            pltpu.VMEM((SSD_HEADS // 2, SSD_STATE, 2 * SSD_HEAD_DIM), F32),
            pltpu.VMEM((SSD_CHUNK, SSD_WIDTH), F32),
        ],
        compiler_params=_cparams(("parallel", "arbitrary")),
        name="ssd",
    )(u, u, dt_raw, conv_w, conv_b, dt_bias, a_neg, dskip_lanes, norm_w)


def _rope_spread():
    half = ROPE_DIM // 2
    lane = jnp.arange(LANES)
    d = lane % ATTN_QK_DIM
    src = jnp.arange(LANES)[:, None]
    cos_src = jnp.where(d < ROPE_DIM, d % half, half)
    cos_tab = (src == cos_src[None, :]).astype(F32)
    lo_tab = -((src == (2 * half + d)[None, :]) & (d < half)[None, :]).astype(F32)
    hi_tab = ((src == (2 * half + d - half)[None, :]) & ((d >= half) & (d < ROPE_DIM))[None, :]).astype(F32)
    return jnp.concatenate([cos_tab, lo_tab, hi_tab], axis=1).astype(BF16)


def _attn_kernel(q_ref, k_ref, vt_ref, lamv_ref, subw_ref, o_ref, acc_ref, *, lam_init):
    qi = pl.program_id(2)
    lane = lax.broadcasted_iota(I32, (TQ, LANES), 1)
    qs = []
    for hb in range(ATTN_HB):
        q = q_ref[:, hb * LANES:(hb + 1) * LANES]
        zero = jnp.zeros_like(q)
        qs.append(jnp.concatenate([jnp.where(lane < ATTN_QK_DIM, q, zero),
                                   jnp.where(lane >= ATTN_QK_DIM, q, zero)], axis=0))
    acc_ref[...] = jnp.zeros_like(acc_ref)
    kv_idx = lax.broadcasted_iota(I32, (TQ, 2 * TQ), 0)
    q_idx = lax.broadcasted_iota(I32, (TQ, 2 * TQ), 1) & (TQ - 1)
    nt = (((1,), (1,)), ((), ()))

    def block(j, carry, masked):
        off = pl.multiple_of(j * TQ, TQ)
        ss = []
        for hb in range(ATTN_HB):
            kb = k_ref[pl.ds(off, TQ), hb * LANES:(hb + 1) * LANES]
            ss.append(lax.dot_general(kb, qs[hb], nt, preferred_element_type=F32))
        new, ps, alphas = [], [], []
        for hb in range(ATTN_HB):
            m_old = carry[hb]
            s = ss[hb]
            if masked:
                s = jnp.where(kv_idx <= q_idx, s, NEG_INF)
            m_new = jnp.maximum(m_old, jnp.max(s, axis=0, keepdims=True))
            alphas.append(jnp.exp2(m_old - m_new))
            ps.append(jnp.exp2(s - m_new).astype(BF16))
            new.append(m_new)
        pvs = []
        for hb in range(ATTN_HB):
            vb = jnp.concatenate([vt_ref[hb * LANES:(hb + 1) * LANES, pl.ds(off, TQ)], ones_rows], axis=0)
            pvs.append(jnp.dot(vb, ps[hb], preferred_element_type=F32))
        for hb in range(ATTN_HB):
            acc_ref[hb] = alphas[hb] * acc_ref[hb] + pvs[hb]
        return tuple(new)

    ones_rows = jnp.ones((ONES_ROWS, TQ), BF16)
    init = (jnp.full((1, 2 * TQ), NEG_INF, F32),) * ATTN_HB
    carry = lax.fori_loop(0, qi, lambda j, cr: block(j, cr, False), init)
    block(qi, carry, True)

    lv = lamv_ref[...]
    lam = (jnp.exp(jnp.sum(lv[0:1] * lv[1:2], axis=1, keepdims=True))
           - jnp.exp(jnp.sum(lv[2:3] * lv[3:4], axis=1, keepdims=True)) + lam_init)
    for hb in range(ATTN_HB):
        acc = acc_ref[hb]
        o2 = acc[:ATTN_V_DIM] * (1.0 / acc[ATTN_V_DIM:ATTN_V_DIM + 1])
        o_t = o2[:, :TQ] - lam * o2[:, TQ:]
        ms = jnp.mean(o_t * o_t, axis=0, keepdims=True)
        o_t = o_t * lax.rsqrt(ms + EPS) * subw_ref[...] * (1.0 - lam_init)
        o_ref[:, hb * LANES:(hb + 1) * LANES] = o_t.T.astype(BF16)


def _attention(u, v_t, lam_vecs, subw_col, bsz, seq, lam_init):
    n = u.shape[0]
    nq = seq // TQ
    w = ATTN_HB * ATTN_V_DIM
    q_blk, k_blk = Q_COL // w, K_COL // w
    return pl.pallas_call(
        functools.partial(_attn_kernel, lam_init=lam_init),
        grid=(bsz, ATTN_HEADS // ATTN_HB, nq),
        in_specs=[
            pl.BlockSpec((TQ, w), lambda b, h, i: (b * nq + i, q_blk + h)),
            pl.BlockSpec((seq, w), lambda b, h, i: (b, k_blk + h)),
            pl.BlockSpec((w, seq), lambda b, h, i: (h, b)),
            pl.BlockSpec((4, ATTN_QK_DIM), lambda b, h, i: (0, 0)),
            pl.BlockSpec((ATTN_V_DIM, 1), lambda b, h, i: (0, 0)),
        ],
        out_specs=pl.BlockSpec((TQ, w), lambda b, h, i: (b * nq + i, h)),
        out_shape=jax.ShapeDtypeStruct((n, ATTN_WIDTH), BF16),
        scratch_shapes=[pltpu.VMEM((ATTN_HB, ATTN_V_DIM + ONES_ROWS, 2 * TQ), F32)],
        compiler_params=_cparams(("parallel", "parallel", "arbitrary")),
        name="attn",
    )(u, u, v_t, lam_vecs, subw_col)


def _pack_halves(x):
    c = x.shape[1] // 2
    lo = pltpu.bitcast(x[:, :c].astype(BF16).astype(F32), U32) >> 16
    hi = pltpu.bitcast(x[:, c:].astype(BF16).astype(F32), U32) & jnp.uint32(0xFFFF0000)
    return hi | lo


TOKEN_ROWS = D_MODEL // 2 // LANES


def _store_token_rows(ref, first_piece, packed):
    tm = packed.shape[0]
    for j in range(packed.shape[1] // LANES):
        ref[pl.ds(first_piece + j, tm, stride=TOKEN_ROWS), :] = packed[:, j * LANES:(j + 1) * LANES]


def _load_token_rows(ref, tm):
    return jnp.concatenate([ref[pl.ds(s, tm, stride=TOKEN_ROWS), :] for s in range(TOKEN_ROWS)], axis=1)


def _unpack_halves(w, dtype=BF16):
    lo = pltpu.bitcast(w << 16, F32).astype(dtype)
    hi = pltpu.bitcast(w & jnp.uint32(0xFFFF0000), F32).astype(dtype)
    return lo, hi


def _outproj_kernel(x_ref, ys_ref, ya_ref, wos_ref, woa_ref, ln2_ref, wr_ref, br_ref,
                    h_ref, hnp_ref, meta_ref, cnt_ref):
    tm = x_ref.shape[0]
    h = (x_ref[...]
         + jnp.dot(ys_ref[...], wos_ref[...], preferred_element_type=F32)
         + jnp.dot(ya_ref[...], woa_ref[...], preferred_element_type=F32))
    h_ref[...] = h
    ms = jnp.mean(h * h, axis=-1, keepdims=True)
    hn = h * lax.rsqrt(ms + EPS) * ln2_ref[...]
    _store_token_rows(hnp_ref, 0, _pack_halves(hn))

    lg_t = lax.dot_general(wr_ref[...], hn.astype(BF16), (((1,), (1,)), ((), ())),
                           preferred_element_type=F32)
    lg = lg_t[0:ROUTER_ROWS, :] + br_ref[0:ROUTER_ROWS, :]
    row = lax.broadcasted_iota(I32, (ROUTER_ROWS, tm), 0).astype(F32)
    big = float(LANES)
    gl = jnp.where(row < N_EXPERT_GROUPS, lg, NEG_INF)
    gmax = jnp.max(gl, axis=0, keepdims=True)
    gsel = jnp.min(jnp.where(gl == gmax, row, big), axis=0, keepdims=True)
    g_w = 1.0 / jnp.sum(jnp.exp(gl - gmax), axis=0, keepdims=True)
    eid = row - N_EXPERT_GROUPS
    lo = gsel * EXPERTS_PER_GROUP
    emask = (eid >= lo) & (eid < lo + EXPERTS_PER_GROUP)
    el = jnp.where(emask, lg, NEG_INF)
    m1 = jnp.max(el, axis=0, keepdims=True)
    i1 = jnp.min(jnp.where(el == m1, eid, big), axis=0, keepdims=True)
    el2 = jnp.where(eid == i1, NEG_INF, el)
    m2 = jnp.max(el2, axis=0, keepdims=True)
    i2 = jnp.min(jnp.where(el2 == m2, eid, big), axis=0, keepdims=True)
    e2 = jnp.exp(m2 - m1)
    w1 = g_w / (1.0 + e2)
    w2 = g_w * e2 / (1.0 + e2)
    mrow = lax.broadcasted_iota(I32, (LANES, tm), 0)
    meta_ref[...] = jnp.where(mrow == 0, i1, jnp.where(mrow == 1, i2, jnp.where(mrow == 2, w1, jnp.where(mrow == 3, w2, 0.0))))
    hits = (eid == i1).astype(F32) + (eid == i2).astype(F32)
    cnt_ref[...] = jnp.broadcast_to(jnp.sum(hits, axis=1, keepdims=True), cnt_ref.shape)


def _out_proj(x2, y_ssd, y_att, wo_s, wo_a, ln2_w, w_router_t, b_router):
    n = x2.shape[0]
    full = lambda shape: pl.BlockSpec(shape, lambda i: (0, 0), pipeline_mode=pl.Buffered(1))
    return pl.pallas_call(
        _outproj_kernel,
        grid=(n // TM_OUT,),
        in_specs=[
            pl.BlockSpec((TM_OUT, D_MODEL), lambda i: (i, 0)),
            pl.BlockSpec((TM_OUT, SSD_WIDTH), lambda i: (i, 0)),
            pl.BlockSpec((TM_OUT, ATTN_WIDTH), lambda i: (i, 0)),
            full((SSD_WIDTH, D_MODEL)), full((ATTN_WIDTH, D_MODEL)),
            full((1, D_MODEL)), full((LANES, D_MODEL)), full((LANES, 1)),
        ],
        out_specs=[
            pl.BlockSpec((TM_OUT, D_MODEL), lambda i: (i, 0)),
            pl.BlockSpec((TM_OUT * TOKEN_ROWS, LANES), lambda i: (i, 0)),
            pl.BlockSpec((LANES, TM_OUT), lambda i: (0, i)),
            pl.BlockSpec((None, ROUTER_ROWS, LANES), lambda i: (i, 0, 0)),
        ],
        out_shape=[
            jax.ShapeDtypeStruct((n, D_MODEL), F32),
            jax.ShapeDtypeStruct((n * TOKEN_ROWS, LANES), U32),
            jax.ShapeDtypeStruct((LANES, n), F32),
            jax.ShapeDtypeStruct((n // TM_OUT, ROUTER_ROWS, LANES), F32),
        ],
        compiler_params=_cparams(("parallel",)),
        name="out_proj",
    )(x2, y_ssd, y_att, wo_s, wo_a, ln2_w, w_router_t, b_router)


def _rank_kernel(meta_ref, offs_ref, dest_ref, run_ref):
    tm = meta_ref.shape[1]
    meta = meta_ref[...]
    row = lax.broadcasted_iota(I32, (N_EXPERTS, tm), 0).astype(F32)
    oh0 = (row == meta[0:1, :]).astype(F32)
    oh1 = (row == meta[1:2, :]).astype(F32)
    oh = oh0 + oh1

    @pl.when(pl.program_id(0) == 0)
    def _():
        run_ref[...] = jnp.zeros_like(run_ref)

    r = lax.broadcasted_iota(I32, (tm, tm), 0)
    c = lax.broadcasted_iota(I32, (tm, tm), 1)
    before = jnp.dot(oh.astype(BF16), (r < c).astype(BF16), preferred_element_type=F32)
    base = before + jnp.tile(run_ref[...] + offs_ref[...], (1, tm // LANES))
    d0 = jnp.sum(oh0 * base, axis=0, keepdims=True)
    d1 = jnp.sum(oh1 * base, axis=0, keepdims=True)
    drow = lax.broadcasted_iota(I32, (8, tm), 0)
    dest_ref[...] = jnp.where(drow == 0, d0, jnp.where(drow == 1, d1, 0.0)).astype(I32)
    run_ref[...] = run_ref[...] + jnp.sum(oh, axis=1, keepdims=True)


def _rank(meta_t, offsets):
    n = meta_t.shape[1]
    return pl.pallas_call(
        _rank_kernel,
        grid=(n // TM_RANK,),
        in_specs=[pl.BlockSpec((8, TM_RANK), lambda i: (0, i)), pl.BlockSpec((N_EXPERTS, LANES), lambda i: (0, 0))],
        out_specs=pl.BlockSpec((8, TM_RANK), lambda i: (0, i)),
        out_shape=jax.ShapeDtypeStruct((8, n), I32),
        scratch_shapes=[pltpu.VMEM((N_EXPERTS, LANES), F32)],
        compiler_params=_cparams(("arbitrary",)),
        name="rank",
    )(meta_t, offsets)


def _token_copy(src_ref, src_token, dst_ref, dst_token, sem):
    src = src_ref.at[pl.ds(pl.multiple_of(src_token * TOKEN_ROWS, TOKEN_ROWS), TOKEN_ROWS)]
    dst = dst_ref.at[pl.ds(pl.multiple_of(dst_token * TOKEN_ROWS, TOKEN_ROWS), TOKEN_ROWS)]
    return pltpu.make_async_copy(src, dst, sem)


CLEAR_GROUP = 8


def _invert_kernel(d0_ref, d1_ref, valid_ref, inv_ref):
    n = d0_ref.shape[0]

    def clear_tile(tile, _):
        def clear(c, _):
            for k in range(CLEAR_GROUP):
                inv_ref[tile * TM_EXP + c * CLEAR_GROUP + k] = 0
            return 0

        lax.fori_loop(valid_ref[tile] // CLEAR_GROUP, TM_EXP // CLEAR_GROUP, clear, 0)
        return 0

    lax.fori_loop(0, valid_ref.shape[0], clear_tile, 0)

    def put(t, _):
        inv_ref[d0_ref[t]] = t
        inv_ref[d1_ref[t]] = n + t
        return 0

    lax.fori_loop(0, n, put, 0, unroll=8)


def _invert(dest0, dest1, tile_valid):
    smem = pl.BlockSpec(memory_space=pltpu.SMEM)
    return pl.pallas_call(
        _invert_kernel,
        in_specs=[smem, smem, smem],
        out_specs=smem,
        out_shape=jax.ShapeDtypeStruct((tile_valid.shape[0] * TM_EXP,), I32),
        name="invert",
    )(dest0, dest1, tile_valid)


CAST_ROWS = 256


def _cast_weight(src_ref, dst_ref):
    def body(i, _):
        rows = pl.ds(pl.multiple_of(i * CAST_ROWS, CAST_ROWS), CAST_ROWS)
        dst_ref[rows, :] = src_ref[rows, :].astype(BF16)
        return 0

    lax.fori_loop(0, src_ref.shape[0] // CAST_ROWS, body, 0)


S_EXPERT, S_FIRST, S_SLOT, S_NEXT, S_HAS_NEXT, S_USED, S_VALID = range(7)
WEIGHT_DMA_PRIORITY = 1


def _load_expert_weights(t, sched_ref, triples, sems):
    def copies(expert, slot):
        return [pltpu.make_async_copy(w.at[expert], stage.at[slot], sems.at[i, slot])
                for i, (w, stage, _) in enumerate(triples)]

    @pl.when(sched_ref[S_FIRST, t] == 1)
    def _():
        slot = sched_ref[S_SLOT, t]

        @pl.when(t == 0)
        def _():
            for cp in copies(sched_ref[S_EXPERT, t], slot):
                cp.start(priority=WEIGHT_DMA_PRIORITY)

        for cp in copies(sched_ref[S_EXPERT, t], slot):
            cp.wait()

        @pl.when(sched_ref[S_HAS_NEXT, t] == 1)
        def _():
            for cp in copies(sched_ref[S_NEXT, t], 1 - slot):
                cp.start(priority=WEIGHT_DMA_PRIORITY)

        for _, stage, dst in triples:
            _cast_weight(stage.at[slot], dst)


ROW_DMA_GROUPS = 4
GATHER_AHEAD = 3


def _tile_rows(sched_ref, tile, group):
    valid = sched_ref[S_VALID, tile]
    per = TM_EXP // ROW_DMA_GROUPS
    for r in range(group * per, (group + 1) * per):
        ok = r < valid
        yield r, jnp.where(ok, tile * TM_EXP + r, 0), ok


def _experts_up_kernel(sched_ref, inv_ref, hnp_hbm, wg_hbm, wu_hbm, hid_ref,
                       xbuf_ref, wg_stage, wu_stage, wgb_ref, wub_ref, wsems, gsem, *, n_tokens):
    t = pl.program_id(0)
    used = sched_ref[S_USED, 0]
    nbuf = GATHER_AHEAD + 1
    cur = t % nbuf
    ahead = (t + GATHER_AHEAD) % nbuf
    _load_expert_weights(t, sched_ref, [(wg_hbm, wg_stage, wgb_ref), (wu_hbm, wu_stage, wub_ref)], wsems)

    def start_rows(tile, buf, group):
        for r, slot, _ in _tile_rows(sched_ref, tile, group):
            token = inv_ref[slot] & (n_tokens - 1)
            _token_copy(hnp_hbm, token, xbuf_ref.at[buf], r, gsem.at[buf]).start()

    def wait_rows(buf):
        pltpu.make_async_copy(hnp_hbm.at[pl.ds(0, TM_EXP * TOKEN_ROWS)], xbuf_ref.at[buf], gsem.at[buf]).wait()

    for first in range(GATHER_AHEAD):
        @pl.when((t == 0) & (first < used))
        def _():
            for g in range(ROW_DMA_GROUPS):
                start_rows(first, first, g)

    def multiply(fetch_ahead):
        wait_rows(cur)
        c = D_MODEL // 2
        x_lo, x_hi = _unpack_halves(_load_token_rows(xbuf_ref.at[cur], TM_EXP))
        pieces = ((x_lo, wgb_ref, 0), (x_hi, wgb_ref, c), (x_lo, wub_ref, 0), (x_hi, wub_ref, c))
        acc = []
        for g, (x, w_ref, row0) in enumerate(pieces):
            if fetch_ahead:
                start_rows(t + GATHER_AHEAD, ahead, g)
            acc.append(jnp.dot(x, w_ref[row0:row0 + c, :], preferred_element_type=F32))
        hid_ref[...] = (_silu(acc[0] + acc[1]) * (acc[2] + acc[3])).astype(BF16)

    @pl.when(t + GATHER_AHEAD < used)
    def _():
        multiply(True)

    @pl.when((t + GATHER_AHEAD >= used) & (t < used))
    def _():
        multiply(False)

    @pl.when(t >= used)
    def _():
        hid_ref[...] = jnp.zeros_like(hid_ref)


def _experts_down_kernel(sched_ref, inv_ref, hid_ref, wd_hbm, y_hbm,
                         ybuf_ref, wd_stage, wdb_ref, wsems, ssem, *, n_tokens):
    t = pl.program_id(0)
    used = sched_ref[S_USED, 0]
    cur = t % 2
    _load_expert_weights(t, sched_ref, [(wd_hbm, wd_stage, wdb_ref)], wsems)

    def start_rows(tile, buf, group):
        for r, slot, ok in _tile_rows(sched_ref, tile, group):
            dst = jnp.where(ok, inv_ref[slot], TOP_K * n_tokens + r)
            _token_copy(ybuf_ref.at[buf], r, y_hbm, dst, ssem.at[buf]).start(priority=r % 2)

    def wait_rows(buf):
        pltpu.make_async_copy(ybuf_ref.at[buf], y_hbm.at[pl.ds(0, TM_EXP * TOKEN_ROWS)], ssem.at[buf]).wait()

    def multiply(send_previous):
        hid = hid_ref[...]
        q = D_MODEL // 4
        for half in range(2):
            if send_previous:
                start_rows(t - 1, 1 - cur, 2 * half)
            lo = jnp.dot(hid, wdb_ref[:, half * q:(half + 1) * q], preferred_element_type=F32)
            if send_previous:
                start_rows(t - 1, 1 - cur, 2 * half + 1)
            hi = jnp.dot(hid, wdb_ref[:, (half + 2) * q:(half + 3) * q], preferred_element_type=F32)
            _store_token_rows(ybuf_ref.at[cur], half * (q // LANES), _pack_halves(jnp.concatenate([lo, hi], axis=1)))

    @pl.when((t >= 2) & (t < used))
    def _():
        wait_rows(cur)

    @pl.when(t == 0)
    def _():
        ybuf_ref[1] = jnp.zeros(ybuf_ref.shape[1:], ybuf_ref.dtype)
        spare = pltpu.make_async_copy(ybuf_ref.at[1], y_hbm.at[pl.ds(TOP_K * n_tokens * TOKEN_ROWS, TM_EXP * TOKEN_ROWS)],
                                      ssem.at[1])
        spare.start()
        spare.wait()
        multiply(False)

    @pl.when((t > 0) & (t < used))
    def _():
        multiply(True)

    @pl.when(t == used - 1)
    def _():
        for g in range(ROW_DMA_GROUPS):
            start_rows(t, cur, g)

        @pl.when(t > 0)
        def _():
            wait_rows(1 - cur)

        wait_rows(cur)


def _tile_schedule(counts, n_tiles):
    tiles_per_expert = (counts + TM_EXP - 1) // TM_EXP
    tile_end = jnp.cumsum(tiles_per_expert)
    ids = jnp.arange(n_tiles, dtype=I32)
    raw = jnp.sum(ids[:, None] >= tile_end[None, :], axis=1)
    expert = jnp.minimum(raw, N_EXPERTS - 1).astype(I32)
    first = jnp.concatenate([jnp.ones((1,), I32), (expert[1:] != expert[:-1]).astype(I32)])
    slot = (jnp.cumsum(first) - 1) % 2
    start_idx = jnp.where(first == 1, ids, n_tiles)
    next_start = jnp.concatenate([lax.cummin(start_idx, reverse=True)[1:], jnp.full((1,), n_tiles, I32)])
    has_next = (next_start < n_tiles).astype(I32)
    next_expert = jnp.sum(jnp.where(ids[None, :] == next_start[:, None], expert[None, :], 0), axis=1)
    used = jnp.full((n_tiles,), tile_end[-1], I32)
    onehot = expert[:, None] == jnp.arange(N_EXPERTS)[None, :]
    tile_in_expert = ids - jnp.sum(jnp.where(onehot, (tile_end - tiles_per_expert)[None, :], 0), axis=1)
    rows_left = jnp.sum(jnp.where(onehot, counts[None, :], 0), axis=1) - tile_in_expert * TM_EXP
    valid = jnp.where(raw < N_EXPERTS, jnp.clip(rows_left, 0, TM_EXP), 0)
    return jnp.stack([expert, first, slot, next_expert, has_next, used, valid]).astype(I32)


def _experts(sched, inv, hnp, wg, wu, wd):
    n = hnp.shape[0] // TOKEN_ROWS
    n_slots = inv.shape[0]
    token_buf = lambda nbuf: pltpu.VMEM((nbuf, TM_EXP * TOKEN_ROWS, LANES), U32)
    row_spec = lambda width: pl.BlockSpec((TM_EXP, width), lambda t, sc, iv: (t, 0))
    hbm = pl.BlockSpec(memory_space=pl.ANY)
    big_vmem = pltpu.CompilerParams(dimension_semantics=("arbitrary",), vmem_limit_bytes=EXPERT_VMEM_LIMIT)
    hid = pl.pallas_call(
        functools.partial(_experts_up_kernel, n_tokens=n),
        grid_spec=pltpu.PrefetchScalarGridSpec(
            num_scalar_prefetch=2,
            grid=(n_slots // TM_EXP,),
            in_specs=[hbm, hbm, hbm],
            out_specs=row_spec(EXPERT_HIDDEN),
            scratch_shapes=[token_buf(GATHER_AHEAD + 1)]
                           + [pltpu.VMEM((2, D_MODEL, EXPERT_HIDDEN), F32)] * 2
                           + [pltpu.VMEM((D_MODEL, EXPERT_HIDDEN), BF16)] * 2
                           + [pltpu.SemaphoreType.DMA((2, 2)), pltpu.SemaphoreType.DMA((GATHER_AHEAD + 1,))],
        ),
        out_shape=jax.ShapeDtypeStruct((n_slots, EXPERT_HIDDEN), BF16),
        compiler_params=big_vmem,
        name="experts_up",
    )(sched, inv, hnp, wg, wu)
    return pl.pallas_call(
        functools.partial(_experts_down_kernel, n_tokens=n),
        grid_spec=pltpu.PrefetchScalarGridSpec(
            num_scalar_prefetch=2,
            grid=(n_slots // TM_EXP,),
            in_specs=[row_spec(EXPERT_HIDDEN), hbm],
            out_specs=hbm,
            scratch_shapes=[token_buf(2),
                            pltpu.VMEM((2, EXPERT_HIDDEN, D_MODEL), F32), pltpu.VMEM((EXPERT_HIDDEN, D_MODEL), BF16),
                            pltpu.SemaphoreType.DMA((1, 2)), pltpu.SemaphoreType.DMA((2,))],
        ),
        out_shape=jax.ShapeDtypeStruct(((TOP_K * n + TM_EXP) * TOKEN_ROWS, LANES), U32),
        compiler_params=big_vmem,
        name="experts_down",
    )(sched, inv, hid, wd)


def _combine_kernel(h_ref, meta_ref, y0_ref, y1_ref, o_ref):
    c = D_MODEL // 2
    meta = meta_ref[...].T
    w0, w1 = meta[:, 2:3], meta[:, 3:4]
    tm = h_ref.shape[0]
    lo0, hi0 = _unpack_halves(_load_token_rows(y0_ref, tm), F32)
    lo1, hi1 = _unpack_halves(_load_token_rows(y1_ref, tm), F32)
    o_ref[:, :c] = h_ref[:, :c] + (lo0 * w0 + lo1 * w1)
    o_ref[:, c:] = h_ref[:, c:] + (hi0 * w0 + hi1 * w1)


def _combine(h, meta, y):
    n = h.shape[0]
    nb = n // TM_COMB
    return pl.pallas_call(
        _combine_kernel,
        grid=(nb,),
        in_specs=[
            pl.BlockSpec((TM_COMB, D_MODEL), lambda i: (i, 0)),
            pl.BlockSpec((LANES, TM_COMB), lambda i: (0, i)),
            pl.BlockSpec((TM_COMB * TOKEN_ROWS, LANES), lambda i: (i, 0)),
            pl.BlockSpec((TM_COMB * TOKEN_ROWS, LANES), lambda i: (nb + i, 0)),
        ],
        out_specs=pl.BlockSpec((TM_COMB, D_MODEL), lambda i: (i, 0)),
        out_shape=jax.ShapeDtypeStruct((n, D_MODEL), F32),
        compiler_params=_cparams(("parallel",)),
        name="combine",
    )(h, meta, y, y)


def _lambda_init(layer_idx):
    return 0.8 - 0.6 * math.exp(-0.3 * layer_idx)


def _pad_lanes(v, width=LANES):
    return jnp.pad(v, ((0, 0), (0, width - v.shape[1])))


def _layer(l, x2, pos_rows, bsz, seq, ln1_w, w_in, conv_w, conv_b, dt_bias, a_log, d_skip, ssd_norm_w,
           q_norm_w, k_norm_w, lambda_q1, lambda_k1, lambda_q2, lambda_k2, subln_w, w_out, ln2_w,
           w_router_group, b_router_group, w_router_expert, b_router_expert, w_gate, w_up, w_down):
    n = x2.shape[0]
    c_z, c_xbc, c_dt = SSD_WIDTH, SSD_WIDTH + SSD_CONV_DIM, SSD_WIDTH + SSD_CONV_DIM + SSD_HEADS
    c_q, c_k = c_dt + ATTN_WIDTH, c_dt + 2 * ATTN_WIDTH
    w_cols = [w_in[:, c0:c1].astype(BF16) for c0, c1 in ((0, c_z), (c_dt, c_q), (c_q, c_k), (c_z, c_xbc))]
    w_vt = w_in[:, c_k:].T.astype(BF16)
    w_dt = _pad_lanes(w_in[:, c_xbc:c_dt]).astype(BF16)

    inv_freq = jnp.power(ROPE_THETA, -jnp.arange(0, ROPE_DIM, 2, dtype=F32) / ROPE_DIM)
    invf_col = jnp.concatenate([inv_freq, jnp.zeros_like(inv_freq)])[:, None]
    seg_ones = (jnp.arange(LANES)[:, None] // ATTN_QK_DIM == jnp.arange(LANES)[None, :] // ATTN_QK_DIM).astype(BF16)
    q_scale = math.log2(math.e) / math.sqrt(ATTN_QK_DIM)
    u, v_t, dt_raw = _in_proj(x2, ln1_w[None, :], w_cols, w_vt, w_dt, pos_rows, invf_col,
                              jnp.tile(q_norm_w, 2)[None, :] * q_scale, jnp.tile(k_norm_w, 2)[None, :],
                              seg_ones, _rope_spread())

    a_neg = _pad_lanes(-jnp.exp(a_log.astype(F32))[None, :])
    y_ssd = _ssd(u, dt_raw, conv_w, conv_b[None, :], _pad_lanes(dt_bias[None, :]), a_neg,
                 jnp.repeat(d_skip, SSD_HEAD_DIM)[None, :], ssd_norm_w[None, :], bsz, seq)

    lam_vecs = jnp.stack([lambda_q1, lambda_k1, lambda_q2, lambda_k2]).astype(F32)
    y_att = _attention(u, v_t, lam_vecs, subln_w[:, None], bsz, seq, _lambda_init(l))

    w_out_b = w_out.astype(BF16)
    w_router_t = _pad_lanes(jnp.concatenate([w_router_group, w_router_expert], axis=1)).T.astype(BF16)
    b_router = _pad_lanes(jnp.concatenate([b_router_group, b_router_expert])[None, :]).T
    h, hnp, meta, tile_counts = _out_proj(x2, y_ssd, y_att, w_out_b[:SSD_WIDTH], w_out_b[SSD_WIDTH:], ln2_w[None, :],
                                          w_router_t, b_router)

    counts = jnp.sum(tile_counts[:, N_EXPERT_GROUPS:N_EXPERT_GROUPS + N_EXPERTS, 0], axis=0).astype(I32)
    padded = (counts + TM_EXP - 1) // TM_EXP * TM_EXP
    offsets = jnp.broadcast_to((jnp.cumsum(padded) - padded).astype(F32)[:, None], (N_EXPERTS, LANES))
    dest = _rank(meta, offsets)
    dest0, dest1 = dest[0], dest[1]
    n_slots = (n * TOP_K + N_EXPERTS * (TM_EXP - 1)) // TM_EXP * TM_EXP
    sched = _tile_schedule(counts, n_slots // TM_EXP)

    inv = _invert(dest0, dest1, sched[S_VALID])
    y = _experts(sched, inv, hnp, w_gate, w_up, w_down)
    return _combine(h, meta, y)


def kernel(x, positions, ln1_w, w_in, conv_w, conv_b, dt_bias, a_log, d_skip, ssd_norm_w, q_norm_w, k_norm_w,
           lambda_q1, lambda_k1, lambda_q2, lambda_k2, subln_w, w_out, ln2_w, w_router_group, b_router_group,
           w_router_expert, b_router_expert, w_gate, w_up, w_down):
    bsz, seq, d = x.shape
    assert d == D_MODEL and seq % TQ == 0 and (bsz * seq) % TM_IN == 0
    x2 = x.reshape(bsz * seq, d)
    pos_rows = jnp.broadcast_to(positions.astype(F32).reshape(1, bsz * seq), (8, bsz * seq))
    params = (ln1_w, w_in, conv_w, conv_b, dt_bias, a_log, d_skip, ssd_norm_w, q_norm_w, k_norm_w,
              lambda_q1, lambda_k1, lambda_q2, lambda_k2, subln_w, w_out, ln2_w, w_router_group, b_router_group,
              w_router_expert, b_router_expert, w_gate, w_up, w_down)
    for l in range(ln1_w.shape[0]):
        x2 = _layer(l, x2, pos_rows, bsz, seq, *[p[l] for p in params])
    return x2.reshape(bsz, seq, d)
```

```python
import functools
import math

import jax
import jax.numpy as jnp
from jax import lax
from jax.experimental import pallas as pl
from jax.experimental.pallas import tpu as pltpu

F32 = jnp.float32
BF16 = jnp.bfloat16
I32 = jnp.int32
U32 = jnp.uint32
HIGHEST = lax.Precision.HIGHEST

D_MODEL = 2048
SSD_WIDTH = 1024
ATTN_WIDTH = 1024
SSD_HEAD_DIM = 64
SSD_HEADS = 16
SSD_GROUPS = 2
SSD_HEADS_PER_GROUP = SSD_HEADS // SSD_GROUPS
SSD_STATE = 128
SSD_CONV = 4
SSD_CHUNK = 128
SSD_CONV_DIM = SSD_WIDTH + 2 * SSD_GROUPS * SSD_STATE
ATTN_V_DIM = 128
ATTN_HEADS = 8
ATTN_QK_DIM = 64
ROPE_THETA = 500000.0
ROPE_DIM = 16
N_EXPERT_GROUPS = 4
EXPERTS_PER_GROUP = 8
N_EXPERTS = 32
TOP_K = 2
EXPERT_HIDDEN = 1024
EPS = 1e-6

LANES = 128
NEG_INF = float("-inf")

TM_IN = 512
TQ = 256
ATTN_HB = 8
ONES_ROWS = 16
TM_OUT = 512
TM_RANK = 512
TM_EXP = 256
TM_COMB = 512
ROUTER_ROWS = 40
Q_COL = SSD_WIDTH
K_COL = Q_COL + ATTN_WIDTH
XBC_COL = K_COL + ATTN_WIDTH
U_COLS = XBC_COL + SSD_CONV_DIM
VMEM_LIMIT = 52 * 1024 * 1024
EXPERT_VMEM_LIMIT = 58 * 1024 * 1024


def _cparams(sem):
    return pltpu.CompilerParams(dimension_semantics=sem, vmem_limit_bytes=VMEM_LIMIT)


def _silu(x):
    return x * (1.0 / (1.0 + jnp.exp(-x)))


def _softplus(x):
    return jnp.maximum(x, 0.0) + jnp.log(1.0 + jnp.exp(-jnp.abs(x)))


def _rope_tables(pos_ref, invf_ref, spread_ref):
    tm = pos_ref.shape[1]
    ang_t = invf_ref[...] * pos_ref[0:1, :]
    trig = jnp.concatenate([jnp.cos(ang_t), jnp.sin(ang_t), jnp.zeros((LANES - 2 * ROPE_DIM, tm), F32)], axis=0).T
    t1 = trig.astype(BF16)
    r1 = trig - t1.astype(F32)
    t2 = r1.astype(BF16)
    t3 = (r1 - t2.astype(F32)).astype(BF16)
    spread = spread_ref[...]
    tab = (jnp.dot(t1, spread, preferred_element_type=F32) + jnp.dot(t2, spread, preferred_element_type=F32)
           + jnp.dot(t3, spread, preferred_element_type=F32))
    return tab[:, :LANES], tab[:, LANES:2 * LANES], tab[:, 2 * LANES:]


def _norm_rope(x, w, tables, seg_ones):
    cs, s_lo, s_hi = tables
    half = ROPE_DIM // 2
    ss = jnp.dot((x * x).astype(BF16), seg_ones, preferred_element_type=F32)
    xn = x * lax.rsqrt(ss * (1.0 / ATTN_QK_DIM) + EPS) * w
    return xn * cs + pltpu.roll(xn, LANES - half, 1) * s_lo + pltpu.roll(xn, half, 1) * s_hi


def _inproj_kernel(x_ref, lnw_ref, wz_ref, wq_ref, wk_ref, wxbc_ref, wvt_ref, wdt_ref, pos_ref, invf_ref, qw_ref, kw_ref,
                   ones_ref, spread_ref, u_ref, vt_ref, dt_ref):
    x = x_ref[...]
    ms = jnp.mean(x * x, axis=-1, keepdims=True)
    xn = (x * lax.rsqrt(ms + EPS) * lnw_ref[...]).astype(BF16)
    tables = _rope_tables(pos_ref, invf_ref, spread_ref)
    for c0, c1, w_ref, head_w in ((0, SSD_WIDTH, wz_ref, None), (Q_COL, K_COL, wq_ref, qw_ref),
                                  (K_COL, XBC_COL, wk_ref, kw_ref), (XBC_COL, U_COLS, wxbc_ref, None)):
        acc = jnp.dot(xn, w_ref[...], preferred_element_type=F32)
        if head_w is None:
            u_ref[:, c0:c1] = acc.astype(BF16)
        else:
            for hb in range(ATTN_HEADS):
                blk = _norm_rope(acc[:, hb * LANES:(hb + 1) * LANES], head_w[...], tables, ones_ref[...])
                u_ref[:, c0 + hb * LANES:c0 + (hb + 1) * LANES] = blk.astype(BF16)
    dt_ref[...] = jnp.dot(xn, wdt_ref[...], preferred_element_type=F32)
    vt_ref[...] = lax.dot_general(wvt_ref[...], xn, (((1,), (1,)), ((), ())),
                                  preferred_element_type=F32).astype(BF16)


def _in_proj(x2, ln_w, w_cols, w_vt, w_dt, pos_rows, invf_col, qw_lanes, kw_lanes, seg_ones, spread):
    n = x2.shape[0]
    resident = lambda shape: pl.BlockSpec(shape, lambda i: (0, 0), pipeline_mode=pl.Buffered(1))
    return pl.pallas_call(
        _inproj_kernel,
        grid=(n // TM_IN,),
        in_specs=[
            pl.BlockSpec((TM_IN, D_MODEL), lambda i: (i, 0)),
            resident((1, D_MODEL)),
            *[resident(w.shape) for w in w_cols],
            resident((ATTN_WIDTH, D_MODEL)),
            resident((D_MODEL, LANES)),
            pl.BlockSpec((8, TM_IN), lambda i: (0, i)),
            resident((ROPE_DIM, 1)), resident((1, LANES)), resident((1, LANES)), resident((LANES, LANES)),
            resident((LANES, 3 * LANES)),
        ],
        out_specs=[
            pl.BlockSpec((TM_IN, U_COLS), lambda i: (i, 0)),
            pl.BlockSpec((ATTN_WIDTH, TM_IN), lambda i: (0, i)),
            pl.BlockSpec((TM_IN, LANES), lambda i: (i, 0)),
        ],
        out_shape=[
            jax.ShapeDtypeStruct((n, U_COLS), BF16),
            jax.ShapeDtypeStruct((ATTN_WIDTH, n), BF16),
            jax.ShapeDtypeStruct((n, LANES), F32),
        ],
        compiler_params=_cparams(("parallel",)),
        name="in_proj",
    )(x2, ln_w, *w_cols, w_vt, w_dt, pos_rows, invf_col, qw_lanes, kw_lanes, seg_ones, spread)


def _ssd_kernel(z_ref, xbc_ref, dt_ref, convw_ref, convb_ref, dtb_ref, aneg_ref, dskip_ref, normw_ref,
                y_ref, xp_ref, st_ref, yacc_ref):
    L = SSD_CHUNK
    P = SSD_HEAD_DIM

    @pl.when(pl.program_id(1) == 0)
    def _():
        xp_ref[0:8, :] = jnp.zeros((8, SSD_CONV_DIM), F32)
        st_ref[...] = jnp.zeros_like(st_ref)

    xp_ref[8:8 + L, :] = xbc_ref[...].astype(F32)
    acc = jnp.broadcast_to(convb_ref[...], (L, SSD_CONV_DIM))
    for k in range(SSD_CONV):
        acc = acc + xp_ref[5 + k:5 + k + L, :] * convw_ref[k:k + 1, :]
    xp_ref[0:8, :] = xp_ref[L:L + 8, :]
    xc = _silu(acc)

    dt = _softplus(dt_ref[...] + dtb_ref[...])
    a = dt * aneg_ref[...]
    row = lax.broadcasted_iota(I32, (L, L), 0)
    col = lax.broadcasted_iota(I32, (L, L), 1)
    causal = row >= col
    a_cs = jnp.dot(causal.astype(F32), a, precision=HIGHEST, preferred_element_type=F32)
    a_last = a_cs[L - 1:L, :]
    ea = jnp.exp(a_cs)
    dsdt = jnp.exp(a_last - a_cs) * dt
    cd = jnp.exp(a_last)
    a_cs_t = a_cs.T
    dt_t = dt.T
    dsdt_t = dsdt.T

    for g in range(SSD_GROUPS):
        b_g = xc[:, SSD_WIDTH + g * SSD_STATE:SSD_WIDTH + (g + 1) * SSD_STATE]
        c_off = SSD_WIDTH + SSD_GROUPS * SSD_STATE
        c_g = xc[:, c_off + g * SSD_STATE:c_off + (g + 1) * SSD_STATE]
        cb = lax.dot_general(c_g.astype(BF16), b_g.astype(BF16), (((1,), (1,)), ((), ())),
                             preferred_element_type=F32)
        b_gt = b_g.T
        low = lax.broadcasted_iota(I32, (L, LANES), 1) < P
        for pair in range(g * SSD_HEADS_PER_GROUP // 2, (g + 1) * SSD_HEADS_PER_GROUP // 2):
            h0 = 2 * pair
            xs_p = xc[:, h0 * P:(h0 + 2) * P].astype(BF16)
            s_prev = st_ref[pair]
            rhs = jnp.concatenate([xs_p, s_prev.astype(BF16)], axis=0)
            ys, news = [], []
            for h in (h0, h0 + 1):
                seg = a_cs[:, h:h + 1] - a_cs_t[h:h + 1, :]
                dec = jnp.exp(jnp.where(causal, seg, NEG_INF))
                m = (cb * dec * dt_t[h:h + 1, :]).astype(BF16)
                c_s = (c_g * ea[:, h:h + 1]).astype(BF16)
                lhs = jnp.concatenate([m, c_s], axis=1)
                ys.append(jnp.dot(lhs, rhs, preferred_element_type=F32))
                bw = (b_gt * dsdt_t[h:h + 1, :]).astype(BF16)
                news.append(jnp.dot(bw, xs_p, preferred_element_type=F32))
            yacc_ref[:, h0 * P:(h0 + 2) * P] = jnp.where(low, ys[0], ys[1])
            cd_p = jnp.where(low[0:1, :], cd[:, h0:h0 + 1], cd[:, h0 + 1:h0 + 2])
            st_ref[pair] = s_prev * cd_p + jnp.where(low, news[0], news[1])

    y = yacc_ref[...] + xc[:, :SSD_WIDTH] * dskip_ref[...]
    y = y * _silu(z_ref[...].astype(F32))
    gw = SSD_WIDTH // SSD_GROUPS
    for g in range(SSD_GROUPS):
        yg = y[:, g * gw:(g + 1) * gw]
        ms = jnp.mean(yg * yg, axis=-1, keepdims=True)
        y_ref[:, g * gw:(g + 1) * gw] = (yg * lax.rsqrt(ms + EPS) * normw_ref[:, g * gw:(g + 1) * gw]).astype(BF16)


def _ssd(u, dt_raw, conv_w, conv_b, dt_bias, a_neg, dskip_lanes, norm_w, bsz, seq):
    n = u.shape[0]
    nc = seq // SSD_CHUNK
    xbc_blk = (SSD_WIDTH + 2 * ATTN_WIDTH) // SSD_CONV_DIM
    full = lambda shape: pl.BlockSpec(shape, lambda b, c: (0, 0))
    return pl.pallas_call(
        _ssd_kernel,
        grid=(bsz, nc),
        in_specs=[
            pl.BlockSpec((SSD_CHUNK, SSD_WIDTH), lambda b, c: (b * nc + c, 0)),
            pl.BlockSpec((SSD_CHUNK, SSD_CONV_DIM), lambda b, c: (b * nc + c, xbc_blk)),
            pl.BlockSpec((SSD_CHUNK, LANES), lambda b, c: (b * nc + c, 0)),
            full((SSD_CONV, SSD_CONV_DIM)),
            full((1, SSD_CONV_DIM)),
            full((1, LANES)),
            full((1, LANES)),
            full((1, SSD_WIDTH)),
            full((1, SSD_WIDTH)),
        ],
        out_specs=pl.BlockSpec((SSD_CHUNK, SSD_WIDTH), lambda b, c: (b * nc + c, 0)),
        out_shape=jax.ShapeDtypeStruct((n, SSD_WIDTH), BF16),
        scratch_shapes=[
            pltpu.VMEM((SSD_CHUNK + 8, SSD_CONV_DIM), F32),
            pltpu.VMEM((SSD_HEADS // 2, SSD_STATE, 2 * SSD_HEAD_DIM), F32),
            pltpu.VMEM((SSD_CHUNK, SSD_WIDTH), F32),
        ],
        compiler_params=_cparams(("parallel", "arbitrary")),
        name="ssd",
    )(u, u, dt_raw, conv_w, conv_b, dt_bias, a_neg, dskip_lanes, norm_w)


def _rope_spread():
    half = ROPE_DIM // 2
    lane = jnp.arange(LANES)
    d = lane % ATTN_QK_DIM
    src = jnp.arange(LANES)[:, None]
    cos_src = jnp.where(d < ROPE_DIM, d % half, half)
    cos_tab = (src == cos_src[None, :]).astype(F32)
    lo_tab = -((src == (2 * half + d)[None, :]) & (d < half)[None, :]).astype(F32)
    hi_tab = ((src == (2 * half + d - half)[None, :]) & ((d >= half) & (d < ROPE_DIM))[None, :]).astype(F32)
    return jnp.concatenate([cos_tab, lo_tab, hi_tab], axis=1).astype(BF16)


def _attn_kernel(q_ref, k_ref, vt_ref, lamv_ref, subw_ref, o_ref, acc_ref, *, lam_init):
    qi = pl.program_id(2)
    lane = lax.broadcasted_iota(I32, (TQ, LANES), 1)
    qs = []
    for hb in range(ATTN_HB):
        q = q_ref[:, hb * LANES:(hb + 1) * LANES]
        zero = jnp.zeros_like(q)
        qs.append(jnp.concatenate([jnp.where(lane < ATTN_QK_DIM, q, zero),
                                   jnp.where(lane >= ATTN_QK_DIM, q, zero)], axis=0))
    acc_ref[...] = jnp.zeros_like(acc_ref)
    kv_idx = lax.broadcasted_iota(I32, (TQ, 2 * TQ), 0)
    q_idx = lax.broadcasted_iota(I32, (TQ, 2 * TQ), 1) & (TQ - 1)
    nt = (((1,), (1,)), ((), ()))

    def block(j, carry, masked):
        off = pl.multiple_of(j * TQ, TQ)
        ss = []
        for hb in range(ATTN_HB):
            kb = k_ref[pl.ds(off, TQ), hb * LANES:(hb + 1) * LANES]
            ss.append(lax.dot_general(kb, qs[hb], nt, preferred_element_type=F32))
        new, ps, alphas = [], [], []
        for hb in range(ATTN_HB):
            m_old = carry[hb]
            s = ss[hb]
            if masked:
                s = jnp.where(kv_idx <= q_idx, s, NEG_INF)
            m_new = jnp.maximum(m_old, jnp.max(s, axis=0, keepdims=True))
            alphas.append(jnp.exp2(m_old - m_new))
            ps.append(jnp.exp2(s - m_new).astype(BF16))
            new.append(m_new)
        pvs = []
        for hb in range(ATTN_HB):
            vb = jnp.concatenate([vt_ref[hb * LANES:(hb + 1) * LANES, pl.ds(off, TQ)], ones_rows], axis=0)
            pvs.append(jnp.dot(vb, ps[hb], preferred_element_type=F32))
        for hb in range(ATTN_HB):
            acc_ref[hb] = alphas[hb] * acc_ref[hb] + pvs[hb]
        return tuple(new)

    ones_rows = jnp.ones((ONES_ROWS, TQ), BF16)
    init = (jnp.full((1, 2 * TQ), NEG_INF, F32),) * ATTN_HB
    carry = lax.fori_loop(0, qi, lambda j, cr: block(j, cr, False), init)
    block(qi, carry, True)

    lv = lamv_ref[...]
    lam = (jnp.exp(jnp.sum(lv[0:1] * lv[1:2], axis=1, keepdims=True))
           - jnp.exp(jnp.sum(lv[2:3] * lv[3:4], axis=1, keepdims=True)) + lam_init)
    for hb in range(ATTN_HB):
        acc = acc_ref[hb]
        o2 = acc[:ATTN_V_DIM] * (1.0 / acc[ATTN_V_DIM:ATTN_V_DIM + 1])
        o_t = o2[:, :TQ] - lam * o2[:, TQ:]
        ms = jnp.mean(o_t * o_t, axis=0, keepdims=True)
        o_t = o_t * lax.rsqrt(ms + EPS) * subw_ref[...] * (1.0 - lam_init)
        o_ref[:, hb * LANES:(hb + 1) * LANES] = o_t.T.astype(BF16)


def _attention(u, v_t, lam_vecs, subw_col, bsz, seq, lam_init):
    n = u.shape[0]
    nq = seq // TQ
    w = ATTN_HB * ATTN_V_DIM
    q_blk, k_blk = Q_COL // w, K_COL // w
    return pl.pallas_call(
        functools.partial(_attn_kernel, lam_init=lam_init),
        grid=(bsz, ATTN_HEADS // ATTN_HB, nq),
        in_specs=[
            pl.BlockSpec((TQ, w), lambda b, h, i: (b * nq + i, q_blk + h)),
            pl.BlockSpec((seq, w), lambda b, h, i: (b, k_blk + h)),
            pl.BlockSpec((w, seq), lambda b, h, i: (h, b)),
            pl.BlockSpec((4, ATTN_QK_DIM), lambda b, h, i: (0, 0)),
            pl.BlockSpec((ATTN_V_DIM, 1), lambda b, h, i: (0, 0)),
        ],
        out_specs=pl.BlockSpec((TQ, w), lambda b, h, i: (b * nq + i, h)),
        out_shape=jax.ShapeDtypeStruct((n, ATTN_WIDTH), BF16),
        scratch_shapes=[pltpu.VMEM((ATTN_HB, ATTN_V_DIM + ONES_ROWS, 2 * TQ), F32)],
        compiler_params=_cparams(("parallel", "parallel", "arbitrary")),
        name="attn",
    )(u, u, v_t, lam_vecs, subw_col)


def _pack_halves(x):
    c = x.shape[1] // 2
    lo = pltpu.bitcast(x[:, :c].astype(BF16).astype(F32), U32) >> 16
    hi = pltpu.bitcast(x[:, c:].astype(BF16).astype(F32), U32) & jnp.uint32(0xFFFF0000)
    return hi | lo


TOKEN_ROWS = D_MODEL // 2 // LANES


def _store_token_rows(ref, first_piece, packed):
    tm = packed.shape[0]
    for j in range(packed.shape[1] // LANES):
        ref[pl.ds(first_piece + j, tm, stride=TOKEN_ROWS), :] = packed[:, j * LANES:(j + 1) * LANES]


def _load_token_rows(ref, tm):
    return jnp.concatenate([ref[pl.ds(s, tm, stride=TOKEN_ROWS), :] for s in range(TOKEN_ROWS)], axis=1)


def _unpack_halves(w, dtype=BF16):
    lo = pltpu.bitcast(w << 16, F32).astype(dtype)
    hi = pltpu.bitcast(w & jnp.uint32(0xFFFF0000), F32).astype(dtype)
    return lo, hi


def _outproj_kernel(x_ref, ys_ref, ya_ref, wos_ref, woa_ref, ln2_ref, wr_ref, br_ref,
                    h_ref, hnp_ref, meta_ref, cnt_ref):
    tm = x_ref.shape[0]
    h = (x_ref[...]
         + jnp.dot(ys_ref[...], wos_ref[...], preferred_element_type=F32)
         + jnp.dot(ya_ref[...], woa_ref[...], preferred_element_type=F32))
    h_ref[...] = h
    ms = jnp.mean(h * h, axis=-1, keepdims=True)
    hn = h * lax.rsqrt(ms + EPS) * ln2_ref[...]
    _store_token_rows(hnp_ref, 0, _pack_halves(hn))

    lg_t = lax.dot_general(wr_ref[...], hn.astype(BF16), (((1,), (1,)), ((), ())),
                           preferred_element_type=F32)
    lg = lg_t[0:ROUTER_ROWS, :] + br_ref[0:ROUTER_ROWS, :]
    row = lax.broadcasted_iota(I32, (ROUTER_ROWS, tm), 0).astype(F32)
    big = float(LANES)
    gl = jnp.where(row < N_EXPERT_GROUPS, lg, NEG_INF)
    gmax = jnp.max(gl, axis=0, keepdims=True)
    gsel = jnp.min(jnp.where(gl == gmax, row, big), axis=0, keepdims=True)
    g_w = 1.0 / jnp.sum(jnp.exp(gl - gmax), axis=0, keepdims=True)
    eid = row - N_EXPERT_GROUPS
    lo = gsel * EXPERTS_PER_GROUP
    emask = (eid >= lo) & (eid < lo + EXPERTS_PER_GROUP)
    el = jnp.where(emask, lg, NEG_INF)
    m1 = jnp.max(el, axis=0, keepdims=True)
    i1 = jnp.min(jnp.where(el == m1, eid, big), axis=0, keepdims=True)
    el2 = jnp.where(eid == i1, NEG_INF, el)
    m2 = jnp.max(el2, axis=0, keepdims=True)
    i2 = jnp.min(jnp.where(el2 == m2, eid, big), axis=0, keepdims=True)
    e2 = jnp.exp(m2 - m1)
    w1 = g_w / (1.0 + e2)
    w2 = g_w * e2 / (1.0 + e2)
    mrow = lax.broadcasted_iota(I32, (LANES, tm), 0)
    meta_ref[...] = jnp.where(mrow == 0, i1, jnp.where(mrow == 1, i2, jnp.where(mrow == 2, w1, jnp.where(mrow == 3, w2, 0.0))))
    hits = (eid == i1).astype(F32) + (eid == i2).astype(F32)
    cnt_ref[...] = jnp.broadcast_to(jnp.sum(hits, axis=1, keepdims=True), cnt_ref.shape)


def _out_proj(x2, y_ssd, y_att, wo_s, wo_a, ln2_w, w_router_t, b_router):
    n = x2.shape[0]
    full = lambda shape: pl.BlockSpec(shape, lambda i: (0, 0), pipeline_mode=pl.Buffered(1))
    return pl.pallas_call(
        _outproj_kernel,
        grid=(n // TM_OUT,),
        in_specs=[
            pl.BlockSpec((TM_OUT, D_MODEL), lambda i: (i, 0)),
            pl.BlockSpec((TM_OUT, SSD_WIDTH), lambda i: (i, 0)),
            pl.BlockSpec((TM_OUT, ATTN_WIDTH), lambda i: (i, 0)),
            full((SSD_WIDTH, D_MODEL)), full((ATTN_WIDTH, D_MODEL)),
            full((1, D_MODEL)), full((LANES, D_MODEL)), full((LANES, 1)),
        ],
        out_specs=[
            pl.BlockSpec((TM_OUT, D_MODEL), lambda i: (i, 0)),
            pl.BlockSpec((TM_OUT * TOKEN_ROWS, LANES), lambda i: (i, 0)),
            pl.BlockSpec((LANES, TM_OUT), lambda i: (0, i)),
            pl.BlockSpec((None, ROUTER_ROWS, LANES), lambda i: (i, 0, 0)),
        ],
        out_shape=[
            jax.ShapeDtypeStruct((n, D_MODEL), F32),
            jax.ShapeDtypeStruct((n * TOKEN_ROWS, LANES), U32),
            jax.ShapeDtypeStruct((LANES, n), F32),
            jax.ShapeDtypeStruct((n // TM_OUT, ROUTER_ROWS, LANES), F32),
        ],
        compiler_params=_cparams(("parallel",)),
        name="out_proj",
    )(x2, y_ssd, y_att, wo_s, wo_a, ln2_w, w_router_t, b_router)


def _rank_kernel(meta_ref, offs_ref, dest_ref, run_ref):
    tm = meta_ref.shape[1]
    meta = meta_ref[...]
    row = lax.broadcasted_iota(I32, (N_EXPERTS, tm), 0).astype(F32)
    oh0 = (row == meta[0:1, :]).astype(F32)
    oh1 = (row == meta[1:2, :]).astype(F32)
    oh = oh0 + oh1

    @pl.when(pl.program_id(0) == 0)
    def _():
        run_ref[...] = jnp.zeros_like(run_ref)

    r = lax.broadcasted_iota(I32, (tm, tm), 0)
    c = lax.broadcasted_iota(I32, (tm, tm), 1)
    before = jnp.dot(oh.astype(BF16), (r < c).astype(BF16), preferred_element_type=F32)
    base = before + jnp.tile(run_ref[...] + offs_ref[...], (1, tm // LANES))
    d0 = jnp.sum(oh0 * base, axis=0, keepdims=True)
    d1 = jnp.sum(oh1 * base, axis=0, keepdims=True)
    drow = lax.broadcasted_iota(I32, (8, tm), 0)
    dest_ref[...] = jnp.where(drow == 0, d0, jnp.where(drow == 1, d1, 0.0)).astype(I32)
    run_ref[...] = run_ref[...] + jnp.sum(oh, axis=1, keepdims=True)


def _rank(meta_t, offsets):
    n = meta_t.shape[1]
    return pl.pallas_call(
        _rank_kernel,
        grid=(n // TM_RANK,),
        in_specs=[pl.BlockSpec((8, TM_RANK), lambda i: (0, i)), pl.BlockSpec((N_EXPERTS, LANES), lambda i: (0, 0))],
        out_specs=pl.BlockSpec((8, TM_RANK), lambda i: (0, i)),
        out_shape=jax.ShapeDtypeStruct((8, n), I32),
        scratch_shapes=[pltpu.VMEM((N_EXPERTS, LANES), F32)],
        compiler_params=_cparams(("arbitrary",)),
        name="rank",
    )(meta_t, offsets)


def _token_copy(src_ref, src_token, dst_ref, dst_token, sem):
    src = src_ref.at[pl.ds(pl.multiple_of(src_token * TOKEN_ROWS, TOKEN_ROWS), TOKEN_ROWS)]
    dst = dst_ref.at[pl.ds(pl.multiple_of(dst_token * TOKEN_ROWS, TOKEN_ROWS), TOKEN_ROWS)]
    return pltpu.make_async_copy(src, dst, sem)


CLEAR_GROUP = 8


def _invert_kernel(d0_ref, d1_ref, valid_ref, inv_ref):
    n = d0_ref.shape[0]

    def clear_tile(tile, _):
        def clear(c, _):
            for k in range(CLEAR_GROUP):
                inv_ref[tile * TM_EXP + c * CLEAR_GROUP + k] = 0
            return 0

        lax.fori_loop(valid_ref[tile] // CLEAR_GROUP, TM_EXP // CLEAR_GROUP, clear, 0)
        return 0

    lax.fori_loop(0, valid_ref.shape[0], clear_tile, 0)

    def put(t, _):
        inv_ref[d0_ref[t]] = t
        inv_ref[d1_ref[t]] = n + t
        return 0

    lax.fori_loop(0, n, put, 0, unroll=8)


def _invert(dest0, dest1, tile_valid):
    smem = pl.BlockSpec(memory_space=pltpu.SMEM)
    return pl.pallas_call(
        _invert_kernel,
        in_specs=[smem, smem, smem],
        out_specs=smem,
        out_shape=jax.ShapeDtypeStruct((tile_valid.shape[0] * TM_EXP,), I32),
        name="invert",
    )(dest0, dest1, tile_valid)


CAST_ROWS = 256


def _cast_weight(src_ref, dst_ref):
    def body(i, _):
        rows = pl.ds(pl.multiple_of(i * CAST_ROWS, CAST_ROWS), CAST_ROWS)
        dst_ref[rows, :] = src_ref[rows, :].astype(BF16)
        return 0

    lax.fori_loop(0, src_ref.shape[0] // CAST_ROWS, body, 0)


S_EXPERT, S_FIRST, S_SLOT, S_NEXT, S_HAS_NEXT, S_USED, S_VALID = range(7)
WEIGHT_DMA_PRIORITY = 1


def _load_expert_weights(t, sched_ref, triples, sems):
    def copies(expert, slot):
        return [pltpu.make_async_copy(w.at[expert], stage.at[slot], sems.at[i, slot])
                for i, (w, stage, _) in enumerate(triples)]

    @pl.when(sched_ref[S_FIRST, t] == 1)
    def _():
        slot = sched_ref[S_SLOT, t]

        @pl.when(t == 0)
        def _():
            for cp in copies(sched_ref[S_EXPERT, t], slot):
                cp.start(priority=WEIGHT_DMA_PRIORITY)

        for cp in copies(sched_ref[S_EXPERT, t], slot):
            cp.wait()

        @pl.when(sched_ref[S_HAS_NEXT, t] == 1)
        def _():
            for cp in copies(sched_ref[S_NEXT, t], 1 - slot):
                cp.start(priority=WEIGHT_DMA_PRIORITY)

        for _, stage, dst in triples:
            _cast_weight(stage.at[slot], dst)


ROW_DMA_GROUPS = 4
GATHER_AHEAD = 3


def _tile_rows(sched_ref, tile, group):
    valid = sched_ref[S_VALID, tile]
    per = TM_EXP // ROW_DMA_GROUPS
    for r in range(group * per, (group + 1) * per):
        ok = r < valid
        yield r, jnp.where(ok, tile * TM_EXP + r, 0), ok


def _experts_up_kernel(sched_ref, inv_ref, hnp_hbm, wg_hbm, wu_hbm, hid_ref,
                       xbuf_ref, wg_stage, wu_stage, wgb_ref, wub_ref, wsems, gsem, *, n_tokens):
    t = pl.program_id(0)
    used = sched_ref[S_USED, 0]
    nbuf = GATHER_AHEAD + 1
    cur = t % nbuf
    ahead = (t + GATHER_AHEAD) % nbuf
    _load_expert_weights(t, sched_ref, [(wg_hbm, wg_stage, wgb_ref), (wu_hbm, wu_stage, wub_ref)], wsems)

    def start_rows(tile, buf, group):
        for r, slot, _ in _tile_rows(sched_ref, tile, group):
            token = inv_ref[slot] & (n_tokens - 1)
            _token_copy(hnp_hbm, token, xbuf_ref.at[buf], r, gsem.at[buf]).start()

    def wait_rows(buf):
        pltpu.make_async_copy(hnp_hbm.at[pl.ds(0, TM_EXP * TOKEN_ROWS)], xbuf_ref.at[buf], gsem.at[buf]).wait()

    for first in range(GATHER_AHEAD):
        @pl.when((t == 0) & (first < used))
        def _():
            for g in range(ROW_DMA_GROUPS):
                start_rows(first, first, g)

    def multiply(fetch_ahead):
        wait_rows(cur)
        c = D_MODEL // 2
        x_lo, x_hi = _unpack_halves(_load_token_rows(xbuf_ref.at[cur], TM_EXP))
        pieces = ((x_lo, wgb_ref, 0), (x_hi, wgb_ref, c), (x_lo, wub_ref, 0), (x_hi, wub_ref, c))
        acc = []
        for g, (x, w_ref, row0) in enumerate(pieces):
            if fetch_ahead:
                start_rows(t + GATHER_AHEAD, ahead, g)
            acc.append(jnp.dot(x, w_ref[row0:row0 + c, :], preferred_element_type=F32))
        hid_ref[...] = (_silu(acc[0] + acc[1]) * (acc[2] + acc[3])).astype(BF16)

    @pl.when(t + GATHER_AHEAD < used)
    def _():
        multiply(True)

    @pl.when((t + GATHER_AHEAD >= used) & (t < used))
    def _():
        multiply(False)

    @pl.when(t >= used)
    def _():
        hid_ref[...] = jnp.zeros_like(hid_ref)


def _experts_down_kernel(sched_ref, inv_ref, hid_ref, wd_hbm, y_hbm,
                         ybuf_ref, wd_stage, wdb_ref, wsems, ssem, *, n_tokens):
    t = pl.program_id(0)
    used = sched_ref[S_USED, 0]
    cur = t % 2
    _load_expert_weights(t, sched_ref, [(wd_hbm, wd_stage, wdb_ref)], wsems)

    def start_rows(tile, buf, group):
        for r, slot, ok in _tile_rows(sched_ref, tile, group):
            dst = jnp.where(ok, inv_ref[slot], TOP_K * n_tokens + r)
            _token_copy(ybuf_ref.at[buf], r, y_hbm, dst, ssem.at[buf]).start(priority=r % 2)

    def wait_rows(buf):
        pltpu.make_async_copy(ybuf_ref.at[buf], y_hbm.at[pl.ds(0, TM_EXP * TOKEN_ROWS)], ssem.at[buf]).wait()

    def multiply(send_previous):
        hid = hid_ref[...]
        q = D_MODEL // 4
        for half in range(2):
            if send_previous:
                start_rows(t - 1, 1 - cur, 2 * half)
            lo = jnp.dot(hid, wdb_ref[:, half * q:(half + 1) * q], preferred_element_type=F32)
            if send_previous:
                start_rows(t - 1, 1 - cur, 2 * half + 1)
            hi = jnp.dot(hid, wdb_ref[:, (half + 2) * q:(half + 3) * q], preferred_element_type=F32)
            _store_token_rows(ybuf_ref.at[cur], half * (q // LANES), _pack_halves(jnp.concatenate([lo, hi], axis=1)))

    @pl.when((t >= 2) & (t < used))
    def _():
        wait_rows(cur)

    @pl.when(t == 0)
    def _():
        ybuf_ref[1] = jnp.zeros(ybuf_ref.shape[1:], ybuf_ref.dtype)
        spare = pltpu.make_async_copy(ybuf_ref.at[1], y_hbm.at[pl.ds(TOP_K * n_tokens * TOKEN_ROWS, TM_EXP * TOKEN_ROWS)],
                                      ssem.at[1])
        spare.start()
        spare.wait()
        multiply(False)

    @pl.when((t > 0) & (t < used))
    def _():
        multiply(True)

    @pl.when(t == used - 1)
    def _():
        for g in range(ROW_DMA_GROUPS):
            start_rows(t, cur, g)

        @pl.when(t > 0)
        def _():
            wait_rows(1 - cur)

        wait_rows(cur)


def _tile_schedule(counts, n_tiles):
    tiles_per_expert = (counts + TM_EXP - 1) // TM_EXP
    tile_end = jnp.cumsum(tiles_per_expert)
    ids = jnp.arange(n_tiles, dtype=I32)
    raw = jnp.sum(ids[:, None] >= tile_end[None, :], axis=1)
    expert = jnp.minimum(raw, N_EXPERTS - 1).astype(I32)
    first = jnp.concatenate([jnp.ones((1,), I32), (expert[1:] != expert[:-1]).astype(I32)])
    slot = (jnp.cumsum(first) - 1) % 2
    start_idx = jnp.where(first == 1, ids, n_tiles)
    next_start = jnp.concatenate([lax.cummin(start_idx, reverse=True)[1:], jnp.full((1,), n_tiles, I32)])
    has_next = (next_start < n_tiles).astype(I32)
    next_expert = jnp.sum(jnp.where(ids[None, :] == next_start[:, None], expert[None, :], 0), axis=1)
    used = jnp.full((n_tiles,), tile_end[-1], I32)
    onehot = expert[:, None] == jnp.arange(N_EXPERTS)[None, :]
    tile_in_expert = ids - jnp.sum(jnp.where(onehot, (tile_end - tiles_per_expert)[None, :], 0), axis=1)
    rows_left = jnp.sum(jnp.where(onehot, counts[None, :], 0), axis=1) - tile_in_expert * TM_EXP
    valid = jnp.where(raw < N_EXPERTS, jnp.clip(rows_left, 0, TM_EXP), 0)
    return jnp.stack([expert, first, slot, next_expert, has_next, used, valid]).astype(I32)


def _experts(sched, inv, hnp, wg, wu, wd):
    n = hnp.shape[0] // TOKEN_ROWS
    n_slots = inv.shape[0]
    token_buf = lambda nbuf: pltpu.VMEM((nbuf, TM_EXP * TOKEN_ROWS, LANES), U32)
    row_spec = lambda width: pl.BlockSpec((TM_EXP, width), lambda t, sc, iv: (t, 0))
    hbm = pl.BlockSpec(memory_space=pl.ANY)
    big_vmem = pltpu.CompilerParams(dimension_semantics=("arbitrary",), vmem_limit_bytes=EXPERT_VMEM_LIMIT)
    hid = pl.pallas_call(
        functools.partial(_experts_up_kernel, n_tokens=n),
        grid_spec=pltpu.PrefetchScalarGridSpec(
            num_scalar_prefetch=2,
            grid=(n_slots // TM_EXP,),
            in_specs=[hbm, hbm, hbm],
            out_specs=row_spec(EXPERT_HIDDEN),
            scratch_shapes=[token_buf(GATHER_AHEAD + 1)]
                           + [pltpu.VMEM((2, D_MODEL, EXPERT_HIDDEN), F32)] * 2
                           + [pltpu.VMEM((D_MODEL, EXPERT_HIDDEN), BF16)] * 2
                           + [pltpu.SemaphoreType.DMA((2, 2)), pltpu.SemaphoreType.DMA((GATHER_AHEAD + 1,))],
        ),
        out_shape=jax.ShapeDtypeStruct((n_slots, EXPERT_HIDDEN), BF16),
        compiler_params=big_vmem,
        name="experts_up",
    )(sched, inv, hnp, wg, wu)
    return pl.pallas_call(
        functools.partial(_experts_down_kernel, n_tokens=n),
        grid_spec=pltpu.PrefetchScalarGridSpec(
            num_scalar_prefetch=2,
            grid=(n_slots // TM_EXP,),
            in_specs=[row_spec(EXPERT_HIDDEN), hbm],
            out_specs=hbm,
            scratch_shapes=[token_buf(2),
                            pltpu.VMEM((2, EXPERT_HIDDEN, D_MODEL), F32), pltpu.VMEM((EXPERT_HIDDEN, D_MODEL), BF16),
                            pltpu.SemaphoreType.DMA((1, 2)), pltpu.SemaphoreType.DMA((2,))],
        ),
        out_shape=jax.ShapeDtypeStruct(((TOP_K * n + TM_EXP) * TOKEN_ROWS, LANES), U32),
        compiler_params=big_vmem,
        name="experts_down",
    )(sched, inv, hid, wd)


def _combine_kernel(h_ref, meta_ref, y0_ref, y1_ref, o_ref):
    c = D_MODEL // 2
    meta = meta_ref[...].T
    w0, w1 = meta[:, 2:3], meta[:, 3:4]
    tm = h_ref.shape[0]
    lo0, hi0 = _unpack_halves(_load_token_rows(y0_ref, tm), F32)
    lo1, hi1 = _unpack_halves(_load_token_rows(y1_ref, tm), F32)
    o_ref[:, :c] = h_ref[:, :c] + (lo0 * w0 + lo1 * w1)
    o_ref[:, c:] = h_ref[:, c:] + (hi0 * w0 + hi1 * w1)


def _combine(h, meta, y):
    n = h.shape[0]
    nb = n // TM_COMB
    return pl.pallas_call(
        _combine_kernel,
        grid=(nb,),
        in_specs=[
            pl.BlockSpec((TM_COMB, D_MODEL), lambda i: (i, 0)),
            pl.BlockSpec((LANES, TM_COMB), lambda i: (0, i)),
            pl.BlockSpec((TM_COMB * TOKEN_ROWS, LANES), lambda i: (i, 0)),
            pl.BlockSpec((TM_COMB * TOKEN_ROWS, LANES), lambda i: (nb + i, 0)),
        ],
        out_specs=pl.BlockSpec((TM_COMB, D_MODEL), lambda i: (i, 0)),
        out_shape=jax.ShapeDtypeStruct((n, D_MODEL), F32),
        compiler_params=_cparams(("parallel",)),
        name="combine",
    )(h, meta, y, y)


def _lambda_init(layer_idx):
    return 0.8 - 0.6 * math.exp(-0.3 * layer_idx)


def _pad_lanes(v, width=LANES):
    return jnp.pad(v, ((0, 0), (0, width - v.shape[1])))


def _layer(l, x2, pos_rows, bsz, seq, ln1_w, w_in, conv_w, conv_b, dt_bias, a_log, d_skip, ssd_norm_w,
           q_norm_w, k_norm_w, lambda_q1, lambda_k1, lambda_q2, lambda_k2, subln_w, w_out, ln2_w,
           w_router_group, b_router_group, w_router_expert, b_router_expert, w_gate, w_up, w_down):
    n = x2.shape[0]
    c_z, c_xbc, c_dt = SSD_WIDTH, SSD_WIDTH + SSD_CONV_DIM, SSD_WIDTH + SSD_CONV_DIM + SSD_HEADS
    c_q, c_k = c_dt + ATTN_WIDTH, c_dt + 2 * ATTN_WIDTH
    w_cols = [w_in[:, c0:c1].astype(BF16) for c0, c1 in ((0, c_z), (c_dt, c_q), (c_q, c_k), (c_z, c_xbc))]
    w_vt = w_in[:, c_k:].astype(BF16).T
    w_dt = _pad_lanes(w_in[:, c_xbc:c_dt]).astype(BF16)

    inv_freq = jnp.power(ROPE_THETA, -jnp.arange(0, ROPE_DIM, 2, dtype=F32) / ROPE_DIM)
    invf_col = jnp.concatenate([inv_freq, jnp.zeros_like(inv_freq)])[:, None]
    seg_ones = (jnp.arange(LANES)[:, None] // ATTN_QK_DIM == jnp.arange(LANES)[None, :] // ATTN_QK_DIM).astype(BF16)
    q_scale = math.log2(math.e) / math.sqrt(ATTN_QK_DIM)
    u, v_t, dt_raw = _in_proj(x2, ln1_w[None, :], w_cols, w_vt, w_dt, pos_rows, invf_col,
                              jnp.tile(q_norm_w, 2)[None, :] * q_scale, jnp.tile(k_norm_w, 2)[None, :],
                              seg_ones, _rope_spread())

    a_neg = _pad_lanes(-jnp.exp(a_log.astype(F32))[None, :])
    y_ssd = _ssd(u, dt_raw, conv_w, conv_b[None, :], _pad_lanes(dt_bias[None, :]), a_neg,
                 jnp.repeat(d_skip, SSD_HEAD_DIM)[None, :], ssd_norm_w[None, :], bsz, seq)

    lam_vecs = jnp.stack([lambda_q1, lambda_k1, lambda_q2, lambda_k2]).astype(F32)
    y_att = _attention(u, v_t, lam_vecs, subln_w[:, None], bsz, seq, _lambda_init(l))

    w_out_b = w_out.astype(BF16)
    w_router_t = _pad_lanes(jnp.concatenate([w_router_group, w_router_expert], axis=1)).T.astype(BF16)
    b_router = _pad_lanes(jnp.concatenate([b_router_group, b_router_expert])[None, :]).T
    h, hnp, meta, tile_counts = _out_proj(x2, y_ssd, y_att, w_out_b[:SSD_WIDTH], w_out_b[SSD_WIDTH:], ln2_w[None, :],
                                          w_router_t, b_router)

    counts = jnp.sum(tile_counts[:, N_EXPERT_GROUPS:N_EXPERT_GROUPS + N_EXPERTS, 0], axis=0).astype(I32)
    padded = (counts + TM_EXP - 1) // TM_EXP * TM_EXP
    offsets = jnp.broadcast_to((jnp.cumsum(padded) - padded).astype(F32)[:, None], (N_EXPERTS, LANES))
    dest = _rank(meta, offsets)
    dest0, dest1 = dest[0], dest[1]
    n_slots = (n * TOP_K + N_EXPERTS * (TM_EXP - 1)) // TM_EXP * TM_EXP
    sched = _tile_schedule(counts, n_slots // TM_EXP)

    inv = _invert(dest0, dest1, sched[S_VALID])
    y = _experts(sched, inv, hnp, w_gate, w_up, w_down)
    return _combine(h, meta, y)


def kernel(x, positions, ln1_w, w_in, conv_w, conv_b, dt_bias, a_log, d_skip, ssd_norm_w, q_norm_w, k_norm_w,
           lambda_q1, lambda_k1, lambda_q2, lambda_k2, subln_w, w_out, ln2_w, w_router_group, b_router_group,
           w_router_expert, b_router_expert, w_gate, w_up, w_down):
    bsz, seq, d = x.shape
    assert d == D_MODEL and seq % TQ == 0 and (bsz * seq) % TM_IN == 0
    x2 = x.reshape(bsz * seq, d)
    pos_rows = jnp.broadcast_to(positions.astype(F32).reshape(1, bsz * seq), (8, bsz * seq))
    params = (ln1_w, w_in, conv_w, conv_b, dt_bias, a_log, d_skip, ssd_norm_w, q_norm_w, k_norm_w,
              lambda_q1, lambda_k1, lambda_q2, lambda_k2, subln_w, w_out, ln2_w, w_router_group, b_router_group,
              w_router_expert, b_router_expert, w_gate, w_up, w_down)
    for l in range(ln1_w.shape[0]):
        x2 = _layer(l, x2, pos_rows, bsz, seq, *[p[l] for p in params])
    return x2.reshape(bsz, seq, d)
```
